```python
import jax, jax.numpy as jnp
from jax import lax
import numpy as np

D_MODEL = 1024
BATCH = 1
SEQ = 16384
DEPTH = 1
DEC_BATCH = 128
DEC_SEQ = 4
PAST_LEN = 8192
PAGE_SIZE = 128

HEAD_DIM = 64
N_ATT_HEADS = 8
ATT_WIDTH = N_ATT_HEADS * HEAD_DIM
ROT_DIM = HEAD_DIM // 4
ROPE_THETA = 500000.0
DILATED_BRANCHES = ((128, 1), (512, 4), (2048, 16))
WIN_MAX = 2048
QB = 128
N_RET_HEADS = 4
RET_DK = 64
RET_DV = 128
RET_QK = N_RET_HEADS * RET_DK
RET_V = N_RET_HEADS * RET_DV
RET_THETA = 10000.0
RET_CHUNK = 128
MIX_WIDTH = ATT_WIDTH + RET_V
IN_WIDTH = 3 * ATT_WIDTH + 2 * RET_QK + 2 * RET_V
D_FF = 2816
NORM_EPS = 1e-6
NEG = -1e30

kernel_name = 'hymba_dilated_retnet_decoder_step'


def rmsnorm(x, g):
    xf = x.astype(jnp.float32)
    y = xf * lax.rsqrt(jnp.mean(xf * xf, axis=-1, keepdims=True) + NORM_EPS)
    return (y * g.astype(jnp.float32)).astype(x.dtype)


def swiglu(u, w_gate, w_up, w_down):
    return (jax.nn.silu(u @ w_gate) * (u @ w_up)) @ w_down


def rope(x, pos, rot_dim, theta):
    half = rot_dim // 2
    inv = theta ** (-jnp.arange(half, dtype=jnp.float32) * (2.0 / rot_dim))
    ang = pos.astype(jnp.float32)[:, None] * inv[None, :]
    cos = jnp.cos(ang)[:, None, :].astype(x.dtype)
    sin = jnp.sin(ang)[:, None, :].astype(x.dtype)
    x1 = x[..., :half]
    x2 = x[..., half:rot_dim]
    return jnp.concatenate([x1 * cos - x2 * sin, x2 * cos + x1 * sin, x[..., rot_dim:]], axis=-1)


def masked_softmax_stats(scores, mask, v, spec):
    scores = jnp.where(mask, scores, NEG)
    m = jnp.max(scores, axis=-1)
    p = jnp.where(mask, jnp.exp(scores - m[..., None]), 0.0)
    s = jnp.sum(p, axis=-1)
    o = jnp.einsum(spec, p, v) / s[..., None]
    return m, s, o


def combine_branches(stats):
    ms = jnp.stack([st[0] for st in stats])
    ss = jnp.stack([st[1] for st in stats])
    os_ = jnp.stack([st[2] for st in stats])
    w = ss * jnp.exp(ms - jnp.max(ms, axis=0))
    return jnp.sum(w[..., None] * os_, axis=0) / jnp.sum(w, axis=0)[..., None]


def dilated_branch_prompt(q, k, v, window, dilation):
    B, S, H, E = q.shape
    steps = window // dilation
    L = S // dilation
    nb = -(-L // QB)
    Lp = nb * QB

    def blocks(t):
        t = t.reshape(B, L, dilation, H, E).transpose(0, 2, 3, 1, 4)
        t = jnp.pad(t, ((0, 0), (0, 0), (0, 0), (0, Lp - L), (0, 0)))
        return t.reshape(B, dilation, H, nb, QB, E)

    def with_prev(t):
        prev = jnp.pad(t, ((0, 0), (0, 0), (0, 0), (1, 0), (0, 0), (0, 0)))[:, :, :, :nb]
        return jnp.concatenate([prev, t], axis=4)

    qb = blocks(q)
    kk = with_prev(blocks(k))
    vv = with_prev(blocks(v))
    scores = jnp.einsum('brhnqe,brhnke->brhnqk', qb, kk) * (E ** -0.5)
    a = jnp.arange(QB)[:, None]
    c = jnp.arange(2 * QB)[None, :]
    dist = QB + a - c
    band = (dist >= 0) & (dist <= steps)
    mask = band[None] & ((jnp.arange(nb)[:, None, None] > 0) | (c[None] >= QB))
    m, s, o = masked_softmax_stats(scores, mask, vv, 'brhnqk,brhnke->brhnqe')
    o = o.reshape(B, dilation, H, Lp, E)[:, :, :, :L].transpose(0, 3, 1, 2, 4).reshape(B, S, H, E)

    def back(t):
        return t.reshape(B, dilation, H, Lp)[..., :L].transpose(0, 3, 1, 2).reshape(B, S, H)

    return back(m), back(s), o


def dilated_branch_sample(q, kc, vc, window, dilation, pos0):
    T = q.shape[1]
    E = q.shape[-1]
    w_tot = kc.shape[1]
    wb = w_tot - T
    steps = window // dilation
    i = jnp.arange(T)[:, None]
    j = jnp.arange(steps + 1)[None, :]
    idx = wb + i - j * dilation
    pos = pos0 + i - j * dilation
    valid = (idx >= 0) & (pos >= 0)
    idx = jnp.clip(idx, 0, w_tot - 1)
    kg = kc[:, idx]
    vg = vc[:, idx]
    scores = jnp.einsum('bthe,btjhe->bthj', q, kg) * (E ** -0.5)
    return masked_softmax_stats(scores, valid[:, None, :], vg, 'bthj,btjhe->bthe')


def attend_prompt(q, k, v):
    return combine_branches([dilated_branch_prompt(q, k, v, w, d) for (w, d) in DILATED_BRANCHES])


def attend_sample(q, k, v, ck, cv):
    kc = jnp.concatenate([ck.astype(jnp.float32), k], axis=1)
    vc = jnp.concatenate([cv.astype(jnp.float32), v], axis=1)
    return combine_branches([dilated_branch_sample(q, kc, vc, w, d, PAST_LEN) for (w, d) in DILATED_BRANCHES])


def retention(q, k, v, s0, chunk):
    B, S, H, DK = q.shape
    DV = v.shape[-1]
    n = S // chunk
    qf = q.astype(jnp.float32).reshape(B, n, chunk, H, DK)
    kf = k.astype(jnp.float32).reshape(B, n, chunk, H, DK)
    vf = v.astype(jnp.float32).reshape(B, n, chunk, H, DV)
    log_g = jnp.log1p(-jnp.exp2(-5.0 - jnp.arange(H, dtype=jnp.float32)))
    i = jnp.arange(chunk, dtype=jnp.float32)
    diff = i[:, None] - i[None, :]
    decay = jnp.where(diff[None] >= 0, jnp.exp(diff[None] * log_g[:, None, None]), 0.0)
    inner = jnp.einsum('bnihd,bnjhd->bnhij', qf, kf) * decay
    o = jnp.einsum('bnhij,bnjhe->bnihe', inner, vf)
    w_tail = jnp.exp((chunk - 1.0 - i)[None, :] * log_g[:, None])
    upd = jnp.einsum('bnjhd,bnjhe,hj->nbhde', kf, vf, w_tail)
    g_chunk = jnp.exp(chunk * log_g)[:, None, None]

    def step(state, u):
        return g_chunk * state + u, state

    s_final, s_prev = lax.scan(step, s0.astype(jnp.float32), upd)
    w_head = jnp.exp((i + 1.0)[:, None] * log_g[None, :])
    cross = jnp.einsum('bnihd,nbhde->bnihe', qf, s_prev) * w_head[None, None, :, :, None]
    return (o + cross).reshape(B, S, H, DV), s_final


def token_mixer(u, pos, attend, s0, chunk, w_in, gn_w, w_out):
    B, S, _ = u.shape
    sizes = (ATT_WIDTH, ATT_WIDTH, ATT_WIDTH, RET_QK, RET_QK, RET_V, RET_V)
    cuts = [int(c) for c in np.cumsum(sizes)[:-1]]
    q_a, k_a, v_a, q_r, k_r, v_r, g_r = jnp.split(u @ w_in, cuts, axis=-1)
    q_a = rope(q_a.reshape(B, S, N_ATT_HEADS, HEAD_DIM), pos, ROT_DIM, ROPE_THETA)
    k_a = rope(k_a.reshape(B, S, N_ATT_HEADS, HEAD_DIM), pos, ROT_DIM, ROPE_THETA)
    v_a = v_a.reshape(B, S, N_ATT_HEADS, HEAD_DIM)
    q_r = rope(q_r.reshape(B, S, N_RET_HEADS, RET_DK), pos, RET_DK, RET_THETA)
    k_r = rope(k_r.reshape(B, S, N_RET_HEADS, RET_DK), pos, RET_DK, RET_THETA) * (RET_DK ** -0.5)
    v_r = v_r.reshape(B, S, N_RET_HEADS, RET_DV)
    o_att = attend(q_a.astype(jnp.float32), k_a.astype(jnp.float32), v_a.astype(jnp.float32))
    o_ret, s_new = retention(q_r, k_r, v_r, s0, chunk)
    mu = jnp.mean(o_ret, axis=-1, keepdims=True)
    var = jnp.mean(jnp.square(o_ret - mu), axis=-1, keepdims=True)
    y_ret = ((o_ret - mu) * lax.rsqrt(var + NORM_EPS)).reshape(B, S, RET_V) * gn_w.astype(jnp.float32)
    y_ret = jax.nn.silu(g_r.astype(jnp.float32)) * y_ret
    mixed = jnp.concatenate([o_att.reshape(B, S, ATT_WIDTH), y_ret], axis=-1).astype(u.dtype)
    return mixed @ w_out, k_a, v_a, s_new


def decoder_layer(h, pos, attend, s0, chunk, g1a, g1b, w1g, w1u, w1d, gma, gmb, w_in, gn_w, w_out,
                  g2a, g2b, w2g, w2u, w2d):
    h = h + 0.5 * rmsnorm(swiglu(rmsnorm(h, g1a), w1g, w1u, w1d), g1b)
    y, k, v, s = token_mixer(rmsnorm(h, gma), pos, attend, s0, chunk, w_in, gn_w, w_out)
    h = h + rmsnorm(y, gmb)
    h = h + 0.5 * rmsnorm(swiglu(rmsnorm(h, g2a), w2g, w2u, w2d), g2b)
    return h, k, v, s


def setup_inputs(seed: int = 0) -> dict:
    key = jax.random.key(seed)
    ks = jax.random.split(key, 24)
    f32 = jnp.float32
    wb = min(WIN_MAX, PAST_LEN)

    def nrm(k, shape, scale):
        return jax.random.normal(k, shape, f32) * scale

    def gain(k, n):
        return 1.0 + 0.05 * jax.random.normal(k, (DEPTH, n), f32)

    return {
        'x_prompt': nrm(ks[0], (BATCH, SEQ, D_MODEL), 1.0),
        'x_sample': nrm(ks[1], (DEC_BATCH, DEC_SEQ, D_MODEL), 1.0),
        'cache_k': nrm(ks[2], (DEPTH, DEC_BATCH, wb, N_ATT_HEADS, HEAD_DIM), 1.0),
        'cache_v': nrm(ks[3], (DEPTH, DEC_BATCH, wb, N_ATT_HEADS, HEAD_DIM), 1.0),
        'state_ret': nrm(ks[4], (DEPTH, DEC_BATCH, N_RET_HEADS, RET_DK, RET_DV), 0.5),
        'g_ffn1_pre': gain(ks[5], D_MODEL),
        'g_ffn1_post': gain(ks[6], D_MODEL),
        'w1_gate': nrm(ks[7], (DEPTH, D_MODEL, D_FF), D_MODEL ** -0.5),
        'w1_up': nrm(ks[8], (DEPTH, D_MODEL, D_FF), D_MODEL ** -0.5),
        'w1_down': nrm(ks[9], (DEPTH, D_FF, D_MODEL), D_FF ** -0.5),
        'g_mix_pre': gain(ks[10], D_MODEL),
        'g_mix_post': gain(ks[11], D_MODEL),
        'w_in': nrm(ks[12], (DEPTH, D_MODEL, IN_WIDTH), D_MODEL ** -0.5),
        'gn_w': gain(ks[13], RET_V),
        'w_out': nrm(ks[14], (DEPTH, MIX_WIDTH, D_MODEL), MIX_WIDTH ** -0.5),
        'g_ffn2_pre': gain(ks[15], D_MODEL),
        'g_ffn2_post': gain(ks[16], D_MODEL),
        'w2_gate': nrm(ks[17], (DEPTH, D_MODEL, D_FF), D_MODEL ** -0.5),
        'w2_up': nrm(ks[18], (DEPTH, D_MODEL, D_FF), D_MODEL ** -0.5),
        'w2_down': nrm(ks[19], (DEPTH, D_FF, D_MODEL), D_FF ** -0.5),
    }


def reference(x_prompt, x_sample, cache_k, cache_v, state_ret,
              g_ffn1_pre, g_ffn1_post, w1_gate, w1_up, w1_down,
              g_mix_pre, g_mix_post, w_in, gn_w, w_out,
              g_ffn2_pre, g_ffn2_post, w2_gate, w2_up, w2_down):
    b_p, s_p = x_prompt.shape[0], x_prompt.shape[1]
    t_s = x_sample.shape[1]
    pos_p = jnp.arange(s_p)
    pos_s = PAST_LEN + jnp.arange(t_s)
    keep = min(WIN_MAX, s_p)
    hp, hs = x_prompt, x_sample
    kp_l, vp_l, sp_l, ks_l, vs_l, ss_l = [], [], [], [], [], []
    for l in range(DEPTH):
        lw = (g_ffn1_pre[l], g_ffn1_post[l], w1_gate[l], w1_up[l], w1_down[l],
              g_mix_pre[l], g_mix_post[l], w_in[l], gn_w[l], w_out[l],
              g_ffn2_pre[l], g_ffn2_post[l], w2_gate[l], w2_up[l], w2_down[l])
        s0_p = jnp.zeros((b_p, N_RET_HEADS, RET_DK, RET_DV), jnp.float32)
        hp, kp, vp, sp = decoder_layer(hp, pos_p, attend_prompt, s0_p, RET_CHUNK, *lw)
        ck, cv = cache_k[l], cache_v[l]
        hs, kn, vn, sn = decoder_layer(hs, pos_s, lambda q, k, v: attend_sample(q, k, v, ck, cv),
                                       state_ret[l], t_s, *lw)
        kp_l.append(kp[:, s_p - keep:])
        vp_l.append(vp[:, s_p - keep:])
        sp_l.append(sp)
        ks_l.append(kn)
        vs_l.append(vn)
        ss_l.append(sn)
    new_k_prompt = jnp.stack(kp_l)
    new_v_prompt = jnp.stack(vp_l)
    new_state_prompt = jnp.stack(sp_l)
    new_k_sample = jnp.stack(ks_l)
    new_v_sample = jnp.stack(vs_l)
    new_state_sample = jnp.stack(ss_l)
    return (hp, hs, new_k_prompt, new_v_prompt, new_state_prompt, new_k_sample, new_v_sample, new_state_sample)
```

```python
import functools
import math

import numpy as np
import jax
import jax.numpy as jnp
from jax import lax
from jax.experimental import pallas as pl
from jax.experimental.pallas import tpu as pltpu

F32 = jnp.float32
BF16 = jnp.bfloat16

D_MODEL = 1024
D_FF = 2816
HEAD_DIM = 64
N_ATT_HEADS = 8
ATT_WIDTH = N_ATT_HEADS * HEAD_DIM
ROT_DIM = HEAD_DIM // 4
ROPE_THETA = 500000.0
WIN_MAX = 2048
BRANCHES = ((128, 1), (512, 4), (2048, 16))
N_RET_HEADS = 4
RET_DK = 64
RET_DV = 128
RET_QK = N_RET_HEADS * RET_DK
RET_V = N_RET_HEADS * RET_DV
RET_THETA = 10000.0
RET_CHUNK = 128
IN_WIDTH = 3 * ATT_WIDTH + 2 * RET_QK + 2 * RET_V
NORM_EPS = 1e-6
NEG = -1e30

LANES = 128
QB = 128
SUPER = 2048
RUN = 512
VMEM_LIMIT = 56 * 1024 * 1024
LOG_G = tuple(math.log1p(-2.0 ** (-5.0 - h)) for h in range(N_RET_HEADS))


def _full_spec(shape):
    nd = len(shape)
    return pl.BlockSpec(shape, lambda *_: (0,) * nd)


def _resident_spec(shape):
    nd = len(shape)
    return pl.BlockSpec(shape, lambda *_: (0,) * nd, pipeline_mode=pl.Buffered(1))


def _params(n_axes):
    return pltpu.CompilerParams(dimension_semantics=("arbitrary",) * n_axes,
                                vmem_limit_bytes=VMEM_LIMIT)


def _rms(x, g):
    return x * lax.rsqrt(jnp.mean(x * x, axis=-1, keepdims=True) + NORM_EPS) * g


def _silu(x):
    return x / (1.0 + jnp.exp(-x))


def _dot(a, b):
    return jnp.dot(a, b, preferred_element_type=F32)


def _dot_nt(a, b):
    return lax.dot_general(a, b, (((1,), (1,)), ((), ())), preferred_element_type=F32)


def _dot_tn(a, b):
    return lax.dot_general(a, b, (((0,), (0,)), ((), ())), preferred_element_type=F32)


FF_CHUNK = 512


def _swiglu(u, wg_ref, wu_ref, wd_ref):
    acc = None
    for c0 in range(0, D_FF, FF_CHUNK):
        c1 = min(c0 + FF_CHUNK, D_FF)
        g = _dot(u, wg_ref[:, c0:c1])
        up = _dot(u, wu_ref[:, c0:c1])
        h = (_silu(g) * up).astype(BF16)
        d = _dot(h, wd_ref[c0:c1, :])
        acc = d if acc is None else acc + d
    return acc


def _ffn_kernel(x_ref, gpre_ref, gpost_ref, wg_ref, wu_ref, wd_ref, o_ref):
    x = x_ref[...]
    u = _rms(x, gpre_ref[...]).astype(BF16)
    y = _swiglu(u, wg_ref, wu_ref, wd_ref)
    o_ref[...] = x + 0.5 * _rms(y, gpost_ref[...])


def _mixout_ffn_kernel(oa_ref, yr_ref, h_ref, wo_ref, gmb_ref, gpre_ref, gpost_ref,
                       wg_ref, wu_ref, wd_ref, o_ref):
    mixed = jnp.concatenate([oa_ref[...].astype(BF16), yr_ref[...].astype(BF16)], axis=1)
    x = h_ref[...] + _rms(_dot(mixed, wo_ref[...]), gmb_ref[...])
    u = _rms(x, gpre_ref[...]).astype(BF16)
    y = _swiglu(u, wg_ref, wu_ref, wd_ref)
    o_ref[...] = x + 0.5 * _rms(y, gpost_ref[...])


def _ffn_specs():
    return [_full_spec((1, D_MODEL)), _full_spec((1, D_MODEL)),
            _resident_spec((D_MODEL, D_FF)), _resident_spec((D_MODEL, D_FF)),
            _resident_spec((D_FF, D_MODEL))]


def _ffn(x, gpre, gpost, wg, wu, wd, tm):
    t = x.shape[0]
    row = pl.BlockSpec((tm, D_MODEL), lambda i: (i, 0))
    return pl.pallas_call(
        _ffn_kernel, grid=(t // tm,),
        in_specs=[row] + _ffn_specs(), out_specs=row,
        out_shape=jax.ShapeDtypeStruct((t, D_MODEL), F32),
        compiler_params=_params(1), name="ffn")(x, gpre, gpost, wg, wu, wd)


def _mixout_ffn(oa, yr, h, wo, gmb, gpre, gpost, wg, wu, wd, tm):
    t = h.shape[0]
    row = pl.BlockSpec((tm, D_MODEL), lambda i: (i, 0))
    half = pl.BlockSpec((tm, ATT_WIDTH), lambda i: (i, 0))
    return pl.pallas_call(
        _mixout_ffn_kernel, grid=(t // tm,),
        in_specs=[half, half, row, _resident_spec((D_MODEL, D_MODEL)), _full_spec((1, D_MODEL))]
        + _ffn_specs(),
        out_specs=row, out_shape=jax.ShapeDtypeStruct((t, D_MODEL), F32),
        compiler_params=_params(1), name="mixout_ffn")(oa, yr, h, wo, gmb, gpre, gpost, wg, wu, wd)


def _rope_tables(pos, rot_dim, theta, head_dim):
    half = rot_dim // 2
    inv = theta ** (-jnp.arange(half, dtype=F32) * (2.0 / rot_dim))
    ang = pos.astype(F32)[:, None] * inv[None, :]
    cos, sin = jnp.cos(ang), jnp.sin(ang)
    t = pos.shape[0]
    rest = head_dim - rot_dim
    c = jnp.concatenate([cos, cos, jnp.ones((t, rest), F32)], axis=1)
    sa = jnp.concatenate([-sin, jnp.zeros((t, half + rest), F32)], axis=1)
    sb = jnp.concatenate([jnp.zeros((t, half), F32), sin, jnp.zeros((t, rest), F32)], axis=1)
    rep = LANES // head_dim
    return jnp.stack([jnp.tile(c, (1, rep)), jnp.tile(sa, (1, rep)), jnp.tile(sb, (1, rep))])


def _rope(x, tab_ref, half):
    return (x * tab_ref[0] + pltpu.roll(x, LANES - half, 1) * tab_ref[1]
            + pltpu.roll(x, half, 1) * tab_ref[2])


def _mixin_kernel(h_ref, g_ref, w_ref, ta_ref, tr_ref,
                  qa_ref, ka_ref, va_ref, kk_ref, vk_ref, qr_ref, kr_ref, vr_ref, gr_ref,
                  *, first_keep):
    i = pl.program_id(0)
    u = _rms(h_ref[...], g_ref[...]).astype(BF16)
    keep = i >= first_keep
    scale = HEAD_DIM ** -0.5

    q = _dot(u, w_ref[:, 0:ATT_WIDTH])
    for s in range(ATT_WIDTH // LANES):
        sl = slice(s * LANES, (s + 1) * LANES)
        qa_ref[:, sl] = (_rope(q[:, sl], ta_ref, ROT_DIM // 2) * scale).astype(qa_ref.dtype)

    k = _dot(u, w_ref[:, ATT_WIDTH:2 * ATT_WIDTH])
    for s in range(ATT_WIDTH // LANES):
        sl = slice(s * LANES, (s + 1) * LANES)
        ks = _rope(k[:, sl], ta_ref, ROT_DIM // 2)
        ka_ref[:, sl] = ks.astype(ka_ref.dtype)

        @pl.when(keep)
        def _():
            kk_ref[:, sl] = ks

    v = _dot(u, w_ref[:, 2 * ATT_WIDTH:3 * ATT_WIDTH])
    va_ref[...] = v.astype(va_ref.dtype)

    @pl.when(keep)
    def _():
        vk_ref[...] = v

    o = 3 * ATT_WIDTH
    qk = _dot(u, w_ref[:, o:o + 2 * RET_QK])
    for s in range(RET_QK // LANES):
        sl = slice(s * LANES, (s + 1) * LANES)
        qr_ref[:, sl] = _rope(qk[:, sl], tr_ref, RET_DK // 2).astype(qr_ref.dtype)
        sk = slice(RET_QK + s * LANES, RET_QK + (s + 1) * LANES)
        kr_ref[:, sl] = (_rope(qk[:, sk], tr_ref, RET_DK // 2) * (RET_DK ** -0.5)).astype(kr_ref.dtype)
    o += 2 * RET_QK
    vr_ref[...] = _dot(u, w_ref[:, o:o + RET_V]).astype(vr_ref.dtype)
    o += RET_V
    gr_ref[...] = _dot(u, w_ref[:, o:o + RET_V])


def _mixin(h, g, w, tab_att, tab_ret, keep_rows, tm, sample):
    t = h.shape[0]
    nt = t // tm
    first_keep = nt - keep_rows // tm
    act = F32 if sample else BF16

    def row(w_):
        return pl.BlockSpec((tm, w_), lambda i: (i, 0))

    keep_spec = pl.BlockSpec((tm, ATT_WIDTH), lambda i: (jnp.maximum(i - first_keep, 0), 0))
    tab = pl.BlockSpec((3, tm, LANES), lambda i: (0, i, 0))
    shapes = [((t, ATT_WIDTH), act), ((t, ATT_WIDTH), BF16), ((t, ATT_WIDTH), BF16),
              ((keep_rows, ATT_WIDTH), F32), ((keep_rows, ATT_WIDTH), F32),
              ((t, RET_QK), act), ((t, RET_QK), act), ((t, RET_V), act), ((t, RET_V), F32)]
    return pl.pallas_call(
        functools.partial(_mixin_kernel, first_keep=first_keep), grid=(nt,),
        in_specs=[row(D_MODEL), _full_spec((1, D_MODEL)), _resident_spec((D_MODEL, IN_WIDTH)), tab, tab],
        out_specs=[row(ATT_WIDTH), row(ATT_WIDTH), row(ATT_WIDTH), keep_spec, keep_spec,
                   row(RET_QK), row(RET_QK), row(RET_V), row(RET_V)],
        out_shape=[jax.ShapeDtypeStruct(s, d) for s, d in shapes],
        compiler_params=_params(1), name="mix_in")(h, g, w, tab_att, tab_ret)


def _attn_bias_tables():
    a = np.arange(QB)[:, None]
    c = np.arange(2 * QB)[None, :]
    steps = BRANCHES[0][0]
    dist = QB + a - c
    band = (dist >= 0) & (dist <= steps)
    cur = c >= QB
    tok_q = 4 * (a % 32) + a // 32
    cc = c % QB
    tok_k = 4 * (cc % 32) + cc // 32 + QB * (c // QB) - QB
    dist2 = tok_q - tok_k
    band2 = (dist2 >= 0) & (dist2 <= steps)
    masks = np.stack([band, band & cur, band2, band2 & cur])
    return np.where(masks, 0.0, NEG).astype(np.float32)


def _attn_qblock(get_q, get_kc, get_kp, get_vc, get_vp, bias, consts):
    head_a, head_a_win, ones_a, ones_b = consts
    out = []
    for s in range(ATT_WIDTH // LANES):
        q2 = get_q(s)
        kwin = jnp.concatenate([get_kp(s), get_kc(s)], axis=0)
        vwin = jnp.concatenate([get_vp(s), get_vc(s)], axis=0)
        zq = jnp.zeros_like(q2)
        qq = jnp.concatenate([jnp.where(head_a, q2, zq), jnp.where(head_a, zq, q2)], axis=0)
        sc = _dot_nt(qq, kwin)
        s_a = sc[:QB] + bias
        s_b = sc[QB:] + bias
        m_a = jnp.max(s_a, axis=-1, keepdims=True)
        m_b = jnp.max(s_b, axis=-1, keepdims=True)
        p = jnp.concatenate([jnp.exp(s_a - m_a).astype(BF16), jnp.exp(s_b - m_b).astype(BF16)], axis=1)
        zv = jnp.zeros_like(vwin)
        w = jnp.concatenate(
            [jnp.concatenate([jnp.where(head_a_win, vwin, zv), ones_a], axis=1),
             jnp.concatenate([jnp.where(head_a_win, zv, vwin), ones_b], axis=1)], axis=0)
        r = _dot(p, w)
        out.append((jnp.where(head_a, m_a, m_b), r[:, LANES:], r[:, :LANES]))
    return out


def _merge(old, new):
    mo, lo, ao = old
    m2, l2, a2 = new
    mn = jnp.maximum(mo, m2)
    eo = jnp.exp(mo - mn)
    e2 = jnp.exp(m2 - mn)
    return mn, eo * lo + e2 * l2, eo * ao + e2 * a2


def _attn_kernel(q16, k16c, k16p, v16c, v16p, q4, k4c, k4p, v4c, v4p, bias_ref, o_ref,
                 m_s, l_s, a_s):
    sb = pl.program_id(0)
    ph = pl.program_id(1)
    j = pl.program_id(2)
    lane = lax.broadcasted_iota(jnp.int32, (QB, LANES), 1)
    head_a = lane < HEAD_DIM
    lane_w = lax.broadcasted_iota(jnp.int32, (2 * QB, LANES), 1)
    head_a_win = lane_w < HEAD_DIM
    ones_a = jnp.where(head_a_win, 1.0, 0.0).astype(BF16)
    ones_b = jnp.where(head_a_win, 0.0, 1.0).astype(BF16)
    consts = (head_a, head_a_win, ones_a, ones_b)
    nslab = ATT_WIDTH // LANES
    sub = QB // 4

    def lanes(r, s):
        return slice(r * ATT_WIDTH + s * LANES, r * ATT_WIDTH + (s + 1) * LANES)

    @pl.when(ph == 0)
    def _():
        bias = bias_ref[(sb == 0).astype(jnp.int32)]
        for r in range(4):
            res = _attn_qblock(lambda s: q16[:, lanes(r, s)], lambda s: k16c[:, lanes(r, s)],
                               lambda s: k16p[:, lanes(r, s)], lambda s: v16c[:, lanes(r, s)],
                               lambda s: v16p[:, lanes(r, s)], bias, consts)
            for s in range(nslab):
                for u in range(SUPER // RUN):
                    rows = pl.ds(u * RUN + r * QB + j, sub, stride=4)
                    for ref, val in zip((m_s, l_s, a_s), res[s]):
                        ref[s, rows, :] = val[u * sub:(u + 1) * sub]

    @pl.when(ph == 1)
    def _():
        bias = bias_ref[(sb * (SUPER // RUN) + j == 0).astype(jnp.int32)]
        for r in range(4):
            res = _attn_qblock(lambda s: q4[:, lanes(r, s)], lambda s: k4c[:, lanes(r, s)],
                               lambda s: k4p[:, lanes(r, s)], lambda s: v4c[:, lanes(r, s)],
                               lambda s: v4p[:, lanes(r, s)], bias, consts)
            rows = pl.ds(pl.multiple_of(j * RUN + r * QB, QB), QB)
            for s in range(nslab):
                mn, ln, an = _merge((m_s[s, rows, :], l_s[s, rows, :], a_s[s, rows, :]), res[s])
                m_s[s, rows, :] = mn
                l_s[s, rows, :] = ln
                a_s[s, rows, :] = an

    @pl.when(ph == 2)
    def _():
        first = sb * (SUPER // RUN) + j == 0
        for b in range(4):
            def cur(ref):
                return lambda s: jnp.concatenate(
                    [ref[b * sub:(b + 1) * sub, lanes(r, s)] for r in range(4)], axis=0)

            def prev(ref_c, ref_p):
                if b > 0:
                    return lambda s: jnp.concatenate(
                        [ref_c[(b - 1) * sub:b * sub, lanes(r, s)] for r in range(4)], axis=0)
                return lambda s: jnp.concatenate(
                    [ref_p[QB - sub:QB, lanes(r, s)] for r in range(4)], axis=0)

            if b == 0:
                bias = bias_ref[2 + first.astype(jnp.int32)]
            else:
                bias = bias_ref[2]
            res = _attn_qblock(cur(q4), cur(k4c), prev(k4c, k4p), cur(v4c), prev(v4c, v4p),
                               bias, consts)
            for s in range(nslab):
                for r in range(4):
                    rows = pl.ds(pl.multiple_of(j * RUN + r * QB + b * sub, sub), sub)
                    part = tuple(x[r * sub:(r + 1) * sub] for x in res[s])
                    _, ln, an = _merge((m_s[s, rows, :], l_s[s, rows, :], a_s[s, rows, :]), part)
                    o_ref[b * sub:(b + 1) * sub, lanes(r, s)] = (an / ln).astype(o_ref.dtype)


def _attn_prompt(q, k, v):
    s_len = q.shape[0]
    nsb = s_len // SUPER
    rps = SUPER // RUN
    wide = 4 * ATT_WIDTH
    v16 = lambda x: x.reshape(nsb, QB, 16 * ATT_WIDTH)
    v4 = lambda x: x.reshape(s_len // RUN, QB, wide)
    blk = (None, QB, wide)

    def j16(ph, j):
        return jnp.where(ph == 0, j, rps - 1)

    def n4(sb, ph, j):
        return sb * rps + jnp.where(ph == 0, 0, j)

    c16 = pl.BlockSpec(blk, lambda sb, ph, j: (sb, 0, j16(ph, j)))
    p16 = pl.BlockSpec(blk, lambda sb, ph, j: (jnp.maximum(sb - 1, 0), 0, j16(ph, j)))
    c4 = pl.BlockSpec(blk, lambda sb, ph, j: (n4(sb, ph, j), 0, 0))
    p4 = pl.BlockSpec(blk, lambda sb, ph, j: (jnp.maximum(n4(sb, ph, j) - 1, 0), 0, 0))
    out = pl.BlockSpec(blk, lambda sb, ph, j: (sb * rps + jnp.where(ph == 2, j, 0), 0, 0))
    bias = jnp.asarray(_attn_bias_tables())
    scratch = [pltpu.VMEM((ATT_WIDTH // LANES, SUPER, LANES), F32) for _ in range(3)]
    o = pl.pallas_call(
        _attn_kernel, grid=(nsb, 3, rps),
        in_specs=[c16, c16, p16, c16, p16, c4, c4, p4, c4, p4, _full_spec(bias.shape)],
        out_specs=out, out_shape=jax.ShapeDtypeStruct((s_len // RUN, QB, wide), BF16),
        scratch_shapes=scratch, compiler_params=_params(3), name="attn_prompt")(
            v16(q), v16(k), v16(k), v16(v), v16(v), v4(q), v4(k), v4(k), v4(v), v4(v), bias)
    return o.reshape(s_len, ATT_WIDTH)


def _ret_kernel(qr_ref, kr_ref, vr_ref, gr_ref, gn_ref, y_ref, s_ref, dec_s, wt_s, wh_s, st_s,
                *, chunks):
    step = pl.program_id(0)
    c = RET_CHUNK

    @pl.when(step == 0)
    def _():
        i = lax.broadcasted_iota(jnp.int32, (c, c), 0)
        jj = lax.broadcasted_iota(jnp.int32, (c, c), 1)
        diff = (i - jj).astype(F32)
        for h in range(N_RET_HEADS):
            dec_s[h] = jnp.where(diff >= 0, jnp.exp(diff * LOG_G[h]), 0.0)
            wt_s[h] = jnp.exp((c - 1.0 - i.astype(F32)) * LOG_G[h])
            wh_s[h] = jnp.exp((i.astype(F32) + 1.0) * LOG_G[h])
        st_s[...] = jnp.zeros_like(st_s)

    lane = lax.broadcasted_iota(jnp.int32, (c, LANES), 1)
    head_a = lane < RET_DK
    row_a = lax.broadcasted_iota(jnp.int32, (LANES, RET_DV), 0) < RET_DK
    for ci in range(chunks):
        rows = slice(ci * c, (ci + 1) * c)
        for p in range(N_RET_HEADS // 2):
            sl = slice(p * LANES, (p + 1) * LANES)
            q2 = qr_ref[rows, sl]
            k2 = kr_ref[rows, sl]
            zq = jnp.zeros_like(q2)
            qq = jnp.concatenate([jnp.where(head_a, q2, zq), jnp.where(head_a, zq, q2)], axis=0)
            inner = _dot_nt(qq, k2)
            state = st_s[p]
            cross = _dot(qq, state.astype(BF16))
            vws = []
            for hh in range(2):
                h = 2 * p + hh
                hs = slice(h * RET_DV, (h + 1) * RET_DV)
                v = vr_ref[rows, hs]
                inn = (inner[hh * c:(hh + 1) * c] * dec_s[h]).astype(BF16)
                o = _dot(inn, v) + cross[hh * c:(hh + 1) * c] * wh_s[h]
                mu = jnp.mean(o, axis=-1, keepdims=True)
                xc = o - mu
                var = jnp.mean(xc * xc, axis=-1, keepdims=True)
                y = xc * lax.rsqrt(var + NORM_EPS) * gn_ref[:, hs]
                y_ref[rows, hs] = (_silu(gr_ref[rows, hs]) * y).astype(y_ref.dtype)
                vws.append((v.astype(F32) * wt_s[h]).astype(BF16))
            upd = _dot_tn(k2, jnp.concatenate(vws, axis=1))
            upd = jnp.where(row_a, upd[:, :RET_DV], upd[:, RET_DV:])
            gch = jnp.where(row_a, math.exp(c * LOG_G[2 * p]), math.exp(c * LOG_G[2 * p + 1]))
            st_s[p] = gch * state + upd

    @pl.when(step == pl.num_programs(0) - 1)
    def _():
        s_ref[...] = st_s[...]


def _ret_prompt(qr, kr, vr, gr, gn, tm):
    t = qr.shape[0]

    def row(w_):
        return pl.BlockSpec((tm, w_), lambda i: (i, 0))

    st_shape = (N_RET_HEADS // 2, 2 * RET_DK, RET_DV)
    tab = pltpu.VMEM((N_RET_HEADS, RET_CHUNK, RET_CHUNK), F32)
    y, s = pl.pallas_call(
        functools.partial(_ret_kernel, chunks=tm // RET_CHUNK), grid=(t // tm,),
        in_specs=[row(RET_QK), row(RET_QK), row(RET_V), row(RET_V), _full_spec((1, RET_V))],
        out_specs=[row(RET_V), _full_spec(st_shape)],
        out_shape=[jax.ShapeDtypeStruct((t, RET_V), BF16), jax.ShapeDtypeStruct(st_shape, F32)],
        scratch_shapes=[tab, tab, tab, pltpu.VMEM(st_shape, F32)],
        compiler_params=_params(1), name="ret_prompt")(qr, kr, vr, gr, gn)
    return y, s


S_RECENT = 512
S_GROUPS = (WIN_MAX - S_RECENT) // 16
S_KEYS = 1024
S_NEW0 = S_RECENT + 4 * S_GROUPS


def _sample_mult_table(t_new, past_len):
    wb = min(WIN_MAX, past_len)
    row_of = np.full((S_KEYS,), -1, np.int64)
    row_of[:S_RECENT] = wb - S_RECENT + np.arange(S_RECENT)
    for r in range(4):
        row_of[S_RECENT + r * S_GROUPS:S_RECENT + (r + 1) * S_GROUPS] = 16 * np.arange(S_GROUPS) + r
    row_of[S_NEW0:S_NEW0 + t_new] = wb + np.arange(t_new)
    tab = np.zeros((8, S_KEYS), np.float32)
    for i in range(8):
        delta = wb + (i % t_new) - row_of
        ok = (row_of >= 0) & (delta >= 0) & (past_len + (i % t_new) - delta >= 0)
        mult = np.zeros((S_KEYS,), np.float32)
        for window, dil in BRANCHES:
            mult += ok & (delta % dil == 0) & (delta <= window)
        tab[i] = mult
    return np.tile(tab, (N_ATT_HEADS, 1))


def _sample_attn_kernel(q_ref, kn_ref, vn_ref, kr_ref, ko_ref, vr_ref, vo_ref, mult_ref, o_ref,
                        *, nseq, t_new):
    rowi = lax.broadcasted_iota(jnp.int32, (8, ATT_WIDTH), 0)
    lane_head = lax.broadcasted_iota(jnp.int32, (8, ATT_WIDTH), 1) // HEAD_DIM
    mult = mult_ref[...]
    bias = jnp.where(mult > 0, 0.0, NEG)
    pad = jnp.zeros((S_KEYS - S_NEW0 - 8, ATT_WIDTH), F32)

    def rows8(ref, b):
        x = jnp.zeros((8, ATT_WIDTH), F32)
        for i in range(t_new):
            x = jnp.where(rowi == i, ref[b, :, i * ATT_WIDTH:(i + 1) * ATT_WIDTH], x)
        return x

    def keys(recent, old, new, b):
        parts = [recent[b]] + [old[b, :, r * ATT_WIDTH:(r + 1) * ATT_WIDTH] for r in range(4)]
        return jnp.concatenate(parts + [rows8(new, b), pad], axis=0).astype(BF16)

    for b in range(nseq):
        q8 = rows8(q_ref, b)
        qrows = jnp.concatenate([jnp.where(lane_head == h, q8, 0.0) for h in range(N_ATT_HEADS)],
                                axis=0).astype(BF16)
        sc = _dot_nt(qrows, keys(kr_ref, ko_ref, kn_ref, b)) + bias
        m = jnp.max(sc, axis=-1, keepdims=True)
        p = jnp.exp(sc - m) * mult
        l = jnp.sum(p, axis=-1, keepdims=True)
        o = _dot(p.astype(BF16), keys(vr_ref, vo_ref, vn_ref, b)) / l
        o8 = jnp.zeros((8, ATT_WIDTH), F32)
        for h in range(N_ATT_HEADS):
            o8 = jnp.where(lane_head == h, o[h * 8:(h + 1) * 8], o8)
        for i in range(t_new):
            o_ref[b, :, i * ATT_WIDTH:(i + 1) * ATT_WIDTH] = o8[i:i + 1]


def _attn_sample(q, kn, vn, cache_k, cache_v, t_new, past_len, nseq=2):
    nb, wb = cache_k.shape[0], cache_k.shape[1]
    assert wb == WIN_MAX and t_new <= 4
    wide = t_new * ATT_WIDTH
    lanes3 = lambda x: x.reshape(nb, 1, wide)
    small = pl.BlockSpec((nseq, 1, wide), lambda b: (b, 0, 0))
    recent = pl.BlockSpec((nseq, S_RECENT, ATT_WIDTH), lambda b: (b, wb // S_RECENT - 1, 0))
    old = pl.BlockSpec((nseq, S_GROUPS, 4 * ATT_WIDTH), lambda b: (b, 0, 0))
    mult = jnp.asarray(_sample_mult_table(t_new, past_len))
    ck = cache_k.reshape(nb, wb, ATT_WIDTH)
    cv = cache_v.reshape(nb, wb, ATT_WIDTH)
    o = pl.pallas_call(
        functools.partial(_sample_attn_kernel, nseq=nseq, t_new=t_new), grid=(nb // nseq,),
        in_specs=[small, small, small, recent, old, recent, old, _full_spec(mult.shape)],
        out_specs=small, out_shape=jax.ShapeDtypeStruct((nb, 1, wide), F32),
        compiler_params=_params(1), name="attn_sample")(
            lanes3(q), lanes3(kn), lanes3(vn),
            ck, ck.reshape(nb, wb // 16, 16 * ATT_WIDTH), cv, cv.reshape(nb, wb // 16, 16 * ATT_WIDTH),
            mult)
    return o.reshape(nb * t_new, ATT_WIDTH)


def _sample_ret_kernel(q_ref, k_ref, v_ref, g_ref, gn_ref, st_ref, ex_ref, y_ref, so_ref, *, t_new):
    nb = q_ref.shape[0]
    hid = lax.broadcasted_iota(jnp.int32, (1, RET_V), 1) // RET_DV

    def headvec(fn):
        out = jnp.full((1, RET_V), fn(N_RET_HEADS - 1), F32)
        for h in range(N_RET_HEADS - 2, -1, -1):
            out = jnp.where(hid == h, fn(h), out)
        return out

    q = [q_ref[:, i * RET_QK:(i + 1) * RET_QK] for i in range(t_new)]
    k = [k_ref[:, i * RET_QK:(i + 1) * RET_QK] for i in range(t_new)]
    v = [v_ref[:, i * RET_V:(i + 1) * RET_V] for i in range(t_new)]
    vw = [v[jj] * headvec(lambda h: math.exp((t_new - 1.0 - jj) * LOG_G[h])) for jj in range(t_new)]
    ex = ex_ref[...]

    o = []
    for i in range(t_new):
        acc = None
        for jj in range(i + 1):
            sij = _dot((q[i] * k[jj]).astype(BF16), ex)
            term = sij * headvec(lambda h: math.exp((i - jj) * LOG_G[h])) * v[jj]
            acc = term if acc is None else acc + term
        o.append(acc)

    cross = [[None] * N_RET_HEADS for _ in range(t_new)]
    for h in range(N_RET_HEADS):
        hs = slice(h * RET_DV, (h + 1) * RET_DV)
        g_t = math.exp(t_new * LOG_G[h])
        for d in range(RET_DK):
            col = h * RET_DK + d
            cs = slice(col * RET_DV, (col + 1) * RET_DV)
            st = st_ref[:, cs]
            new = g_t * st
            for i in range(t_new):
                term = q[i][:, col:col + 1] * st
                cross[i][h] = term if cross[i][h] is None else cross[i][h] + term
                new = new + k[i][:, col:col + 1] * vw[i][:, hs]
            so_ref[:, cs] = new

    for i in range(t_new):
        for h in range(N_RET_HEADS):
            hs = slice(h * RET_DV, (h + 1) * RET_DV)
            oh = o[i][:, hs] + cross[i][h] * math.exp((i + 1.0) * LOG_G[h])
            mu = jnp.mean(oh, axis=-1, keepdims=True)
            xc = oh - mu
            var = jnp.mean(xc * xc, axis=-1, keepdims=True)
            y = xc * lax.rsqrt(var + NORM_EPS) * gn_ref[:, hs]
            gs = slice(i * RET_V + h * RET_DV, i * RET_V + (h + 1) * RET_DV)
            y_ref[:, gs] = _silu(g_ref[:, gs]) * y


def _ret_sample(qr, kr, vr, gr, gn, state, t_new, nb_blk=16):
    nb = state.shape[0]
    st_w = N_RET_HEADS * RET_DK * RET_DV
    ex = (np.arange(RET_QK)[:, None] // RET_DK == np.arange(RET_V)[None, :] // RET_DV)
    ex = jnp.asarray(ex.astype(np.float32), dtype=BF16)

    def blk(w_):
        return pl.BlockSpec((nb_blk, w_), lambda i: (i, 0))

    y, s = pl.pallas_call(
        functools.partial(_sample_ret_kernel, t_new=t_new), grid=(nb // nb_blk,),
        in_specs=[blk(t_new * RET_QK), blk(t_new * RET_QK), blk(t_new * RET_V), blk(t_new * RET_V),
                  _full_spec((1, RET_V)), blk(st_w), _full_spec(ex.shape)],
        out_specs=[blk(t_new * RET_V), blk(st_w)],
        out_shape=[jax.ShapeDtypeStruct((nb, t_new * RET_V), F32), jax.ShapeDtypeStruct((nb, st_w), F32)],
        compiler_params=_params(1), name="ret_sample")(
            qr.reshape(nb, -1), kr.reshape(nb, -1), vr.reshape(nb, -1), gr.reshape(nb, -1), gn,
            state.reshape(nb, st_w), ex)
    return y.reshape(nb * t_new, RET_V), s.reshape(state.shape)


def kernel(x_prompt, x_sample, cache_k, cache_v, state_ret, g_ffn1_pre, g_ffn1_post, w1_gate, w1_up,
           w1_down, g_mix_pre, g_mix_post, w_in, gn_w, w_out, g_ffn2_pre, g_ffn2_post, w2_gate, w2_up,
           w2_down):
    b_p, s_p, _ = x_prompt.shape
    b_s, t_s, _ = x_sample.shape
    depth = w_in.shape[0]
    assert depth == 1 and b_p == 1
    past_len = 8192
    keep = min(WIN_MAX, s_p)
    l = 0
    bf = lambda w: w[l].astype(BF16)
    vec = lambda g: g[l].reshape(1, -1)
    w1 = (bf(w1_gate), bf(w1_up), bf(w1_down))
    w2 = (bf(w2_gate), bf(w2_up), bf(w2_down))
    wi, wo = bf(w_in), bf(w_out)
    g1a, g1b, gma, gmb = vec(g_ffn1_pre), vec(g_ffn1_post), vec(g_mix_pre), vec(g_mix_post)
    g2a, g2b, gn = vec(g_ffn2_pre), vec(g_ffn2_post), vec(gn_w)

    def tables(pos):
        return (_rope_tables(pos, ROT_DIM, ROPE_THETA, HEAD_DIM),
                _rope_tables(pos, RET_DK, RET_THETA, RET_DK))

    tm = 512
    xp = x_prompt.reshape(s_p, D_MODEL)
    h1 = _ffn(xp, g1a, g1b, *w1, tm)
    ta, tr = tables(jnp.arange(s_p))
    qa, ka, va, kk, vk, qr, kr, vr, gr = _mixin(h1, gma, wi, ta, tr, keep, tm, sample=False)
    o_att = _attn_prompt(qa, ka, va)
    y_ret, st_p = _ret_prompt(qr, kr, vr, gr, gn, tm)
    y_prompt = _mixout_ffn(o_att, y_ret, h1, wo, gmb, g2a, g2b, *w2, tm)

    n_s = b_s * t_s
    xs = x_sample.reshape(n_s, D_MODEL)
    h1s = _ffn(xs, g1a, g1b, *w1, n_s)
    pos_s = past_len + jnp.arange(t_s)
    tas, trs = tables(jnp.tile(pos_s, b_s))
    qas, _, _, kks, vks, qrs, krs, vrs, grs = _mixin(h1s, gma, wi, tas, trs, n_s, n_s, sample=True)
    o_att_s = _attn_sample(qas, kks, vks, cache_k[l], cache_v[l], t_s, past_len)
    y_ret_s, st_s = _ret_sample(qrs, krs, vrs, grs, gn, state_ret[l], t_s)
    y_sample = _mixout_ffn(o_att_s, y_ret_s, h1s, wo, gmb, g2a, g2b, *w2, n_s)

    hd = (N_ATT_HEADS, HEAD_DIM)
    return (y_prompt.reshape(b_p, s_p, D_MODEL),
            y_sample.reshape(b_s, t_s, D_MODEL),
            kk.reshape(depth, b_p, keep, *hd),
            vk.reshape(depth, b_p, keep, *hd),
            st_p.reshape(depth, b_p, N_RET_HEADS, RET_DK, RET_DV),
            kks.reshape(depth, b_s, t_s, *hd),
            vks.reshape(depth, b_s, t_s, *hd),
            st_s.reshape(depth, b_s, N_RET_HEADS, RET_DK, RET_DV))
```

```python
import functools
import math

import numpy as np
import jax
import jax.numpy as jnp
from jax import lax
from jax.experimental import pallas as pl
from jax.experimental.pallas import tpu as pltpu

F32 = jnp.float32
BF16 = jnp.bfloat16

D_MODEL = 1024
D_FF = 2816
HEAD_DIM = 64
N_ATT_HEADS = 8
ATT_WIDTH = N_ATT_HEADS * HEAD_DIM
ROT_DIM = HEAD_DIM // 4
ROPE_THETA = 500000.0
WIN_MAX = 2048
PAST_LEN = 8192
BRANCHES = ((128, 1), (512, 4), (2048, 16))
N_RET_HEADS = 4
RET_DK = 64
RET_DV = 128
RET_QK = N_RET_HEADS * RET_DK
RET_V = N_RET_HEADS * RET_DV
RET_THETA = 10000.0
RET_CHUNK = 128
IN_WIDTH = 3 * ATT_WIDTH + 2 * RET_QK + 2 * RET_V
NORM_EPS = 1e-6
NEG = -1e30

LANES = 128
QB = 128
SUPER = 2048
RUN = 512
VMEM_LIMIT = 56 * 1024 * 1024
LOG_G = tuple(math.log1p(-2.0 ** (-5.0 - h)) for h in range(N_RET_HEADS))


def _full_spec(shape):
    nd = len(shape)
    return pl.BlockSpec(shape, lambda *_: (0,) * nd)


def _resident_spec(shape):
    nd = len(shape)
    return pl.BlockSpec(shape, lambda *_: (0,) * nd, pipeline_mode=pl.Buffered(1))


def _params(n_axes):
    return pltpu.CompilerParams(dimension_semantics=("arbitrary",) * n_axes,
                                vmem_limit_bytes=VMEM_LIMIT)


def _rms(x, g):
    return x * lax.rsqrt(jnp.mean(x * x, axis=-1, keepdims=True) + NORM_EPS) * g


def _silu(x):
    return x / (1.0 + jnp.exp(-x))


def _dot(a, b):
    return jnp.dot(a, b, preferred_element_type=F32)


def _dot_nt(a, b):
    return lax.dot_general(a, b, (((1,), (1,)), ((), ())), preferred_element_type=F32)


def _dot_tn(a, b):
    return lax.dot_general(a, b, (((0,), (0,)), ((), ())), preferred_element_type=F32)


FF_CHUNK = 512


def _swiglu(u, wg_ref, wu_ref, wd_ref):
    acc = None
    for c0 in range(0, D_FF, FF_CHUNK):
        c1 = min(c0 + FF_CHUNK, D_FF)
        g = _dot(u, wg_ref[:, c0:c1])
        up = _dot(u, wu_ref[:, c0:c1])
        h = (_silu(g) * up).astype(BF16)
        d = _dot(h, wd_ref[c0:c1, :])
        acc = d if acc is None else acc + d
    return acc


def _ffn_kernel(x_ref, gpre_ref, gpost_ref, wg_ref, wu_ref, wd_ref, o_ref):
    x = x_ref[...]
    u = _rms(x, gpre_ref[...]).astype(BF16)
    y = _swiglu(u, wg_ref, wu_ref, wd_ref)
    o_ref[...] = x + 0.5 * _rms(y, gpost_ref[...])


def _mixout_ffn_kernel(oa_ref, yr_ref, h_ref, wo_ref, gmb_ref, gpre_ref, gpost_ref,
                       wg_ref, wu_ref, wd_ref, o_ref):
    mixed = jnp.concatenate([oa_ref[...].astype(BF16), yr_ref[...].astype(BF16)], axis=1)
    x = h_ref[...] + _rms(_dot(mixed, wo_ref[...]), gmb_ref[...])
    u = _rms(x, gpre_ref[...]).astype(BF16)
    y = _swiglu(u, wg_ref, wu_ref, wd_ref)
    o_ref[...] = x + 0.5 * _rms(y, gpost_ref[...])


def _ffn_specs():
    return [_full_spec((1, D_MODEL)), _full_spec((1, D_MODEL)),
            _resident_spec((D_MODEL, D_FF)), _resident_spec((D_MODEL, D_FF)),
            _resident_spec((D_FF, D_MODEL))]


def _ffn(x, gpre, gpost, wg, wu, wd, tm):
    t = x.shape[0]
    row = pl.BlockSpec((tm, D_MODEL), lambda i: (i, 0))
    return pl.pallas_call(
        _ffn_kernel, grid=(t // tm,),
        in_specs=[row] + _ffn_specs(), out_specs=row,
        out_shape=jax.ShapeDtypeStruct((t, D_MODEL), F32),
        compiler_params=_params(1), name="ffn")(x, gpre, gpost, wg, wu, wd)


def _mixout_ffn(oa, yr, h, wo, gmb, gpre, gpost, wg, wu, wd, tm):
    t = h.shape[0]
    row = pl.BlockSpec((tm, D_MODEL), lambda i: (i, 0))
    half = pl.BlockSpec((tm, ATT_WIDTH), lambda i: (i, 0))
    return pl.pallas_call(
        _mixout_ffn_kernel, grid=(t // tm,),
        in_specs=[half, half, row, _resident_spec((D_MODEL, D_MODEL)), _full_spec((1, D_MODEL))]
        + _ffn_specs(),
        out_specs=row, out_shape=jax.ShapeDtypeStruct((t, D_MODEL), F32),
        compiler_params=_params(1), name="mixout_ffn")(oa, yr, h, wo, gmb, gpre, gpost, wg, wu, wd)


def _rope_tables(pos, rot_dim, theta, head_dim):
    half = rot_dim // 2
    inv = theta ** (-jnp.arange(half, dtype=F32) * (2.0 / rot_dim))
    lane = np.arange(LANES) % head_dim
    ang = pos.astype(F32)[:, None] * inv[lane % half][None, :]
    first = jnp.asarray(lane < half)[None, :]
    second = jnp.asarray((lane >= half) & (lane < rot_dim))[None, :]
    cos, sin = jnp.cos(ang), jnp.sin(ang)
    return jnp.stack([jnp.where(first | second, cos, 1.0), jnp.where(first, -sin, 0.0),
                      jnp.where(second, sin, 0.0)])


def _rope(x, tab_ref, half):
    return (x * tab_ref[0] + pltpu.roll(x, LANES - half, 1) * tab_ref[1]
            + pltpu.roll(x, half, 1) * tab_ref[2])


def _lanes(r, s):
    return slice(r * ATT_WIDTH + s * LANES, r * ATT_WIDTH + (s + 1) * LANES)


def _emit_dilated(nat_s, o4_ref, o16_ref):
    for s in range(ATT_WIDTH // LANES):
        for r in range(4):
            o4_ref[:, _lanes(r, s)] = nat_s[s, pl.ds(r, RUN // 4, stride=4), :].astype(BF16)
        for r in range(16):
            o16_ref[:, _lanes(r, s)] = nat_s[s, pl.ds(r, RUN // 16, stride=16), :].astype(BF16)


def _mixin_kernel(h_ref, g_ref, w_ref, ta_ref, tr_ref, *refs, first_keep, sample):
    if sample:
        qa_ref, kk_ref, vk_ref, qr_ref, kr_ref, vr_ref, gr_ref = refs
    else:
        (q4_ref, k4_ref, v4_ref, q16_ref, k16_ref, v16_ref, kk_ref, vk_ref,
         qr_ref, kr_ref, vr_ref, gr_ref, nat_s) = refs
    i = pl.program_id(0)
    u = _rms(h_ref[...], g_ref[...]).astype(BF16)
    keep = i >= first_keep
    scale = HEAD_DIM ** -0.5
    nslab = ATT_WIDTH // LANES

    q = _dot(u, w_ref[:, 0:ATT_WIDTH])
    for s in range(nslab):
        sl = slice(s * LANES, (s + 1) * LANES)
        qs = _rope(q[:, sl], ta_ref, ROT_DIM // 2) * scale
        if sample:
            qa_ref[:, sl] = qs
        else:
            nat_s[s] = qs
    if not sample:
        _emit_dilated(nat_s, q4_ref, q16_ref)

    k = _dot(u, w_ref[:, ATT_WIDTH:2 * ATT_WIDTH])
    for s in range(nslab):
        sl = slice(s * LANES, (s + 1) * LANES)
        ks = _rope(k[:, sl], ta_ref, ROT_DIM // 2)
        if not sample:
            nat_s[s] = ks

        @pl.when(keep)
        def _():
            kk_ref[:, sl] = ks
    if not sample:
        _emit_dilated(nat_s, k4_ref, k16_ref)

    v = _dot(u, w_ref[:, 2 * ATT_WIDTH:3 * ATT_WIDTH])
    if not sample:
        for s in range(nslab):
            nat_s[s] = v[:, s * LANES:(s + 1) * LANES]
        _emit_dilated(nat_s, v4_ref, v16_ref)

    @pl.when(keep)
    def _():
        vk_ref[...] = v

    o = 3 * ATT_WIDTH
    qk = _dot(u, w_ref[:, o:o + 2 * RET_QK])
    for s in range(RET_QK // LANES):
        sl = slice(s * LANES, (s + 1) * LANES)
        qr_ref[:, sl] = _rope(qk[:, sl], tr_ref, RET_DK // 2).astype(qr_ref.dtype)
        sk = slice(RET_QK + s * LANES, RET_QK + (s + 1) * LANES)
        kr_ref[:, sl] = (_rope(qk[:, sk], tr_ref, RET_DK // 2) * (RET_DK ** -0.5)).astype(kr_ref.dtype)
    o += 2 * RET_QK
    vr_ref[...] = _dot(u, w_ref[:, o:o + RET_V]).astype(vr_ref.dtype)
    o += RET_V
    gr_ref[...] = _dot(u, w_ref[:, o:o + RET_V])


def _mixin(h, g, w, tab_att, tab_ret, keep_rows, tm, sample):
    t = h.shape[0]
    nt = t // tm
    first_keep = nt - keep_rows // tm
    act = F32 if sample else BF16

    def row(w_):
        return pl.BlockSpec((tm, w_), lambda i: (i, 0))

    keep_spec = pl.BlockSpec((tm, ATT_WIDTH), lambda i: (jnp.maximum(i - first_keep, 0), 0))
    tab = pl.BlockSpec((3, tm, LANES), lambda i: (0, i, 0))
    tail_specs = [keep_spec, keep_spec, row(RET_QK), row(RET_QK), row(RET_V), row(RET_V)]
    tail_shapes = [((keep_rows, ATT_WIDTH), F32), ((keep_rows, ATT_WIDTH), F32),
                   ((t, RET_QK), act), ((t, RET_QK), act), ((t, RET_V), act), ((t, RET_V), F32)]
    if sample:
        specs = [row(ATT_WIDTH)] + tail_specs
        shapes = [((t, ATT_WIDTH), F32)] + tail_shapes
        scratch = []
    else:
        assert tm == RUN and t % SUPER == 0
        rps = SUPER // RUN
        s4 = pl.BlockSpec((None, RUN // 4, 4 * ATT_WIDTH), lambda i: (i, 0, 0))
        s16 = pl.BlockSpec((None, RUN // 16, 16 * ATT_WIDTH), lambda i: (i // rps, i % rps, 0))
        specs = [s4] * 3 + [s16] * 3 + tail_specs
        shapes = ([((t // RUN, RUN // 4, 4 * ATT_WIDTH), BF16)] * 3
                  + [((t // SUPER, SUPER // 16, 16 * ATT_WIDTH), BF16)] * 3 + tail_shapes)
        scratch = [pltpu.VMEM((ATT_WIDTH // LANES, RUN, LANES), F32)]
    return pl.pallas_call(
        functools.partial(_mixin_kernel, first_keep=first_keep, sample=sample), grid=(nt,),
        in_specs=[row(D_MODEL), _full_spec((1, D_MODEL)), _resident_spec((D_MODEL, IN_WIDTH)), tab, tab],
        out_specs=specs, out_shape=[jax.ShapeDtypeStruct(s, d) for s, d in shapes],
        scratch_shapes=scratch, compiler_params=_params(1), name="mix_in")(h, g, w, tab_att, tab_ret)


def _attn_bias_tables():
    a = np.arange(QB)[:, None]
    c = np.arange(2 * QB)[None, :]
    steps = BRANCHES[0][0]
    dist = QB + a - c
    band = (dist >= 0) & (dist <= steps)
    cur = c >= QB
    tok_q = 4 * (a % 32) + a // 32
    cc = c % QB
    tok_k = 4 * (cc % 32) + cc // 32 + QB * (c // QB) - QB
    dist2 = tok_q - tok_k
    band2 = (dist2 >= 0) & (dist2 <= steps)
    masks = np.stack([band, band & cur, band2, band2 & cur])
    return np.where(masks, 0.0, NEG).astype(np.float32)


def _attn_qblock(get_q, get_kc, get_kp, get_vc, get_vp, bias, consts):
    head_a, head_a_win, ones_a, ones_b = consts
    out = []
    for s in range(ATT_WIDTH // LANES):
        q2 = get_q(s)
        kwin = jnp.concatenate([get_kp(s), get_kc(s)], axis=0)
        vwin = jnp.concatenate([get_vp(s), get_vc(s)], axis=0)
        zq = jnp.zeros_like(q2)
        qq = jnp.concatenate([jnp.where(head_a, q2, zq), jnp.where(head_a, zq, q2)], axis=0)
        sc = _dot_nt(qq, kwin)
        s_a = sc[:QB] + bias
        s_b = sc[QB:] + bias
        m_a = jnp.max(s_a, axis=-1, keepdims=True)
        m_b = jnp.max(s_b, axis=-1, keepdims=True)
        p = jnp.concatenate([jnp.exp(s_a - m_a).astype(BF16), jnp.exp(s_b - m_b).astype(BF16)], axis=1)
        zv = jnp.zeros_like(vwin)
        w = jnp.concatenate(
            [jnp.concatenate([jnp.where(head_a_win, vwin, zv), ones_a], axis=1),
             jnp.concatenate([jnp.where(head_a_win, zv, vwin), ones_b], axis=1)], axis=0)
        r = _dot(p, w)
        out.append((jnp.where(head_a, m_a, m_b), r[:, LANES:], r[:, :LANES]))
    return out


def _merge(old, new):
    mo, lo, ao = old
    m2, l2, a2 = new
    mn = jnp.maximum(mo, m2)
    eo = jnp.exp(mo - mn)
    e2 = jnp.exp(m2 - mn)
    return mn, eo * lo + e2 * l2, eo * ao + e2 * a2


def _attn_kernel(q16, k16c, k16p, v16c, v16p, q4, k4c, k4p, v4c, v4p, bias_ref, o_ref,
                 m_s, l_s, a_s, nat_s):
    sb = pl.program_id(0)
    ph = pl.program_id(1)
    j = pl.program_id(2)
    lane = lax.broadcasted_iota(jnp.int32, (QB, LANES), 1)
    head_a = lane < HEAD_DIM
    lane_w = lax.broadcasted_iota(jnp.int32, (2 * QB, LANES), 1)
    head_a_win = lane_w < HEAD_DIM
    ones_a = jnp.where(head_a_win, 1.0, 0.0).astype(BF16)
    ones_b = jnp.where(head_a_win, 0.0, 1.0).astype(BF16)
    consts = (head_a, head_a_win, ones_a, ones_b)
    nslab = ATT_WIDTH // LANES
    sub = QB // 4
    lanes = _lanes

    @pl.when(ph == 0)
    def _():
        bias = bias_ref[(sb == 0).astype(jnp.int32)]
        for r in range(4):
            res = _attn_qblock(lambda s: q16[:, lanes(r, s)], lambda s: k16c[:, lanes(r, s)],
                               lambda s: k16p[:, lanes(r, s)], lambda s: v16c[:, lanes(r, s)],
                               lambda s: v16p[:, lanes(r, s)], bias, consts)
            for s in range(nslab):
                for u in range(SUPER // RUN):
                    rows = pl.ds(u * RUN + r * QB + j, sub, stride=4)
                    for ref, val in zip((m_s, l_s, a_s), res[s]):
                        ref[s, rows, :] = val[u * sub:(u + 1) * sub]

    @pl.when(ph == 1)
    def _():
        bias = bias_ref[(sb * (SUPER // RUN) + j == 0).astype(jnp.int32)]
        for r in range(4):
            res = _attn_qblock(lambda s: q4[:, lanes(r, s)], lambda s: k4c[:, lanes(r, s)],
                               lambda s: k4p[:, lanes(r, s)], lambda s: v4c[:, lanes(r, s)],
                               lambda s: v4p[:, lanes(r, s)], bias, consts)
            rows = pl.ds(pl.multiple_of(j * RUN + r * QB, QB), QB)
            for s in range(nslab):
                mn, ln, an = _merge((m_s[s, rows, :], l_s[s, rows, :], a_s[s, rows, :]), res[s])
                m_s[s, rows, :] = mn
                l_s[s, rows, :] = ln
                a_s[s, rows, :] = an

    @pl.when(ph == 2)
    def _():
        first = sb * (SUPER // RUN) + j == 0
        for b in range(4):
            def cur(ref):
                return lambda s: jnp.concatenate(
                    [ref[b * sub:(b + 1) * sub, lanes(r, s)] for r in range(4)], axis=0)

            def prev(ref_c, ref_p):
                if b > 0:
                    return lambda s: jnp.concatenate(
                        [ref_c[(b - 1) * sub:b * sub, lanes(r, s)] for r in range(4)], axis=0)
                return lambda s: jnp.concatenate(
                    [ref_p[QB - sub:QB, lanes(r, s)] for r in range(4)], axis=0)

            if b == 0:
                bias = bias_ref[2 + first.astype(jnp.int32)]
            else:
                bias = bias_ref[2]
            res = _attn_qblock(cur(q4), cur(k4c), prev(k4c, k4p), cur(v4c), prev(v4c, v4p),
                               bias, consts)
            for s in range(nslab):
                for r in range(4):
                    rows = pl.ds(pl.multiple_of(j * RUN + r * QB + b * sub, sub), sub)
                    part = tuple(x[r * sub:(r + 1) * sub] for x in res[s])
                    _, ln, an = _merge((m_s[s, rows, :], l_s[s, rows, :], a_s[s, rows, :]), part)
                    nat_s[s, pl.ds(b * QB + r, sub, stride=4), :] = an / ln
        for s in range(nslab):
            o_ref[:, s * LANES:(s + 1) * LANES] = nat_s[s].astype(o_ref.dtype)


def _attn_prompt(q4, k4, v4, q16, k16, v16):
    nsb = q16.shape[0]
    s_len = nsb * SUPER
    rps = SUPER // RUN
    blk = (None, QB, 4 * ATT_WIDTH)

    def j16(ph, j):
        return jnp.where(ph == 0, j, rps - 1)

    def n4(sb, ph, j):
        return sb * rps + jnp.where(ph == 0, 0, j)

    c16 = pl.BlockSpec(blk, lambda sb, ph, j: (sb, 0, j16(ph, j)))
    p16 = pl.BlockSpec(blk, lambda sb, ph, j: (jnp.maximum(sb - 1, 0), 0, j16(ph, j)))
    c4 = pl.BlockSpec(blk, lambda sb, ph, j: (n4(sb, ph, j), 0, 0))
    p4 = pl.BlockSpec(blk, lambda sb, ph, j: (jnp.maximum(n4(sb, ph, j) - 1, 0), 0, 0))
    out = pl.BlockSpec((RUN, ATT_WIDTH), lambda sb, ph, j: (sb * rps + jnp.where(ph == 2, j, 0), 0))
    bias = jnp.asarray(_attn_bias_tables())
    nslab = ATT_WIDTH // LANES
    scratch = ([pltpu.VMEM((nslab, SUPER, LANES), F32) for _ in range(3)]
               + [pltpu.VMEM((nslab, RUN, LANES), F32)])
    return pl.pallas_call(
        _attn_kernel, grid=(nsb, 3, rps),
        in_specs=[c16, c16, p16, c16, p16, c4, c4, p4, c4, p4, _full_spec(bias.shape)],
        out_specs=out, out_shape=jax.ShapeDtypeStruct((s_len, ATT_WIDTH), BF16),
        scratch_shapes=scratch, compiler_params=_params(3), name="attn_prompt")(
            q16, k16, k16, v16, v16, q4, k4, k4, v4, v4, bias)


def _ret_kernel(qr_ref, kr_ref, vr_ref, gr_ref, gn_ref, y_ref, s_ref, dec_s, wt_s, wh_s, st_s,
                *, chunks):
    step = pl.program_id(0)
    c = RET_CHUNK

    @pl.when(step == 0)
    def _():
        i = lax.broadcasted_iota(jnp.int32, (c, c), 0)
        jj = lax.broadcasted_iota(jnp.int32, (c, c), 1)
        diff = (i - jj).astype(F32)
        for h in range(N_RET_HEADS):
            dec_s[h] = jnp.where(diff >= 0, jnp.exp(diff * LOG_G[h]), 0.0)
            wt_s[h] = jnp.exp((c - 1.0 - i.astype(F32)) * LOG_G[h])
            wh_s[h] = jnp.exp((i.astype(F32) + 1.0) * LOG_G[h])
        st_s[...] = jnp.zeros_like(st_s)

    lane = lax.broadcasted_iota(jnp.int32, (c, LANES), 1)
    head_a = lane < RET_DK
    row_a = lax.broadcasted_iota(jnp.int32, (LANES, RET_DV), 0) < RET_DK
    for ci in range(chunks):
        rows = slice(ci * c, (ci + 1) * c)
        for p in range(N_RET_HEADS // 2):
            sl = slice(p * LANES, (p + 1) * LANES)
            q2 = qr_ref[rows, sl]
            k2 = kr_ref[rows, sl]
            zq = jnp.zeros_like(q2)
            qq = jnp.concatenate([jnp.where(head_a, q2, zq), jnp.where(head_a, zq, q2)], axis=0)
            inner = _dot_nt(qq, k2)
            state = st_s[p]
            cross = _dot(qq, state.astype(BF16))
            vws = []
            for hh in range(2):
                h = 2 * p + hh
                hs = slice(h * RET_DV, (h + 1) * RET_DV)
                v = vr_ref[rows, hs]
                inn = (inner[hh * c:(hh + 1) * c] * dec_s[h]).astype(BF16)
                o = _dot(inn, v) + cross[hh * c:(hh + 1) * c] * wh_s[h]
                mu = jnp.mean(o, axis=-1, keepdims=True)
                xc = o - mu
                var = jnp.mean(xc * xc, axis=-1, keepdims=True)
                y = xc * lax.rsqrt(var + NORM_EPS) * gn_ref[:, hs]
                y_ref[rows, hs] = (_silu(gr_ref[rows, hs]) * y).astype(y_ref.dtype)
                vws.append((v.astype(F32) * wt_s[h]).astype(BF16))
            upd = _dot_tn(k2, jnp.concatenate(vws, axis=1))
            upd = jnp.where(row_a, upd[:, :RET_DV], upd[:, RET_DV:])
            gch = jnp.where(row_a, math.exp(c * LOG_G[2 * p]), math.exp(c * LOG_G[2 * p + 1]))
            st_s[p] = gch * state + upd

    @pl.when(step == pl.num_programs(0) - 1)
    def _():
        s_ref[...] = st_s[...]


def _ret_prompt(qr, kr, vr, gr, gn, tm):
    t = qr.shape[0]

    def row(w_):
        return pl.BlockSpec((tm, w_), lambda i: (i, 0))

    st_shape = (N_RET_HEADS // 2, 2 * RET_DK, RET_DV)
    tab = pltpu.VMEM((N_RET_HEADS, RET_CHUNK, RET_CHUNK), F32)
    y, s = pl.pallas_call(
        functools.partial(_ret_kernel, chunks=tm // RET_CHUNK), grid=(t // tm,),
        in_specs=[row(RET_QK), row(RET_QK), row(RET_V), row(RET_V), _full_spec((1, RET_V))],
        out_specs=[row(RET_V), _full_spec(st_shape)],
        out_shape=[jax.ShapeDtypeStruct((t, RET_V), BF16), jax.ShapeDtypeStruct(st_shape, F32)],
        scratch_shapes=[tab, tab, tab, pltpu.VMEM(st_shape, F32)],
        compiler_params=_params(1), name="ret_prompt")(qr, kr, vr, gr, gn)
    return y, s


S_ROWS = 8


def _sample_tables(t_new, past_len, wb):
    rows = np.concatenate([np.arange(wb), wb + np.arange(S_ROWS)])
    tab = np.zeros((S_ROWS, wb + S_ROWS), np.float32)
    for r in range(S_ROWS):
        i = r % t_new
        delta = wb + i - rows
        ok = (delta >= 0) & (past_len + i - delta >= 0) & (rows < wb + t_new)
        for window, dil in BRANCHES:
            tab[r] += ok & (delta % dil == 0) & (delta <= window)
    return tab[:, :wb], tab[:, wb:]


def _sample_attn_kernel(q_ref, kn_ref, vn_ref, kt_ref, vt_ref, cc_ref, cn_ref, o_ref):
    mult_c, mult_n = cc_ref[...], cn_ref[...]
    bias_c = jnp.where(mult_c > 0, 0.0, NEG)
    bias_n = jnp.where(mult_n > 0, 0.0, NEG)
    for h in range(N_ATT_HEADS):
        q = q_ref[h].astype(BF16)
        s_c = _dot(q, kt_ref[h].astype(BF16)) + bias_c
        s_n = _dot_nt(q, kn_ref[h].astype(BF16)) + bias_n
        m = jnp.maximum(jnp.max(s_c, axis=-1, keepdims=True), jnp.max(s_n, axis=-1, keepdims=True))
        p_c = jnp.exp(s_c - m) * mult_c
        p_n = jnp.exp(s_n - m) * mult_n
        l = jnp.sum(p_c, axis=-1, keepdims=True) + jnp.sum(p_n, axis=-1, keepdims=True)
        o = _dot_nt(p_c.astype(BF16), vt_ref[h].astype(BF16)) + _dot(p_n.astype(BF16),
                                                                    vn_ref[h].astype(BF16))
        o_ref[h] = o / l


def _attn_sample(q, kn, vn, cache_k, cache_v, t_new, past_len):
    nb, wb = cache_k.shape[0], cache_k.shape[1]
    assert t_new <= S_ROWS

    def heads(x):
        x = x.reshape(nb, t_new, N_ATT_HEADS, HEAD_DIM).transpose(0, 2, 1, 3)
        return jnp.pad(x, ((0, 0), (0, 0), (0, S_ROWS - t_new), (0, 0)))

    small = pl.BlockSpec((None, N_ATT_HEADS, S_ROWS, HEAD_DIM), lambda b: (b, 0, 0, 0))
    big = pl.BlockSpec((None, N_ATT_HEADS, HEAD_DIM, wb), lambda b: (b, 0, 0, 0))
    tabs = [jnp.asarray(t) for t in _sample_tables(t_new, past_len, wb)]
    o = pl.pallas_call(
        _sample_attn_kernel, grid=(nb,),
        in_specs=[small, small, small, big, big] + [_full_spec(t.shape) for t in tabs],
        out_specs=small, out_shape=jax.ShapeDtypeStruct((nb, N_ATT_HEADS, S_ROWS, HEAD_DIM), F32),
        compiler_params=_params(1), name="attn_sample")(
            heads(q), heads(kn), heads(vn), cache_k.transpose(0, 2, 3, 1), cache_v.transpose(0, 2, 3, 1),
            *tabs)
    return o[:, :, :t_new].transpose(0, 2, 1, 3).reshape(nb * t_new, ATT_WIDTH)


def _sample_ret_kernel(q_ref, k_ref, v_ref, g_ref, gn_ref, st_ref, ex_ref, y_ref, so_ref, *, t_new):
    hid = lax.broadcasted_iota(jnp.int32, (1, RET_V), 1) // RET_DV

    def headvec(fn):
        out = jnp.full((1, RET_V), fn(N_RET_HEADS - 1), F32)
        for h in range(N_RET_HEADS - 2, -1, -1):
            out = jnp.where(hid == h, fn(h), out)
        return out

    q = [q_ref[:, i * RET_QK:(i + 1) * RET_QK] for i in range(t_new)]
    k = [k_ref[:, i * RET_QK:(i + 1) * RET_QK] for i in range(t_new)]
    v = [v_ref[:, i * RET_V:(i + 1) * RET_V] for i in range(t_new)]
    vw = [v[jj] * headvec(lambda h: math.exp((t_new - 1.0 - jj) * LOG_G[h])) for jj in range(t_new)]
    ex = ex_ref[...]

    o = []
    for i in range(t_new):
        acc = None
        for jj in range(i + 1):
            sij = _dot((q[i] * k[jj]).astype(BF16), ex)
            term = sij * headvec(lambda h: math.exp((i - jj) * LOG_G[h])) * v[jj]
            acc = term if acc is None else acc + term
        o.append(acc)

    cross = [[None] * N_RET_HEADS for _ in range(t_new)]
    for h in range(N_RET_HEADS):
        hs = slice(h * RET_DV, (h + 1) * RET_DV)
        g_t = math.exp(t_new * LOG_G[h])
        for d in range(RET_DK):
            col = h * RET_DK + d
            cs = slice(col * RET_DV, (col + 1) * RET_DV)
            st = st_ref[:, cs]
            new = g_t * st
            for i in range(t_new):
                term = q[i][:, col:col + 1] * st
                cross[i][h] = term if cross[i][h] is None else cross[i][h] + term
                new = new + k[i][:, col:col + 1] * vw[i][:, hs]
            so_ref[:, cs] = new

    for i in range(t_new):
        for h in range(N_RET_HEADS):
            hs = slice(h * RET_DV, (h + 1) * RET_DV)
            oh = o[i][:, hs] + cross[i][h] * math.exp((i + 1.0) * LOG_G[h])
            mu = jnp.mean(oh, axis=-1, keepdims=True)
            xc = oh - mu
            var = jnp.mean(xc * xc, axis=-1, keepdims=True)
            y = xc * lax.rsqrt(var + NORM_EPS) * gn_ref[:, hs]
            gs = slice(i * RET_V + h * RET_DV, i * RET_V + (h + 1) * RET_DV)
            y_ref[:, gs] = _silu(g_ref[:, gs]) * y


def _ret_sample(qr, kr, vr, gr, gn, state, t_new, nb_blk=16):
    nb = state.shape[0]
    st_w = N_RET_HEADS * RET_DK * RET_DV
    ex = (np.arange(RET_QK)[:, None] // RET_DK == np.arange(RET_V)[None, :] // RET_DV)
    ex = jnp.asarray(ex.astype(np.float32), dtype=BF16)

    def blk(w_):
        return pl.BlockSpec((nb_blk, w_), lambda i: (i, 0))

    y, s = pl.pallas_call(
        functools.partial(_sample_ret_kernel, t_new=t_new), grid=(nb // nb_blk,),
        in_specs=[blk(t_new * RET_QK), blk(t_new * RET_QK), blk(t_new * RET_V), blk(t_new * RET_V),
                  _full_spec((1, RET_V)), blk(st_w), _full_spec(ex.shape)],
        out_specs=[blk(t_new * RET_V), blk(st_w)],
        out_shape=[jax.ShapeDtypeStruct((nb, t_new * RET_V), F32), jax.ShapeDtypeStruct((nb, st_w), F32)],
        compiler_params=_params(1), name="ret_sample")(
            qr.reshape(nb, -1), kr.reshape(nb, -1), vr.reshape(nb, -1), gr.reshape(nb, -1), gn,
            state.reshape(nb, st_w), ex)
    return y.reshape(nb * t_new, RET_V), s.reshape(state.shape)


def kernel(x_prompt, x_sample, cache_k, cache_v, state_ret, g_ffn1_pre, g_ffn1_post, w1_gate, w1_up,
           w1_down, g_mix_pre, g_mix_post, w_in, gn_w, w_out, g_ffn2_pre, g_ffn2_post, w2_gate, w2_up,
           w2_down):
    b_p, s_p, _ = x_prompt.shape
    b_s, t_s, _ = x_sample.shape
    depth = w_in.shape[0]
    assert depth == 1 and b_p == 1
    keep = min(WIN_MAX, s_p)
    l = 0
    bf = lambda w: w[l].astype(BF16)
    vec = lambda g: g[l].reshape(1, -1)
    w1 = (bf(w1_gate), bf(w1_up), bf(w1_down))
    w2 = (bf(w2_gate), bf(w2_up), bf(w2_down))
    wi, wo = bf(w_in), bf(w_out)
    g1a, g1b, gma, gmb = vec(g_ffn1_pre), vec(g_ffn1_post), vec(g_mix_pre), vec(g_mix_post)
    g2a, g2b, gn = vec(g_ffn2_pre), vec(g_ffn2_post), vec(gn_w)

    def tables(pos):
        return (_rope_tables(pos, ROT_DIM, ROPE_THETA, HEAD_DIM),
                _rope_tables(pos, RET_DK, RET_THETA, RET_DK))

    tm = RUN
    xp = x_prompt.reshape(s_p, D_MODEL)
    h1 = _ffn(xp, g1a, g1b, *w1, tm)
    ta, tr = tables(jnp.arange(s_p))
    q4, k4, v4, q16, k16, v16, kk, vk, qr, kr, vr, gr = _mixin(h1, gma, wi, ta, tr, keep, tm,
                                                                sample=False)
    o_att = _attn_prompt(q4, k4, v4, q16, k16, v16)
    y_ret, st_p = _ret_prompt(qr, kr, vr, gr, gn, tm)
    y_prompt = _mixout_ffn(o_att, y_ret, h1, wo, gmb, g2a, g2b, *w2, tm)

    n_s = b_s * t_s
    xs = x_sample.reshape(n_s, D_MODEL)
    h1s = _ffn(xs, g1a, g1b, *w1, n_s)
    pos_s = PAST_LEN + jnp.arange(t_s)
    tas, trs = tables(jnp.tile(pos_s, b_s))
    qas, kks, vks, qrs, krs, vrs, grs = _mixin(h1s, gma, wi, tas, trs, n_s, n_s, sample=True)
    o_att_s = _attn_sample(qas, kks, vks, cache_k[l], cache_v[l], t_s, PAST_LEN)
    y_ret_s, st_s = _ret_sample(qrs, krs, vrs, grs, gn, state_ret[l], t_s)
    y_sample = _mixout_ffn(o_att_s, y_ret_s, h1s, wo, gmb, g2a, g2b, *w2, n_s)

    hd = (N_ATT_HEADS, HEAD_DIM)
    return (y_prompt.reshape(b_p, s_p, D_MODEL),
            y_sample.reshape(b_s, t_s, D_MODEL),
            kk.reshape(depth, b_p, keep, *hd),
            vk.reshape(depth, b_p, keep, *hd),
            st_p.reshape(depth, b_p, N_RET_HEADS, RET_DK, RET_DV),
            kks.reshape(depth, b_s, t_s, *hd),
            vks.reshape(depth, b_s, t_s, *hd),
            st_s.reshape(depth, b_s, N_RET_HEADS, RET_DK, RET_DV))
```

```python
import functools
import math

import numpy as np
import jax
import jax.numpy as jnp
from jax import lax
from jax.experimental import pallas as pl
from jax.experimental.pallas import tpu as pltpu

F32 = jnp.float32
BF16 = jnp.bfloat16

D_MODEL = 1024
D_FF = 2816
HEAD_DIM = 64
N_ATT_HEADS = 8
ATT_WIDTH = N_ATT_HEADS * HEAD_DIM
ROT_DIM = HEAD_DIM // 4
ROPE_THETA = 500000.0
WIN_MAX = 2048
PAST_LEN = 8192
BRANCHES = ((128, 1), (512, 4), (2048, 16))
N_RET_HEADS = 4
RET_DK = 64
RET_DV = 128
RET_QK = N_RET_HEADS * RET_DK
RET_V = N_RET_HEADS * RET_DV
RET_THETA = 10000.0
RET_CHUNK = 128
IN_WIDTH = 3 * ATT_WIDTH + 2 * RET_QK + 2 * RET_V
NORM_EPS = 1e-6
NEG = -1e30

LANES = 128
QB = 128
SUPER = 2048
RUN = 512
VMEM_LIMIT = 56 * 1024 * 1024
LOG_G = tuple(math.log1p(-2.0 ** (-5.0 - h)) for h in range(N_RET_HEADS))


def _full_spec(shape):
    nd = len(shape)
    return pl.BlockSpec(shape, lambda *_: (0,) * nd)


def _resident_spec(shape):
    nd = len(shape)
    return pl.BlockSpec(shape, lambda *_: (0,) * nd, pipeline_mode=pl.Buffered(1))


def _params(n_axes):
    return pltpu.CompilerParams(dimension_semantics=("arbitrary",) * n_axes,
                                vmem_limit_bytes=VMEM_LIMIT)


def _rms(x, g):
    return x * lax.rsqrt(jnp.mean(x * x, axis=-1, keepdims=True) + NORM_EPS) * g


def _silu(x):
    return x / (1.0 + jnp.exp(-x))


def _dot(a, b):
    return jnp.dot(a, b, preferred_element_type=F32)


def _dot_nt(a, b):
    return lax.dot_general(a, b, (((1,), (1,)), ((), ())), preferred_element_type=F32)


def _dot_tn(a, b):
    return lax.dot_general(a, b, (((0,), (0,)), ((), ())), preferred_element_type=F32)


FF_CHUNK = 512


def _swiglu(u, wg_ref, wu_ref, wd_ref):
    acc = None
    for c0 in range(0, D_FF, FF_CHUNK):
        c1 = min(c0 + FF_CHUNK, D_FF)
        g = _dot(u, wg_ref[:, c0:c1])
        up = _dot(u, wu_ref[:, c0:c1])
        h = (_silu(g) * up).astype(BF16)
        d = _dot(h, wd_ref[c0:c1, :])
        acc = d if acc is None else acc + d
    return acc


def _ffn_kernel(x_ref, gpre_ref, gpost_ref, wg_ref, wu_ref, wd_ref, o_ref):
    x = x_ref[...]
    u = _rms(x, gpre_ref[...]).astype(BF16)
    y = _swiglu(u, wg_ref, wu_ref, wd_ref)
    o_ref[...] = x + 0.5 * _rms(y, gpost_ref[...])


def _mixout_ffn_kernel(oa_ref, yr_ref, h_ref, wo_ref, gmb_ref, gpre_ref, gpost_ref,
                       wg_ref, wu_ref, wd_ref, o_ref):
    mixed = jnp.concatenate([oa_ref[...].astype(BF16), yr_ref[...].astype(BF16)], axis=1)
    x = h_ref[...] + _rms(_dot(mixed, wo_ref[...]), gmb_ref[...])
    u = _rms(x, gpre_ref[...]).astype(BF16)
    y = _swiglu(u, wg_ref, wu_ref, wd_ref)
    o_ref[...] = x + 0.5 * _rms(y, gpost_ref[...])


def _ffn_specs():
    return [_full_spec((1, D_MODEL)), _full_spec((1, D_MODEL)),
            _resident_spec((D_MODEL, D_FF)), _resident_spec((D_MODEL, D_FF)),
            _resident_spec((D_FF, D_MODEL))]


def _ffn(x, gpre, gpost, wg, wu, wd, tm):
    t = x.shape[0]
    row = pl.BlockSpec((tm, D_MODEL), lambda i: (i, 0))
    return pl.pallas_call(
        _ffn_kernel, grid=(t // tm,),
        in_specs=[row] + _ffn_specs(), out_specs=row,
        out_shape=jax.ShapeDtypeStruct((t, D_MODEL), F32),
        compiler_params=_params(1), name="ffn")(x, gpre, gpost, wg, wu, wd)


def _mixout_ffn(oa, yr, h, wo, gmb, gpre, gpost, wg, wu, wd, tm):
    t = h.shape[0]
    row = pl.BlockSpec((tm, D_MODEL), lambda i: (i, 0))
    half = pl.BlockSpec((tm, ATT_WIDTH), lambda i: (i, 0))
    return pl.pallas_call(
        _mixout_ffn_kernel, grid=(t // tm,),
        in_specs=[half, half, row, _resident_spec((D_MODEL, D_MODEL)), _full_spec((1, D_MODEL))]
        + _ffn_specs(),
        out_specs=row, out_shape=jax.ShapeDtypeStruct((t, D_MODEL), F32),
        compiler_params=_params(1), name="mixout_ffn")(oa, yr, h, wo, gmb, gpre, gpost, wg, wu, wd)


def _rope_consts(base_pos, off_pos, rot_dim, theta, head_dim):
    half = rot_dim // 2
    inv = theta ** (-jnp.arange(half, dtype=F32) * (2.0 / rot_dim))
    lane = np.arange(LANES) % head_dim
    inv_l = inv[lane % half][None, :]
    first = jnp.asarray(lane < half, F32)[None, :]
    second = jnp.asarray((lane >= half) & (lane < rot_dim), F32)[None, :]
    rot = first + second
    a = base_pos.astype(F32)[:, None] * inv_l
    b = off_pos.astype(F32)[:, None] * inv_l
    cb, sb = jnp.cos(b), jnp.sin(b)
    base = jnp.stack([jnp.cos(a), jnp.sin(a)], axis=1)
    off = jnp.stack([cb * rot, sb * rot, jnp.broadcast_to(1.0 - rot, cb.shape),
                     -cb * first, -sb * first, cb * second, sb * second])
    return base, off


def _rope_fill(base_ref, off_ref, tab_s):
    ca = base_ref[0:1, :]
    sa = base_ref[1:2, :]
    tab_s[0] = ca * off_ref[0] - sa * off_ref[1] + off_ref[2]
    tab_s[1] = sa * off_ref[3] + ca * off_ref[4]
    tab_s[2] = sa * off_ref[5] + ca * off_ref[6]


def _rope(x, tab_ref, half):
    return (x * tab_ref[0] + pltpu.roll(x, LANES - half, 1) * tab_ref[1]
            + pltpu.roll(x, half, 1) * tab_ref[2])


def _lanes(r, s):
    return slice(r * ATT_WIDTH + s * LANES, r * ATT_WIDTH + (s + 1) * LANES)


def _emit_dilated(nat_s, o4_ref, o16_ref):
    for s in range(ATT_WIDTH // LANES):
        for r in range(4):
            o4_ref[:, _lanes(r, s)] = nat_s[s, pl.ds(r, RUN // 4, stride=4), :].astype(BF16)
        for r in range(16):
            o16_ref[:, _lanes(r, s)] = nat_s[s, pl.ds(r, RUN // 16, stride=16), :].astype(BF16)


def _mixin_kernel(h_ref, g_ref, w_ref, ba_ref, oa_ref, br_ref, or_ref, *refs, sample):
    if sample:
        qa_ref, kk_ref, vk_ref, qr_ref, kr_ref, vr_ref, gr_ref, ta_ref, tr_ref = refs
    else:
        (q4_ref, k4_ref, v4_ref, q16_ref, k16_ref, v16_ref, kk_ref, vk_ref,
         qr_ref, kr_ref, vr_ref, gr_ref, ta_ref, tr_ref, nat_s) = refs
    _rope_fill(ba_ref, oa_ref, ta_ref)
    _rope_fill(br_ref, or_ref, tr_ref)
    u = _rms(h_ref[...], g_ref[...]).astype(BF16)
    scale = HEAD_DIM ** -0.5
    nslab = ATT_WIDTH // LANES

    q = _dot(u, w_ref[:, 0:ATT_WIDTH])
    for s in range(nslab):
        sl = slice(s * LANES, (s + 1) * LANES)
        qs = _rope(q[:, sl], ta_ref, ROT_DIM // 2) * scale
        if sample:
            qa_ref[:, sl] = qs
        else:
            nat_s[s] = qs
    if not sample:
        _emit_dilated(nat_s, q4_ref, q16_ref)

    k = _dot(u, w_ref[:, ATT_WIDTH:2 * ATT_WIDTH])
    for s in range(nslab):
        sl = slice(s * LANES, (s + 1) * LANES)
        ks = _rope(k[:, sl], ta_ref, ROT_DIM // 2)
        if not sample:
            nat_s[s] = ks
        kk_ref[:, sl] = ks
    if not sample:
        _emit_dilated(nat_s, k4_ref, k16_ref)

    v = _dot(u, w_ref[:, 2 * ATT_WIDTH:3 * ATT_WIDTH])
    if not sample:
        for s in range(nslab):
            nat_s[s] = v[:, s * LANES:(s + 1) * LANES]
        _emit_dilated(nat_s, v4_ref, v16_ref)
    vk_ref[...] = v

    o = 3 * ATT_WIDTH
    qk = _dot(u, w_ref[:, o:o + 2 * RET_QK])
    for s in range(RET_QK // LANES):
        sl = slice(s * LANES, (s + 1) * LANES)
        qr_ref[:, sl] = _rope(qk[:, sl], tr_ref, RET_DK // 2).astype(qr_ref.dtype)
        sk = slice(RET_QK + s * LANES, RET_QK + (s + 1) * LANES)
        kr_ref[:, sl] = (_rope(qk[:, sk], tr_ref, RET_DK // 2) * (RET_DK ** -0.5)).astype(kr_ref.dtype)
    o += 2 * RET_QK
    vr_ref[...] = _dot(u, w_ref[:, o:o + RET_V]).astype(vr_ref.dtype)
    o += RET_V
    gr_ref[...] = _dot(u, w_ref[:, o:o + RET_V])


def _mixin(h, g, w, base_pos, off_pos, keep_rows, tm, sample):
    t = h.shape[0]
    nt = t // tm
    first_keep = nt - keep_rows // tm
    act = F32 if sample else BF16
    base_a, off_a = _rope_consts(base_pos, off_pos, ROT_DIM, ROPE_THETA, HEAD_DIM)
    base_r, off_r = _rope_consts(base_pos, off_pos, RET_DK, RET_THETA, RET_DK)

    def row(w_):
        return pl.BlockSpec((tm, w_), lambda i: (i, 0))

    keep_spec = pl.BlockSpec((tm, ATT_WIDTH), lambda i: (jnp.maximum(i - first_keep, 0), 0))
    base_spec = pl.BlockSpec((None, 2, LANES), lambda i: (i, 0, 0))
    off_spec = _resident_spec((7, tm, LANES))
    tab = pltpu.VMEM((3, tm, LANES), F32)
    tail_specs = [keep_spec, keep_spec, row(RET_QK), row(RET_QK), row(RET_V), row(RET_V)]
    tail_shapes = [((keep_rows, ATT_WIDTH), F32), ((keep_rows, ATT_WIDTH), F32),
                   ((t, RET_QK), act), ((t, RET_QK), act), ((t, RET_V), act), ((t, RET_V), F32)]
    if sample:
        specs = [row(ATT_WIDTH)] + tail_specs
        shapes = [((t, ATT_WIDTH), F32)] + tail_shapes
        scratch = [tab, tab]
    else:
        assert tm == RUN and t % SUPER == 0
        rps = SUPER // RUN
        s4 = pl.BlockSpec((None, RUN // 4, 4 * ATT_WIDTH), lambda i: (i, 0, 0))
        s16 = pl.BlockSpec((None, RUN // 16, 16 * ATT_WIDTH), lambda i: (i // rps, i % rps, 0))
        specs = [s4] * 3 + [s16] * 3 + tail_specs
        shapes = ([((t // RUN, RUN // 4, 4 * ATT_WIDTH), BF16)] * 3
                  + [((t // SUPER, SUPER // 16, 16 * ATT_WIDTH), BF16)] * 3 + tail_shapes)
        scratch = [tab, tab, pltpu.VMEM((ATT_WIDTH // LANES, RUN, LANES), F32)]
    return pl.pallas_call(
        functools.partial(_mixin_kernel, sample=sample), grid=(nt,),
        in_specs=[row(D_MODEL), _full_spec((1, D_MODEL)), _resident_spec((D_MODEL, IN_WIDTH)),
                  base_spec, off_spec, base_spec, off_spec],
        out_specs=specs, out_shape=[jax.ShapeDtypeStruct(s, d) for s, d in shapes],
        scratch_shapes=scratch, compiler_params=_params(1), name="mix_in")(
            h, g, w, base_a, off_a, base_r, off_r)


def _attn_bias_tables():
    a = np.arange(QB)[:, None]
    c = np.arange(2 * QB)[None, :]
    steps = BRANCHES[0][0]
    dist = QB + a - c
    band = (dist >= 0) & (dist <= steps)
    cur = c >= QB
    tok_q = 4 * (a % 32) + a // 32
    cc = c % QB
    tok_k = 4 * (cc % 32) + cc // 32 + QB * (c // QB) - QB
    dist2 = tok_q - tok_k
    band2 = (dist2 >= 0) & (dist2 <= steps)
    masks = np.stack([band, band & cur, band2, band2 & cur])
    return np.where(masks, 0.0, NEG).astype(np.float32)


def _attn_qblock(get_q, get_kc, get_kp, get_vc, get_vp, bias, consts):
    head_a, head_a_win, ones_a, ones_b = consts
    out = []
    for s in range(ATT_WIDTH // LANES):
        q2 = get_q(s)
        kwin = jnp.concatenate([get_kp(s), get_kc(s)], axis=0)
        vwin = jnp.concatenate([get_vp(s), get_vc(s)], axis=0)
        zq = jnp.zeros_like(q2)
        qq = jnp.concatenate([jnp.where(head_a, q2, zq), jnp.where(head_a, zq, q2)], axis=0)
        sc = _dot_nt(qq, kwin)
        s_a = sc[:QB] + bias
        s_b = sc[QB:] + bias
        m_a = jnp.max(s_a, axis=-1, keepdims=True)
        m_b = jnp.max(s_b, axis=-1, keepdims=True)
        p = jnp.concatenate([jnp.exp(s_a - m_a).astype(BF16), jnp.exp(s_b - m_b).astype(BF16)], axis=1)
        zv = jnp.zeros_like(vwin)
        w = jnp.concatenate(
            [jnp.concatenate([jnp.where(head_a_win, vwin, zv), ones_a], axis=1),
             jnp.concatenate([jnp.where(head_a_win, zv, vwin), ones_b], axis=1)], axis=0)
        r = _dot(p, w)
        out.append((jnp.where(head_a, m_a, m_b), r[:, LANES:], r[:, :LANES]))
    return out


def _merge(old, new):
    mo, lo, ao = old
    m2, l2, a2 = new
    mn = jnp.maximum(mo, m2)
    eo = jnp.exp(mo - mn)
    e2 = jnp.exp(m2 - mn)
    return mn, eo * lo + e2 * l2, eo * ao + e2 * a2


def _attn_kernel(q16, k16c, k16p, v16c, v16p, q4, k4c, k4p, v4c, v4p, bias_ref, o_ref,
                 m_s, l_s, a_s, nat_s):
    sb = pl.program_id(0)
    ph = pl.program_id(1)
    j = pl.program_id(2)
    lane = lax.broadcasted_iota(jnp.int32, (QB, LANES), 1)
    head_a = lane < HEAD_DIM
    lane_w = lax.broadcasted_iota(jnp.int32, (2 * QB, LANES), 1)
    head_a_win = lane_w < HEAD_DIM
    ones_a = jnp.where(head_a_win, 1.0, 0.0).astype(BF16)
    ones_b = jnp.where(head_a_win, 0.0, 1.0).astype(BF16)
    consts = (head_a, head_a_win, ones_a, ones_b)
    nslab = ATT_WIDTH // LANES
    sub = QB // 4
    lanes = _lanes

    @pl.when(ph == 0)
    def _():
        bias = bias_ref[(sb == 0).astype(jnp.int32)]
        for r in range(4):
            res = _attn_qblock(lambda s: q16[:, lanes(r, s)], lambda s: k16c[:, lanes(r, s)],
                               lambda s: k16p[:, lanes(r, s)], lambda s: v16c[:, lanes(r, s)],
                               lambda s: v16p[:, lanes(r, s)], bias, consts)
            for s in range(nslab):
                for u in range(SUPER // RUN):
                    rows = pl.ds(u * RUN + r * QB + j, sub, stride=4)
                    for ref, val in zip((m_s, l_s, a_s), res[s]):
                        ref[s, rows, :] = val[u * sub:(u + 1) * sub]

    @pl.when(ph == 1)
    def _():
        bias = bias_ref[(sb * (SUPER // RUN) + j == 0).astype(jnp.int32)]
        for r in range(4):
            res = _attn_qblock(lambda s: q4[:, lanes(r, s)], lambda s: k4c[:, lanes(r, s)],
                               lambda s: k4p[:, lanes(r, s)], lambda s: v4c[:, lanes(r, s)],
                               lambda s: v4p[:, lanes(r, s)], bias, consts)
            rows = pl.ds(pl.multiple_of(j * RUN + r * QB, QB), QB)
            for s in range(nslab):
                mn, ln, an = _merge((m_s[s, rows, :], l_s[s, rows, :], a_s[s, rows, :]), res[s])
                m_s[s, rows, :] = mn
                l_s[s, rows, :] = ln
                a_s[s, rows, :] = an

    @pl.when(ph == 2)
    def _():
        first = sb * (SUPER // RUN) + j == 0
        for b in range(4):
            def cur(ref):
                return lambda s: jnp.concatenate(
                    [ref[b * sub:(b + 1) * sub, lanes(r, s)] for r in range(4)], axis=0)

            def prev(ref_c, ref_p):
                if b > 0:
                    return lambda s: jnp.concatenate(
                        [ref_c[(b - 1) * sub:b * sub, lanes(r, s)] for r in range(4)], axis=0)
                return lambda s: jnp.concatenate(
                    [ref_p[QB - sub:QB, lanes(r, s)] for r in range(4)], axis=0)

            if b == 0:
                bias = bias_ref[2 + first.astype(jnp.int32)]
            else:
                bias = bias_ref[2]
            res = _attn_qblock(cur(q4), cur(k4c), prev(k4c, k4p), cur(v4c), prev(v4c, v4p),
                               bias, consts)
            for s in range(nslab):
                for r in range(4):
                    rows = pl.ds(pl.multiple_of(j * RUN + r * QB + b * sub, sub), sub)
                    part = tuple(x[r * sub:(r + 1) * sub] for x in res[s])
                    _, ln, an = _merge((m_s[s, rows, :], l_s[s, rows, :], a_s[s, rows, :]), part)
                    nat_s[s, pl.ds(b * QB + r, sub, stride=4), :] = an / ln
        for s in range(nslab):
            o_ref[:, s * LANES:(s + 1) * LANES] = nat_s[s].astype(o_ref.dtype)


def _attn_prompt(q4, k4, v4, q16, k16, v16):
    nsb = q16.shape[0]
    s_len = nsb * SUPER
    rps = SUPER // RUN
    blk = (None, QB, 4 * ATT_WIDTH)

    def j16(ph, j):
        return jnp.where(ph == 0, j, rps - 1)

    def n4(sb, ph, j):
        return sb * rps + jnp.where(ph == 0, 0, j)

    c16 = pl.BlockSpec(blk, lambda sb, ph, j: (sb, 0, j16(ph, j)))
    p16 = pl.BlockSpec(blk, lambda sb, ph, j: (jnp.maximum(sb - 1, 0), 0, j16(ph, j)))
    c4 = pl.BlockSpec(blk, lambda sb, ph, j: (n4(sb, ph, j), 0, 0))
    p4 = pl.BlockSpec(blk, lambda sb, ph, j: (jnp.maximum(n4(sb, ph, j) - 1, 0), 0, 0))
    out = pl.BlockSpec((RUN, ATT_WIDTH), lambda sb, ph, j: (sb * rps + jnp.where(ph == 2, j, 0), 0))
    bias = jnp.asarray(_attn_bias_tables())
    nslab = ATT_WIDTH // LANES
    scratch = ([pltpu.VMEM((nslab, SUPER, LANES), F32) for _ in range(3)]
               + [pltpu.VMEM((nslab, RUN, LANES), F32)])
    return pl.pallas_call(
        _attn_kernel, grid=(nsb, 3, rps),
        in_specs=[c16, c16, p16, c16, p16, c4, c4, p4, c4, p4, _full_spec(bias.shape)],
        out_specs=out, out_shape=jax.ShapeDtypeStruct((s_len, ATT_WIDTH), BF16),
        scratch_shapes=scratch, compiler_params=_params(3), name="attn_prompt")(
            q16, k16, k16, v16, v16, q4, k4, k4, v4, v4, bias)


def _ret_kernel(qr_ref, kr_ref, vr_ref, gr_ref, gn_ref, y_ref, s_ref, dec_s, wt_s, wh_s, st_s,
                *, chunks):
    step = pl.program_id(0)
    c = RET_CHUNK

    @pl.when(step == 0)
    def _():
        i = lax.broadcasted_iota(jnp.int32, (c, c), 0)
        jj = lax.broadcasted_iota(jnp.int32, (c, c), 1)
        diff = (i - jj).astype(F32)
        for h in range(N_RET_HEADS):
            dec_s[h] = jnp.where(diff >= 0, jnp.exp(diff * LOG_G[h]), 0.0)
            wt_s[h] = jnp.exp((c - 1.0 - i.astype(F32)) * LOG_G[h])
            wh_s[h] = jnp.exp((i.astype(F32) + 1.0) * LOG_G[h])
        st_s[...] = jnp.zeros_like(st_s)

    lane = lax.broadcasted_iota(jnp.int32, (c, LANES), 1)
    head_a = lane < RET_DK
    row_a = lax.broadcasted_iota(jnp.int32, (LANES, RET_DV), 0) < RET_DK
    for ci in range(chunks):
        rows = slice(ci * c, (ci + 1) * c)
        for p in range(N_RET_HEADS // 2):
            sl = slice(p * LANES, (p + 1) * LANES)
            q2 = qr_ref[rows, sl]
            k2 = kr_ref[rows, sl]
            zq = jnp.zeros_like(q2)
            qq = jnp.concatenate([jnp.where(head_a, q2, zq), jnp.where(head_a, zq, q2)], axis=0)
            inner = _dot_nt(qq, k2)
            state = st_s[p]
            cross = _dot(qq, state.astype(BF16))
            vws = []
            for hh in range(2):
                h = 2 * p + hh
                hs = slice(h * RET_DV, (h + 1) * RET_DV)
                v = vr_ref[rows, hs]
                inn = (inner[hh * c:(hh + 1) * c] * dec_s[h]).astype(BF16)
                o = _dot(inn, v) + cross[hh * c:(hh + 1) * c] * wh_s[h]
                mu = jnp.mean(o, axis=-1, keepdims=True)
                xc = o - mu
                var = jnp.mean(xc * xc, axis=-1, keepdims=True)
                y = xc * lax.rsqrt(var + NORM_EPS) * gn_ref[:, hs]
                y_ref[rows, hs] = (_silu(gr_ref[rows, hs]) * y).astype(y_ref.dtype)
                vws.append((v.astype(F32) * wt_s[h]).astype(BF16))
            upd = _dot_tn(k2, jnp.concatenate(vws, axis=1))
            upd = jnp.where(row_a, upd[:, :RET_DV], upd[:, RET_DV:])
            gch = jnp.where(row_a, math.exp(c * LOG_G[2 * p]), math.exp(c * LOG_G[2 * p + 1]))
            st_s[p] = gch * state + upd

    @pl.when(step == pl.num_programs(0) - 1)
    def _():
        s_ref[...] = st_s[...]


def _ret_prompt(qr, kr, vr, gr, gn, tm):
    t = qr.shape[0]

    def row(w_):
        return pl.BlockSpec((tm, w_), lambda i: (i, 0))

    st_shape = (N_RET_HEADS // 2, 2 * RET_DK, RET_DV)
    tab = pltpu.VMEM((N_RET_HEADS, RET_CHUNK, RET_CHUNK), F32)
    y, s = pl.pallas_call(
        functools.partial(_ret_kernel, chunks=tm // RET_CHUNK), grid=(t // tm,),
        in_specs=[row(RET_QK), row(RET_QK), row(RET_V), row(RET_V), _full_spec((1, RET_V))],
        out_specs=[row(RET_V), _full_spec(st_shape)],
        out_shape=[jax.ShapeDtypeStruct((t, RET_V), BF16), jax.ShapeDtypeStruct(st_shape, F32)],
        scratch_shapes=[tab, tab, tab, pltpu.VMEM(st_shape, F32)],
        compiler_params=_params(1), name="ret_prompt")(qr, kr, vr, gr, gn)
    return y, s


S_ROWS = 8


def _sample_tables(t_new, past_len, wb):
    rows = np.concatenate([np.arange(wb), wb + np.arange(S_ROWS)])
    tab = np.zeros((S_ROWS, wb + S_ROWS), np.float32)
    for r in range(S_ROWS):
        i = r % t_new
        delta = wb + i - rows
        ok = (delta >= 0) & (past_len + i - delta >= 0) & (rows < wb + t_new)
        for window, dil in BRANCHES:
            tab[r] += ok & (delta % dil == 0) & (delta <= window)
    return tab[:, :wb], tab[:, wb:]


def _sample_attn_kernel(q_ref, kn_ref, vn_ref, kt_ref, vt_ref, cc_ref, cn_ref, o_ref):
    mult_c, mult_n = cc_ref[...], cn_ref[...]
    bias_c = jnp.where(mult_c > 0, 0.0, NEG)
    bias_n = jnp.where(mult_n > 0, 0.0, NEG)
    for h in range(N_ATT_HEADS):
        q = q_ref[h].astype(BF16)
        s_c = _dot(q, kt_ref[h].astype(BF16)) + bias_c
        s_n = _dot_nt(q, kn_ref[h].astype(BF16)) + bias_n
        m = jnp.maximum(jnp.max(s_c, axis=-1, keepdims=True), jnp.max(s_n, axis=-1, keepdims=True))
        p_c = jnp.exp(s_c - m) * mult_c
        p_n = jnp.exp(s_n - m) * mult_n
        l = jnp.sum(p_c, axis=-1, keepdims=True) + jnp.sum(p_n, axis=-1, keepdims=True)
        o = _dot_nt(p_c.astype(BF16), vt_ref[h].astype(BF16)) + _dot(p_n.astype(BF16),
                                                                    vn_ref[h].astype(BF16))
        o_ref[h] = o / l


def _attn_sample(q, kn, vn, cache_k, cache_v, t_new, past_len):
    nb, wb = cache_k.shape[0], cache_k.shape[1]
    assert t_new <= S_ROWS

    def heads(x):
        x = x.reshape(nb, t_new, N_ATT_HEADS, HEAD_DIM).transpose(0, 2, 1, 3)
        return jnp.pad(x, ((0, 0), (0, 0), (0, S_ROWS - t_new), (0, 0)))

    small = pl.BlockSpec((None, N_ATT_HEADS, S_ROWS, HEAD_DIM), lambda b: (b, 0, 0, 0))
    big = pl.BlockSpec((None, N_ATT_HEADS, HEAD_DIM, wb), lambda b: (b, 0, 0, 0))
    tabs = [jnp.asarray(t) for t in _sample_tables(t_new, past_len, wb)]
    o = pl.pallas_call(
        _sample_attn_kernel, grid=(nb,),
        in_specs=[small, small, small, big, big] + [_full_spec(t.shape) for t in tabs],
        out_specs=small, out_shape=jax.ShapeDtypeStruct((nb, N_ATT_HEADS, S_ROWS, HEAD_DIM), F32),
        compiler_params=_params(1), name="attn_sample")(
            heads(q), heads(kn), heads(vn), cache_k.transpose(0, 2, 3, 1), cache_v.transpose(0, 2, 3, 1),
            *tabs)
    return o[:, :, :t_new].transpose(0, 2, 1, 3).reshape(nb * t_new, ATT_WIDTH)


def _sample_ret_kernel(q_ref, k_ref, v_ref, g_ref, gn_ref, st_ref, ex_ref, y_ref, so_ref, *, t_new):
    hid = lax.broadcasted_iota(jnp.int32, (1, RET_V), 1) // RET_DV

    def headvec(fn):
        out = jnp.full((1, RET_V), fn(N_RET_HEADS - 1), F32)
        for h in range(N_RET_HEADS - 2, -1, -1):
            out = jnp.where(hid == h, fn(h), out)
        return out

    q = [q_ref[:, i * RET_QK:(i + 1) * RET_QK] for i in range(t_new)]
    k = [k_ref[:, i * RET_QK:(i + 1) * RET_QK] for i in range(t_new)]
    v = [v_ref[:, i * RET_V:(i + 1) * RET_V] for i in range(t_new)]
    vw = [v[jj] * headvec(lambda h: math.exp((t_new - 1.0 - jj) * LOG_G[h])) for jj in range(t_new)]
    ex = ex_ref[...]

    o = []
    for i in range(t_new):
        acc = None
        for jj in range(i + 1):
            sij = _dot((q[i] * k[jj]).astype(BF16), ex)
            term = sij * headvec(lambda h: math.exp((i - jj) * LOG_G[h])) * v[jj]
            acc = term if acc is None else acc + term
        o.append(acc)

    cross = [[None] * N_RET_HEADS for _ in range(t_new)]
    for h in range(N_RET_HEADS):
        hs = slice(h * RET_DV, (h + 1) * RET_DV)
        g_t = math.exp(t_new * LOG_G[h])
        for d in range(RET_DK):
            col = h * RET_DK + d
            cs = slice(col * RET_DV, (col + 1) * RET_DV)
            st = st_ref[:, cs]
            new = g_t * st
            for i in range(t_new):
                term = q[i][:, col:col + 1] * st
                cross[i][h] = term if cross[i][h] is None else cross[i][h] + term
                new = new + k[i][:, col:col + 1] * vw[i][:, hs]
            so_ref[:, cs] = new

    for i in range(t_new):
        for h in range(N_RET_HEADS):
            hs = slice(h * RET_DV, (h + 1) * RET_DV)
            oh = o[i][:, hs] + cross[i][h] * math.exp((i + 1.0) * LOG_G[h])
            mu = jnp.mean(oh, axis=-1, keepdims=True)
            xc = oh - mu
            var = jnp.mean(xc * xc, axis=-1, keepdims=True)
            y = xc * lax.rsqrt(var + NORM_EPS) * gn_ref[:, hs]
            gs = slice(i * RET_V + h * RET_DV, i * RET_V + (h + 1) * RET_DV)
            y_ref[:, gs] = _silu(g_ref[:, gs]) * y


def _ret_sample(qr, kr, vr, gr, gn, state, t_new, nb_blk=16):
    nb = state.shape[0]
    st_w = N_RET_HEADS * RET_DK * RET_DV
    ex = (np.arange(RET_QK)[:, None] // RET_DK == np.arange(RET_V)[None, :] // RET_DV)
    ex = jnp.asarray(ex.astype(np.float32), dtype=BF16)

    def blk(w_):
        return pl.BlockSpec((nb_blk, w_), lambda i: (i, 0))

    y, s = pl.pallas_call(
        functools.partial(_sample_ret_kernel, t_new=t_new), grid=(nb // nb_blk,),
        in_specs=[blk(t_new * RET_QK), blk(t_new * RET_QK), blk(t_new * RET_V), blk(t_new * RET_V),
                  _full_spec((1, RET_V)), blk(st_w), _full_spec(ex.shape)],
        out_specs=[blk(t_new * RET_V), blk(st_w)],
        out_shape=[jax.ShapeDtypeStruct((nb, t_new * RET_V), F32), jax.ShapeDtypeStruct((nb, st_w), F32)],
        compiler_params=_params(1), name="ret_sample")(
            qr.reshape(nb, -1), kr.reshape(nb, -1), vr.reshape(nb, -1), gr.reshape(nb, -1), gn,
            state.reshape(nb, st_w), ex)
    return y.reshape(nb * t_new, RET_V), s.reshape(state.shape)


def kernel(x_prompt, x_sample, cache_k, cache_v, state_ret, g_ffn1_pre, g_ffn1_post, w1_gate, w1_up,
           w1_down, g_mix_pre, g_mix_post, w_in, gn_w, w_out, g_ffn2_pre, g_ffn2_post, w2_gate, w2_up,
           w2_down):
    b_p, s_p, _ = x_prompt.shape
    b_s, t_s, _ = x_sample.shape
    depth = w_in.shape[0]
    assert depth == 1 and b_p == 1
    keep = min(WIN_MAX, s_p)
    l = 0
    bf = lambda w: w[l].astype(BF16)
    vec = lambda g: g[l].reshape(1, -1)
    w1 = (bf(w1_gate), bf(w1_up), bf(w1_down))
    w2 = (bf(w2_gate), bf(w2_up), bf(w2_down))
    wi, wo = bf(w_in), bf(w_out)
    g1a, g1b, gma, gmb = vec(g_ffn1_pre), vec(g_ffn1_post), vec(g_mix_pre), vec(g_mix_post)
    g2a, g2b, gn = vec(g_ffn2_pre), vec(g_ffn2_post), vec(gn_w)

    tm = RUN
    xp = x_prompt.reshape(s_p, D_MODEL)
    h1 = _ffn(xp, g1a, g1b, *w1, tm)
    q4, k4, v4, q16, k16, v16, kk, vk, qr, kr, vr, gr = _mixin(
        h1, gma, wi, tm * jnp.arange(s_p // tm), jnp.arange(tm), keep, tm, sample=False)
    o_att = _attn_prompt(q4, k4, v4, q16, k16, v16)
    y_ret, st_p = _ret_prompt(qr, kr, vr, gr, gn, tm)
    y_prompt = _mixout_ffn(o_att, y_ret, h1, wo, gmb, g2a, g2b, *w2, tm)

    n_s = b_s * t_s
    xs = x_sample.reshape(n_s, D_MODEL)
    h1s = _ffn(xs, g1a, g1b, *w1, n_s)
    qas, kks, vks, qrs, krs, vrs, grs = _mixin(
        h1s, gma, wi, jnp.full((1,), PAST_LEN), jnp.tile(jnp.arange(t_s), b_s), n_s, n_s, sample=True)
    o_att_s = _attn_sample(qas, kks, vks, cache_k[l], cache_v[l], t_s, PAST_LEN)
    y_ret_s, st_s = _ret_sample(qrs, krs, vrs, grs, gn, state_ret[l], t_s)
    y_sample = _mixout_ffn(o_att_s, y_ret_s, h1s, wo, gmb, g2a, g2b, *w2, n_s)

    hd = (N_ATT_HEADS, HEAD_DIM)
    return (y_prompt.reshape(b_p, s_p, D_MODEL),
            y_sample.reshape(b_s, t_s, D_MODEL),
            kk.reshape(depth, b_p, keep, *hd),
            vk.reshape(depth, b_p, keep, *hd),
            st_p.reshape(depth, b_p, N_RET_HEADS, RET_DK, RET_DV),
            kks.reshape(depth, b_s, t_s, *hd),
            vks.reshape(depth, b_s, t_s, *hd),
            st_s.reshape(depth, b_s, N_RET_HEADS, RET_DK, RET_DV))
```

```python
import functools
import math

import numpy as np
import jax
import jax.numpy as jnp
from jax import lax
from jax.experimental import pallas as pl
from jax.experimental.pallas import tpu as pltpu

F32 = jnp.float32
BF16 = jnp.bfloat16

D_MODEL = 1024
D_FF = 2816
HEAD_DIM = 64
N_ATT_HEADS = 8
ATT_WIDTH = N_ATT_HEADS * HEAD_DIM
ROT_DIM = HEAD_DIM // 4
ROPE_THETA = 500000.0
WIN_MAX = 2048
PAST_LEN = 8192
BRANCHES = ((128, 1), (512, 4), (2048, 16))
N_RET_HEADS = 4
RET_DK = 64
RET_DV = 128
RET_QK = N_RET_HEADS * RET_DK
RET_V = N_RET_HEADS * RET_DV
RET_THETA = 10000.0
RET_CHUNK = 128
IN_WIDTH = 3 * ATT_WIDTH + 2 * RET_QK + 2 * RET_V
NORM_EPS = 1e-6
NEG = -1e30

LANES = 128
QB = 128
SUPER = 2048
RUN = 512
VMEM_LIMIT = 56 * 1024 * 1024
LOG_G = tuple(math.log1p(-2.0 ** (-5.0 - h)) for h in range(N_RET_HEADS))


def _full_spec(shape):
    nd = len(shape)
    return pl.BlockSpec(shape, lambda *_: (0,) * nd)


def _resident_spec(shape):
    nd = len(shape)
    return pl.BlockSpec(shape, lambda *_: (0,) * nd, pipeline_mode=pl.Buffered(1))


def _params(n_axes):
    return pltpu.CompilerParams(dimension_semantics=("arbitrary",) * n_axes,
                                vmem_limit_bytes=VMEM_LIMIT)


def _rms(x, g):
    return x * lax.rsqrt(jnp.mean(x * x, axis=-1, keepdims=True) + NORM_EPS) * g


def _silu(x):
    return x / (1.0 + jnp.exp(-x))


def _dot(a, b):
    return jnp.dot(a, b, preferred_element_type=F32)


def _dot_nt(a, b):
    return lax.dot_general(a, b, (((1,), (1,)), ((), ())), preferred_element_type=F32)


def _dot_tn(a, b):
    return lax.dot_general(a, b, (((0,), (0,)), ((), ())), preferred_element_type=F32)


FF_CHUNK = 512


def _swiglu(u, wg_ref, wu_ref, wd_ref):
    acc = None
    for c0 in range(0, D_FF, FF_CHUNK):
        c1 = min(c0 + FF_CHUNK, D_FF)
        g = _dot(u, wg_ref[:, c0:c1])
        up = _dot(u, wu_ref[:, c0:c1])
        h = (_silu(g) * up).astype(BF16)
        d = _dot(h, wd_ref[c0:c1, :])
        acc = d if acc is None else acc + d
    return acc


def _ffn_kernel(x_ref, gpre_ref, gpost_ref, wg_ref, wu_ref, wd_ref, o_ref):
    x = x_ref[...]
    u = _rms(x, gpre_ref[...]).astype(BF16)
    y = _swiglu(u, wg_ref, wu_ref, wd_ref)
    o_ref[...] = x + 0.5 * _rms(y, gpost_ref[...])


def _mixout_ffn_kernel(oa_ref, yr_ref, h_ref, wo_ref, gmb_ref, gpre_ref, gpost_ref,
                       wg_ref, wu_ref, wd_ref, o_ref):
    mixed = jnp.concatenate([oa_ref[...].astype(BF16), yr_ref[...].astype(BF16)], axis=1)
    x = h_ref[...] + _rms(_dot(mixed, wo_ref[...]), gmb_ref[...])
    u = _rms(x, gpre_ref[...]).astype(BF16)
    y = _swiglu(u, wg_ref, wu_ref, wd_ref)
    o_ref[...] = x + 0.5 * _rms(y, gpost_ref[...])


def _ffn_specs():
    return [_full_spec((1, D_MODEL)), _full_spec((1, D_MODEL)),
            _resident_spec((D_MODEL, D_FF)), _resident_spec((D_MODEL, D_FF)),
            _resident_spec((D_FF, D_MODEL))]


def _ffn(x, gpre, gpost, wg, wu, wd, tm):
    t = x.shape[0]
    row = pl.BlockSpec((tm, D_MODEL), lambda i: (i, 0))
    return pl.pallas_call(
        _ffn_kernel, grid=(t // tm,),
        in_specs=[row] + _ffn_specs(), out_specs=row,
        out_shape=jax.ShapeDtypeStruct((t, D_MODEL), F32),
        compiler_params=_params(1), name="ffn")(x, gpre, gpost, wg, wu, wd)


def _mixout_ffn(oa, yr, h, wo, gmb, gpre, gpost, wg, wu, wd, tm):
    t = h.shape[0]
    row = pl.BlockSpec((tm, D_MODEL), lambda i: (i, 0))
    half = pl.BlockSpec((tm, ATT_WIDTH), lambda i: (i, 0))
    return pl.pallas_call(
        _mixout_ffn_kernel, grid=(t // tm,),
        in_specs=[half, half, row, _resident_spec((D_MODEL, D_MODEL)), _full_spec((1, D_MODEL))]
        + _ffn_specs(),
        out_specs=row, out_shape=jax.ShapeDtypeStruct((t, D_MODEL), F32),
        compiler_params=_params(1), name="mixout_ffn")(oa, yr, h, wo, gmb, gpre, gpost, wg, wu, wd)


def _rope_consts(base_pos, off_pos, rot_dim, theta, head_dim):
    half = rot_dim // 2
    inv = theta ** (-jnp.arange(half, dtype=F32) * (2.0 / rot_dim))
    lane = np.arange(LANES) % head_dim
    inv_l = inv[lane % half][None, :]
    first = jnp.asarray(lane < half, F32)[None, :]
    second = jnp.asarray((lane >= half) & (lane < rot_dim), F32)[None, :]
    rot = first + second
    a = base_pos.astype(F32)[:, None] * inv_l
    b = off_pos.astype(F32)[:, None] * inv_l
    cb, sb = jnp.cos(b), jnp.sin(b)
    base = jnp.stack([jnp.cos(a), jnp.sin(a)], axis=1)
    off = jnp.stack([cb * rot, sb * rot, jnp.broadcast_to(1.0 - rot, cb.shape),
                     -cb * first, -sb * first, cb * second, sb * second])
    return base, off


def _rope_fill(base_ref, off_ref, tab_s):
    ca = base_ref[0:1, :]
    sa = base_ref[1:2, :]
    tab_s[0] = ca * off_ref[0] - sa * off_ref[1] + off_ref[2]
    tab_s[1] = sa * off_ref[3] + ca * off_ref[4]
    tab_s[2] = sa * off_ref[5] + ca * off_ref[6]


def _rope(x, tab_ref, half):
    return (x * tab_ref[0] + pltpu.roll(x, LANES - half, 1) * tab_ref[1]
            + pltpu.roll(x, half, 1) * tab_ref[2])


def _lanes(r, s):
    return slice(r * ATT_WIDTH + s * LANES, r * ATT_WIDTH + (s + 1) * LANES)


def _emit_dilated(nat_s, o4_ref, o16_ref):
    for s in range(ATT_WIDTH // LANES):
        for r in range(4):
            o4_ref[:, _lanes(r, s)] = nat_s[s, pl.ds(r, RUN // 4, stride=4), :].astype(BF16)
        for r in range(16):
            o16_ref[:, _lanes(r, s)] = nat_s[s, pl.ds(r, RUN // 16, stride=16), :].astype(BF16)


def _mixin_kernel(h_ref, g_ref, w_ref, ba_ref, oa_ref, br_ref, or_ref, *refs, sample):
    if sample:
        qa_ref, kk_ref, vk_ref, qr_ref, kr_ref, vr_ref, gr_ref, ta_ref, tr_ref = refs
    else:
        (q4_ref, k4_ref, v4_ref, q16_ref, k16_ref, v16_ref, kk_ref, vk_ref,
         qr_ref, kr_ref, vr_ref, gr_ref, ta_ref, tr_ref, nat_s) = refs
    _rope_fill(ba_ref, oa_ref, ta_ref)
    _rope_fill(br_ref, or_ref, tr_ref)
    u = _rms(h_ref[...], g_ref[...]).astype(BF16)
    scale = HEAD_DIM ** -0.5
    nslab = ATT_WIDTH // LANES

    q = _dot(u, w_ref[:, 0:ATT_WIDTH])
    for s in range(nslab):
        sl = slice(s * LANES, (s + 1) * LANES)
        qs = _rope(q[:, sl], ta_ref, ROT_DIM // 2) * scale
        if sample:
            qa_ref[:, sl] = qs
        else:
            nat_s[s] = qs
    if not sample:
        _emit_dilated(nat_s, q4_ref, q16_ref)

    k = _dot(u, w_ref[:, ATT_WIDTH:2 * ATT_WIDTH])
    for s in range(nslab):
        sl = slice(s * LANES, (s + 1) * LANES)
        ks = _rope(k[:, sl], ta_ref, ROT_DIM // 2)
        if not sample:
            nat_s[s] = ks
        kk_ref[:, sl] = ks
    if not sample:
        _emit_dilated(nat_s, k4_ref, k16_ref)

    v = _dot(u, w_ref[:, 2 * ATT_WIDTH:3 * ATT_WIDTH])
    if not sample:
        for s in range(nslab):
            nat_s[s] = v[:, s * LANES:(s + 1) * LANES]
        _emit_dilated(nat_s, v4_ref, v16_ref)
    vk_ref[...] = v

    o = 3 * ATT_WIDTH
    qk = _dot(u, w_ref[:, o:o + 2 * RET_QK])
    for s in range(RET_QK // LANES):
        sl = slice(s * LANES, (s + 1) * LANES)
        qr_ref[:, sl] = _rope(qk[:, sl], tr_ref, RET_DK // 2).astype(qr_ref.dtype)
        sk = slice(RET_QK + s * LANES, RET_QK + (s + 1) * LANES)
        kr_ref[:, sl] = (_rope(qk[:, sk], tr_ref, RET_DK // 2) * (RET_DK ** -0.5)).astype(kr_ref.dtype)
    o += 2 * RET_QK
    vr_ref[...] = _dot(u, w_ref[:, o:o + RET_V]).astype(vr_ref.dtype)
    o += RET_V
    gr_ref[...] = _dot(u, w_ref[:, o:o + RET_V])


def _mixin(h, g, w, base_pos, off_pos, keep_rows, tm, sample):
    t = h.shape[0]
    nt = t // tm
    first_keep = nt - keep_rows // tm
    act = F32 if sample else BF16
    base_a, off_a = _rope_consts(base_pos, off_pos, ROT_DIM, ROPE_THETA, HEAD_DIM)
    base_r, off_r = _rope_consts(base_pos, off_pos, RET_DK, RET_THETA, RET_DK)

    def row(w_):
        return pl.BlockSpec((tm, w_), lambda i: (i, 0))

    keep_spec = pl.BlockSpec((tm, ATT_WIDTH), lambda i: (jnp.maximum(i - first_keep, 0), 0))
    base_spec = pl.BlockSpec((None, 2, LANES), lambda i: (i, 0, 0))
    off_spec = _resident_spec((7, tm, LANES))
    tab = pltpu.VMEM((3, tm, LANES), F32)
    tail_specs = [keep_spec, keep_spec, row(RET_QK), row(RET_QK), row(RET_V), row(RET_V)]
    tail_shapes = [((keep_rows, ATT_WIDTH), F32), ((keep_rows, ATT_WIDTH), F32),
                   ((t, RET_QK), act), ((t, RET_QK), act), ((t, RET_V), act), ((t, RET_V), F32)]
    if sample:
        specs = [row(ATT_WIDTH)] + tail_specs
        shapes = [((t, ATT_WIDTH), F32)] + tail_shapes
        scratch = [tab, tab]
    else:
        assert tm == RUN and t % SUPER == 0
        rps = SUPER // RUN
        s4 = pl.BlockSpec((None, RUN // 4, 4 * ATT_WIDTH), lambda i: (i, 0, 0))
        s16 = pl.BlockSpec((None, RUN // 16, 16 * ATT_WIDTH), lambda i: (i // rps, i % rps, 0))
        specs = [s4] * 3 + [s16] * 3 + tail_specs
        shapes = ([((t // RUN, RUN // 4, 4 * ATT_WIDTH), BF16)] * 3
                  + [((t // SUPER, SUPER // 16, 16 * ATT_WIDTH), BF16)] * 3 + tail_shapes)
        scratch = [tab, tab, pltpu.VMEM((ATT_WIDTH // LANES, RUN, LANES), F32)]
    return pl.pallas_call(
        functools.partial(_mixin_kernel, sample=sample), grid=(nt,),
        in_specs=[row(D_MODEL), _full_spec((1, D_MODEL)), _resident_spec((D_MODEL, IN_WIDTH)),
                  base_spec, off_spec, base_spec, off_spec],
        out_specs=specs, out_shape=[jax.ShapeDtypeStruct(s, d) for s, d in shapes],
        scratch_shapes=scratch, compiler_params=_params(1), name="mix_in")(
            h, g, w, base_a, off_a, base_r, off_r)


def _attn_bias_tables():
    a = np.arange(QB)[:, None]
    c = np.arange(2 * QB)[None, :]
    steps = BRANCHES[0][0]
    dist = QB + a - c
    band = (dist >= 0) & (dist <= steps)
    cur = c >= QB
    tok_q = 4 * (a % 32) + a // 32
    cc = c % QB
    tok_k = 4 * (cc % 32) + cc // 32 + QB * (c // QB) - QB
    dist2 = tok_q - tok_k
    band2 = (dist2 >= 0) & (dist2 <= steps)
    masks = np.stack([band, band & cur, band2, band2 & cur])
    return np.where(masks, 0.0, NEG).astype(np.float32)


def _attn_qblock(get_q, get_kc, get_kp, get_vc, get_vp, bias, consts):
    head_a, head_a_win, ones_a, ones_b = consts
    out = []
    for s in range(ATT_WIDTH // LANES):
        q2 = get_q(s)
        kwin = jnp.concatenate([get_kp(s), get_kc(s)], axis=0)
        vwin = jnp.concatenate([get_vp(s), get_vc(s)], axis=0)
        zq = jnp.zeros_like(q2)
        qq = jnp.concatenate([jnp.where(head_a, q2, zq), jnp.where(head_a, zq, q2)], axis=0)
        sc = _dot_nt(qq, kwin)
        s_a = sc[:QB] + bias
        s_b = sc[QB:] + bias
        m_a = jnp.max(s_a, axis=-1, keepdims=True)
        m_b = jnp.max(s_b, axis=-1, keepdims=True)
        p = jnp.concatenate([jnp.exp(s_a - m_a).astype(BF16), jnp.exp(s_b - m_b).astype(BF16)], axis=1)
        zv = jnp.zeros_like(vwin)
        w = jnp.concatenate(
            [jnp.concatenate([jnp.where(head_a_win, vwin, zv), ones_a], axis=1),
             jnp.concatenate([jnp.where(head_a_win, zv, vwin), ones_b], axis=1)], axis=0)
        r = _dot(p, w)
        out.append((jnp.where(head_a, m_a, m_b), r[:, LANES:], r[:, :LANES]))
    return out


def _merge(old, new):
    mo, lo, ao = old
    m2, l2, a2 = new
    mn = jnp.maximum(mo, m2)
    eo = jnp.exp(mo - mn)
    e2 = jnp.exp(m2 - mn)
    return mn, eo * lo + e2 * l2, eo * ao + e2 * a2


def _attn_kernel(q16, k16c, k16p, v16c, v16p, q4, k4c, k4p, v4c, v4p, bias_ref, o_ref,
                 m_s, l_s, a_s, nat_s):
    sb = pl.program_id(0)
    ph = pl.program_id(1)
    j = pl.program_id(2)
    lane = lax.broadcasted_iota(jnp.int32, (QB, LANES), 1)
    head_a = lane < HEAD_DIM
    lane_w = lax.broadcasted_iota(jnp.int32, (2 * QB, LANES), 1)
    head_a_win = lane_w < HEAD_DIM
    ones_a = jnp.where(head_a_win, 1.0, 0.0).astype(BF16)
    ones_b = jnp.where(head_a_win, 0.0, 1.0).astype(BF16)
    consts = (head_a, head_a_win, ones_a, ones_b)
    nslab = ATT_WIDTH // LANES
    sub = QB // 4
    lanes = _lanes

    @pl.when(ph == 0)
    def _():
        bias = bias_ref[(sb == 0).astype(jnp.int32)]
        for r in range(4):
            res = _attn_qblock(lambda s: q16[:, lanes(r, s)], lambda s: k16c[:, lanes(r, s)],
                               lambda s: k16p[:, lanes(r, s)], lambda s: v16c[:, lanes(r, s)],
                               lambda s: v16p[:, lanes(r, s)], bias, consts)
            for s in range(nslab):
                for u in range(SUPER // RUN):
                    rows = pl.ds(u * RUN + r * QB + j, sub, stride=4)
                    for ref, val in zip((m_s, l_s, a_s), res[s]):
                        ref[s, rows, :] = val[u * sub:(u + 1) * sub]

    @pl.when(ph == 1)
    def _():
        bias = bias_ref[(sb * (SUPER // RUN) + j == 0).astype(jnp.int32)]
        for r in range(4):
            res = _attn_qblock(lambda s: q4[:, lanes(r, s)], lambda s: k4c[:, lanes(r, s)],
                               lambda s: k4p[:, lanes(r, s)], lambda s: v4c[:, lanes(r, s)],
                               lambda s: v4p[:, lanes(r, s)], bias, consts)
            rows = pl.ds(pl.multiple_of(j * RUN + r * QB, QB), QB)
            for s in range(nslab):
                mn, ln, an = _merge((m_s[s, rows, :], l_s[s, rows, :], a_s[s, rows, :]), res[s])
                m_s[s, rows, :] = mn
                l_s[s, rows, :] = ln
                a_s[s, rows, :] = an

    @pl.when(ph == 2)
    def _():
        first = sb * (SUPER // RUN) + j == 0
        for b in range(4):
            def cur(ref):
                return lambda s: jnp.concatenate(
                    [ref[b * sub:(b + 1) * sub, lanes(r, s)] for r in range(4)], axis=0)

            def prev(ref_c, ref_p):
                if b > 0:
                    return lambda s: jnp.concatenate(
                        [ref_c[(b - 1) * sub:b * sub, lanes(r, s)] for r in range(4)], axis=0)
                return lambda s: jnp.concatenate(
                    [ref_p[QB - sub:QB, lanes(r, s)] for r in range(4)], axis=0)

            if b == 0:
                bias = bias_ref[2 + first.astype(jnp.int32)]
            else:
                bias = bias_ref[2]
            res = _attn_qblock(cur(q4), cur(k4c), prev(k4c, k4p), cur(v4c), prev(v4c, v4p),
                               bias, consts)
            for s in range(nslab):
                for r in range(4):
                    rows = pl.ds(pl.multiple_of(j * RUN + r * QB + b * sub, sub), sub)
                    part = tuple(x[r * sub:(r + 1) * sub] for x in res[s])
                    _, ln, an = _merge((m_s[s, rows, :], l_s[s, rows, :], a_s[s, rows, :]), part)
                    nat_s[s, pl.ds(b * QB + r, sub, stride=4), :] = an / ln
        for s in range(nslab):
            o_ref[:, s * LANES:(s + 1) * LANES] = nat_s[s].astype(o_ref.dtype)


def _attn_prompt(q4, k4, v4, q16, k16, v16):
    nsb = q16.shape[0]
    s_len = nsb * SUPER
    rps = SUPER // RUN
    blk = (None, QB, 4 * ATT_WIDTH)

    def j16(ph, j):
        return jnp.where(ph == 0, j, rps - 1)

    def n4(sb, ph, j):
        return sb * rps + jnp.where(ph == 0, 0, j)

    c16 = pl.BlockSpec(blk, lambda sb, ph, j: (sb, 0, j16(ph, j)))
    p16 = pl.BlockSpec(blk, lambda sb, ph, j: (jnp.maximum(sb - 1, 0), 0, j16(ph, j)))
    c4 = pl.BlockSpec(blk, lambda sb, ph, j: (n4(sb, ph, j), 0, 0))
    p4 = pl.BlockSpec(blk, lambda sb, ph, j: (jnp.maximum(n4(sb, ph, j) - 1, 0), 0, 0))
    out = pl.BlockSpec((RUN, ATT_WIDTH), lambda sb, ph, j: (sb * rps + jnp.where(ph == 2, j, 0), 0))
    bias = jnp.asarray(_attn_bias_tables())
    nslab = ATT_WIDTH // LANES
    scratch = ([pltpu.VMEM((nslab, SUPER, LANES), F32) for _ in range(3)]
               + [pltpu.VMEM((nslab, RUN, LANES), F32)])
    return pl.pallas_call(
        _attn_kernel, grid=(nsb, 3, rps),
        in_specs=[c16, c16, p16, c16, p16, c4, c4, p4, c4, p4, _full_spec(bias.shape)],
        out_specs=out, out_shape=jax.ShapeDtypeStruct((s_len, ATT_WIDTH), BF16),
        scratch_shapes=scratch, compiler_params=_params(3), name="attn_prompt")(
            q16, k16, k16, v16, v16, q4, k4, k4, v4, v4, bias)


def _ret_kernel(qr_ref, kr_ref, vr_ref, gr_ref, gn_ref, y_ref, s_ref, dec_s, wt_s, wh_s, st_s,
                *, chunks):
    step = pl.program_id(0)
    c = RET_CHUNK

    @pl.when(step == 0)
    def _():
        i = lax.broadcasted_iota(jnp.int32, (c, c), 0)
        jj = lax.broadcasted_iota(jnp.int32, (c, c), 1)
        diff = (i - jj).astype(F32)
        for h in range(N_RET_HEADS):
            dec_s[h] = jnp.where(diff >= 0, jnp.exp(diff * LOG_G[h]), 0.0)
            wt_s[h] = jnp.exp((c - 1.0 - i.astype(F32)) * LOG_G[h])
            wh_s[h] = jnp.exp((i.astype(F32) + 1.0) * LOG_G[h])
        st_s[...] = jnp.zeros_like(st_s)

    lane = lax.broadcasted_iota(jnp.int32, (c, LANES), 1)
    head_a = lane < RET_DK
    row_a = lax.broadcasted_iota(jnp.int32, (LANES, RET_DV), 0) < RET_DK
    for ci in range(chunks):
        rows = slice(ci * c, (ci + 1) * c)
        for p in range(N_RET_HEADS // 2):
            sl = slice(p * LANES, (p + 1) * LANES)
            q2 = qr_ref[rows, sl]
            k2 = kr_ref[rows, sl]
            zq = jnp.zeros_like(q2)
            qq = jnp.concatenate([jnp.where(head_a, q2, zq), jnp.where(head_a, zq, q2)], axis=0)
            inner = _dot_nt(qq, k2)
            state = st_s[p]
            cross = _dot(qq, state.astype(BF16))
            vws = []
            for hh in range(2):
                h = 2 * p + hh
                hs = slice(h * RET_DV, (h + 1) * RET_DV)
                v = vr_ref[rows, hs]
                inn = (inner[hh * c:(hh + 1) * c] * dec_s[h]).astype(BF16)
                o = _dot(inn, v) + cross[hh * c:(hh + 1) * c] * wh_s[h]
                mu = jnp.mean(o, axis=-1, keepdims=True)
                xc = o - mu
                var = jnp.mean(xc * xc, axis=-1, keepdims=True)
                y = xc * lax.rsqrt(var + NORM_EPS) * gn_ref[:, hs]
                y_ref[rows, hs] = (_silu(gr_ref[rows, hs]) * y).astype(y_ref.dtype)
                vws.append((v.astype(F32) * wt_s[h]).astype(BF16))
            upd = _dot_tn(k2, jnp.concatenate(vws, axis=1))
            upd = jnp.where(row_a, upd[:, :RET_DV], upd[:, RET_DV:])
            gch = jnp.where(row_a, math.exp(c * LOG_G[2 * p]), math.exp(c * LOG_G[2 * p + 1]))
            st_s[p] = gch * state + upd

    @pl.when(step == pl.num_programs(0) - 1)
    def _():
        s_ref[...] = st_s[...]


def _ret_prompt(qr, kr, vr, gr, gn, tm):
    t = qr.shape[0]

    def row(w_):
        return pl.BlockSpec((tm, w_), lambda i: (i, 0))

    st_shape = (N_RET_HEADS // 2, 2 * RET_DK, RET_DV)
    tab = pltpu.VMEM((N_RET_HEADS, RET_CHUNK, RET_CHUNK), F32)
    y, s = pl.pallas_call(
        functools.partial(_ret_kernel, chunks=tm // RET_CHUNK), grid=(t // tm,),
        in_specs=[row(RET_QK), row(RET_QK), row(RET_V), row(RET_V), _full_spec((1, RET_V))],
        out_specs=[row(RET_V), _full_spec(st_shape)],
        out_shape=[jax.ShapeDtypeStruct((t, RET_V), BF16), jax.ShapeDtypeStruct(st_shape, F32)],
        scratch_shapes=[tab, tab, tab, pltpu.VMEM(st_shape, F32)],
        compiler_params=_params(1), name="ret_prompt")(qr, kr, vr, gr, gn)
    return y, s


S_ROWS = 8


def _sample_tables(t_new, past_len, wb):
    rows = np.concatenate([np.arange(wb), wb + np.arange(S_ROWS)])
    tab = np.zeros((S_ROWS, wb + S_ROWS), np.float32)
    for r in range(S_ROWS):
        i = r % t_new
        delta = wb + i - rows
        ok = (delta >= 0) & (past_len + i - delta >= 0) & (rows < wb + t_new)
        for window, dil in BRANCHES:
            tab[r] += ok & (delta % dil == 0) & (delta <= window)
    return tab[:, :wb], tab[:, wb:]


def _sample_attn_body(q_ref, kn_ref, vn_ref, kt_ref, vt_ref, cc_ref, cn_ref, o_ref):
    mult_c, mult_n = cc_ref[...], cn_ref[...]
    bias_c = jnp.where(mult_c > 0, 0.0, NEG)
    bias_n = jnp.where(mult_n > 0, 0.0, NEG)
    for h in range(N_ATT_HEADS):
        q = q_ref[h].astype(BF16)
        s_c = _dot(q, kt_ref[h].astype(BF16)) + bias_c
        s_n = _dot_nt(q, kn_ref[h].astype(BF16)) + bias_n
        m = jnp.maximum(jnp.max(s_c, axis=-1, keepdims=True), jnp.max(s_n, axis=-1, keepdims=True))
        p_c = jnp.exp(s_c - m) * mult_c
        p_n = jnp.exp(s_n - m) * mult_n
        l = jnp.sum(p_c, axis=-1, keepdims=True) + jnp.sum(p_n, axis=-1, keepdims=True)
        o = _dot_nt(p_c.astype(BF16), vt_ref[h].astype(BF16)) + _dot(p_n.astype(BF16),
                                                                    vn_ref[h].astype(BF16))
        o_ref[h] = o / l


FF_SPLITS = ((0, 768), (768, 1536), (1536, 2304), (2304, D_FF))


def _ffn_attn_kernel(x_ref, gpre_ref, gpost_ref, wg_ref, wu_ref, wd_ref,
                     q_ref, kn_ref, vn_ref, kt_ref, vt_ref, cc_ref, cn_ref,
                     o_ref, oa_ref, u_s, acc_s):
    j = pl.program_id(1)
    last = len(FF_SPLITS) - 1
    for k, (c0, c1) in enumerate(FF_SPLITS):
        @pl.when(j == k)
        def _():
            if k == 0:
                u = _rms(x_ref[...], gpre_ref[...]).astype(BF16)
                u_s[...] = u
            else:
                u = u_s[...]
            h = (_silu(_dot(u, wg_ref[:, c0:c1])) * _dot(u, wu_ref[:, c0:c1])).astype(BF16)
            d = _dot(h, wd_ref[c0:c1, :])
            if k == 0:
                acc_s[...] = d
            elif k < last:
                acc_s[...] += d
            else:
                o_ref[...] = x_ref[...] + 0.5 * _rms(acc_s[...] + d, gpost_ref[...])
            _sample_attn_body(q_ref, kn_ref, vn_ref, kt_ref, vt_ref, cc_ref, cn_ref, oa_ref)


def _ffn_with_sample_attn(x, gpre, gpost, wg, wu, wd, q, kn, vn, cache_k, cache_v, t_new, past_len, tm):
    t = x.shape[0]
    nb, wb = cache_k.shape[0], cache_k.shape[1]
    nsplit = len(FF_SPLITS)
    assert t_new <= S_ROWS and nb == (t // tm) * nsplit

    def heads(a):
        a = a.reshape(nb, t_new, N_ATT_HEADS, HEAD_DIM).transpose(0, 2, 1, 3)
        return jnp.pad(a, ((0, 0), (0, 0), (0, S_ROWS - t_new), (0, 0)))

    row = pl.BlockSpec((tm, D_MODEL), lambda i, j: (i, 0))
    small = pl.BlockSpec((None, N_ATT_HEADS, S_ROWS, HEAD_DIM), lambda i, j: (i * nsplit + j, 0, 0, 0))
    big = pl.BlockSpec((None, N_ATT_HEADS, HEAD_DIM, wb), lambda i, j: (i * nsplit + j, 0, 0, 0))
    tabs = [jnp.asarray(a) for a in _sample_tables(t_new, past_len, wb)]
    h1, o = pl.pallas_call(
        _ffn_attn_kernel, grid=(t // tm, nsplit),
        in_specs=[row] + _ffn_specs() + [small, small, small, big, big]
        + [_full_spec(a.shape) for a in tabs],
        out_specs=[row, small],
        out_shape=[jax.ShapeDtypeStruct((t, D_MODEL), F32),
                   jax.ShapeDtypeStruct((nb, N_ATT_HEADS, S_ROWS, HEAD_DIM), F32)],
        scratch_shapes=[pltpu.VMEM((tm, D_MODEL), BF16), pltpu.VMEM((tm, D_MODEL), F32)],
        compiler_params=_params(2), name="ffn_attn_sample")(
            x, gpre, gpost, wg, wu, wd, heads(q), heads(kn), heads(vn),
            cache_k.transpose(0, 2, 3, 1), cache_v.transpose(0, 2, 3, 1), *tabs)
    return h1, o[:, :, :t_new].transpose(0, 2, 1, 3).reshape(nb * t_new, ATT_WIDTH)


def _sample_ret_kernel(q_ref, k_ref, v_ref, g_ref, gn_ref, st_ref, ex_ref, y_ref, so_ref, *, t_new):
    hid = lax.broadcasted_iota(jnp.int32, (1, RET_V), 1) // RET_DV

    def headvec(fn):
        out = jnp.full((1, RET_V), fn(N_RET_HEADS - 1), F32)
        for h in range(N_RET_HEADS - 2, -1, -1):
            out = jnp.where(hid == h, fn(h), out)
        return out

    q = [q_ref[:, i * RET_QK:(i + 1) * RET_QK] for i in range(t_new)]
    k = [k_ref[:, i * RET_QK:(i + 1) * RET_QK] for i in range(t_new)]
    v = [v_ref[:, i * RET_V:(i + 1) * RET_V] for i in range(t_new)]
    vw = [v[jj] * headvec(lambda h: math.exp((t_new - 1.0 - jj) * LOG_G[h])) for jj in range(t_new)]
    ex = ex_ref[...]

    o = []
    for i in range(t_new):
        acc = None
        for jj in range(i + 1):
            sij = _dot((q[i] * k[jj]).astype(BF16), ex)
            term = sij * headvec(lambda h: math.exp((i - jj) * LOG_G[h])) * v[jj]
            acc = term if acc is None else acc + term
        o.append(acc)

    cross = [[None] * N_RET_HEADS for _ in range(t_new)]
    for h in range(N_RET_HEADS):
        hs = slice(h * RET_DV, (h + 1) * RET_DV)
        g_t = math.exp(t_new * LOG_G[h])
        for d in range(RET_DK):
            col = h * RET_DK + d
            cs = slice(col * RET_DV, (col + 1) * RET_DV)
            st = st_ref[:, cs]
            new = g_t * st
            for i in range(t_new):
                term = q[i][:, col:col + 1] * st
                cross[i][h] = term if cross[i][h] is None else cross[i][h] + term
                new = new + k[i][:, col:col + 1] * vw[i][:, hs]
            so_ref[:, cs] = new

    for i in range(t_new):
        for h in range(N_RET_HEADS):
            hs = slice(h * RET_DV, (h + 1) * RET_DV)
            oh = o[i][:, hs] + cross[i][h] * math.exp((i + 1.0) * LOG_G[h])
            mu = jnp.mean(oh, axis=-1, keepdims=True)
            xc = oh - mu
            var = jnp.mean(xc * xc, axis=-1, keepdims=True)
            y = xc * lax.rsqrt(var + NORM_EPS) * gn_ref[:, hs]
            gs = slice(i * RET_V + h * RET_DV, i * RET_V + (h + 1) * RET_DV)
            y_ref[:, gs] = _silu(g_ref[:, gs]) * y


def _ret_sample(qr, kr, vr, gr, gn, state, t_new, nb_blk=16):
    nb = state.shape[0]
    st_w = N_RET_HEADS * RET_DK * RET_DV
    ex = (np.arange(RET_QK)[:, None] // RET_DK == np.arange(RET_V)[None, :] // RET_DV)
    ex = jnp.asarray(ex.astype(np.float32), dtype=BF16)

    def blk(w_):
        return pl.BlockSpec((nb_blk, w_), lambda i: (i, 0))

    y, s = pl.pallas_call(
        functools.partial(_sample_ret_kernel, t_new=t_new), grid=(nb // nb_blk,),
        in_specs=[blk(t_new * RET_QK), blk(t_new * RET_QK), blk(t_new * RET_V), blk(t_new * RET_V),
                  _full_spec((1, RET_V)), blk(st_w), _full_spec(ex.shape)],
        out_specs=[blk(t_new * RET_V), blk(st_w)],
        out_shape=[jax.ShapeDtypeStruct((nb, t_new * RET_V), F32), jax.ShapeDtypeStruct((nb, st_w), F32)],
        compiler_params=_params(1), name="ret_sample")(
            qr.reshape(nb, -1), kr.reshape(nb, -1), vr.reshape(nb, -1), gr.reshape(nb, -1), gn,
            state.reshape(nb, st_w), ex)
    return y.reshape(nb * t_new, RET_V), s.reshape(state.shape)


def kernel(x_prompt, x_sample, cache_k, cache_v, state_ret, g_ffn1_pre, g_ffn1_post, w1_gate, w1_up,
           w1_down, g_mix_pre, g_mix_post, w_in, gn_w, w_out, g_ffn2_pre, g_ffn2_post, w2_gate, w2_up,
           w2_down):
    b_p, s_p, _ = x_prompt.shape
    b_s, t_s, _ = x_sample.shape
    depth = w_in.shape[0]
    assert depth == 1 and b_p == 1
    keep = min(WIN_MAX, s_p)
    l = 0
    bf = lambda w: w[l].astype(BF16)
    vec = lambda g: g[l].reshape(1, -1)
    w1 = (bf(w1_gate), bf(w1_up), bf(w1_down))
    w2 = (bf(w2_gate), bf(w2_up), bf(w2_down))
    wi, wo = bf(w_in), bf(w_out)
    g1a, g1b, gma, gmb = vec(g_ffn1_pre), vec(g_ffn1_post), vec(g_mix_pre), vec(g_mix_post)
    g2a, g2b, gn = vec(g_ffn2_pre), vec(g_ffn2_post), vec(gn_w)

    n_s = b_s * t_s
    xs = x_sample.reshape(n_s, D_MODEL)
    h1s = _ffn(xs, g1a, g1b, *w1, n_s)
    qas, kks, vks, qrs, krs, vrs, grs = _mixin(
        h1s, gma, wi, jnp.full((1,), PAST_LEN), jnp.tile(jnp.arange(t_s), b_s), n_s, n_s, sample=True)

    tm = RUN
    xp = x_prompt.reshape(s_p, D_MODEL)
    h1, o_att_s = _ffn_with_sample_attn(xp, g1a, g1b, *w1, qas, kks, vks, cache_k[l], cache_v[l],
                                        t_s, PAST_LEN, tm)
    q4, k4, v4, q16, k16, v16, kk, vk, qr, kr, vr, gr = _mixin(
        h1, gma, wi, tm * jnp.arange(s_p // tm), jnp.arange(tm), keep, tm, sample=False)
    o_att = _attn_prompt(q4, k4, v4, q16, k16, v16)
    y_ret, st_p = _ret_prompt(qr, kr, vr, gr, gn, tm)
    y_prompt = _mixout_ffn(o_att, y_ret, h1, wo, gmb, g2a, g2b, *w2, tm)

    y_ret_s, st_s = _ret_sample(qrs, krs, vrs, grs, gn, state_ret[l], t_s)
    y_sample = _mixout_ffn(o_att_s, y_ret_s, h1s, wo, gmb, g2a, g2b, *w2, n_s)

    hd = (N_ATT_HEADS, HEAD_DIM)
    return (y_prompt.reshape(b_p, s_p, D_MODEL),
            y_sample.reshape(b_s, t_s, D_MODEL),
            kk.reshape(depth, b_p, keep, *hd),
            vk.reshape(depth, b_p, keep, *hd),
            st_p.reshape(depth, b_p, N_RET_HEADS, RET_DK, RET_DV),
            kks.reshape(depth, b_s, t_s, *hd),
            vks.reshape(depth, b_s, t_s, *hd),
            st_s.reshape(depth, b_s, N_RET_HEADS, RET_DK, RET_DV))
```

```python
import functools
import math

import numpy as np
import jax
import jax.numpy as jnp
from jax import lax
from jax.experimental import pallas as pl
from jax.experimental.pallas import tpu as pltpu

F32 = jnp.float32
BF16 = jnp.bfloat16

D_MODEL = 1024
D_FF = 2816
HEAD_DIM = 64
N_ATT_HEADS = 8
ATT_WIDTH = N_ATT_HEADS * HEAD_DIM
ROT_DIM = HEAD_DIM // 4
ROPE_THETA = 500000.0
WIN_MAX = 2048
PAST_LEN = 8192
BRANCHES = ((128, 1), (512, 4), (2048, 16))
N_RET_HEADS = 4
RET_DK = 64
RET_DV = 128
RET_QK = N_RET_HEADS * RET_DK
RET_V = N_RET_HEADS * RET_DV
RET_THETA = 10000.0
RET_CHUNK = 128
IN_WIDTH = 3 * ATT_WIDTH + 2 * RET_QK + 2 * RET_V
NORM_EPS = 1e-6
NEG = -1e30

LANES = 128
QB = 128
SUPER = 2048
RUN = 512
VMEM_LIMIT = 56 * 1024 * 1024
LOG_G = tuple(math.log1p(-2.0 ** (-5.0 - h)) for h in range(N_RET_HEADS))


def _full_spec(shape):
    nd = len(shape)
    return pl.BlockSpec(shape, lambda *_: (0,) * nd)


def _resident_spec(shape):
    nd = len(shape)
    return pl.BlockSpec(shape, lambda *_: (0,) * nd, pipeline_mode=pl.Buffered(1))


def _params(n_axes):
    return pltpu.CompilerParams(dimension_semantics=("arbitrary",) * n_axes,
                                vmem_limit_bytes=VMEM_LIMIT)


def _rms(x, g):
    return x * lax.rsqrt(jnp.mean(x * x, axis=-1, keepdims=True) + NORM_EPS) * g


def _silu(x):
    return x / (1.0 + jnp.exp(-x))


def _dot(a, b):
    return jnp.dot(a, b, preferred_element_type=F32)


def _dot_nt(a, b):
    return lax.dot_general(a, b, (((1,), (1,)), ((), ())), preferred_element_type=F32)


def _dot_tn(a, b):
    return lax.dot_general(a, b, (((0,), (0,)), ((), ())), preferred_element_type=F32)


FF_CHUNK = 512


def _swiglu(u, wg_ref, wu_ref, wd_ref):
    acc = None
    for c0 in range(0, D_FF, FF_CHUNK):
        c1 = min(c0 + FF_CHUNK, D_FF)
        g = _dot(u, wg_ref[:, c0:c1])
        up = _dot(u, wu_ref[:, c0:c1])
        h = (_silu(g) * up).astype(BF16)
        d = _dot(h, wd_ref[c0:c1, :])
        acc = d if acc is None else acc + d
    return acc


def _ffn_kernel(x_ref, gpre_ref, gpost_ref, wg_ref, wu_ref, wd_ref, o_ref):
    x = x_ref[...]
    u = _rms(x, gpre_ref[...]).astype(BF16)
    y = _swiglu(u, wg_ref, wu_ref, wd_ref)
    o_ref[...] = x + 0.5 * _rms(y, gpost_ref[...])


def _mixout_ffn_kernel(oa_ref, yr_ref, h_ref, wo_ref, gmb_ref, gpre_ref, gpost_ref,
                       wg_ref, wu_ref, wd_ref, o_ref):
    mixed = jnp.concatenate([oa_ref[...].astype(BF16), yr_ref[...].astype(BF16)], axis=1)
    x = h_ref[...] + _rms(_dot(mixed, wo_ref[...]), gmb_ref[...])
    u = _rms(x, gpre_ref[...]).astype(BF16)
    y = _swiglu(u, wg_ref, wu_ref, wd_ref)
    o_ref[...] = x + 0.5 * _rms(y, gpost_ref[...])


def _ffn_specs():
    return [_full_spec((1, D_MODEL)), _full_spec((1, D_MODEL)),
            _resident_spec((D_MODEL, D_FF)), _resident_spec((D_MODEL, D_FF)),
            _resident_spec((D_FF, D_MODEL))]


def _ffn(x, gpre, gpost, wg, wu, wd, tm):
    t = x.shape[0]
    row = pl.BlockSpec((tm, D_MODEL), lambda i: (i, 0))
    return pl.pallas_call(
        _ffn_kernel, grid=(t // tm,),
        in_specs=[row] + _ffn_specs(), out_specs=row,
        out_shape=jax.ShapeDtypeStruct((t, D_MODEL), F32),
        compiler_params=_params(1), name="ffn")(x, gpre, gpost, wg, wu, wd)


def _mixout_ffn(oa, yr, h, wo, gmb, gpre, gpost, wg, wu, wd, tm):
    t = h.shape[0]
    row = pl.BlockSpec((tm, D_MODEL), lambda i: (i, 0))
    half = pl.BlockSpec((tm, ATT_WIDTH), lambda i: (i, 0))
    return pl.pallas_call(
        _mixout_ffn_kernel, grid=(t // tm,),
        in_specs=[half, half, row, _resident_spec((D_MODEL, D_MODEL)), _full_spec((1, D_MODEL))]
        + _ffn_specs(),
        out_specs=row, out_shape=jax.ShapeDtypeStruct((t, D_MODEL), F32),
        compiler_params=_params(1), name="mixout_ffn")(oa, yr, h, wo, gmb, gpre, gpost, wg, wu, wd)


def _rope_consts(base_pos, off_pos, rot_dim, theta, head_dim):
    half = rot_dim // 2
    inv = theta ** (-jnp.arange(half, dtype=F32) * (2.0 / rot_dim))
    lane = np.arange(LANES) % head_dim
    inv_l = inv[lane % half][None, :]
    first = jnp.asarray(lane < half, F32)[None, :]
    second = jnp.asarray((lane >= half) & (lane < rot_dim), F32)[None, :]
    rot = first + second
    a = base_pos.astype(F32)[:, None] * inv_l
    b = off_pos.astype(F32)[:, None] * inv_l
    cb, sb = jnp.cos(b), jnp.sin(b)
    base = jnp.stack([jnp.cos(a), jnp.sin(a)], axis=1)
    off = jnp.stack([cb * rot, sb * rot, jnp.broadcast_to(1.0 - rot, cb.shape),
                     -cb * first, -sb * first, cb * second, sb * second])
    return base, off


def _rope_fill(base_ref, off_ref, tab_s):
    ca = base_ref[0:1, :]
    sa = base_ref[1:2, :]
    tab_s[0] = ca * off_ref[0] - sa * off_ref[1] + off_ref[2]
    tab_s[1] = sa * off_ref[3] + ca * off_ref[4]
    tab_s[2] = sa * off_ref[5] + ca * off_ref[6]


def _rope(x, tab_ref, half):
    return (x * tab_ref[0] + pltpu.roll(x, LANES - half, 1) * tab_ref[1]
            + pltpu.roll(x, half, 1) * tab_ref[2])


def _lanes(r, s):
    return slice(r * ATT_WIDTH + s * LANES, r * ATT_WIDTH + (s + 1) * LANES)


def _emit_dilated(nat_s, o4_ref, o16_ref):
    for s in range(ATT_WIDTH // LANES):
        for r in range(4):
            o4_ref[:, _lanes(r, s)] = nat_s[s, pl.ds(r, RUN // 4, stride=4), :].astype(BF16)
        for r in range(16):
            o16_ref[:, _lanes(r, s)] = nat_s[s, pl.ds(r, RUN // 16, stride=16), :].astype(BF16)


def _mixin_kernel(h_ref, g_ref, w_ref, ba_ref, oa_ref, br_ref, or_ref, *refs, sample):
    if sample:
        qa_ref, kk_ref, vk_ref, qr_ref, kr_ref, vr_ref, gr_ref, ta_ref, tr_ref = refs
    else:
        (q4_ref, k4_ref, v4_ref, q16_ref, k16_ref, v16_ref, kk_ref, vk_ref,
         qr_ref, kr_ref, vr_ref, gr_ref, ta_ref, tr_ref, nat_s) = refs
    _rope_fill(ba_ref, oa_ref, ta_ref)
    _rope_fill(br_ref, or_ref, tr_ref)
    u = _rms(h_ref[...], g_ref[...]).astype(BF16)
    scale = HEAD_DIM ** -0.5
    nslab = ATT_WIDTH // LANES

    q = _dot(u, w_ref[:, 0:ATT_WIDTH])
    for s in range(nslab):
        sl = slice(s * LANES, (s + 1) * LANES)
        qs = _rope(q[:, sl], ta_ref, ROT_DIM // 2) * scale
        if sample:
            qa_ref[:, sl] = qs
        else:
            nat_s[s] = qs
    if not sample:
        _emit_dilated(nat_s, q4_ref, q16_ref)

    k = _dot(u, w_ref[:, ATT_WIDTH:2 * ATT_WIDTH])
    for s in range(nslab):
        sl = slice(s * LANES, (s + 1) * LANES)
        ks = _rope(k[:, sl], ta_ref, ROT_DIM // 2)
        if not sample:
            nat_s[s] = ks
        kk_ref[:, sl] = ks
    if not sample:
        _emit_dilated(nat_s, k4_ref, k16_ref)

    v = _dot(u, w_ref[:, 2 * ATT_WIDTH:3 * ATT_WIDTH])
    if not sample:
        for s in range(nslab):
            nat_s[s] = v[:, s * LANES:(s + 1) * LANES]
        _emit_dilated(nat_s, v4_ref, v16_ref)
    vk_ref[...] = v

    o = 3 * ATT_WIDTH
    qk = _dot(u, w_ref[:, o:o + 2 * RET_QK])
    for s in range(RET_QK // LANES):
        sl = slice(s * LANES, (s + 1) * LANES)
        qr_ref[:, sl] = _rope(qk[:, sl], tr_ref, RET_DK // 2).astype(qr_ref.dtype)
        sk = slice(RET_QK + s * LANES, RET_QK + (s + 1) * LANES)
        kr_ref[:, sl] = (_rope(qk[:, sk], tr_ref, RET_DK // 2) * (RET_DK ** -0.5)).astype(kr_ref.dtype)
    o += 2 * RET_QK
    vr_ref[...] = _dot(u, w_ref[:, o:o + RET_V]).astype(vr_ref.dtype)
    o += RET_V
    gr_ref[...] = _dot(u, w_ref[:, o:o + RET_V])


def _mixin(h, g, w, base_pos, off_pos, keep_rows, tm, sample):
    t = h.shape[0]
    nt = t // tm
    first_keep = nt - keep_rows // tm
    act = F32 if sample else BF16
    base_a, off_a = _rope_consts(base_pos, off_pos, ROT_DIM, ROPE_THETA, HEAD_DIM)
    base_r, off_r = _rope_consts(base_pos, off_pos, RET_DK, RET_THETA, RET_DK)

    def row(w_):
        return pl.BlockSpec((tm, w_), lambda i: (i, 0))

    keep_spec = pl.BlockSpec((tm, ATT_WIDTH), lambda i: (jnp.maximum(i - first_keep, 0), 0))
    base_spec = pl.BlockSpec((None, 2, LANES), lambda i: (i, 0, 0))
    off_spec = _resident_spec((7, tm, LANES))
    tab = pltpu.VMEM((3, tm, LANES), F32)
    tail_specs = [keep_spec, keep_spec, row(RET_QK), row(RET_QK), row(RET_V), row(RET_V)]
    tail_shapes = [((keep_rows, ATT_WIDTH), F32), ((keep_rows, ATT_WIDTH), F32),
                   ((t, RET_QK), act), ((t, RET_QK), act), ((t, RET_V), act), ((t, RET_V), F32)]
    if sample:
        specs = [row(ATT_WIDTH)] + tail_specs
        shapes = [((t, ATT_WIDTH), F32)] + tail_shapes
        scratch = [tab, tab]
    else:
        assert tm == RUN and t % SUPER == 0
        rps = SUPER // RUN
        s4 = pl.BlockSpec((None, RUN // 4, 4 * ATT_WIDTH), lambda i: (i, 0, 0))
        s16 = pl.BlockSpec((None, RUN // 16, 16 * ATT_WIDTH), lambda i: (i // rps, i % rps, 0))
        specs = [s4] * 3 + [s16] * 3 + tail_specs
        shapes = ([((t // RUN, RUN // 4, 4 * ATT_WIDTH), BF16)] * 3
                  + [((t // SUPER, SUPER // 16, 16 * ATT_WIDTH), BF16)] * 3 + tail_shapes)
        scratch = [tab, tab, pltpu.VMEM((ATT_WIDTH // LANES, RUN, LANES), F32)]
    return pl.pallas_call(
        functools.partial(_mixin_kernel, sample=sample), grid=(nt,),
        in_specs=[row(D_MODEL), _full_spec((1, D_MODEL)), _resident_spec((D_MODEL, IN_WIDTH)),
                  base_spec, off_spec, base_spec, off_spec],
        out_specs=specs, out_shape=[jax.ShapeDtypeStruct(s, d) for s, d in shapes],
        scratch_shapes=scratch, compiler_params=_params(1), name="mix_in")(
            h, g, w, base_a, off_a, base_r, off_r)


def _attn_bias_tables():
    a = np.arange(QB)[:, None]
    c = np.arange(2 * QB)[None, :]
    steps = BRANCHES[0][0]
    dist = QB + a - c
    band = (dist >= 0) & (dist <= steps)
    cur = c >= QB
    tok_q = 4 * (a % 32) + a // 32
    cc = c % QB
    tok_k = 4 * (cc % 32) + cc // 32 + QB * (c // QB) - QB
    dist2 = tok_q - tok_k
    band2 = (dist2 >= 0) & (dist2 <= steps)
    masks = np.stack([band, band & cur, band2, band2 & cur])
    return np.where(masks, 0.0, NEG).astype(np.float32)


def _attn_qblock(get_q, get_kc, get_kp, get_vc, get_vp, bias, consts):
    head_a, head_a_win, ones_a, ones_b = consts
    out = []
    for s in range(ATT_WIDTH // LANES):
        q2 = get_q(s)
        kwin = jnp.concatenate([get_kp(s), get_kc(s)], axis=0)
        vwin = jnp.concatenate([get_vp(s), get_vc(s)], axis=0)
        zq = jnp.zeros_like(q2)
        qq = jnp.concatenate([jnp.where(head_a, q2, zq), jnp.where(head_a, zq, q2)], axis=0)
        sc = _dot_nt(qq, kwin)
        s_a = sc[:QB] + bias
        s_b = sc[QB:] + bias
        m_a = jnp.max(s_a, axis=-1, keepdims=True)
        m_b = jnp.max(s_b, axis=-1, keepdims=True)
        p = jnp.concatenate([jnp.exp(s_a - m_a).astype(BF16), jnp.exp(s_b - m_b).astype(BF16)], axis=1)
        zv = jnp.zeros_like(vwin)
        w = jnp.concatenate(
            [jnp.concatenate([jnp.where(head_a_win, vwin, zv), ones_a], axis=1),
             jnp.concatenate([jnp.where(head_a_win, zv, vwin), ones_b], axis=1)], axis=0)
        r = _dot(p, w)
        out.append((jnp.where(head_a, m_a, m_b), r[:, LANES:], r[:, :LANES]))
    return out


def _merge(old, new):
    mo, lo, ao = old
    m2, l2, a2 = new
    mn = jnp.maximum(mo, m2)
    eo = jnp.exp(mo - mn)
    e2 = jnp.exp(m2 - mn)
    return mn, eo * lo + e2 * l2, eo * ao + e2 * a2


def _attn_kernel(q16, k16c, k16p, v16c, v16p, q4, k4c, k4p, v4c, v4p, bias_ref, o_ref,
                 m_s, l_s, a_s, nat_s):
    sb = pl.program_id(0)
    ph = pl.program_id(1)
    j = pl.program_id(2)
    lane = lax.broadcasted_iota(jnp.int32, (QB, LANES), 1)
    head_a = lane < HEAD_DIM
    lane_w = lax.broadcasted_iota(jnp.int32, (2 * QB, LANES), 1)
    head_a_win = lane_w < HEAD_DIM
    ones_a = jnp.where(head_a_win, 1.0, 0.0).astype(BF16)
    ones_b = jnp.where(head_a_win, 0.0, 1.0).astype(BF16)
    consts = (head_a, head_a_win, ones_a, ones_b)
    nslab = ATT_WIDTH // LANES
    sub = QB // 4
    lanes = _lanes

    @pl.when(ph == 0)
    def _():
        bias = bias_ref[(sb == 0).astype(jnp.int32)]
        for r in range(4):
            res = _attn_qblock(lambda s: q16[:, lanes(r, s)], lambda s: k16c[:, lanes(r, s)],
                               lambda s: k16p[:, lanes(r, s)], lambda s: v16c[:, lanes(r, s)],
                               lambda s: v16p[:, lanes(r, s)], bias, consts)
            for s in range(nslab):
                for u in range(SUPER // RUN):
                    rows = pl.ds(u * RUN + r * QB + j, sub, stride=4)
                    for ref, val in zip((m_s, l_s, a_s), res[s]):
                        ref[s, rows, :] = val[u * sub:(u + 1) * sub]

    @pl.when(ph == 1)
    def _():
        bias = bias_ref[(sb * (SUPER // RUN) + j == 0).astype(jnp.int32)]
        for r in range(4):
            res = _attn_qblock(lambda s: q4[:, lanes(r, s)], lambda s: k4c[:, lanes(r, s)],
                               lambda s: k4p[:, lanes(r, s)], lambda s: v4c[:, lanes(r, s)],
                               lambda s: v4p[:, lanes(r, s)], bias, consts)
            rows = pl.ds(pl.multiple_of(j * RUN + r * QB, QB), QB)
            for s in range(nslab):
                mn, ln, an = _merge((m_s[s, rows, :], l_s[s, rows, :], a_s[s, rows, :]), res[s])
                m_s[s, rows, :] = mn
                l_s[s, rows, :] = ln
                a_s[s, rows, :] = an

    @pl.when(ph == 2)
    def _():
        first = sb * (SUPER // RUN) + j == 0
        for b in range(4):
            def cur(ref):
                return lambda s: jnp.concatenate(
                    [ref[b * sub:(b + 1) * sub, lanes(r, s)] for r in range(4)], axis=0)

            def prev(ref_c, ref_p):
                if b > 0:
                    return lambda s: jnp.concatenate(
                        [ref_c[(b - 1) * sub:b * sub, lanes(r, s)] for r in range(4)], axis=0)
                return lambda s: jnp.concatenate(
                    [ref_p[QB - sub:QB, lanes(r, s)] for r in range(4)], axis=0)

            if b == 0:
                bias = bias_ref[2 + first.astype(jnp.int32)]
            else:
                bias = bias_ref[2]
            res = _attn_qblock(cur(q4), cur(k4c), prev(k4c, k4p), cur(v4c), prev(v4c, v4p),
                               bias, consts)
            for s in range(nslab):
                for r in range(4):
                    rows = pl.ds(pl.multiple_of(j * RUN + r * QB + b * sub, sub), sub)
                    part = tuple(x[r * sub:(r + 1) * sub] for x in res[s])
                    _, ln, an = _merge((m_s[s, rows, :], l_s[s, rows, :], a_s[s, rows, :]), part)
                    nat_s[s, pl.ds(b * QB + r, sub, stride=4), :] = an / ln
        for s in range(nslab):
            o_ref[:, s * LANES:(s + 1) * LANES] = nat_s[s].astype(o_ref.dtype)


def _attn_prompt(q4, k4, v4, q16, k16, v16):
    nsb = q16.shape[0]
    s_len = nsb * SUPER
    rps = SUPER // RUN
    blk = (None, QB, 4 * ATT_WIDTH)

    def j16(ph, j):
        return jnp.where(ph == 0, j, rps - 1)

    def n4(sb, ph, j):
        return sb * rps + jnp.where(ph == 0, 0, j)

    c16 = pl.BlockSpec(blk, lambda sb, ph, j: (sb, 0, j16(ph, j)))
    p16 = pl.BlockSpec(blk, lambda sb, ph, j: (jnp.maximum(sb - 1, 0), 0, j16(ph, j)))
    c4 = pl.BlockSpec(blk, lambda sb, ph, j: (n4(sb, ph, j), 0, 0))
    p4 = pl.BlockSpec(blk, lambda sb, ph, j: (jnp.maximum(n4(sb, ph, j) - 1, 0), 0, 0))
    out = pl.BlockSpec((RUN, ATT_WIDTH), lambda sb, ph, j: (sb * rps + jnp.where(ph == 2, j, 0), 0))
    bias = jnp.asarray(_attn_bias_tables())
    nslab = ATT_WIDTH // LANES
    scratch = ([pltpu.VMEM((nslab, SUPER, LANES), F32) for _ in range(3)]
               + [pltpu.VMEM((nslab, RUN, LANES), F32)])
    return pl.pallas_call(
        _attn_kernel, grid=(nsb, 3, rps),
        in_specs=[c16, c16, p16, c16, p16, c4, c4, p4, c4, p4, _full_spec(bias.shape)],
        out_specs=out, out_shape=jax.ShapeDtypeStruct((s_len, ATT_WIDTH), BF16),
        scratch_shapes=scratch, compiler_params=_params(3), name="attn_prompt")(
            q16, k16, k16, v16, v16, q4, k4, k4, v4, v4, bias)


def _ret_kernel(qr_ref, kr_ref, vr_ref, gr_ref, gn_ref, y_ref, s_ref, dec_s, wt_s, wh_s, st_s,
                *, chunks):
    step = pl.program_id(0)
    c = RET_CHUNK

    @pl.when(step == 0)
    def _():
        i = lax.broadcasted_iota(jnp.int32, (c, c), 0)
        jj = lax.broadcasted_iota(jnp.int32, (c, c), 1)
        diff = (i - jj).astype(F32)
        for h in range(N_RET_HEADS):
            dec_s[h] = jnp.where(diff >= 0, jnp.exp(diff * LOG_G[h]), 0.0)
            wt_s[h] = jnp.exp((c - 1.0 - i.astype(F32)) * LOG_G[h])
            wh_s[h] = jnp.exp((i.astype(F32) + 1.0) * LOG_G[h])
        st_s[...] = jnp.zeros_like(st_s)

    lane = lax.broadcasted_iota(jnp.int32, (c, LANES), 1)
    head_a = lane < RET_DK
    row_a = lax.broadcasted_iota(jnp.int32, (LANES, RET_DV), 0) < RET_DK
    for ci in range(chunks):
        rows = slice(ci * c, (ci + 1) * c)
        for p in range(N_RET_HEADS // 2):
            sl = slice(p * LANES, (p + 1) * LANES)
            q2 = qr_ref[rows, sl]
            k2 = kr_ref[rows, sl]
            zq = jnp.zeros_like(q2)
            qq = jnp.concatenate([jnp.where(head_a, q2, zq), jnp.where(head_a, zq, q2)], axis=0)
            inner = _dot_nt(qq, k2)
            state = st_s[p]
            cross = _dot(qq, state.astype(BF16))
            vws = []
            for hh in range(2):
                h = 2 * p + hh
                hs = slice(h * RET_DV, (h + 1) * RET_DV)
                v = vr_ref[rows, hs]
                inn = (inner[hh * c:(hh + 1) * c] * dec_s[h]).astype(BF16)
                o = _dot(inn, v) + cross[hh * c:(hh + 1) * c] * wh_s[h]
                mu = jnp.mean(o, axis=-1, keepdims=True)
                xc = o - mu
                var = jnp.mean(xc * xc, axis=-1, keepdims=True)
                y = xc * lax.rsqrt(var + NORM_EPS) * gn_ref[:, hs]
                y_ref[rows, hs] = (_silu(gr_ref[rows, hs]) * y).astype(y_ref.dtype)
                vws.append((v.astype(F32) * wt_s[h]).astype(BF16))
            upd = _dot_tn(k2, jnp.concatenate(vws, axis=1))
            upd = jnp.where(row_a, upd[:, :RET_DV], upd[:, RET_DV:])
            gch = jnp.where(row_a, math.exp(c * LOG_G[2 * p]), math.exp(c * LOG_G[2 * p + 1]))
            st_s[p] = gch * state + upd

    @pl.when(step == pl.num_programs(0) - 1)
    def _():
        s_ref[...] = st_s[...]


def _ret_prompt(qr, kr, vr, gr, gn, tm):
    t = qr.shape[0]

    def row(w_):
        return pl.BlockSpec((tm, w_), lambda i: (i, 0))

    st_shape = (N_RET_HEADS // 2, 2 * RET_DK, RET_DV)
    tab = pltpu.VMEM((N_RET_HEADS, RET_CHUNK, RET_CHUNK), F32)
    y, s = pl.pallas_call(
        functools.partial(_ret_kernel, chunks=tm // RET_CHUNK), grid=(t // tm,),
        in_specs=[row(RET_QK), row(RET_QK), row(RET_V), row(RET_V), _full_spec((1, RET_V))],
        out_specs=[row(RET_V), _full_spec(st_shape)],
        out_shape=[jax.ShapeDtypeStruct((t, RET_V), BF16), jax.ShapeDtypeStruct(st_shape, F32)],
        scratch_shapes=[tab, tab, tab, pltpu.VMEM(st_shape, F32)],
        compiler_params=_params(1), name="ret_prompt")(qr, kr, vr, gr, gn)
    return y, s


S_ROWS = 8
S_HGRP = 4


def _sample_tables(t_new, past_len, wb):
    rows = np.concatenate([np.arange(wb), wb + np.arange(S_ROWS)])
    tab = np.zeros((S_ROWS, wb + S_ROWS), np.float32)
    for r in range(S_ROWS):
        i = r % t_new
        delta = wb + i - rows
        ok = (delta >= 0) & (past_len + i - delta >= 0) & (rows < wb + t_new)
        for window, dil in BRANCHES:
            tab[r] += ok & (delta % dil == 0) & (delta <= window)
    tab = np.tile(tab, (S_HGRP, 1))
    return tab[:, :wb], tab[:, wb:]


def _sample_attn_body(q_ref, kn_ref, vn_ref, kt_ref, vt_ref, cc_ref, cn_ref, o_ref):
    mult_c, mult_n = cc_ref[...], cn_ref[...]
    bias_c = jnp.where(mult_c > 0, 0.0, NEG)
    bias_n = jnp.where(mult_n > 0, 0.0, NEG)
    gw = S_HGRP * HEAD_DIM
    lane_head = lax.broadcasted_iota(jnp.int32, (S_ROWS, gw), 1) // HEAD_DIM
    for g in range(N_ATT_HEADS // S_HGRP):
        gs = slice(g * gw, (g + 1) * gw)
        q8 = q_ref[:, gs]
        q = jnp.concatenate([jnp.where(lane_head == h, q8, 0.0) for h in range(S_HGRP)],
                            axis=0).astype(BF16)
        s_c = _dot(q, kt_ref[g].astype(BF16)) + bias_c
        s_n = _dot_nt(q, kn_ref[:, gs].astype(BF16)) + bias_n
        m = jnp.maximum(jnp.max(s_c, axis=-1, keepdims=True), jnp.max(s_n, axis=-1, keepdims=True))
        p_c = jnp.exp(s_c - m) * mult_c
        p_n = jnp.exp(s_n - m) * mult_n
        l = jnp.sum(p_c, axis=-1, keepdims=True) + jnp.sum(p_n, axis=-1, keepdims=True)
        o = (_dot_nt(p_c.astype(BF16), vt_ref[g].astype(BF16))
             + _dot(p_n.astype(BF16), vn_ref[:, gs].astype(BF16))) / l
        o8 = jnp.zeros((S_ROWS, gw), F32)
        for h in range(S_HGRP):
            o8 = jnp.where(lane_head == h, o[h * S_ROWS:(h + 1) * S_ROWS], o8)
        o_ref[:, gs] = o8


FF_SPLITS = ((0, 768), (768, 1536), (1536, 2304), (2304, D_FF))


def _ffn_attn_kernel(x_ref, gpre_ref, gpost_ref, wg_ref, wu_ref, wd_ref,
                     q_ref, kn_ref, vn_ref, kt_ref, vt_ref, cc_ref, cn_ref,
                     o_ref, oa_ref, u_s, acc_s):
    j = pl.program_id(1)
    last = len(FF_SPLITS) - 1
    for k, (c0, c1) in enumerate(FF_SPLITS):
        @pl.when(j == k)
        def _():
            if k == 0:
                u = _rms(x_ref[...], gpre_ref[...]).astype(BF16)
                u_s[...] = u
            else:
                u = u_s[...]
            h = (_silu(_dot(u, wg_ref[:, c0:c1])) * _dot(u, wu_ref[:, c0:c1])).astype(BF16)
            d = _dot(h, wd_ref[c0:c1, :])
            if k == 0:
                acc_s[...] = d
            elif k < last:
                acc_s[...] += d
            else:
                o_ref[...] = x_ref[...] + 0.5 * _rms(acc_s[...] + d, gpost_ref[...])
            _sample_attn_body(q_ref, kn_ref, vn_ref, kt_ref, vt_ref, cc_ref, cn_ref, oa_ref)


def _ffn_with_sample_attn(x, gpre, gpost, wg, wu, wd, q, kn, vn, cache_k, cache_v, t_new, past_len, tm):
    t = x.shape[0]
    nb, wb = cache_k.shape[0], cache_k.shape[1]
    nsplit = len(FF_SPLITS)
    assert t_new <= S_ROWS and nb == (t // tm) * nsplit

    def rows(a):
        return jnp.pad(a.reshape(nb, t_new, ATT_WIDTH), ((0, 0), (0, S_ROWS - t_new), (0, 0)))

    def grouped(c):
        return c.transpose(0, 2, 3, 1).reshape(nb, N_ATT_HEADS // S_HGRP, S_HGRP * HEAD_DIM, wb)

    row = pl.BlockSpec((tm, D_MODEL), lambda i, j: (i, 0))
    small = pl.BlockSpec((None, S_ROWS, ATT_WIDTH), lambda i, j: (i * nsplit + j, 0, 0))
    big = pl.BlockSpec((None, N_ATT_HEADS // S_HGRP, S_HGRP * HEAD_DIM, wb),
                       lambda i, j: (i * nsplit + j, 0, 0, 0))
    tabs = [jnp.asarray(a) for a in _sample_tables(t_new, past_len, wb)]
    h1, o = pl.pallas_call(
        _ffn_attn_kernel, grid=(t // tm, nsplit),
        in_specs=[row] + _ffn_specs() + [small, small, small, big, big]
        + [_full_spec(a.shape) for a in tabs],
        out_specs=[row, small],
        out_shape=[jax.ShapeDtypeStruct((t, D_MODEL), F32),
                   jax.ShapeDtypeStruct((nb, S_ROWS, ATT_WIDTH), F32)],
        scratch_shapes=[pltpu.VMEM((tm, D_MODEL), BF16), pltpu.VMEM((tm, D_MODEL), F32)],
        compiler_params=_params(2), name="ffn_attn_sample")(
            x, gpre, gpost, wg, wu, wd, rows(q), rows(kn), rows(vn),
            grouped(cache_k), grouped(cache_v), *tabs)
    return h1, o[:, :t_new].reshape(nb * t_new, ATT_WIDTH)


def _sample_ret_kernel(q_ref, k_ref, v_ref, g_ref, gn_ref, st_ref, ex_ref, y_ref, so_ref, *, t_new):
    hid = lax.broadcasted_iota(jnp.int32, (1, RET_V), 1) // RET_DV

    def headvec(fn):
        out = jnp.full((1, RET_V), fn(N_RET_HEADS - 1), F32)
        for h in range(N_RET_HEADS - 2, -1, -1):
            out = jnp.where(hid == h, fn(h), out)
        return out

    q = [q_ref[:, i * RET_QK:(i + 1) * RET_QK] for i in range(t_new)]
    k = [k_ref[:, i * RET_QK:(i + 1) * RET_QK] for i in range(t_new)]
    v = [v_ref[:, i * RET_V:(i + 1) * RET_V] for i in range(t_new)]
    vw = [v[jj] * headvec(lambda h: math.exp((t_new - 1.0 - jj) * LOG_G[h])) for jj in range(t_new)]
    ex = ex_ref[...]

    o = []
    for i in range(t_new):
        acc = None
        for jj in range(i + 1):
            sij = _dot((q[i] * k[jj]).astype(BF16), ex)
            term = sij * headvec(lambda h: math.exp((i - jj) * LOG_G[h])) * v[jj]
            acc = term if acc is None else acc + term
        o.append(acc)

    cross = [[None] * N_RET_HEADS for _ in range(t_new)]
    for h in range(N_RET_HEADS):
        hs = slice(h * RET_DV, (h + 1) * RET_DV)
        g_t = math.exp(t_new * LOG_G[h])
        for d in range(RET_DK):
            col = h * RET_DK + d
            cs = slice(col * RET_DV, (col + 1) * RET_DV)
            st = st_ref[:, cs]
            new = g_t * st
            for i in range(t_new):
                term = q[i][:, col:col + 1] * st
                cross[i][h] = term if cross[i][h] is None else cross[i][h] + term
                new = new + k[i][:, col:col + 1] * vw[i][:, hs]
            so_ref[:, cs] = new

    for i in range(t_new):
        for h in range(N_RET_HEADS):
            hs = slice(h * RET_DV, (h + 1) * RET_DV)
            oh = o[i][:, hs] + cross[i][h] * math.exp((i + 1.0) * LOG_G[h])
            mu = jnp.mean(oh, axis=-1, keepdims=True)
            xc = oh - mu
            var = jnp.mean(xc * xc, axis=-1, keepdims=True)
            y = xc * lax.rsqrt(var + NORM_EPS) * gn_ref[:, hs]
            gs = slice(i * RET_V + h * RET_DV, i * RET_V + (h + 1) * RET_DV)
            y_ref[:, gs] = _silu(g_ref[:, gs]) * y


def _ret_sample(qr, kr, vr, gr, gn, state, t_new, nb_blk=16):
    nb = state.shape[0]
    st_w = N_RET_HEADS * RET_DK * RET_DV
    ex = (np.arange(RET_QK)[:, None] // RET_DK == np.arange(RET_V)[None, :] // RET_DV)
    ex = jnp.asarray(ex.astype(np.float32), dtype=BF16)

    def blk(w_):
        return pl.BlockSpec((nb_blk, w_), lambda i: (i, 0))

    y, s = pl.pallas_call(
        functools.partial(_sample_ret_kernel, t_new=t_new), grid=(nb // nb_blk,),
        in_specs=[blk(t_new * RET_QK), blk(t_new * RET_QK), blk(t_new * RET_V), blk(t_new * RET_V),
                  _full_spec((1, RET_V)), blk(st_w), _full_spec(ex.shape)],
        out_specs=[blk(t_new * RET_V), blk(st_w)],
        out_shape=[jax.ShapeDtypeStruct((nb, t_new * RET_V), F32), jax.ShapeDtypeStruct((nb, st_w), F32)],
        compiler_params=_params(1), name="ret_sample")(
            qr.reshape(nb, -1), kr.reshape(nb, -1), vr.reshape(nb, -1), gr.reshape(nb, -1), gn,
            state.reshape(nb, st_w), ex)
    return y.reshape(nb * t_new, RET_V), s.reshape(state.shape)


def kernel(x_prompt, x_sample, cache_k, cache_v, state_ret, g_ffn1_pre, g_ffn1_post, w1_gate, w1_up,
           w1_down, g_mix_pre, g_mix_post, w_in, gn_w, w_out, g_ffn2_pre, g_ffn2_post, w2_gate, w2_up,
           w2_down):
    b_p, s_p, _ = x_prompt.shape
    b_s, t_s, _ = x_sample.shape
    depth = w_in.shape[0]
    assert depth == 1 and b_p == 1
    keep = min(WIN_MAX, s_p)
    l = 0
    bf = lambda w: w[l].astype(BF16)
    vec = lambda g: g[l].reshape(1, -1)
    w1 = (bf(w1_gate), bf(w1_up), bf(w1_down))
    w2 = (bf(w2_gate), bf(w2_up), bf(w2_down))
    wi, wo = bf(w_in), bf(w_out)
    g1a, g1b, gma, gmb = vec(g_ffn1_pre), vec(g_ffn1_post), vec(g_mix_pre), vec(g_mix_post)
    g2a, g2b, gn = vec(g_ffn2_pre), vec(g_ffn2_post), vec(gn_w)

    n_s = b_s * t_s
    xs = x_sample.reshape(n_s, D_MODEL)
    h1s = _ffn(xs, g1a, g1b, *w1, n_s)
    qas, kks, vks, qrs, krs, vrs, grs = _mixin(
        h1s, gma, wi, jnp.full((1,), PAST_LEN), jnp.tile(jnp.arange(t_s), b_s), n_s, n_s, sample=True)

    tm = RUN
    xp = x_prompt.reshape(s_p, D_MODEL)
    h1, o_att_s = _ffn_with_sample_attn(xp, g1a, g1b, *w1, qas, kks, vks, cache_k[l], cache_v[l],
                                        t_s, PAST_LEN, tm)
    q4, k4, v4, q16, k16, v16, kk, vk, qr, kr, vr, gr = _mixin(
        h1, gma, wi, tm * jnp.arange(s_p // tm), jnp.arange(tm), keep, tm, sample=False)
    o_att = _attn_prompt(q4, k4, v4, q16, k16, v16)
    y_ret, st_p = _ret_prompt(qr, kr, vr, gr, gn, tm)
    y_prompt = _mixout_ffn(o_att, y_ret, h1, wo, gmb, g2a, g2b, *w2, tm)

    y_ret_s, st_s = _ret_sample(qrs, krs, vrs, grs, gn, state_ret[l], t_s)
    y_sample = _mixout_ffn(o_att_s, y_ret_s, h1s, wo, gmb, g2a, g2b, *w2, n_s)

    hd = (N_ATT_HEADS, HEAD_DIM)
    return (y_prompt.reshape(b_p, s_p, D_MODEL),
            y_sample.reshape(b_s, t_s, D_MODEL),
            kk.reshape(depth, b_p, keep, *hd),
            vk.reshape(depth, b_p, keep, *hd),
            st_p.reshape(depth, b_p, N_RET_HEADS, RET_DK, RET_DV),
            kks.reshape(depth, b_s, t_s, *hd),
            vks.reshape(depth, b_s, t_s, *hd),
            st_s.reshape(depth, b_s, N_RET_HEADS, RET_DK, RET_DV))
```

```python
import functools
import math

import numpy as np
import jax
import jax.numpy as jnp
from jax import lax
from jax.experimental import pallas as pl
from jax.experimental.pallas import tpu as pltpu

F32 = jnp.float32
BF16 = jnp.bfloat16

D_MODEL = 1024
D_FF = 2816
HEAD_DIM = 64
N_ATT_HEADS = 8
ATT_WIDTH = N_ATT_HEADS * HEAD_DIM
ROT_DIM = HEAD_DIM // 4
ROPE_THETA = 500000.0
WIN_MAX = 2048
PAST_LEN = 8192
BRANCHES = ((128, 1), (512, 4), (2048, 16))
N_RET_HEADS = 4
RET_DK = 64
RET_DV = 128
RET_QK = N_RET_HEADS * RET_DK
RET_V = N_RET_HEADS * RET_DV
RET_THETA = 10000.0
RET_CHUNK = 128
IN_WIDTH = 3 * ATT_WIDTH + 2 * RET_QK + 2 * RET_V
NORM_EPS = 1e-6
NEG = -1e30

LANES = 128
QB = 128
SUPER = 2048
RUN = 512
VMEM_LIMIT = 56 * 1024 * 1024
LOG_G = tuple(math.log1p(-2.0 ** (-5.0 - h)) for h in range(N_RET_HEADS))
LOG2_E = math.log2(math.e)


def _full_spec(shape):
    nd = len(shape)
    return pl.BlockSpec(shape, lambda *_: (0,) * nd)


def _resident_spec(shape):
    nd = len(shape)
    return pl.BlockSpec(shape, lambda *_: (0,) * nd, pipeline_mode=pl.Buffered(1))


def _params(n_axes):
    return pltpu.CompilerParams(dimension_semantics=("arbitrary",) * n_axes,
                                vmem_limit_bytes=VMEM_LIMIT)


def _rms(x, g):
    return x * lax.rsqrt(jnp.mean(x * x, axis=-1, keepdims=True) + NORM_EPS) * g


def _silu(x):
    return x / (1.0 + jnp.exp(-x))


def _dot(a, b):
    return jnp.dot(a, b, preferred_element_type=F32)


def _dot_nt(a, b):
    return lax.dot_general(a, b, (((1,), (1,)), ((), ())), preferred_element_type=F32)


def _dot_tn(a, b):
    return lax.dot_general(a, b, (((0,), (0,)), ((), ())), preferred_element_type=F32)


FF_CHUNK = 512


def _swiglu(u, wg_ref, wu_ref, wd_ref):
    acc = None
    for c0 in range(0, D_FF, FF_CHUNK):
        c1 = min(c0 + FF_CHUNK, D_FF)
        g = _dot(u, wg_ref[:, c0:c1])
        up = _dot(u, wu_ref[:, c0:c1])
        h = (_silu(g) * up).astype(BF16)
        d = _dot(h, wd_ref[c0:c1, :])
        acc = d if acc is None else acc + d
    return acc


def _ffn_kernel(x_ref, gpre_ref, gpost_ref, wg_ref, wu_ref, wd_ref, o_ref):
    x = x_ref[...]
    u = _rms(x, gpre_ref[...]).astype(BF16)
    y = _swiglu(u, wg_ref, wu_ref, wd_ref)
    o_ref[...] = x + 0.5 * _rms(y, gpost_ref[...])


def _mixout_ffn_kernel(oa_ref, yr_ref, h_ref, wo_ref, gmb_ref, gpre_ref, gpost_ref,
                       wg_ref, wu_ref, wd_ref, o_ref):
    mixed = jnp.concatenate([oa_ref[...].astype(BF16), yr_ref[...].astype(BF16)], axis=1)
    x = h_ref[...] + _rms(_dot(mixed, wo_ref[...]), gmb_ref[...])
    u = _rms(x, gpre_ref[...]).astype(BF16)
    y = _swiglu(u, wg_ref, wu_ref, wd_ref)
    o_ref[...] = x + 0.5 * _rms(y, gpost_ref[...])


def _ffn_specs():
    return [_full_spec((1, D_MODEL)), _full_spec((1, D_MODEL)),
            _resident_spec((D_MODEL, D_FF)), _resident_spec((D_MODEL, D_FF)),
            _resident_spec((D_FF, D_MODEL))]


def _ffn(x, gpre, gpost, wg, wu, wd, tm):
    t = x.shape[0]
    row = pl.BlockSpec((tm, D_MODEL), lambda i: (i, 0))
    return pl.pallas_call(
        _ffn_kernel, grid=(t // tm,),
        in_specs=[row] + _ffn_specs(), out_specs=row,
        out_shape=jax.ShapeDtypeStruct((t, D_MODEL), F32),
        compiler_params=_params(1), name="ffn")(x, gpre, gpost, wg, wu, wd)


def _mixout_ffn(oa, yr, h, wo, gmb, gpre, gpost, wg, wu, wd, tm):
    t = h.shape[0]
    row = pl.BlockSpec((tm, D_MODEL), lambda i: (i, 0))
    half = pl.BlockSpec((tm, ATT_WIDTH), lambda i: (i, 0))
    return pl.pallas_call(
        _mixout_ffn_kernel, grid=(t // tm,),
        in_specs=[half, half, row, _resident_spec((D_MODEL, D_MODEL)), _full_spec((1, D_MODEL))]
        + _ffn_specs(),
        out_specs=row, out_shape=jax.ShapeDtypeStruct((t, D_MODEL), F32),
        compiler_params=_params(1), name="mixout_ffn")(oa, yr, h, wo, gmb, gpre, gpost, wg, wu, wd)


def _rope_consts(base_pos, off_pos, rot_dim, theta, head_dim):
    half = rot_dim // 2
    inv = theta ** (-jnp.arange(half, dtype=F32) * (2.0 / rot_dim))
    lane = np.arange(LANES) % head_dim
    inv_l = inv[lane % half][None, :]
    first = jnp.asarray(lane < half, F32)[None, :]
    second = jnp.asarray((lane >= half) & (lane < rot_dim), F32)[None, :]
    rot = first + second
    a = base_pos.astype(F32)[:, None] * inv_l
    b = off_pos.astype(F32)[:, None] * inv_l
    cb, sb = jnp.cos(b), jnp.sin(b)
    base = jnp.stack([jnp.cos(a), jnp.sin(a)], axis=1)
    off = jnp.stack([cb * rot, sb * rot, jnp.broadcast_to(1.0 - rot, cb.shape),
                     -cb * first, -sb * first, cb * second, sb * second])
    return base, off


def _rope_fill(base_ref, off_ref, tab_s):
    ca = base_ref[0:1, :]
    sa = base_ref[1:2, :]
    tab_s[0] = ca * off_ref[0] - sa * off_ref[1] + off_ref[2]
    tab_s[1] = sa * off_ref[3] + ca * off_ref[4]
    tab_s[2] = sa * off_ref[5] + ca * off_ref[6]


def _rope(x, tab_ref, half):
    return (x * tab_ref[0] + pltpu.roll(x, LANES - half, 1) * tab_ref[1]
            + pltpu.roll(x, half, 1) * tab_ref[2])


def _lanes(r, s):
    return slice(r * ATT_WIDTH + s * LANES, r * ATT_WIDTH + (s + 1) * LANES)


def _emit_dilated(nat_s, o4_ref, o16_ref):
    for s in range(ATT_WIDTH // LANES):
        for r in range(4):
            o4_ref[:, _lanes(r, s)] = nat_s[s, pl.ds(r, RUN // 4, stride=4), :].astype(BF16)
        for r in range(16):
            o16_ref[:, _lanes(r, s)] = nat_s[s, pl.ds(r, RUN // 16, stride=16), :].astype(BF16)


def _mixin_kernel(h_ref, g_ref, w_ref, ba_ref, oa_ref, br_ref, or_ref, *refs, sample):
    if sample:
        qa_ref, kk_ref, vk_ref, qr_ref, kr_ref, vr_ref, gr_ref, ta_ref, tr_ref = refs
    else:
        (gn_ref, q4_ref, k4_ref, v4_ref, q16_ref, k16_ref, v16_ref, kk_ref, vk_ref, y_ref, st_ref,
         ta_ref, tr_ref, nat_s, qr_ref, kr_ref, vr_ref, gr_ref, dec_s, wt_s, wh_s, st_s) = refs
        _ret_init(dec_s, wt_s, wh_s, st_s)
    _rope_fill(ba_ref, oa_ref, ta_ref)
    _rope_fill(br_ref, or_ref, tr_ref)
    u = _rms(h_ref[...], g_ref[...]).astype(BF16)
    scale = HEAD_DIM ** -0.5 * LOG2_E
    nslab = ATT_WIDTH // LANES

    q = _dot(u, w_ref[:, 0:ATT_WIDTH])
    for s in range(nslab):
        sl = slice(s * LANES, (s + 1) * LANES)
        qs = _rope(q[:, sl], ta_ref, ROT_DIM // 2) * scale
        if sample:
            qa_ref[:, sl] = qs
        else:
            nat_s[s] = qs
    if not sample:
        _emit_dilated(nat_s, q4_ref, q16_ref)

    k = _dot(u, w_ref[:, ATT_WIDTH:2 * ATT_WIDTH])
    for s in range(nslab):
        sl = slice(s * LANES, (s + 1) * LANES)
        ks = _rope(k[:, sl], ta_ref, ROT_DIM // 2)
        if not sample:
            nat_s[s] = ks
        kk_ref[:, sl] = ks
    if not sample:
        _emit_dilated(nat_s, k4_ref, k16_ref)

    v = _dot(u, w_ref[:, 2 * ATT_WIDTH:3 * ATT_WIDTH])
    if not sample:
        for s in range(nslab):
            nat_s[s] = v[:, s * LANES:(s + 1) * LANES]
        _emit_dilated(nat_s, v4_ref, v16_ref)
    vk_ref[...] = v

    o = 3 * ATT_WIDTH
    qk = _dot(u, w_ref[:, o:o + 2 * RET_QK])
    for s in range(RET_QK // LANES):
        sl = slice(s * LANES, (s + 1) * LANES)
        qr_ref[:, sl] = _rope(qk[:, sl], tr_ref, RET_DK // 2).astype(qr_ref.dtype)
        sk = slice(RET_QK + s * LANES, RET_QK + (s + 1) * LANES)
        kr_ref[:, sl] = (_rope(qk[:, sk], tr_ref, RET_DK // 2) * (RET_DK ** -0.5)).astype(kr_ref.dtype)
    o += 2 * RET_QK
    vr_ref[...] = _dot(u, w_ref[:, o:o + RET_V]).astype(vr_ref.dtype)
    o += RET_V
    gr_ref[...] = _dot(u, w_ref[:, o:o + RET_V])
    if not sample:
        _ret_body(qr_ref, kr_ref, vr_ref, gr_ref, gn_ref, y_ref, st_ref, dec_s, wt_s, wh_s, st_s,
                  qr_ref.shape[0] // RET_CHUNK)


def _mixin(h, g, w, base_pos, off_pos, keep_rows, tm, sample, gn=None):
    t = h.shape[0]
    nt = t // tm
    first_keep = nt - keep_rows // tm
    base_a, off_a = _rope_consts(base_pos, off_pos, ROT_DIM, ROPE_THETA, HEAD_DIM)
    base_r, off_r = _rope_consts(base_pos, off_pos, RET_DK, RET_THETA, RET_DK)

    def row(w_):
        return pl.BlockSpec((tm, w_), lambda i: (i, 0))

    keep_spec = pl.BlockSpec((tm, ATT_WIDTH), lambda i: (jnp.maximum(i - first_keep, 0), 0))
    keep_shape = ((keep_rows, ATT_WIDTH), F32)
    base_spec = pl.BlockSpec((None, 2, LANES), lambda i: (i, 0, 0))
    off_spec = _resident_spec((7, tm, LANES))
    tab = pltpu.VMEM((3, tm, LANES), F32)
    ins = [h, g, w, base_a, off_a, base_r, off_r]
    in_specs = [row(D_MODEL), _full_spec((1, D_MODEL)), _resident_spec((D_MODEL, IN_WIDTH)),
                base_spec, off_spec, base_spec, off_spec]
    if sample:
        specs = [row(ATT_WIDTH), keep_spec, keep_spec, row(RET_QK), row(RET_QK), row(RET_V), row(RET_V)]
        shapes = [((t, ATT_WIDTH), F32), keep_shape, keep_shape,
                  ((t, RET_QK), F32), ((t, RET_QK), F32), ((t, RET_V), F32), ((t, RET_V), F32)]
        scratch = [tab, tab]
    else:
        assert tm == RUN and t % SUPER == 0
        rps = SUPER // RUN
        ins.append(gn)
        in_specs.append(_full_spec((1, RET_V)))
        s4 = pl.BlockSpec((None, RUN // 4, 4 * ATT_WIDTH), lambda i: (i, 0, 0))
        s16 = pl.BlockSpec((None, RUN // 16, 16 * ATT_WIDTH), lambda i: (i // rps, i % rps, 0))
        st_shape = (N_RET_HEADS // 2, 2 * RET_DK, RET_DV)
        specs = [s4] * 3 + [s16] * 3 + [keep_spec, keep_spec, row(RET_V), _full_spec(st_shape)]
        shapes = ([((t // RUN, RUN // 4, 4 * ATT_WIDTH), BF16)] * 3
                  + [((t // SUPER, SUPER // 16, 16 * ATT_WIDTH), BF16)] * 3
                  + [keep_shape, keep_shape, ((t, RET_V), BF16), (st_shape, F32)])
        dec = pltpu.VMEM((N_RET_HEADS, RET_CHUNK, RET_CHUNK), F32)
        scratch = [tab, tab, pltpu.VMEM((ATT_WIDTH // LANES, RUN, LANES), F32),
                   pltpu.VMEM((tm, RET_QK), BF16), pltpu.VMEM((tm, RET_QK), BF16),
                   pltpu.VMEM((tm, RET_V), BF16), pltpu.VMEM((tm, RET_V), F32),
                   dec, dec, dec, pltpu.VMEM(st_shape, F32)]
    return pl.pallas_call(
        functools.partial(_mixin_kernel, sample=sample), grid=(nt,),
        in_specs=in_specs, out_specs=specs,
        out_shape=[jax.ShapeDtypeStruct(s, d) for s, d in shapes],
        scratch_shapes=scratch, compiler_params=_params(1), name="mix_in")(*ins)


def _attn_bias_tables():
    a = np.arange(QB)[:, None]
    c = np.arange(2 * QB)[None, :]
    steps = BRANCHES[0][0]
    dist = QB + a - c
    band = (dist >= 0) & (dist <= steps)
    cur = c >= QB
    tok_q = 4 * (a % 32) + a // 32
    cc = c % QB
    tok_k = 4 * (cc % 32) + cc // 32 + QB * (c // QB) - QB
    dist2 = tok_q - tok_k
    band2 = (dist2 >= 0) & (dist2 <= steps)
    masks = np.stack([band, band & cur, band2, band2 & cur])
    return np.where(masks, 0.0, NEG).astype(np.float32)


def _attn_qblock(get_q, get_kc, get_kp, get_vc, get_vp, bias, consts):
    head_a, head_a_win, ones_a, ones_b = consts
    out = []
    for s in range(ATT_WIDTH // LANES):
        q2 = get_q(s)
        kwin = jnp.concatenate([get_kp(s), get_kc(s)], axis=0)
        vwin = jnp.concatenate([get_vp(s), get_vc(s)], axis=0)
        zq = jnp.zeros_like(q2)
        qq = jnp.concatenate([jnp.where(head_a, q2, zq), jnp.where(head_a, zq, q2)], axis=0)
        sc = _dot_nt(qq, kwin)
        s_a = sc[:QB] + bias
        s_b = sc[QB:] + bias
        m_a = jnp.max(s_a, axis=-1, keepdims=True)
        m_b = jnp.max(s_b, axis=-1, keepdims=True)
        p = jnp.concatenate([jnp.exp2(s_a - m_a).astype(BF16), jnp.exp2(s_b - m_b).astype(BF16)],
                            axis=1)
        zv = jnp.zeros_like(vwin)
        w = jnp.concatenate(
            [jnp.concatenate([jnp.where(head_a_win, vwin, zv), ones_a], axis=1),
             jnp.concatenate([jnp.where(head_a_win, zv, vwin), ones_b], axis=1)], axis=0)
        r = _dot(p, w)
        out.append((jnp.where(head_a, m_a, m_b), r[:, LANES:], r[:, :LANES]))
    return out


def _merge(old, new):
    mo, lo, ao = old
    m2, l2, a2 = new
    mn = jnp.maximum(mo, m2)
    eo = jnp.exp2(mo - mn)
    e2 = jnp.exp2(m2 - mn)
    return mn, eo * lo + e2 * l2, eo * ao + e2 * a2


def _attn_kernel(q16, k16c, k16p, v16c, v16p, q4, k4c, k4p, v4c, v4p, bias_ref, o_ref,
                 m_s, l_s, a_s, nat_s):
    sb = pl.program_id(0)
    ph = pl.program_id(1)
    j = pl.program_id(2)
    lane = lax.broadcasted_iota(jnp.int32, (QB, LANES), 1)
    head_a = lane < HEAD_DIM
    lane_w = lax.broadcasted_iota(jnp.int32, (2 * QB, LANES), 1)
    head_a_win = lane_w < HEAD_DIM
    ones_a = jnp.where(head_a_win, 1.0, 0.0).astype(BF16)
    ones_b = jnp.where(head_a_win, 0.0, 1.0).astype(BF16)
    consts = (head_a, head_a_win, ones_a, ones_b)
    nslab = ATT_WIDTH // LANES
    sub = QB // 4
    lanes = _lanes

    @pl.when(ph == 0)
    def _():
        bias = bias_ref[(sb == 0).astype(jnp.int32)]
        for r in range(4):
            res = _attn_qblock(lambda s: q16[:, lanes(r, s)], lambda s: k16c[:, lanes(r, s)],
                               lambda s: k16p[:, lanes(r, s)], lambda s: v16c[:, lanes(r, s)],
                               lambda s: v16p[:, lanes(r, s)], bias, consts)
            for s in range(nslab):
                for u in range(SUPER // RUN):
                    rows = pl.ds(u * RUN + r * QB + j, sub, stride=4)
                    for ref, val in zip((m_s, l_s, a_s), res[s]):
                        ref[s, rows, :] = val[u * sub:(u + 1) * sub]

    @pl.when(ph == 1)
    def _():
        bias = bias_ref[(sb * (SUPER // RUN) + j == 0).astype(jnp.int32)]
        for r in range(4):
            res = _attn_qblock(lambda s: q4[:, lanes(r, s)], lambda s: k4c[:, lanes(r, s)],
                               lambda s: k4p[:, lanes(r, s)], lambda s: v4c[:, lanes(r, s)],
                               lambda s: v4p[:, lanes(r, s)], bias, consts)
            rows = pl.ds(pl.multiple_of(j * RUN + r * QB, QB), QB)
            for s in range(nslab):
                mn, ln, an = _merge((m_s[s, rows, :], l_s[s, rows, :], a_s[s, rows, :]), res[s])
                m_s[s, rows, :] = mn
                l_s[s, rows, :] = ln
                a_s[s, rows, :] = an

    @pl.when(ph == 2)
    def _():
        first = sb * (SUPER // RUN) + j == 0
        for b in range(4):
            def cur(ref):
                return lambda s: jnp.concatenate(
                    [ref[b * sub:(b + 1) * sub, lanes(r, s)] for r in range(4)], axis=0)

            def prev(ref_c, ref_p):
                if b > 0:
                    return lambda s: jnp.concatenate(
                        [ref_c[(b - 1) * sub:b * sub, lanes(r, s)] for r in range(4)], axis=0)
                return lambda s: jnp.concatenate(
                    [ref_p[QB - sub:QB, lanes(r, s)] for r in range(4)], axis=0)

            if b == 0:
                bias = bias_ref[2 + first.astype(jnp.int32)]
            else:
                bias = bias_ref[2]
            res = _attn_qblock(cur(q4), cur(k4c), prev(k4c, k4p), cur(v4c), prev(v4c, v4p),
                               bias, consts)
            for s in range(nslab):
                for r in range(4):
                    rows = pl.ds(pl.multiple_of(j * RUN + r * QB + b * sub, sub), sub)
                    part = tuple(x[r * sub:(r + 1) * sub] for x in res[s])
                    _, ln, an = _merge((m_s[s, rows, :], l_s[s, rows, :], a_s[s, rows, :]), part)
                    nat_s[s, pl.ds(b * QB + r, sub, stride=4), :] = an / ln
        for s in range(nslab):
            o_ref[:, s * LANES:(s + 1) * LANES] = nat_s[s].astype(o_ref.dtype)


def _attn_prompt(q4, k4, v4, q16, k16, v16):
    nsb = q16.shape[0]
    s_len = nsb * SUPER
    rps = SUPER // RUN
    blk = (None, QB, 4 * ATT_WIDTH)

    def j16(ph, j):
        return jnp.where(ph == 0, j, rps - 1)

    def n4(sb, ph, j):
        return sb * rps + jnp.where(ph == 0, 0, j)

    c16 = pl.BlockSpec(blk, lambda sb, ph, j: (sb, 0, j16(ph, j)))
    p16 = pl.BlockSpec(blk, lambda sb, ph, j: (jnp.maximum(sb - 1, 0), 0, j16(ph, j)))
    c4 = pl.BlockSpec(blk, lambda sb, ph, j: (n4(sb, ph, j), 0, 0))
    p4 = pl.BlockSpec(blk, lambda sb, ph, j: (jnp.maximum(n4(sb, ph, j) - 1, 0), 0, 0))
    out = pl.BlockSpec((RUN, ATT_WIDTH), lambda sb, ph, j: (sb * rps + jnp.where(ph == 2, j, 0), 0))
    bias = jnp.asarray(_attn_bias_tables())
    nslab = ATT_WIDTH // LANES
    scratch = ([pltpu.VMEM((nslab, SUPER, LANES), F32) for _ in range(3)]
               + [pltpu.VMEM((nslab, RUN, LANES), F32)])
    return pl.pallas_call(
        _attn_kernel, grid=(nsb, 3, rps),
        in_specs=[c16, c16, p16, c16, p16, c4, c4, p4, c4, p4, _full_spec(bias.shape)],
        out_specs=out, out_shape=jax.ShapeDtypeStruct((s_len, ATT_WIDTH), BF16),
        scratch_shapes=scratch, compiler_params=_params(3), name="attn_prompt")(
            q16, k16, k16, v16, v16, q4, k4, k4, v4, v4, bias)


def _ret_init(dec_s, wt_s, wh_s, st_s):
    c = RET_CHUNK

    @pl.when(pl.program_id(0) == 0)
    def _():
        i = lax.broadcasted_iota(jnp.int32, (c, c), 0)
        jj = lax.broadcasted_iota(jnp.int32, (c, c), 1)
        diff = (i - jj).astype(F32)
        for h in range(N_RET_HEADS):
            dec_s[h] = jnp.where(diff >= 0, jnp.exp(diff * LOG_G[h]), 0.0)
            wt_s[h] = jnp.exp((c - 1.0 - i.astype(F32)) * LOG_G[h])
            wh_s[h] = jnp.exp((i.astype(F32) + 1.0) * LOG_G[h])
        st_s[...] = jnp.zeros_like(st_s)


def _ret_body(qr_ref, kr_ref, vr_ref, gr_ref, gn_ref, y_ref, s_ref, dec_s, wt_s, wh_s, st_s, chunks):
    c = RET_CHUNK
    lane = lax.broadcasted_iota(jnp.int32, (c, LANES), 1)
    head_a = lane < RET_DK
    row_a = lax.broadcasted_iota(jnp.int32, (LANES, RET_DV), 0) < RET_DK
    for ci in range(chunks):
        rows = slice(ci * c, (ci + 1) * c)
        for p in range(N_RET_HEADS // 2):
            sl = slice(p * LANES, (p + 1) * LANES)
            q2 = qr_ref[rows, sl]
            k2 = kr_ref[rows, sl]
            zq = jnp.zeros_like(q2)
            qq = jnp.concatenate([jnp.where(head_a, q2, zq), jnp.where(head_a, zq, q2)], axis=0)
            inner = _dot_nt(qq, k2)
            state = st_s[p]
            cross = _dot(qq, state.astype(BF16))
            vws = []
            for hh in range(2):
                h = 2 * p + hh
                hs = slice(h * RET_DV, (h + 1) * RET_DV)
                v = vr_ref[rows, hs]
                inn = (inner[hh * c:(hh + 1) * c] * dec_s[h]).astype(BF16)
                o = _dot(inn, v) + cross[hh * c:(hh + 1) * c] * wh_s[h]
                mu = jnp.mean(o, axis=-1, keepdims=True)
                xc = o - mu
                var = jnp.mean(xc * xc, axis=-1, keepdims=True)
                y = xc * lax.rsqrt(var + NORM_EPS) * gn_ref[:, hs]
                y_ref[rows, hs] = (_silu(gr_ref[rows, hs]) * y).astype(y_ref.dtype)
                vws.append((v.astype(F32) * wt_s[h]).astype(BF16))
            upd = _dot_tn(k2, jnp.concatenate(vws, axis=1))
            upd = jnp.where(row_a, upd[:, :RET_DV], upd[:, RET_DV:])
            gch = jnp.where(row_a, math.exp(c * LOG_G[2 * p]), math.exp(c * LOG_G[2 * p + 1]))
            st_s[p] = gch * state + upd

    @pl.when(pl.program_id(0) == pl.num_programs(0) - 1)
    def _():
        s_ref[...] = st_s[...]


S_ROWS = 8
S_HGRP = 4


def _sample_tables(t_new, past_len, wb):
    rows = np.concatenate([np.arange(wb), wb + np.arange(S_ROWS)])
    tab = np.zeros((S_ROWS, wb + S_ROWS), np.float32)
    for r in range(S_ROWS):
        i = r % t_new
        delta = wb + i - rows
        ok = (delta >= 0) & (past_len + i - delta >= 0) & (rows < wb + t_new)
        for window, dil in BRANCHES:
            tab[r] += ok & (delta % dil == 0) & (delta <= window)
    tab = np.tile(tab, (S_HGRP, 1))
    return tab[:, :wb], tab[:, wb:]


def _sample_attn_body(q_ref, kn_ref, vn_ref, kt_ref, vt_ref, cc_ref, cn_ref, o_ref):
    mult_c, mult_n = cc_ref[...], cn_ref[...]
    bias_c = jnp.where(mult_c > 0, 0.0, NEG)
    bias_n = jnp.where(mult_n > 0, 0.0, NEG)
    gw = S_HGRP * HEAD_DIM
    lane_head = lax.broadcasted_iota(jnp.int32, (S_ROWS, gw), 1) // HEAD_DIM
    for g in range(N_ATT_HEADS // S_HGRP):
        gs = slice(g * gw, (g + 1) * gw)
        q8 = q_ref[:, gs]
        q = jnp.concatenate([jnp.where(lane_head == h, q8, 0.0) for h in range(S_HGRP)],
                            axis=0).astype(BF16)
        s_c = _dot(q, kt_ref[g].astype(BF16)) + bias_c
        s_n = _dot_nt(q, kn_ref[:, gs].astype(BF16)) + bias_n
        m = jnp.maximum(jnp.max(s_c, axis=-1, keepdims=True), jnp.max(s_n, axis=-1, keepdims=True))
        p_c = jnp.exp2(s_c - m) * mult_c
        p_n = jnp.exp2(s_n - m) * mult_n
        l = jnp.sum(p_c, axis=-1, keepdims=True) + jnp.sum(p_n, axis=-1, keepdims=True)
        o = (_dot_nt(p_c.astype(BF16), vt_ref[g].astype(BF16))
             + _dot(p_n.astype(BF16), vn_ref[:, gs].astype(BF16))) / l
        o8 = jnp.zeros((S_ROWS, gw), F32)
        for h in range(S_HGRP):
            o8 = jnp.where(lane_head == h, o[h * S_ROWS:(h + 1) * S_ROWS], o8)
        o_ref[:, gs] = o8


FF_SPLITS = ((0, 768), (768, 1536), (1536, 2304), (2304, D_FF))


def _ffn_attn_kernel(x_ref, gpre_ref, gpost_ref, wg_ref, wu_ref, wd_ref,
                     q_ref, kn_ref, vn_ref, kt_ref, vt_ref, cc_ref, cn_ref,
                     o_ref, oa_ref, u_s, acc_s):
    j = pl.program_id(1)
    last = len(FF_SPLITS) - 1
    for k, (c0, c1) in enumerate(FF_SPLITS):
        @pl.when(j == k)
        def _():
            if k == 0:
                u = _rms(x_ref[...], gpre_ref[...]).astype(BF16)
                u_s[...] = u
            else:
                u = u_s[...]
            h = (_silu(_dot(u, wg_ref[:, c0:c1])) * _dot(u, wu_ref[:, c0:c1])).astype(BF16)
            d = _dot(h, wd_ref[c0:c1, :])
            if k == 0:
                acc_s[...] = d
            elif k < last:
                acc_s[...] += d
            else:
                o_ref[...] = x_ref[...] + 0.5 * _rms(acc_s[...] + d, gpost_ref[...])
            _sample_attn_body(q_ref, kn_ref, vn_ref, kt_ref, vt_ref, cc_ref, cn_ref, oa_ref)


def _ffn_with_sample_attn(x, gpre, gpost, wg, wu, wd, q, kn, vn, cache_k, cache_v, t_new, past_len, tm):
    t = x.shape[0]
    nb, wb = cache_k.shape[0], cache_k.shape[1]
    nsplit = len(FF_SPLITS)
    assert t_new <= S_ROWS and nb == (t // tm) * nsplit

    def rows(a):
        return jnp.pad(a.reshape(nb, t_new, ATT_WIDTH), ((0, 0), (0, S_ROWS - t_new), (0, 0)))

    def grouped(c):
        return c.transpose(0, 2, 3, 1).reshape(nb, N_ATT_HEADS // S_HGRP, S_HGRP * HEAD_DIM, wb)

    row = pl.BlockSpec((tm, D_MODEL), lambda i, j: (i, 0))
    small = pl.BlockSpec((None, S_ROWS, ATT_WIDTH), lambda i, j: (i * nsplit + j, 0, 0))
    big = pl.BlockSpec((None, N_ATT_HEADS // S_HGRP, S_HGRP * HEAD_DIM, wb),
                       lambda i, j: (i * nsplit + j, 0, 0, 0))
    tabs = [jnp.asarray(a) for a in _sample_tables(t_new, past_len, wb)]
    h1, o = pl.pallas_call(
        _ffn_attn_kernel, grid=(t // tm, nsplit),
        in_specs=[row] + _ffn_specs() + [small, small, small, big, big]
        + [_full_spec(a.shape) for a in tabs],
        out_specs=[row, small],
        out_shape=[jax.ShapeDtypeStruct((t, D_MODEL), F32),
                   jax.ShapeDtypeStruct((nb, S_ROWS, ATT_WIDTH), F32)],
        scratch_shapes=[pltpu.VMEM((tm, D_MODEL), BF16), pltpu.VMEM((tm, D_MODEL), F32)],
        compiler_params=_params(2), name="ffn_attn_sample")(
            x, gpre, gpost, wg, wu, wd, rows(q), rows(kn), rows(vn),
            grouped(cache_k), grouped(cache_v), *tabs)
    return h1, o[:, :t_new].reshape(nb * t_new, ATT_WIDTH)


def _sample_ret_kernel(q_ref, k_ref, v_ref, g_ref, gn_ref, st_ref, ex_ref, y_ref, so_ref, *, t_new):
    hid = lax.broadcasted_iota(jnp.int32, (1, RET_V), 1) // RET_DV

    def headvec(fn):
        out = jnp.full((1, RET_V), fn(N_RET_HEADS - 1), F32)
        for h in range(N_RET_HEADS - 2, -1, -1):
            out = jnp.where(hid == h, fn(h), out)
        return out

    q = [q_ref[:, i * RET_QK:(i + 1) * RET_QK] for i in range(t_new)]
    k = [k_ref[:, i * RET_QK:(i + 1) * RET_QK] for i in range(t_new)]
    v = [v_ref[:, i * RET_V:(i + 1) * RET_V] for i in range(t_new)]
    vw = [v[jj] * headvec(lambda h: math.exp((t_new - 1.0 - jj) * LOG_G[h])) for jj in range(t_new)]
    ex = ex_ref[...]

    o = []
    for i in range(t_new):
        acc = None
        for jj in range(i + 1):
            sij = _dot((q[i] * k[jj]).astype(BF16), ex)
            term = sij * headvec(lambda h: math.exp((i - jj) * LOG_G[h])) * v[jj]
            acc = term if acc is None else acc + term
        o.append(acc)

    cross = [[None] * N_RET_HEADS for _ in range(t_new)]
    for h in range(N_RET_HEADS):
        hs = slice(h * RET_DV, (h + 1) * RET_DV)
        g_t = math.exp(t_new * LOG_G[h])
        for d in range(RET_DK):
            col = h * RET_DK + d
            cs = slice(col * RET_DV, (col + 1) * RET_DV)
            st = st_ref[:, cs]
            new = g_t * st
            for i in range(t_new):
                term = q[i][:, col:col + 1] * st
                cross[i][h] = term if cross[i][h] is None else cross[i][h] + term
                new = new + k[i][:, col:col + 1] * vw[i][:, hs]
            so_ref[:, cs] = new

    for i in range(t_new):
        for h in range(N_RET_HEADS):
            hs = slice(h * RET_DV, (h + 1) * RET_DV)
            oh = o[i][:, hs] + cross[i][h] * math.exp((i + 1.0) * LOG_G[h])
            mu = jnp.mean(oh, axis=-1, keepdims=True)
            xc = oh - mu
            var = jnp.mean(xc * xc, axis=-1, keepdims=True)
            y = xc * lax.rsqrt(var + NORM_EPS) * gn_ref[:, hs]
            gs = slice(i * RET_V + h * RET_DV, i * RET_V + (h + 1) * RET_DV)
            y_ref[:, gs] = _silu(g_ref[:, gs]) * y


def _ret_sample(qr, kr, vr, gr, gn, state, t_new, nb_blk=16):
    nb = state.shape[0]
    st_w = N_RET_HEADS * RET_DK * RET_DV
    ex = (np.arange(RET_QK)[:, None] // RET_DK == np.arange(RET_V)[None, :] // RET_DV)
    ex = jnp.asarray(ex.astype(np.float32), dtype=BF16)

    def blk(w_):
        return pl.BlockSpec((nb_blk, w_), lambda i: (i, 0))

    y, s = pl.pallas_call(
        functools.partial(_sample_ret_kernel, t_new=t_new), grid=(nb // nb_blk,),
        in_specs=[blk(t_new * RET_QK), blk(t_new * RET_QK), blk(t_new * RET_V), blk(t_new * RET_V),
                  _full_spec((1, RET_V)), blk(st_w), _full_spec(ex.shape)],
        out_specs=[blk(t_new * RET_V), blk(st_w)],
        out_shape=[jax.ShapeDtypeStruct((nb, t_new * RET_V), F32), jax.ShapeDtypeStruct((nb, st_w), F32)],
        compiler_params=_params(1), name="ret_sample")(
            qr.reshape(nb, -1), kr.reshape(nb, -1), vr.reshape(nb, -1), gr.reshape(nb, -1), gn,
            state.reshape(nb, st_w), ex)
    return y.reshape(nb * t_new, RET_V), s.reshape(state.shape)


def kernel(x_prompt, x_sample, cache_k, cache_v, state_ret, g_ffn1_pre, g_ffn1_post, w1_gate, w1_up,
           w1_down, g_mix_pre, g_mix_post, w_in, gn_w, w_out, g_ffn2_pre, g_ffn2_post, w2_gate, w2_up,
           w2_down):
    b_p, s_p, _ = x_prompt.shape
    b_s, t_s, _ = x_sample.shape
    depth = w_in.shape[0]
    assert depth == 1 and b_p == 1
    keep = min(WIN_MAX, s_p)
    l = 0
    bf = lambda w: w[l].astype(BF16)
    vec = lambda g: g[l].reshape(1, -1)
    w1 = (bf(w1_gate), bf(w1_up), bf(w1_down))
    w2 = (bf(w2_gate), bf(w2_up), bf(w2_down))
    wi, wo = bf(w_in), bf(w_out)
    g1a, g1b, gma, gmb = vec(g_ffn1_pre), vec(g_ffn1_post), vec(g_mix_pre), vec(g_mix_post)
    g2a, g2b, gn = vec(g_ffn2_pre), vec(g_ffn2_post), vec(gn_w)

    n_s = b_s * t_s
    xs = x_sample.reshape(n_s, D_MODEL)
    h1s = _ffn(xs, g1a, g1b, *w1, n_s)
    qas, kks, vks, qrs, krs, vrs, grs = _mixin(
        h1s, gma, wi, jnp.full((1,), PAST_LEN), jnp.tile(jnp.arange(t_s), b_s), n_s, n_s, sample=True)

    tm = RUN
    xp = x_prompt.reshape(s_p, D_MODEL)
    h1, o_att_s = _ffn_with_sample_attn(xp, g1a, g1b, *w1, qas, kks, vks, cache_k[l], cache_v[l],
                                        t_s, PAST_LEN, tm)
    q4, k4, v4, q16, k16, v16, kk, vk, y_ret, st_p = _mixin(
        h1, gma, wi, tm * jnp.arange(s_p // tm), jnp.arange(tm), keep, tm, sample=False, gn=gn)
    o_att = _attn_prompt(q4, k4, v4, q16, k16, v16)
    y_prompt = _mixout_ffn(o_att, y_ret, h1, wo, gmb, g2a, g2b, *w2, tm)

    y_ret_s, st_s = _ret_sample(qrs, krs, vrs, grs, gn, state_ret[l], t_s)
    y_sample = _mixout_ffn(o_att_s, y_ret_s, h1s, wo, gmb, g2a, g2b, *w2, n_s)

    hd = (N_ATT_HEADS, HEAD_DIM)
    return (y_prompt.reshape(b_p, s_p, D_MODEL),
            y_sample.reshape(b_s, t_s, D_MODEL),
            kk.reshape(depth, b_p, keep, *hd),
            vk.reshape(depth, b_p, keep, *hd),
            st_p.reshape(depth, b_p, N_RET_HEADS, RET_DK, RET_DV),
            kks.reshape(depth, b_s, t_s, *hd),
            vks.reshape(depth, b_s, t_s, *hd),
            st_s.reshape(depth, b_s, N_RET_HEADS, RET_DK, RET_DV))
```

```python
import functools
import math

import numpy as np
import jax
import jax.numpy as jnp
from jax import lax
from jax.experimental import pallas as pl
from jax.experimental.pallas import tpu as pltpu

F32 = jnp.float32
BF16 = jnp.bfloat16

D_MODEL = 1024
D_FF = 2816
HEAD_DIM = 64
N_ATT_HEADS = 8
ATT_WIDTH = N_ATT_HEADS * HEAD_DIM
ROT_DIM = HEAD_DIM // 4
ROPE_THETA = 500000.0
WIN_MAX = 2048
PAST_LEN = 8192
BRANCHES = ((128, 1), (512, 4), (2048, 16))
N_RET_HEADS = 4
RET_DK = 64
RET_DV = 128
RET_QK = N_RET_HEADS * RET_DK
RET_V = N_RET_HEADS * RET_DV
RET_THETA = 10000.0
RET_CHUNK = 128
IN_WIDTH = 3 * ATT_WIDTH + 2 * RET_QK + 2 * RET_V
NORM_EPS = 1e-6
NEG = -1e30

LANES = 128
QB = 128
SUPER = 2048
RUN = 512
VMEM_LIMIT = 56 * 1024 * 1024
LOG_G = tuple(math.log1p(-2.0 ** (-5.0 - h)) for h in range(N_RET_HEADS))
LOG2_E = math.log2(math.e)


def _full_spec(shape):
    nd = len(shape)
    return pl.BlockSpec(shape, lambda *_: (0,) * nd)


def _resident_spec(shape):
    nd = len(shape)
    return pl.BlockSpec(shape, lambda *_: (0,) * nd, pipeline_mode=pl.Buffered(1))


def _params(n_axes):
    return pltpu.CompilerParams(dimension_semantics=("arbitrary",) * n_axes,
                                vmem_limit_bytes=VMEM_LIMIT)


def _rms(x, g):
    return x * lax.rsqrt(jnp.mean(x * x, axis=-1, keepdims=True) + NORM_EPS) * g


def _silu(x):
    return x / (1.0 + jnp.exp(-x))


def _dot(a, b):
    return jnp.dot(a, b, preferred_element_type=F32)


def _dot_nt(a, b):
    return lax.dot_general(a, b, (((1,), (1,)), ((), ())), preferred_element_type=F32)


def _dot_tn(a, b):
    return lax.dot_general(a, b, (((0,), (0,)), ((), ())), preferred_element_type=F32)


FF_CHUNK = 512


def _swiglu(u, wg_ref, wu_ref, wd_ref):
    acc = None
    for c0 in range(0, D_FF, FF_CHUNK):
        c1 = min(c0 + FF_CHUNK, D_FF)
        g = _dot(u, wg_ref[:, c0:c1])
        up = _dot(u, wu_ref[:, c0:c1])
        h = (_silu(g) * up).astype(BF16)
        d = _dot(h, wd_ref[c0:c1, :])
        acc = d if acc is None else acc + d
    return acc


def _ffn_kernel(x_ref, gpre_ref, gpost_ref, wg_ref, wu_ref, wd_ref, o_ref):
    x = x_ref[...]
    u = _rms(x, gpre_ref[...]).astype(BF16)
    y = _swiglu(u, wg_ref, wu_ref, wd_ref)
    o_ref[...] = x + 0.5 * _rms(y, gpost_ref[...])


def _mixout_ffn_kernel(oa_ref, yr_ref, h_ref, wo_ref, gmb_ref, gpre_ref, gpost_ref,
                       wg_ref, wu_ref, wd_ref, o_ref):
    mixed = jnp.concatenate([oa_ref[...].astype(BF16), yr_ref[...].astype(BF16)], axis=1)
    x = h_ref[...] + _rms(_dot(mixed, wo_ref[...]), gmb_ref[...])
    u = _rms(x, gpre_ref[...]).astype(BF16)
    y = _swiglu(u, wg_ref, wu_ref, wd_ref)
    o_ref[...] = x + 0.5 * _rms(y, gpost_ref[...])


def _ffn_specs():
    return [_full_spec((1, D_MODEL)), _full_spec((1, D_MODEL)),
            _resident_spec((D_MODEL, D_FF)), _resident_spec((D_MODEL, D_FF)),
            _resident_spec((D_FF, D_MODEL))]


def _ffn(x, gpre, gpost, wg, wu, wd, tm):
    t = x.shape[0]
    row = pl.BlockSpec((tm, D_MODEL), lambda i: (i, 0))
    return pl.pallas_call(
        _ffn_kernel, grid=(t // tm,),
        in_specs=[row] + _ffn_specs(), out_specs=row,
        out_shape=jax.ShapeDtypeStruct((t, D_MODEL), F32),
        compiler_params=_params(1), name="ffn")(x, gpre, gpost, wg, wu, wd)


def _mixout_ffn(oa, yr, h, wo, gmb, gpre, gpost, wg, wu, wd, tm):
    t = h.shape[0]
    row = pl.BlockSpec((tm, D_MODEL), lambda i: (i, 0))
    half = pl.BlockSpec((tm, ATT_WIDTH), lambda i: (i, 0))
    return pl.pallas_call(
        _mixout_ffn_kernel, grid=(t // tm,),
        in_specs=[half, half, row, _resident_spec((D_MODEL, D_MODEL)), _full_spec((1, D_MODEL))]
        + _ffn_specs(),
        out_specs=row, out_shape=jax.ShapeDtypeStruct((t, D_MODEL), F32),
        compiler_params=_params(1), name="mixout_ffn")(oa, yr, h, wo, gmb, gpre, gpost, wg, wu, wd)


def _rope_consts(base_pos, off_pos, rot_dim, theta, head_dim):
    half = rot_dim // 2
    inv = theta ** (-np.arange(half, dtype=np.float64) * (2.0 / rot_dim))
    lane = np.arange(LANES) % head_dim
    inv_l = inv[lane % half][None, :]
    first = (lane < half).astype(np.float64)[None, :]
    second = ((lane >= half) & (lane < rot_dim)).astype(np.float64)[None, :]
    rot = first + second
    a = np.asarray(base_pos, np.float64)[:, None] * inv_l
    b = np.asarray(off_pos, np.float64)[:, None] * inv_l
    cb, sb = np.cos(b), np.sin(b)
    base = np.stack([np.cos(a), np.sin(a)], axis=1)
    off = np.stack([cb * rot, sb * rot, np.broadcast_to(1.0 - rot, cb.shape),
                    -cb * first, -sb * first, cb * second, sb * second])
    return jnp.asarray(base, F32), jnp.asarray(off, F32)


def _rope_fill(base_ref, off_ref, tab_s):
    ca = base_ref[0:1, :]
    sa = base_ref[1:2, :]
    tab_s[0] = ca * off_ref[0] - sa * off_ref[1] + off_ref[2]
    tab_s[1] = sa * off_ref[3] + ca * off_ref[4]
    tab_s[2] = sa * off_ref[5] + ca * off_ref[6]


def _rope(x, tab_ref, half):
    return (x * tab_ref[0] + pltpu.roll(x, LANES - half, 1) * tab_ref[1]
            + pltpu.roll(x, half, 1) * tab_ref[2])


def _lanes(r, s):
    return slice(r * ATT_WIDTH + s * LANES, r * ATT_WIDTH + (s + 1) * LANES)


def _emit_dilated(nat_s, o4_ref, o16_ref):
    for s in range(ATT_WIDTH // LANES):
        for r in range(4):
            o4_ref[:, _lanes(r, s)] = nat_s[s, pl.ds(r, RUN // 4, stride=4), :].astype(BF16)
        for r in range(16):
            o16_ref[:, _lanes(r, s)] = nat_s[s, pl.ds(r, RUN // 16, stride=16), :].astype(BF16)


def _mixin_kernel(h_ref, g_ref, w_ref, ba_ref, oa_ref, br_ref, or_ref, *refs, sample):
    if sample:
        qa_ref, kk_ref, vk_ref, qr_ref, kr_ref, vr_ref, gr_ref, ta_ref, tr_ref = refs
    else:
        (gn_ref, q4_ref, k4_ref, v4_ref, q16_ref, k16_ref, v16_ref, kk_ref, vk_ref, y_ref, st_ref,
         ta_ref, tr_ref, nat_s, qr_ref, kr_ref, vr_ref, gr_ref, dec_s, wt_s, wh_s, st_s) = refs
        _ret_init(dec_s, wt_s, wh_s, st_s)
    _rope_fill(ba_ref, oa_ref, ta_ref)
    _rope_fill(br_ref, or_ref, tr_ref)
    u = _rms(h_ref[...], g_ref[...]).astype(BF16)
    scale = HEAD_DIM ** -0.5 * LOG2_E
    nslab = ATT_WIDTH // LANES

    q = _dot(u, w_ref[:, 0:ATT_WIDTH])
    for s in range(nslab):
        sl = slice(s * LANES, (s + 1) * LANES)
        qs = _rope(q[:, sl], ta_ref, ROT_DIM // 2) * scale
        if sample:
            qa_ref[:, sl] = qs
        else:
            nat_s[s] = qs
    if not sample:
        _emit_dilated(nat_s, q4_ref, q16_ref)

    k = _dot(u, w_ref[:, ATT_WIDTH:2 * ATT_WIDTH])
    for s in range(nslab):
        sl = slice(s * LANES, (s + 1) * LANES)
        ks = _rope(k[:, sl], ta_ref, ROT_DIM // 2)
        if not sample:
            nat_s[s] = ks
        kk_ref[:, sl] = ks
    if not sample:
        _emit_dilated(nat_s, k4_ref, k16_ref)

    v = _dot(u, w_ref[:, 2 * ATT_WIDTH:3 * ATT_WIDTH])
    if not sample:
        for s in range(nslab):
            nat_s[s] = v[:, s * LANES:(s + 1) * LANES]
        _emit_dilated(nat_s, v4_ref, v16_ref)
    vk_ref[...] = v

    o = 3 * ATT_WIDTH
    qk = _dot(u, w_ref[:, o:o + 2 * RET_QK])
    for s in range(RET_QK // LANES):
        sl = slice(s * LANES, (s + 1) * LANES)
        qr_ref[:, sl] = _rope(qk[:, sl], tr_ref, RET_DK // 2).astype(qr_ref.dtype)
        sk = slice(RET_QK + s * LANES, RET_QK + (s + 1) * LANES)
        kr_ref[:, sl] = (_rope(qk[:, sk], tr_ref, RET_DK // 2) * (RET_DK ** -0.5)).astype(kr_ref.dtype)
    o += 2 * RET_QK
    vr_ref[...] = _dot(u, w_ref[:, o:o + RET_V]).astype(vr_ref.dtype)
    o += RET_V
    gr_ref[...] = _dot(u, w_ref[:, o:o + RET_V])
    if not sample:
        _ret_body(qr_ref, kr_ref, vr_ref, gr_ref, gn_ref, y_ref, st_ref, dec_s, wt_s, wh_s, st_s,
                  qr_ref.shape[0] // RET_CHUNK)


def _mixin(h, g, w, base_pos, off_pos, keep_rows, tm, sample, gn=None):
    t = h.shape[0]
    nt = t // tm
    first_keep = nt - keep_rows // tm
    base_a, off_a = _rope_consts(base_pos, off_pos, ROT_DIM, ROPE_THETA, HEAD_DIM)
    base_r, off_r = _rope_consts(base_pos, off_pos, RET_DK, RET_THETA, RET_DK)

    def row(w_):
        return pl.BlockSpec((tm, w_), lambda i: (i, 0))

    keep_spec = pl.BlockSpec((tm, ATT_WIDTH), lambda i: (jnp.maximum(i - first_keep, 0), 0))
    keep_shape = ((keep_rows, ATT_WIDTH), F32)
    base_spec = pl.BlockSpec((None, 2, LANES), lambda i: (i, 0, 0))
    off_spec = _resident_spec((7, tm, LANES))
    tab = pltpu.VMEM((3, tm, LANES), F32)
    ins = [h, g, w, base_a, off_a, base_r, off_r]
    in_specs = [row(D_MODEL), _full_spec((1, D_MODEL)), _resident_spec((D_MODEL, IN_WIDTH)),
                base_spec, off_spec, base_spec, off_spec]
    if sample:
        specs = [row(ATT_WIDTH), keep_spec, keep_spec, row(RET_QK), row(RET_QK), row(RET_V), row(RET_V)]
        shapes = [((t, ATT_WIDTH), F32), keep_shape, keep_shape,
                  ((t, RET_QK), F32), ((t, RET_QK), F32), ((t, RET_V), F32), ((t, RET_V), F32)]
        scratch = [tab, tab]
    else:
        assert tm == RUN and t % SUPER == 0
        rps = SUPER // RUN
        ins.append(gn)
        in_specs.append(_full_spec((1, RET_V)))
        s4 = pl.BlockSpec((None, RUN // 4, 4 * ATT_WIDTH), lambda i: (i, 0, 0))
        s16 = pl.BlockSpec((None, RUN // 16, 16 * ATT_WIDTH), lambda i: (i // rps, i % rps, 0))
        st_shape = (N_RET_HEADS // 2, 2 * RET_DK, RET_DV)
        specs = [s4] * 3 + [s16] * 3 + [keep_spec, keep_spec, row(RET_V), _full_spec(st_shape)]
        shapes = ([((t // RUN, RUN // 4, 4 * ATT_WIDTH), BF16)] * 3
                  + [((t // SUPER, SUPER // 16, 16 * ATT_WIDTH), BF16)] * 3
                  + [keep_shape, keep_shape, ((t, RET_V), BF16), (st_shape, F32)])
        dec = pltpu.VMEM((N_RET_HEADS, RET_CHUNK, RET_CHUNK), F32)
        scratch = [tab, tab, pltpu.VMEM((ATT_WIDTH // LANES, RUN, LANES), F32),
                   pltpu.VMEM((tm, RET_QK), BF16), pltpu.VMEM((tm, RET_QK), BF16),
                   pltpu.VMEM((tm, RET_V), BF16), pltpu.VMEM((tm, RET_V), F32),
                   dec, dec, dec, pltpu.VMEM(st_shape, F32)]
    return pl.pallas_call(
        functools.partial(_mixin_kernel, sample=sample), grid=(nt,),
        in_specs=in_specs, out_specs=specs,
        out_shape=[jax.ShapeDtypeStruct(s, d) for s, d in shapes],
        scratch_shapes=scratch, compiler_params=_params(1), name="mix_in")(*ins)


def _attn_bias_tables():
    a = np.arange(QB)[:, None]
    c = np.arange(2 * QB)[None, :]
    steps = BRANCHES[0][0]
    dist = QB + a - c
    band = (dist >= 0) & (dist <= steps)
    cur = c >= QB
    tok_q = 4 * (a % 32) + a // 32
    cc = c % QB
    tok_k = 4 * (cc % 32) + cc // 32 + QB * (c // QB) - QB
    dist2 = tok_q - tok_k
    band2 = (dist2 >= 0) & (dist2 <= steps)
    masks = np.stack([band, band & cur, band2, band2 & cur])
    return np.where(masks, 0.0, NEG).astype(np.float32)


def _attn_qblock(get_q, get_kc, get_kp, get_vc, get_vp, bias, consts):
    head_a, head_a_win, ones_a, ones_b = consts
    out = []
    for s in range(ATT_WIDTH // LANES):
        q2 = get_q(s)
        kwin = jnp.concatenate([get_kp(s), get_kc(s)], axis=0)
        vwin = jnp.concatenate([get_vp(s), get_vc(s)], axis=0)
        zq = jnp.zeros_like(q2)
        qq = jnp.concatenate([jnp.where(head_a, q2, zq), jnp.where(head_a, zq, q2)], axis=0)
        sc = _dot_nt(qq, kwin)
        s_a = sc[:QB] + bias
        s_b = sc[QB:] + bias
        m_a = jnp.max(s_a, axis=-1, keepdims=True)
        m_b = jnp.max(s_b, axis=-1, keepdims=True)
        p = jnp.concatenate([jnp.exp2(s_a - m_a).astype(BF16), jnp.exp2(s_b - m_b).astype(BF16)],
                            axis=1)
        zv = jnp.zeros_like(vwin)
        w = jnp.concatenate(
            [jnp.concatenate([jnp.where(head_a_win, vwin, zv), ones_a], axis=1),
             jnp.concatenate([jnp.where(head_a_win, zv, vwin), ones_b], axis=1)], axis=0)
        r = _dot(p, w)
        out.append((jnp.where(head_a, m_a, m_b), r[:, LANES:], r[:, :LANES]))
    return out


def _merge(old, new):
    mo, lo, ao = old
    m2, l2, a2 = new
    mn = jnp.maximum(mo, m2)
    eo = jnp.exp2(mo - mn)
    e2 = jnp.exp2(m2 - mn)
    return mn, eo * lo + e2 * l2, eo * ao + e2 * a2


def _attn_kernel(q16, k16c, k16p, v16c, v16p, q4, k4c, k4p, v4c, v4p, bias_ref, o_ref,
                 m_s, l_s, a_s, nat_s):
    sb = pl.program_id(0)
    ph = pl.program_id(1)
    j = pl.program_id(2)
    lane = lax.broadcasted_iota(jnp.int32, (QB, LANES), 1)
    head_a = lane < HEAD_DIM
    lane_w = lax.broadcasted_iota(jnp.int32, (2 * QB, LANES), 1)
    head_a_win = lane_w < HEAD_DIM
    ones_a = jnp.where(head_a_win, 1.0, 0.0).astype(BF16)
    ones_b = jnp.where(head_a_win, 0.0, 1.0).astype(BF16)
    consts = (head_a, head_a_win, ones_a, ones_b)
    nslab = ATT_WIDTH // LANES
    sub = QB // 4
    lanes = _lanes

    @pl.when(ph == 0)
    def _():
        bias = bias_ref[(sb == 0).astype(jnp.int32)]
        for r in range(4):
            res = _attn_qblock(lambda s: q16[:, lanes(r, s)], lambda s: k16c[:, lanes(r, s)],
                               lambda s: k16p[:, lanes(r, s)], lambda s: v16c[:, lanes(r, s)],
                               lambda s: v16p[:, lanes(r, s)], bias, consts)
            for s in range(nslab):
                for u in range(SUPER // RUN):
                    rows = pl.ds(u * RUN + r * QB + j, sub, stride=4)
                    for ref, val in zip((m_s, l_s, a_s), res[s]):
                        ref[s, rows, :] = val[u * sub:(u + 1) * sub]

    @pl.when(ph == 1)
    def _():
        bias = bias_ref[(sb * (SUPER // RUN) + j == 0).astype(jnp.int32)]
        for r in range(4):
            res = _attn_qblock(lambda s: q4[:, lanes(r, s)], lambda s: k4c[:, lanes(r, s)],
                               lambda s: k4p[:, lanes(r, s)], lambda s: v4c[:, lanes(r, s)],
                               lambda s: v4p[:, lanes(r, s)], bias, consts)
            rows = pl.ds(pl.multiple_of(j * RUN + r * QB, QB), QB)
            for s in range(nslab):
                mn, ln, an = _merge((m_s[s, rows, :], l_s[s, rows, :], a_s[s, rows, :]), res[s])
                m_s[s, rows, :] = mn
                l_s[s, rows, :] = ln
                a_s[s, rows, :] = an

    @pl.when(ph == 2)
    def _():
        first = sb * (SUPER // RUN) + j == 0
        for b in range(4):
            def cur(ref):
                return lambda s: jnp.concatenate(
                    [ref[b * sub:(b + 1) * sub, lanes(r, s)] for r in range(4)], axis=0)

            def prev(ref_c, ref_p):
                if b > 0:
                    return lambda s: jnp.concatenate(
                        [ref_c[(b - 1) * sub:b * sub, lanes(r, s)] for r in range(4)], axis=0)
                return lambda s: jnp.concatenate(
                    [ref_p[QB - sub:QB, lanes(r, s)] for r in range(4)], axis=0)

            if b == 0:
                bias = bias_ref[2 + first.astype(jnp.int32)]
            else:
                bias = bias_ref[2]
            res = _attn_qblock(cur(q4), cur(k4c), prev(k4c, k4p), cur(v4c), prev(v4c, v4p),
                               bias, consts)
            for s in range(nslab):
                for r in range(4):
                    rows = pl.ds(pl.multiple_of(j * RUN + r * QB + b * sub, sub), sub)
                    part = tuple(x[r * sub:(r + 1) * sub] for x in res[s])
                    _, ln, an = _merge((m_s[s, rows, :], l_s[s, rows, :], a_s[s, rows, :]), part)
                    nat_s[s, pl.ds(b * QB + r, sub, stride=4), :] = an / ln
        for s in range(nslab):
            o_ref[:, s * LANES:(s + 1) * LANES] = nat_s[s].astype(o_ref.dtype)


def _attn_prompt(q4, k4, v4, q16, k16, v16):
    nsb = q16.shape[0]
    s_len = nsb * SUPER
    rps = SUPER // RUN
    blk = (None, QB, 4 * ATT_WIDTH)

    def j16(ph, j):
        return jnp.where(ph == 0, j, rps - 1)

    def n4(sb, ph, j):
        return sb * rps + jnp.where(ph == 0, 0, j)

    c16 = pl.BlockSpec(blk, lambda sb, ph, j: (sb, 0, j16(ph, j)))
    p16 = pl.BlockSpec(blk, lambda sb, ph, j: (jnp.maximum(sb - 1, 0), 0, j16(ph, j)))
    c4 = pl.BlockSpec(blk, lambda sb, ph, j: (n4(sb, ph, j), 0, 0))
    p4 = pl.BlockSpec(blk, lambda sb, ph, j: (jnp.maximum(n4(sb, ph, j) - 1, 0), 0, 0))
    out = pl.BlockSpec((RUN, ATT_WIDTH), lambda sb, ph, j: (sb * rps + jnp.where(ph == 2, j, 0), 0))
    bias = jnp.asarray(_attn_bias_tables())
    nslab = ATT_WIDTH // LANES
    scratch = ([pltpu.VMEM((nslab, SUPER, LANES), F32) for _ in range(3)]
               + [pltpu.VMEM((nslab, RUN, LANES), F32)])
    return pl.pallas_call(
        _attn_kernel, grid=(nsb, 3, rps),
        in_specs=[c16, c16, p16, c16, p16, c4, c4, p4, c4, p4, _full_spec(bias.shape)],
        out_specs=out, out_shape=jax.ShapeDtypeStruct((s_len, ATT_WIDTH), BF16),
        scratch_shapes=scratch, compiler_params=_params(3), name="attn_prompt")(
            q16, k16, k16, v16, v16, q4, k4, k4, v4, v4, bias)


def _ret_init(dec_s, wt_s, wh_s, st_s):
    c = RET_CHUNK

    @pl.when(pl.program_id(0) == 0)
    def _():
        i = lax.broadcasted_iota(jnp.int32, (c, c), 0)
        jj = lax.broadcasted_iota(jnp.int32, (c, c), 1)
        diff = (i - jj).astype(F32)
        for h in range(N_RET_HEADS):
            dec_s[h] = jnp.where(diff >= 0, jnp.exp(diff * LOG_G[h]), 0.0)
            wt_s[h] = jnp.exp((c - 1.0 - i.astype(F32)) * LOG_G[h])
            wh_s[h] = jnp.exp((i.astype(F32) + 1.0) * LOG_G[h])
        st_s[...] = jnp.zeros_like(st_s)


def _ret_body(qr_ref, kr_ref, vr_ref, gr_ref, gn_ref, y_ref, s_ref, dec_s, wt_s, wh_s, st_s, chunks):
    c = RET_CHUNK
    lane = lax.broadcasted_iota(jnp.int32, (c, LANES), 1)
    head_a = lane < RET_DK
    row_a = lax.broadcasted_iota(jnp.int32, (LANES, RET_DV), 0) < RET_DK
    for ci in range(chunks):
        rows = slice(ci * c, (ci + 1) * c)
        for p in range(N_RET_HEADS // 2):
            sl = slice(p * LANES, (p + 1) * LANES)
            q2 = qr_ref[rows, sl]
            k2 = kr_ref[rows, sl]
            zq = jnp.zeros_like(q2)
            qq = jnp.concatenate([jnp.where(head_a, q2, zq), jnp.where(head_a, zq, q2)], axis=0)
            inner = _dot_nt(qq, k2)
            state = st_s[p]
            cross = _dot(qq, state.astype(BF16))
            vws = []
            for hh in range(2):
                h = 2 * p + hh
                hs = slice(h * RET_DV, (h + 1) * RET_DV)
                v = vr_ref[rows, hs]
                inn = (inner[hh * c:(hh + 1) * c] * dec_s[h]).astype(BF16)
                o = _dot(inn, v) + cross[hh * c:(hh + 1) * c] * wh_s[h]
                mu = jnp.mean(o, axis=-1, keepdims=True)
                xc = o - mu
                var = jnp.mean(xc * xc, axis=-1, keepdims=True)
                y = xc * lax.rsqrt(var + NORM_EPS) * gn_ref[:, hs]
                y_ref[rows, hs] = (_silu(gr_ref[rows, hs]) * y).astype(y_ref.dtype)
                vws.append((v.astype(F32) * wt_s[h]).astype(BF16))
            upd = _dot_tn(k2, jnp.concatenate(vws, axis=1))
            upd = jnp.where(row_a, upd[:, :RET_DV], upd[:, RET_DV:])
            gch = jnp.where(row_a, math.exp(c * LOG_G[2 * p]), math.exp(c * LOG_G[2 * p + 1]))
            st_s[p] = gch * state + upd

    @pl.when(pl.program_id(0) == pl.num_programs(0) - 1)
    def _():
        s_ref[...] = st_s[...]


S_ROWS = 8
S_HGRP = 4


def _sample_tables(t_new, past_len, wb):
    rows = np.concatenate([np.arange(wb), wb + np.arange(S_ROWS)])
    tab = np.zeros((S_ROWS, wb + S_ROWS), np.float32)
    for r in range(S_ROWS):
        i = r % t_new
        delta = wb + i - rows
        ok = (delta >= 0) & (past_len + i - delta >= 0) & (rows < wb + t_new)
        for window, dil in BRANCHES:
            tab[r] += ok & (delta % dil == 0) & (delta <= window)
    tab = np.tile(tab, (S_HGRP, 1))
    return tab[:, :wb], tab[:, wb:]


def _sample_attn_body(q_ref, kn_ref, vn_ref, kt_ref, vt_ref, cc_ref, cn_ref, o_ref):
    mult_c, mult_n = cc_ref[...], cn_ref[...]
    bias_c = jnp.where(mult_c > 0, 0.0, NEG)
    bias_n = jnp.where(mult_n > 0, 0.0, NEG)
    gw = S_HGRP * HEAD_DIM
    lane_head = lax.broadcasted_iota(jnp.int32, (S_ROWS, gw), 1) // HEAD_DIM
    for g in range(N_ATT_HEADS // S_HGRP):
        gs = slice(g * gw, (g + 1) * gw)
        q8 = q_ref[:, gs]
        q = jnp.concatenate([jnp.where(lane_head == h, q8, 0.0) for h in range(S_HGRP)],
                            axis=0).astype(BF16)
        s_c = _dot(q, kt_ref[g].astype(BF16)) + bias_c
        s_n = _dot_nt(q, kn_ref[:, gs].astype(BF16)) + bias_n
        m = jnp.maximum(jnp.max(s_c, axis=-1, keepdims=True), jnp.max(s_n, axis=-1, keepdims=True))
        p_c = jnp.exp2(s_c - m) * mult_c
        p_n = jnp.exp2(s_n - m) * mult_n
        l = jnp.sum(p_c, axis=-1, keepdims=True) + jnp.sum(p_n, axis=-1, keepdims=True)
        o = (_dot_nt(p_c.astype(BF16), vt_ref[g].astype(BF16))
             + _dot(p_n.astype(BF16), vn_ref[:, gs].astype(BF16))) / l
        o8 = jnp.zeros((S_ROWS, gw), F32)
        for h in range(S_HGRP):
            o8 = jnp.where(lane_head == h, o[h * S_ROWS:(h + 1) * S_ROWS], o8)
        o_ref[:, gs] = o8


FF_SPLITS = ((0, 768), (768, 1536), (1536, 2304), (2304, D_FF))


def _ffn_attn_kernel(x_ref, gpre_ref, gpost_ref, wg_ref, wu_ref, wd_ref,
                     q_ref, kn_ref, vn_ref, kt_ref, vt_ref, cc_ref, cn_ref,
                     o_ref, oa_ref, u_s, acc_s):
    j = pl.program_id(1)
    last = len(FF_SPLITS) - 1
    for k, (c0, c1) in enumerate(FF_SPLITS):
        @pl.when(j == k)
        def _():
            if k == 0:
                u = _rms(x_ref[...], gpre_ref[...]).astype(BF16)
                u_s[...] = u
            else:
                u = u_s[...]
            h = (_silu(_dot(u, wg_ref[:, c0:c1])) * _dot(u, wu_ref[:, c0:c1])).astype(BF16)
            d = _dot(h, wd_ref[c0:c1, :])
            if k == 0:
                acc_s[...] = d
            elif k < last:
                acc_s[...] += d
            else:
                o_ref[...] = x_ref[...] + 0.5 * _rms(acc_s[...] + d, gpost_ref[...])
            _sample_attn_body(q_ref, kn_ref, vn_ref, kt_ref, vt_ref, cc_ref, cn_ref, oa_ref)


def _ffn_with_sample_attn(x, gpre, gpost, wg, wu, wd, q, kn, vn, cache_k, cache_v, t_new, past_len, tm):
    t = x.shape[0]
    nb, wb = cache_k.shape[0], cache_k.shape[1]
    nsplit = len(FF_SPLITS)
    assert t_new <= S_ROWS and nb == (t // tm) * nsplit

    def rows(a):
        return jnp.pad(a.reshape(nb, t_new, ATT_WIDTH), ((0, 0), (0, S_ROWS - t_new), (0, 0)))

    def grouped(c):
        return c.transpose(0, 2, 3, 1).reshape(nb, N_ATT_HEADS // S_HGRP, S_HGRP * HEAD_DIM, wb)

    row = pl.BlockSpec((tm, D_MODEL), lambda i, j: (i, 0))
    small = pl.BlockSpec((None, S_ROWS, ATT_WIDTH), lambda i, j: (i * nsplit + j, 0, 0))
    big = pl.BlockSpec((None, N_ATT_HEADS // S_HGRP, S_HGRP * HEAD_DIM, wb),
                       lambda i, j: (i * nsplit + j, 0, 0, 0))
    tabs = [jnp.asarray(a) for a in _sample_tables(t_new, past_len, wb)]
    h1, o = pl.pallas_call(
        _ffn_attn_kernel, grid=(t // tm, nsplit),
        in_specs=[row] + _ffn_specs() + [small, small, small, big, big]
        + [_full_spec(a.shape) for a in tabs],
        out_specs=[row, small],
        out_shape=[jax.ShapeDtypeStruct((t, D_MODEL), F32),
                   jax.ShapeDtypeStruct((nb, S_ROWS, ATT_WIDTH), F32)],
        scratch_shapes=[pltpu.VMEM((tm, D_MODEL), BF16), pltpu.VMEM((tm, D_MODEL), F32)],
        compiler_params=_params(2), name="ffn_attn_sample")(
            x, gpre, gpost, wg, wu, wd, rows(q), rows(kn), rows(vn),
            grouped(cache_k), grouped(cache_v), *tabs)
    return h1, o[:, :t_new].reshape(nb * t_new, ATT_WIDTH)


R_PAIR = 2


def _sample_ret_tables(t_new):
    r = np.arange(R_PAIR * t_new)
    seq, step = r // t_new, r % t_new
    lg = np.asarray(LOG_G)[:, None, None]
    diff = (step[:, None] - step[None, :])[None]
    ok = ((seq[:, None] == seq[None, :]) & (diff[0] >= 0))[None]
    decay = np.where(ok, np.exp(diff * lg), 0.0).reshape(N_RET_HEADS * r.size, r.size)
    w_head = np.exp((step[None, :] + 1.0) * lg[:, :, 0]).reshape(-1, 1) * np.ones((1, RET_DV))
    w_tail = np.exp((t_new - 1.0 - step)[:, None, None] * lg[None, :, :, 0]) * np.ones((1, 1, RET_DV))
    g_chunk = np.repeat(np.exp(t_new * lg[:, 0, 0]), RET_DK)[:, None] * np.ones((1, RET_DV))
    return [a.astype(np.float32) for a in (decay, w_head, w_tail.reshape(r.size, -1), g_chunk)]


def _sample_ret_kernel(q_ref, k_ref, v_ref, g_ref, gn_ref, st_ref, dec_ref, wh_ref, wt_ref, gc_ref,
                       y_ref, so_ref, *, blocks, t_new):
    nrow = R_PAIR * t_new
    lane_head = lax.broadcasted_iota(jnp.int32, (nrow, RET_QK), 1) // RET_DK
    row_seq = lax.broadcasted_iota(jnp.int32, (nrow, RET_QK), 0) // t_new
    out_seq = (lax.broadcasted_iota(jnp.int32, (N_RET_HEADS * nrow, RET_DV), 0) % nrow) // t_new
    for blk in range(blocks):
        rs = slice(blk * nrow, (blk + 1) * nrow)
        q8, k8, v8 = q_ref[rs, :], k_ref[rs, :], v_ref[rs, :]
        qm = jnp.concatenate([jnp.where(lane_head == h, q8, 0.0) for h in range(N_RET_HEADS)],
                             axis=0).astype(BF16)
        inner = (_dot_nt(qm, k8.astype(BF16)) * dec_ref[...]).astype(BF16)
        o_all = _dot(inner, v8.astype(BF16))
        vw = (v8 * wt_ref[...]).astype(BF16)
        cross = None
        for s in range(R_PAIR):
            state = st_ref[blk * R_PAIR + s]
            c = _dot(qm, state.astype(BF16))
            cross = c if cross is None else jnp.where(out_seq == s, c, cross)
            km = jnp.where(row_seq == s, k8, 0.0).astype(BF16)
            upd = _dot_tn(km, vw)
            upd = jnp.concatenate([upd[h * RET_DK:(h + 1) * RET_DK, h * RET_DV:(h + 1) * RET_DV]
                                   for h in range(N_RET_HEADS)], axis=0)
            so_ref[blk * R_PAIR + s] = gc_ref[...] * state + upd
        cross = cross * wh_ref[...]
        for h in range(N_RET_HEADS):
            hs = slice(h * RET_DV, (h + 1) * RET_DV)
            oh = o_all[h * nrow:(h + 1) * nrow, hs] + cross[h * nrow:(h + 1) * nrow]
            mu = jnp.mean(oh, axis=-1, keepdims=True)
            xc = oh - mu
            var = jnp.mean(xc * xc, axis=-1, keepdims=True)
            y = xc * lax.rsqrt(var + NORM_EPS) * gn_ref[:, hs]
            y_ref[rs, hs] = _silu(g_ref[rs, hs]) * y


def _ret_sample(qr, kr, vr, gr, gn, state, t_new, blocks=8):
    nb = state.shape[0]
    assert R_PAIR * t_new == 8 and nb % (R_PAIR * blocks) == 0
    rows = R_PAIR * t_new * blocks
    tabs = [jnp.asarray(a) for a in _sample_ret_tables(t_new)]
    st = state.reshape(nb, N_RET_HEADS * RET_DK, RET_DV)

    def blk(w_):
        return pl.BlockSpec((rows, w_), lambda i: (i, 0))

    st_spec = pl.BlockSpec((R_PAIR * blocks, N_RET_HEADS * RET_DK, RET_DV), lambda i: (i, 0, 0))
    y, s = pl.pallas_call(
        functools.partial(_sample_ret_kernel, blocks=blocks, t_new=t_new),
        grid=(nb // (R_PAIR * blocks),),
        in_specs=[blk(RET_QK), blk(RET_QK), blk(RET_V), blk(RET_V), _full_spec((1, RET_V)), st_spec]
        + [_full_spec(a.shape) for a in tabs],
        out_specs=[blk(RET_V), st_spec],
        out_shape=[jax.ShapeDtypeStruct((nb * t_new, RET_V), F32), jax.ShapeDtypeStruct(st.shape, F32)],
        compiler_params=_params(1), name="ret_sample")(qr, kr, vr, gr, gn, st, *tabs)
    return y, s.reshape(state.shape)


def kernel(x_prompt, x_sample, cache_k, cache_v, state_ret, g_ffn1_pre, g_ffn1_post, w1_gate, w1_up,
           w1_down, g_mix_pre, g_mix_post, w_in, gn_w, w_out, g_ffn2_pre, g_ffn2_post, w2_gate, w2_up,
           w2_down):
    b_p, s_p, _ = x_prompt.shape
    b_s, t_s, _ = x_sample.shape
    depth = w_in.shape[0]
    assert depth == 1 and b_p == 1
    keep = min(WIN_MAX, s_p)
    l = 0
    bf = lambda w: w[l].astype(BF16)
    vec = lambda g: g[l].reshape(1, -1)
    w1 = (bf(w1_gate), bf(w1_up), bf(w1_down))
    w2 = (bf(w2_gate), bf(w2_up), bf(w2_down))
    wi, wo = bf(w_in), bf(w_out)
    g1a, g1b, gma, gmb = vec(g_ffn1_pre), vec(g_ffn1_post), vec(g_mix_pre), vec(g_mix_post)
    g2a, g2b, gn = vec(g_ffn2_pre), vec(g_ffn2_post), vec(gn_w)

    n_s = b_s * t_s
    xs = x_sample.reshape(n_s, D_MODEL)
    h1s = _ffn(xs, g1a, g1b, *w1, n_s)
    qas, kks, vks, qrs, krs, vrs, grs = _mixin(
        h1s, gma, wi, np.full((1,), PAST_LEN), np.tile(np.arange(t_s), b_s), n_s, n_s, sample=True)

    tm = RUN
    xp = x_prompt.reshape(s_p, D_MODEL)
    h1, o_att_s = _ffn_with_sample_attn(xp, g1a, g1b, *w1, qas, kks, vks, cache_k[l], cache_v[l],
                                        t_s, PAST_LEN, tm)
    q4, k4, v4, q16, k16, v16, kk, vk, y_ret, st_p = _mixin(
        h1, gma, wi, tm * np.arange(s_p // tm), np.arange(tm), keep, tm, sample=False, gn=gn)
    o_att = _attn_prompt(q4, k4, v4, q16, k16, v16)
    y_prompt = _mixout_ffn(o_att, y_ret, h1, wo, gmb, g2a, g2b, *w2, tm)

    y_ret_s, st_s = _ret_sample(qrs, krs, vrs, grs, gn, state_ret[l], t_s)
    y_sample = _mixout_ffn(o_att_s, y_ret_s, h1s, wo, gmb, g2a, g2b, *w2, n_s)

    hd = (N_ATT_HEADS, HEAD_DIM)
    return (y_prompt.reshape(b_p, s_p, D_MODEL),
            y_sample.reshape(b_s, t_s, D_MODEL),
            kk.reshape(depth, b_p, keep, *hd),
            vk.reshape(depth, b_p, keep, *hd),
            st_p.reshape(depth, b_p, N_RET_HEADS, RET_DK, RET_DV),
            kks.reshape(depth, b_s, t_s, *hd),
            vks.reshape(depth, b_s, t_s, *hd),
            st_s.reshape(depth, b_s, N_RET_HEADS, RET_DK, RET_DV))
```

```python
import functools
import math

import numpy as np
import jax
import jax.numpy as jnp
from jax import lax
from jax.experimental import pallas as pl
from jax.experimental.pallas import tpu as pltpu

F32 = jnp.float32
BF16 = jnp.bfloat16

D_MODEL = 1024
D_FF = 2816
HEAD_DIM = 64
N_ATT_HEADS = 8
ATT_WIDTH = N_ATT_HEADS * HEAD_DIM
ROT_DIM = HEAD_DIM // 4
ROPE_THETA = 500000.0
WIN_MAX = 2048
PAST_LEN = 8192
BRANCHES = ((128, 1), (512, 4), (2048, 16))
N_RET_HEADS = 4
RET_DK = 64
RET_DV = 128
RET_QK = N_RET_HEADS * RET_DK
RET_V = N_RET_HEADS * RET_DV
RET_THETA = 10000.0
RET_CHUNK = 128
IN_WIDTH = 3 * ATT_WIDTH + 2 * RET_QK + 2 * RET_V
NORM_EPS = 1e-6
NEG = -1e30

LANES = 128
QB = 128
SUPER = 2048
RUN = 512
VMEM_LIMIT = 56 * 1024 * 1024
LOG_G = tuple(math.log1p(-2.0 ** (-5.0 - h)) for h in range(N_RET_HEADS))
LOG2_E = math.log2(math.e)


def _full_spec(shape):
    nd = len(shape)
    return pl.BlockSpec(shape, lambda *_: (0,) * nd)


def _resident_spec(shape):
    nd = len(shape)
    return pl.BlockSpec(shape, lambda *_: (0,) * nd, pipeline_mode=pl.Buffered(1))


def _params(n_axes):
    return pltpu.CompilerParams(dimension_semantics=("arbitrary",) * n_axes,
                                vmem_limit_bytes=VMEM_LIMIT)


def _rms(x, g):
    return x * lax.rsqrt(jnp.mean(x * x, axis=-1, keepdims=True) + NORM_EPS) * g


def _silu(x):
    return x / (1.0 + jnp.exp(-x))


def _dot(a, b):
    return jnp.dot(a, b, preferred_element_type=F32)


def _dot_nt(a, b):
    return lax.dot_general(a, b, (((1,), (1,)), ((), ())), preferred_element_type=F32)


def _dot_tn(a, b):
    return lax.dot_general(a, b, (((0,), (0,)), ((), ())), preferred_element_type=F32)


FF_CHUNK = 512


def _swiglu(u, wg_ref, wu_ref, wd_ref):
    acc = None
    for c0 in range(0, D_FF, FF_CHUNK):
        c1 = min(c0 + FF_CHUNK, D_FF)
        g = _dot(u, wg_ref[:, c0:c1])
        up = _dot(u, wu_ref[:, c0:c1])
        h = (_silu(g) * up).astype(BF16)
        d = _dot(h, wd_ref[c0:c1, :])
        acc = d if acc is None else acc + d
    return acc


def _ffn_kernel(x_ref, gpre_ref, gpost_ref, wg_ref, wu_ref, wd_ref, o_ref):
    x = x_ref[...]
    u = _rms(x, gpre_ref[...]).astype(BF16)
    y = _swiglu(u, wg_ref, wu_ref, wd_ref)
    o_ref[...] = x + 0.5 * _rms(y, gpost_ref[...])


def _mixout_ffn_kernel(oa_ref, yr_ref, h_ref, wo_ref, gmb_ref, gpre_ref, gpost_ref,
                       wg_ref, wu_ref, wd_ref, o_ref):
    mixed = jnp.concatenate([oa_ref[...].astype(BF16), yr_ref[...].astype(BF16)], axis=1)
    x = h_ref[...] + _rms(_dot(mixed, wo_ref[...]), gmb_ref[...])
    u = _rms(x, gpre_ref[...]).astype(BF16)
    y = _swiglu(u, wg_ref, wu_ref, wd_ref)
    o_ref[...] = x + 0.5 * _rms(y, gpost_ref[...])


def _ffn_specs():
    return [_full_spec((1, D_MODEL)), _full_spec((1, D_MODEL)),
            _resident_spec((D_MODEL, D_FF)), _resident_spec((D_MODEL, D_FF)),
            _resident_spec((D_FF, D_MODEL))]


def _ffn(x, gpre, gpost, wg, wu, wd, tm):
    t = x.shape[0]
    row = pl.BlockSpec((tm, D_MODEL), lambda i: (i, 0))
    return pl.pallas_call(
        _ffn_kernel, grid=(t // tm,),
        in_specs=[row] + _ffn_specs(), out_specs=row,
        out_shape=jax.ShapeDtypeStruct((t, D_MODEL), F32),
        compiler_params=_params(1), name="ffn")(x, gpre, gpost, wg, wu, wd)


def _mixout_ffn(oa, yr, h, wo, gmb, gpre, gpost, wg, wu, wd, tm):
    t = h.shape[0]
    row = pl.BlockSpec((tm, D_MODEL), lambda i: (i, 0))
    half = pl.BlockSpec((tm, ATT_WIDTH), lambda i: (i, 0))
    return pl.pallas_call(
        _mixout_ffn_kernel, grid=(t // tm,),
        in_specs=[half, half, row, _resident_spec((D_MODEL, D_MODEL)), _full_spec((1, D_MODEL))]
        + _ffn_specs(),
        out_specs=row, out_shape=jax.ShapeDtypeStruct((t, D_MODEL), F32),
        compiler_params=_params(1), name="mixout_ffn")(oa, yr, h, wo, gmb, gpre, gpost, wg, wu, wd)


def _rope_consts(base_pos, off_pos, rot_dim, theta, head_dim):
    half = rot_dim // 2
    inv = theta ** (-np.arange(half, dtype=np.float64) * (2.0 / rot_dim))
    lane = np.arange(LANES) % head_dim
    inv_l = inv[lane % half][None, :]
    first = (lane < half).astype(np.float64)[None, :]
    second = ((lane >= half) & (lane < rot_dim)).astype(np.float64)[None, :]
    rot = first + second
    a = np.asarray(base_pos, np.float64)[:, None] * inv_l
    b = np.asarray(off_pos, np.float64)[:, None] * inv_l
    cb, sb = np.cos(b), np.sin(b)
    base = np.stack([np.cos(a), np.sin(a)], axis=1)
    off = np.stack([cb * rot, sb * rot, np.broadcast_to(1.0 - rot, cb.shape),
                    -cb * first, -sb * first, cb * second, sb * second])
    return jnp.asarray(base, F32), jnp.asarray(off, F32)


def _rope_fill(base_ref, off_ref, tab_s):
    ca = base_ref[0:1, :]
    sa = base_ref[1:2, :]
    tab_s[0] = ca * off_ref[0] - sa * off_ref[1] + off_ref[2]
    tab_s[1] = sa * off_ref[3] + ca * off_ref[4]
    tab_s[2] = sa * off_ref[5] + ca * off_ref[6]


def _rope(x, tab_ref, half):
    return (x * tab_ref[0] + pltpu.roll(x, LANES - half, 1) * tab_ref[1]
            + pltpu.roll(x, half, 1) * tab_ref[2])


def _lanes(r, s):
    return slice(r * ATT_WIDTH + s * LANES, r * ATT_WIDTH + (s + 1) * LANES)


def _emit_dilated(nat_s, x4_s, o4_ref, o16_ref):
    for s in range(ATT_WIDTH // LANES):
        for r in range(4):
            x4 = nat_s[s, pl.ds(r, RUN // 4, stride=4), :]
            o4_ref[:, _lanes(r, s)] = x4.astype(BF16)
            x4_s[r] = x4
        for r in range(4):
            for c in range(4):
                o16_ref[:, _lanes(4 * c + r, s)] = x4_s[r, pl.ds(c, RUN // 16, stride=4), :].astype(BF16)


def _mixin_kernel(h_ref, g_ref, w_ref, ba_ref, oa_ref, br_ref, or_ref, *refs, sample):
    if sample:
        qa_ref, kk_ref, vk_ref, qr_ref, kr_ref, vr_ref, gr_ref, ta_ref, tr_ref = refs
    else:
        (gn_ref, q4_ref, k4_ref, v4_ref, q16_ref, k16_ref, v16_ref, kk_ref, vk_ref, y_ref, st_ref,
         ta_ref, tr_ref, nat_s, x4_s, qr_ref, kr_ref, vr_ref, gr_ref, dec_s, wt_s, wh_s, st_s) = refs
        _ret_init(dec_s, wt_s, wh_s, st_s)
    _rope_fill(ba_ref, oa_ref, ta_ref)
    _rope_fill(br_ref, or_ref, tr_ref)
    u = _rms(h_ref[...], g_ref[...]).astype(BF16)
    scale = HEAD_DIM ** -0.5 * LOG2_E
    nslab = ATT_WIDTH // LANES

    q = _dot(u, w_ref[:, 0:ATT_WIDTH])
    for s in range(nslab):
        sl = slice(s * LANES, (s + 1) * LANES)
        qs = _rope(q[:, sl], ta_ref, ROT_DIM // 2) * scale
        if sample:
            qa_ref[:, sl] = qs
        else:
            nat_s[s] = qs
    if not sample:
        _emit_dilated(nat_s, x4_s, q4_ref, q16_ref)

    k = _dot(u, w_ref[:, ATT_WIDTH:2 * ATT_WIDTH])
    for s in range(nslab):
        sl = slice(s * LANES, (s + 1) * LANES)
        ks = _rope(k[:, sl], ta_ref, ROT_DIM // 2)
        if not sample:
            nat_s[s] = ks
        kk_ref[:, sl] = ks
    if not sample:
        _emit_dilated(nat_s, x4_s, k4_ref, k16_ref)

    v = _dot(u, w_ref[:, 2 * ATT_WIDTH:3 * ATT_WIDTH])
    if not sample:
        for s in range(nslab):
            nat_s[s] = v[:, s * LANES:(s + 1) * LANES]
        _emit_dilated(nat_s, x4_s, v4_ref, v16_ref)
    vk_ref[...] = v

    o = 3 * ATT_WIDTH
    qk = _dot(u, w_ref[:, o:o + 2 * RET_QK])
    for s in range(RET_QK // LANES):
        sl = slice(s * LANES, (s + 1) * LANES)
        qr_ref[:, sl] = _rope(qk[:, sl], tr_ref, RET_DK // 2).astype(qr_ref.dtype)
        sk = slice(RET_QK + s * LANES, RET_QK + (s + 1) * LANES)
        kr_ref[:, sl] = (_rope(qk[:, sk], tr_ref, RET_DK // 2) * (RET_DK ** -0.5)).astype(kr_ref.dtype)
    o += 2 * RET_QK
    vr_ref[...] = _dot(u, w_ref[:, o:o + RET_V]).astype(vr_ref.dtype)
    o += RET_V
    gr_ref[...] = _dot(u, w_ref[:, o:o + RET_V])
    if not sample:
        _ret_body(qr_ref, kr_ref, vr_ref, gr_ref, gn_ref, y_ref, st_ref, dec_s, wt_s, wh_s, st_s,
                  qr_ref.shape[0] // RET_CHUNK)


def _mixin(h, g, w, base_pos, off_pos, keep_rows, tm, sample, gn=None):
    t = h.shape[0]
    nt = t // tm
    first_keep = nt - keep_rows // tm
    base_a, off_a = _rope_consts(base_pos, off_pos, ROT_DIM, ROPE_THETA, HEAD_DIM)
    base_r, off_r = _rope_consts(base_pos, off_pos, RET_DK, RET_THETA, RET_DK)

    def row(w_):
        return pl.BlockSpec((tm, w_), lambda i: (i, 0))

    keep_spec = pl.BlockSpec((tm, ATT_WIDTH), lambda i: (jnp.maximum(i - first_keep, 0), 0))
    keep_shape = ((keep_rows, ATT_WIDTH), F32)
    base_spec = pl.BlockSpec((None, 2, LANES), lambda i: (i, 0, 0))
    off_spec = _resident_spec((7, tm, LANES))
    tab = pltpu.VMEM((3, tm, LANES), F32)
    ins = [h, g, w, base_a, off_a, base_r, off_r]
    in_specs = [row(D_MODEL), _full_spec((1, D_MODEL)), _resident_spec((D_MODEL, IN_WIDTH)),
                base_spec, off_spec, base_spec, off_spec]
    if sample:
        specs = [row(ATT_WIDTH), keep_spec, keep_spec, row(RET_QK), row(RET_QK), row(RET_V), row(RET_V)]
        shapes = [((t, ATT_WIDTH), F32), keep_shape, keep_shape,
                  ((t, RET_QK), F32), ((t, RET_QK), F32), ((t, RET_V), F32), ((t, RET_V), F32)]
        scratch = [tab, tab]
    else:
        assert tm == RUN and t % SUPER == 0
        rps = SUPER // RUN
        ins.append(gn)
        in_specs.append(_full_spec((1, RET_V)))
        s4 = pl.BlockSpec((None, RUN // 4, 4 * ATT_WIDTH), lambda i: (i, 0, 0))
        s16 = pl.BlockSpec((None, RUN // 16, 16 * ATT_WIDTH), lambda i: (i // rps, i % rps, 0))
        st_shape = (N_RET_HEADS // 2, 2 * RET_DK, RET_DV)
        specs = [s4] * 3 + [s16] * 3 + [keep_spec, keep_spec, row(RET_V), _full_spec(st_shape)]
        shapes = ([((t // RUN, RUN // 4, 4 * ATT_WIDTH), BF16)] * 3
                  + [((t // SUPER, SUPER // 16, 16 * ATT_WIDTH), BF16)] * 3
                  + [keep_shape, keep_shape, ((t, RET_V), BF16), (st_shape, F32)])
        dec = pltpu.VMEM((N_RET_HEADS, RET_CHUNK, RET_CHUNK), F32)
        scratch = [tab, tab, pltpu.VMEM((ATT_WIDTH // LANES, RUN, LANES), F32),
                   pltpu.VMEM((4, RUN // 4, LANES), F32),
                   pltpu.VMEM((tm, RET_QK), BF16), pltpu.VMEM((tm, RET_QK), BF16),
                   pltpu.VMEM((tm, RET_V), BF16), pltpu.VMEM((tm, RET_V), F32),
                   dec, dec, dec, pltpu.VMEM(st_shape, F32)]
    return pl.pallas_call(
        functools.partial(_mixin_kernel, sample=sample), grid=(nt,),
        in_specs=in_specs, out_specs=specs,
        out_shape=[jax.ShapeDtypeStruct(s, d) for s, d in shapes],
        scratch_shapes=scratch, compiler_params=_params(1), name="mix_in")(*ins)


def _attn_bias_tables():
    a = np.arange(QB)[:, None]
    c = np.arange(2 * QB)[None, :]
    steps = BRANCHES[0][0]
    dist = QB + a - c
    band = (dist >= 0) & (dist <= steps)
    cur = c >= QB
    tok_q = 4 * (a % 32) + a // 32
    cc = c % QB
    tok_k = 4 * (cc % 32) + cc // 32 + QB * (c // QB) - QB
    dist2 = tok_q - tok_k
    band2 = (dist2 >= 0) & (dist2 <= steps)
    masks = np.stack([band, band & cur, band2, band2 & cur])
    return np.where(masks, 0.0, NEG).astype(np.float32)


def _attn_qblock(get_q, get_kc, get_kp, get_vc, get_vp, bias, consts, sink):
    head_a, head_a_win, ones_a, ones_b = consts
    for s in range(ATT_WIDTH // LANES):
        q2 = get_q(s)
        kwin = jnp.concatenate([get_kp(s), get_kc(s)], axis=0)
        vwin = jnp.concatenate([get_vp(s), get_vc(s)], axis=0)
        zq = jnp.zeros_like(q2)
        qq = jnp.concatenate([jnp.where(head_a, q2, zq), jnp.where(head_a, zq, q2)], axis=0)
        sc = _dot_nt(qq, kwin)
        s_a = sc[:QB] + bias
        s_b = sc[QB:] + bias
        m_a = jnp.max(s_a, axis=-1, keepdims=True)
        m_b = jnp.max(s_b, axis=-1, keepdims=True)
        p = jnp.concatenate([jnp.exp2(s_a - m_a).astype(BF16), jnp.exp2(s_b - m_b).astype(BF16)],
                            axis=1)
        zv = jnp.zeros_like(vwin)
        w = jnp.concatenate(
            [jnp.concatenate([jnp.where(head_a_win, vwin, zv), ones_a], axis=1),
             jnp.concatenate([jnp.where(head_a_win, zv, vwin), ones_b], axis=1)], axis=0)
        r = _dot(p, w)
        sink(s, (jnp.where(head_a, m_a, m_b), r[:, LANES:], r[:, :LANES]))


def _merge(old, new):
    mo, lo, ao = old
    m2, l2, a2 = new
    mn = jnp.maximum(mo, m2)
    eo = jnp.exp2(mo - mn)
    e2 = jnp.exp2(m2 - mn)
    return mn, eo * lo + e2 * l2, eo * ao + e2 * a2


def _attn_kernel(q16, k16c, k16p, v16c, v16p, q4, k4c, k4p, v4c, v4p, bias_ref, o_ref,
                 m_s, l_s, a_s, nat_s):
    sb = pl.program_id(0)
    ph = pl.program_id(1)
    j = pl.program_id(2)
    lane = lax.broadcasted_iota(jnp.int32, (QB, LANES), 1)
    head_a = lane < HEAD_DIM
    lane_w = lax.broadcasted_iota(jnp.int32, (2 * QB, LANES), 1)
    head_a_win = lane_w < HEAD_DIM
    ones_a = jnp.where(head_a_win, 1.0, 0.0).astype(BF16)
    ones_b = jnp.where(head_a_win, 0.0, 1.0).astype(BF16)
    consts = (head_a, head_a_win, ones_a, ones_b)
    nslab = ATT_WIDTH // LANES
    sub = QB // 4
    lanes = _lanes

    @pl.when(ph == 0)
    def _():
        bias = bias_ref[(sb == 0).astype(jnp.int32)]
        for r in range(4):
            def scatter(s, res):
                for u in range(SUPER // RUN):
                    rows = pl.ds(u * RUN + r * QB + j, sub, stride=4)
                    for ref, val in zip((m_s, l_s, a_s), res):
                        ref[s, rows, :] = val[u * sub:(u + 1) * sub]

            _attn_qblock(lambda s: q16[:, lanes(r, s)], lambda s: k16c[:, lanes(r, s)],
                         lambda s: k16p[:, lanes(r, s)], lambda s: v16c[:, lanes(r, s)],
                         lambda s: v16p[:, lanes(r, s)], bias, consts, scatter)

    @pl.when(ph == 1)
    def _():
        bias = bias_ref[(sb * (SUPER // RUN) + j == 0).astype(jnp.int32)]
        for r in range(4):
            rows = pl.ds(pl.multiple_of(j * RUN + r * QB, QB), QB)

            def merge_in(s, res):
                mn, ln, an = _merge((m_s[s, rows, :], l_s[s, rows, :], a_s[s, rows, :]), res)
                m_s[s, rows, :] = mn
                l_s[s, rows, :] = ln
                a_s[s, rows, :] = an

            _attn_qblock(lambda s: q4[:, lanes(r, s)], lambda s: k4c[:, lanes(r, s)],
                         lambda s: k4p[:, lanes(r, s)], lambda s: v4c[:, lanes(r, s)],
                         lambda s: v4p[:, lanes(r, s)], bias, consts, merge_in)

    @pl.when(ph == 2)
    def _():
        first = sb * (SUPER // RUN) + j == 0
        for b in range(4):
            def cur(ref):
                return lambda s: jnp.concatenate(
                    [ref[b * sub:(b + 1) * sub, lanes(r, s)] for r in range(4)], axis=0)

            def prev(ref_c, ref_p):
                if b > 0:
                    return lambda s: jnp.concatenate(
                        [ref_c[(b - 1) * sub:b * sub, lanes(r, s)] for r in range(4)], axis=0)
                return lambda s: jnp.concatenate(
                    [ref_p[QB - sub:QB, lanes(r, s)] for r in range(4)], axis=0)

            if b == 0:
                bias = bias_ref[2 + first.astype(jnp.int32)]
            else:
                bias = bias_ref[2]
            def finish(s, res):
                for r in range(4):
                    rows = pl.ds(pl.multiple_of(j * RUN + r * QB + b * sub, sub), sub)
                    part = tuple(x[r * sub:(r + 1) * sub] for x in res)
                    _, ln, an = _merge((m_s[s, rows, :], l_s[s, rows, :], a_s[s, rows, :]), part)
                    nat_s[s, pl.ds(b * QB + r, sub, stride=4), :] = an / ln

            _attn_qblock(cur(q4), cur(k4c), prev(k4c, k4p), cur(v4c), prev(v4c, v4p),
                         bias, consts, finish)
        for s in range(nslab):
            o_ref[:, s * LANES:(s + 1) * LANES] = nat_s[s].astype(o_ref.dtype)


def _attn_prompt(q4, k4, v4, q16, k16, v16):
    nsb = q16.shape[0]
    s_len = nsb * SUPER
    rps = SUPER // RUN
    blk = (None, QB, 4 * ATT_WIDTH)

    def j16(ph, j):
        return jnp.where(ph == 0, j, rps - 1)

    def n4(sb, ph, j):
        return sb * rps + jnp.where(ph == 0, 0, j)

    c16 = pl.BlockSpec(blk, lambda sb, ph, j: (sb, 0, j16(ph, j)))
    p16 = pl.BlockSpec(blk, lambda sb, ph, j: (jnp.maximum(sb - 1, 0), 0, j16(ph, j)))
    c4 = pl.BlockSpec(blk, lambda sb, ph, j: (n4(sb, ph, j), 0, 0))
    p4 = pl.BlockSpec(blk, lambda sb, ph, j: (jnp.maximum(n4(sb, ph, j) - 1, 0), 0, 0))
    out = pl.BlockSpec((RUN, ATT_WIDTH), lambda sb, ph, j: (sb * rps + jnp.where(ph == 2, j, 0), 0))
    bias = jnp.asarray(_attn_bias_tables())
    nslab = ATT_WIDTH // LANES
    scratch = ([pltpu.VMEM((nslab, SUPER, LANES), F32) for _ in range(3)]
               + [pltpu.VMEM((nslab, RUN, LANES), F32)])
    return pl.pallas_call(
        _attn_kernel, grid=(nsb, 3, rps),
        in_specs=[c16, c16, p16, c16, p16, c4, c4, p4, c4, p4, _full_spec(bias.shape)],
        out_specs=out, out_shape=jax.ShapeDtypeStruct((s_len, ATT_WIDTH), BF16),
        scratch_shapes=scratch, compiler_params=_params(3), name="attn_prompt")(
            q16, k16, k16, v16, v16, q4, k4, k4, v4, v4, bias)


def _ret_init(dec_s, wt_s, wh_s, st_s):
    c = RET_CHUNK

    @pl.when(pl.program_id(0) == 0)
    def _():
        i = lax.broadcasted_iota(jnp.int32, (c, c), 0)
        jj = lax.broadcasted_iota(jnp.int32, (c, c), 1)
        diff = (i - jj).astype(F32)
        for h in range(N_RET_HEADS):
            dec_s[h] = jnp.where(diff >= 0, jnp.exp(diff * LOG_G[h]), 0.0)
            wt_s[h] = jnp.exp((c - 1.0 - i.astype(F32)) * LOG_G[h])
            wh_s[h] = jnp.exp((i.astype(F32) + 1.0) * LOG_G[h])
        st_s[...] = jnp.zeros_like(st_s)


def _ret_body(qr_ref, kr_ref, vr_ref, gr_ref, gn_ref, y_ref, s_ref, dec_s, wt_s, wh_s, st_s, chunks):
    c = RET_CHUNK
    lane = lax.broadcasted_iota(jnp.int32, (c, LANES), 1)
    head_a = lane < RET_DK
    row_a = lax.broadcasted_iota(jnp.int32, (LANES, RET_DV), 0) < RET_DK
    for ci in range(chunks):
        rows = slice(ci * c, (ci + 1) * c)
        for p in range(N_RET_HEADS // 2):
            sl = slice(p * LANES, (p + 1) * LANES)
            q2 = qr_ref[rows, sl]
            k2 = kr_ref[rows, sl]
            zq = jnp.zeros_like(q2)
            qq = jnp.concatenate([jnp.where(head_a, q2, zq), jnp.where(head_a, zq, q2)], axis=0)
            inner = _dot_nt(qq, k2)
            state = st_s[p]
            cross = _dot(qq, state.astype(BF16))
            vws = []
            for hh in range(2):
                h = 2 * p + hh
                hs = slice(h * RET_DV, (h + 1) * RET_DV)
                v = vr_ref[rows, hs]
                inn = (inner[hh * c:(hh + 1) * c] * dec_s[h]).astype(BF16)
                o = _dot(inn, v) + cross[hh * c:(hh + 1) * c] * wh_s[h]
                mu = jnp.mean(o, axis=-1, keepdims=True)
                xc = o - mu
                var = jnp.mean(xc * xc, axis=-1, keepdims=True)
                y = xc * lax.rsqrt(var + NORM_EPS) * gn_ref[:, hs]
                y_ref[rows, hs] = (_silu(gr_ref[rows, hs]) * y).astype(y_ref.dtype)
                vws.append((v.astype(F32) * wt_s[h]).astype(BF16))
            upd = _dot_tn(k2, jnp.concatenate(vws, axis=1))
            upd = jnp.where(row_a, upd[:, :RET_DV], upd[:, RET_DV:])
            gch = jnp.where(row_a, math.exp(c * LOG_G[2 * p]), math.exp(c * LOG_G[2 * p + 1]))
            st_s[p] = gch * state + upd

    @pl.when(pl.program_id(0) == pl.num_programs(0) - 1)
    def _():
        s_ref[...] = st_s[...]


S_ROWS = 8
S_HGRP = 4


def _sample_tables(t_new, past_len, wb):
    rows = np.concatenate([np.arange(wb), wb + np.arange(S_ROWS)])
    tab = np.zeros((S_ROWS, wb + S_ROWS), np.float32)
    for r in range(S_ROWS):
        i = r % t_new
        delta = wb + i - rows
        ok = (delta >= 0) & (past_len + i - delta >= 0) & (rows < wb + t_new)
        for window, dil in BRANCHES:
            tab[r] += ok & (delta % dil == 0) & (delta <= window)
    tab = np.tile(tab, (S_HGRP, 1))
    return tab[:, :wb], tab[:, wb:]


def _sample_attn_body(q_ref, kn_ref, vn_ref, kt_ref, vt_ref, cc_ref, cn_ref, o_ref):
    mult_c, mult_n = cc_ref[...], cn_ref[...]
    bias_c = jnp.where(mult_c > 0, 0.0, NEG)
    bias_n = jnp.where(mult_n > 0, 0.0, NEG)
    gw = S_HGRP * HEAD_DIM
    lane_head = lax.broadcasted_iota(jnp.int32, (S_ROWS, gw), 1) // HEAD_DIM
    for g in range(N_ATT_HEADS // S_HGRP):
        gs = slice(g * gw, (g + 1) * gw)
        q8 = q_ref[:, gs]
        q = jnp.concatenate([jnp.where(lane_head == h, q8, 0.0) for h in range(S_HGRP)],
                            axis=0).astype(BF16)
        s_c = _dot(q, kt_ref[g].astype(BF16)) + bias_c
        s_n = _dot_nt(q, kn_ref[:, gs].astype(BF16)) + bias_n
        m = jnp.maximum(jnp.max(s_c, axis=-1, keepdims=True), jnp.max(s_n, axis=-1, keepdims=True))
        p_c = jnp.exp2(s_c - m) * mult_c
        p_n = jnp.exp2(s_n - m) * mult_n
        l = jnp.sum(p_c, axis=-1, keepdims=True) + jnp.sum(p_n, axis=-1, keepdims=True)
        o = (_dot_nt(p_c.astype(BF16), vt_ref[g].astype(BF16))
             + _dot(p_n.astype(BF16), vn_ref[:, gs].astype(BF16))) / l
        o8 = jnp.zeros((S_ROWS, gw), F32)
        for h in range(S_HGRP):
            o8 = jnp.where(lane_head == h, o[h * S_ROWS:(h + 1) * S_ROWS], o8)
        o_ref[:, gs] = o8


FF_SPLITS = ((0, 768), (768, 1536), (1536, 2304), (2304, D_FF))


def _ffn_attn_kernel(x_ref, gpre_ref, gpost_ref, wg_ref, wu_ref, wd_ref,
                     q_ref, kn_ref, vn_ref, kt_ref, vt_ref, cc_ref, cn_ref,
                     o_ref, oa_ref, u_s, acc_s):
    j = pl.program_id(1)
    last = len(FF_SPLITS) - 1
    for k, (c0, c1) in enumerate(FF_SPLITS):
        @pl.when(j == k)
        def _():
            if k == 0:
                u = _rms(x_ref[...], gpre_ref[...]).astype(BF16)
                u_s[...] = u
            else:
                u = u_s[...]
            h = (_silu(_dot(u, wg_ref[:, c0:c1])) * _dot(u, wu_ref[:, c0:c1])).astype(BF16)
            d = _dot(h, wd_ref[c0:c1, :])
            if k == 0:
                acc_s[...] = d
            elif k < last:
                acc_s[...] += d
            else:
                o_ref[...] = x_ref[...] + 0.5 * _rms(acc_s[...] + d, gpost_ref[...])
            _sample_attn_body(q_ref, kn_ref, vn_ref, kt_ref, vt_ref, cc_ref, cn_ref, oa_ref)


def _ffn_with_sample_attn(x, gpre, gpost, wg, wu, wd, q, kn, vn, cache_k, cache_v, t_new, past_len, tm):
    t = x.shape[0]
    nb, wb = cache_k.shape[0], cache_k.shape[1]
    nsplit = len(FF_SPLITS)
    assert t_new <= S_ROWS and nb == (t // tm) * nsplit

    def rows(a):
        return jnp.pad(a.reshape(nb, t_new, ATT_WIDTH), ((0, 0), (0, S_ROWS - t_new), (0, 0)))

    def grouped(c):
        return c.transpose(0, 2, 3, 1).reshape(nb, N_ATT_HEADS // S_HGRP, S_HGRP * HEAD_DIM, wb)

    row = pl.BlockSpec((tm, D_MODEL), lambda i, j: (i, 0))
    small = pl.BlockSpec((None, S_ROWS, ATT_WIDTH), lambda i, j: (i * nsplit + j, 0, 0))
    big = pl.BlockSpec((None, N_ATT_HEADS // S_HGRP, S_HGRP * HEAD_DIM, wb),
                       lambda i, j: (i * nsplit + j, 0, 0, 0))
    tabs = [jnp.asarray(a) for a in _sample_tables(t_new, past_len, wb)]
    h1, o = pl.pallas_call(
        _ffn_attn_kernel, grid=(t // tm, nsplit),
        in_specs=[row] + _ffn_specs() + [small, small, small, big, big]
        + [_full_spec(a.shape) for a in tabs],
        out_specs=[row, small],
        out_shape=[jax.ShapeDtypeStruct((t, D_MODEL), F32),
                   jax.ShapeDtypeStruct((nb, S_ROWS, ATT_WIDTH), F32)],
        scratch_shapes=[pltpu.VMEM((tm, D_MODEL), BF16), pltpu.VMEM((tm, D_MODEL), F32)],
        compiler_params=_params(2), name="ffn_attn_sample")(
            x, gpre, gpost, wg, wu, wd, rows(q), rows(kn), rows(vn),
            grouped(cache_k), grouped(cache_v), *tabs)
    return h1, o[:, :t_new].reshape(nb * t_new, ATT_WIDTH)


R_PAIR = 2


def _sample_ret_tables(t_new):
    r = np.arange(R_PAIR * t_new)
    seq, step = r // t_new, r % t_new
    lg = np.asarray(LOG_G)[:, None, None]
    diff = (step[:, None] - step[None, :])[None]
    ok = ((seq[:, None] == seq[None, :]) & (diff[0] >= 0))[None]
    decay = np.where(ok, np.exp(diff * lg), 0.0).reshape(N_RET_HEADS * r.size, r.size)
    w_head = np.exp((step[None, :] + 1.0) * lg[:, :, 0]).reshape(-1, 1) * np.ones((1, RET_DV))
    w_tail = np.exp((t_new - 1.0 - step)[:, None, None] * lg[None, :, :, 0]) * np.ones((1, 1, RET_DV))
    g_chunk = np.repeat(np.exp(t_new * lg[:, 0, 0]), RET_DK)[:, None] * np.ones((1, RET_DV))
    return [a.astype(np.float32) for a in (decay, w_head, w_tail.reshape(r.size, -1), g_chunk)]


def _sample_ret_kernel(q_ref, k_ref, v_ref, g_ref, gn_ref, st_ref, dec_ref, wh_ref, wt_ref, gc_ref,
                       y_ref, so_ref, *, blocks, t_new):
    nrow = R_PAIR * t_new
    lane_head = lax.broadcasted_iota(jnp.int32, (nrow, RET_QK), 1) // RET_DK
    row_seq = lax.broadcasted_iota(jnp.int32, (nrow, RET_QK), 0) // t_new
    out_seq = (lax.broadcasted_iota(jnp.int32, (N_RET_HEADS * nrow, RET_DV), 0) % nrow) // t_new
    for blk in range(blocks):
        rs = slice(blk * nrow, (blk + 1) * nrow)
        q8, k8, v8 = q_ref[rs, :], k_ref[rs, :], v_ref[rs, :]
        qm = jnp.concatenate([jnp.where(lane_head == h, q8, 0.0) for h in range(N_RET_HEADS)],
                             axis=0).astype(BF16)
        inner = (_dot_nt(qm, k8.astype(BF16)) * dec_ref[...]).astype(BF16)
        o_all = _dot(inner, v8.astype(BF16))
        vw = (v8 * wt_ref[...]).astype(BF16)
        cross = None
        for s in range(R_PAIR):
            state = st_ref[blk * R_PAIR + s]
            c = _dot(qm, state.astype(BF16))
            cross = c if cross is None else jnp.where(out_seq == s, c, cross)
            km = jnp.where(row_seq == s, k8, 0.0).astype(BF16)
            upd = _dot_tn(km, vw)
            upd = jnp.concatenate([upd[h * RET_DK:(h + 1) * RET_DK, h * RET_DV:(h + 1) * RET_DV]
                                   for h in range(N_RET_HEADS)], axis=0)
            so_ref[blk * R_PAIR + s] = gc_ref[...] * state + upd
        cross = cross * wh_ref[...]
        for h in range(N_RET_HEADS):
            hs = slice(h * RET_DV, (h + 1) * RET_DV)
            oh = o_all[h * nrow:(h + 1) * nrow, hs] + cross[h * nrow:(h + 1) * nrow]
            mu = jnp.mean(oh, axis=-1, keepdims=True)
            xc = oh - mu
            var = jnp.mean(xc * xc, axis=-1, keepdims=True)
            y = xc * lax.rsqrt(var + NORM_EPS) * gn_ref[:, hs]
            y_ref[rs, hs] = _silu(g_ref[rs, hs]) * y


def _ret_sample(qr, kr, vr, gr, gn, state, t_new, blocks=8):
    nb = state.shape[0]
    assert R_PAIR * t_new == 8 and nb % (R_PAIR * blocks) == 0
    rows = R_PAIR * t_new * blocks
    tabs = [jnp.asarray(a) for a in _sample_ret_tables(t_new)]
    st = state.reshape(nb, N_RET_HEADS * RET_DK, RET_DV)

    def blk(w_):
        return pl.BlockSpec((rows, w_), lambda i: (i, 0))

    st_spec = pl.BlockSpec((R_PAIR * blocks, N_RET_HEADS * RET_DK, RET_DV), lambda i: (i, 0, 0))
    y, s = pl.pallas_call(
        functools.partial(_sample_ret_kernel, blocks=blocks, t_new=t_new),
        grid=(nb // (R_PAIR * blocks),),
        in_specs=[blk(RET_QK), blk(RET_QK), blk(RET_V), blk(RET_V), _full_spec((1, RET_V)), st_spec]
        + [_full_spec(a.shape) for a in tabs],
        out_specs=[blk(RET_V), st_spec],
        out_shape=[jax.ShapeDtypeStruct((nb * t_new, RET_V), F32), jax.ShapeDtypeStruct(st.shape, F32)],
        compiler_params=_params(1), name="ret_sample")(qr, kr, vr, gr, gn, st, *tabs)
    return y, s.reshape(state.shape)


def kernel(x_prompt, x_sample, cache_k, cache_v, state_ret, g_ffn1_pre, g_ffn1_post, w1_gate, w1_up,
           w1_down, g_mix_pre, g_mix_post, w_in, gn_w, w_out, g_ffn2_pre, g_ffn2_post, w2_gate, w2_up,
           w2_down):
    b_p, s_p, _ = x_prompt.shape
    b_s, t_s, _ = x_sample.shape
    depth = w_in.shape[0]
    assert depth == 1 and b_p == 1
    keep = min(WIN_MAX, s_p)
    l = 0
    bf = lambda w: w[l].astype(BF16)
    vec = lambda g: g[l].reshape(1, -1)
    w1 = (bf(w1_gate), bf(w1_up), bf(w1_down))
    w2 = (bf(w2_gate), bf(w2_up), bf(w2_down))
    wi, wo = bf(w_in), bf(w_out)
    g1a, g1b, gma, gmb = vec(g_ffn1_pre), vec(g_ffn1_post), vec(g_mix_pre), vec(g_mix_post)
    g2a, g2b, gn = vec(g_ffn2_pre), vec(g_ffn2_post), vec(gn_w)

    n_s = b_s * t_s
    xs = x_sample.reshape(n_s, D_MODEL)
    h1s = _ffn(xs, g1a, g1b, *w1, n_s)
    qas, kks, vks, qrs, krs, vrs, grs = _mixin(
        h1s, gma, wi, np.full((1,), PAST_LEN), np.tile(np.arange(t_s), b_s), n_s, n_s, sample=True)

    tm = RUN
    xp = x_prompt.reshape(s_p, D_MODEL)
    h1, o_att_s = _ffn_with_sample_attn(xp, g1a, g1b, *w1, qas, kks, vks, cache_k[l], cache_v[l],
                                        t_s, PAST_LEN, tm)
    q4, k4, v4, q16, k16, v16, kk, vk, y_ret, st_p = _mixin(
        h1, gma, wi, tm * np.arange(s_p // tm), np.arange(tm), keep, tm, sample=False, gn=gn)
    o_att = _attn_prompt(q4, k4, v4, q16, k16, v16)
    y_prompt = _mixout_ffn(o_att, y_ret, h1, wo, gmb, g2a, g2b, *w2, tm)

    y_ret_s, st_s = _ret_sample(qrs, krs, vrs, grs, gn, state_ret[l], t_s)
    y_sample = _mixout_ffn(o_att_s, y_ret_s, h1s, wo, gmb, g2a, g2b, *w2, n_s)

    hd = (N_ATT_HEADS, HEAD_DIM)
    return (y_prompt.reshape(b_p, s_p, D_MODEL),
            y_sample.reshape(b_s, t_s, D_MODEL),
            kk.reshape(depth, b_p, keep, *hd),
            vk.reshape(depth, b_p, keep, *hd),
            st_p.reshape(depth, b_p, N_RET_HEADS, RET_DK, RET_DV),
            kks.reshape(depth, b_s, t_s, *hd),
            vks.reshape(depth, b_s, t_s, *hd),
            st_s.reshape(depth, b_s, N_RET_HEADS, RET_DK, RET_DV))
```

```python
import functools
import math

import numpy as np
import jax
import jax.numpy as jnp
from jax import lax
from jax.experimental import pallas as pl
from jax.experimental.pallas import tpu as pltpu

F32 = jnp.float32
BF16 = jnp.bfloat16

D_MODEL = 1024
D_FF = 2816
HEAD_DIM = 64
N_ATT_HEADS = 8
ATT_WIDTH = N_ATT_HEADS * HEAD_DIM
ROT_DIM = HEAD_DIM // 4
ROPE_THETA = 500000.0
WIN_MAX = 2048
PAST_LEN = 8192
BRANCHES = ((128, 1), (512, 4), (2048, 16))
N_RET_HEADS = 4
RET_DK = 64
RET_DV = 128
RET_QK = N_RET_HEADS * RET_DK
RET_V = N_RET_HEADS * RET_DV
RET_THETA = 10000.0
RET_CHUNK = 128
IN_WIDTH = 3 * ATT_WIDTH + 2 * RET_QK + 2 * RET_V
NORM_EPS = 1e-6
NEG = -1e30

LANES = 128
QB = 128
SUPER = 2048
RUN = 512
VMEM_LIMIT = 56 * 1024 * 1024
LOG_G = tuple(math.log1p(-2.0 ** (-5.0 - h)) for h in range(N_RET_HEADS))
LOG2_E = math.log2(math.e)


def _full_spec(shape):
    nd = len(shape)
    return pl.BlockSpec(shape, lambda *_: (0,) * nd)


def _resident_spec(shape):
    nd = len(shape)
    return pl.BlockSpec(shape, lambda *_: (0,) * nd, pipeline_mode=pl.Buffered(1))


def _params(n_axes):
    return pltpu.CompilerParams(dimension_semantics=("arbitrary",) * n_axes,
                                vmem_limit_bytes=VMEM_LIMIT)


def _rms(x, g):
    return x * lax.rsqrt(jnp.mean(x * x, axis=-1, keepdims=True) + NORM_EPS) * g


def _silu(x):
    return x / (1.0 + jnp.exp(-x))


def _dot(a, b):
    return jnp.dot(a, b, preferred_element_type=F32)


def _dot_nt(a, b):
    return lax.dot_general(a, b, (((1,), (1,)), ((), ())), preferred_element_type=F32)


def _dot_tn(a, b):
    return lax.dot_general(a, b, (((0,), (0,)), ((), ())), preferred_element_type=F32)


FF_CHUNK = 512


def _swiglu(u, wg_ref, wu_ref, wd_ref):
    acc = None
    for c0 in range(0, D_FF, FF_CHUNK):
        c1 = min(c0 + FF_CHUNK, D_FF)
        g = _dot(u, wg_ref[:, c0:c1])
        up = _dot(u, wu_ref[:, c0:c1])
        h = (_silu(g) * up).astype(BF16)
        d = _dot(h, wd_ref[c0:c1, :])
        acc = d if acc is None else acc + d
    return acc


def _ffn_kernel(x_ref, gpre_ref, gpost_ref, wg_ref, wu_ref, wd_ref, o_ref):
    x = x_ref[...]
    u = _rms(x, gpre_ref[...]).astype(BF16)
    y = _swiglu(u, wg_ref, wu_ref, wd_ref)
    o_ref[...] = x + 0.5 * _rms(y, gpost_ref[...])


def _mixout_ffn_kernel(oa_ref, yr_ref, h_ref, wo_ref, gmb_ref, gpre_ref, gpost_ref,
                       wg_ref, wu_ref, wd_ref, o_ref):
    mixed = jnp.concatenate([oa_ref[...].astype(BF16), yr_ref[...].astype(BF16)], axis=1)
    x = h_ref[...] + _rms(_dot(mixed, wo_ref[...]), gmb_ref[...])
    u = _rms(x, gpre_ref[...]).astype(BF16)
    y = _swiglu(u, wg_ref, wu_ref, wd_ref)
    o_ref[...] = x + 0.5 * _rms(y, gpost_ref[...])


def _ffn_specs():
    return [_full_spec((1, D_MODEL)), _full_spec((1, D_MODEL)),
            _resident_spec((D_MODEL, D_FF)), _resident_spec((D_MODEL, D_FF)),
            _resident_spec((D_FF, D_MODEL))]


def _ffn(x, gpre, gpost, wg, wu, wd, tm):
    t = x.shape[0]
    row = pl.BlockSpec((tm, D_MODEL), lambda i: (i, 0))
    return pl.pallas_call(
        _ffn_kernel, grid=(t // tm,),
        in_specs=[row] + _ffn_specs(), out_specs=row,
        out_shape=jax.ShapeDtypeStruct((t, D_MODEL), F32),
        compiler_params=_params(1), name="ffn")(x, gpre, gpost, wg, wu, wd)


def _mixout_ffn(oa, yr, h, wo, gmb, gpre, gpost, wg, wu, wd, tm):
    t = h.shape[0]
    row = pl.BlockSpec((tm, D_MODEL), lambda i: (i, 0))
    half = pl.BlockSpec((tm, ATT_WIDTH), lambda i: (i, 0))
    return pl.pallas_call(
        _mixout_ffn_kernel, grid=(t // tm,),
        in_specs=[half, half, row, _resident_spec((D_MODEL, D_MODEL)), _full_spec((1, D_MODEL))]
        + _ffn_specs(),
        out_specs=row, out_shape=jax.ShapeDtypeStruct((t, D_MODEL), F32),
        compiler_params=_params(1), name="mixout_ffn")(oa, yr, h, wo, gmb, gpre, gpost, wg, wu, wd)


def _rope_consts(base_pos, off_pos, rot_dim, theta, head_dim):
    half = rot_dim // 2
    inv = theta ** (-np.arange(half, dtype=np.float64) * (2.0 / rot_dim))
    lane = np.arange(LANES) % head_dim
    inv_l = inv[lane % half][None, :]
    first = (lane < half).astype(np.float64)[None, :]
    second = ((lane >= half) & (lane < rot_dim)).astype(np.float64)[None, :]
    rot = first + second
    a = np.asarray(base_pos, np.float64)[:, None] * inv_l
    b = np.asarray(off_pos, np.float64)[:, None] * inv_l
    cb, sb = np.cos(b), np.sin(b)
    base = np.stack([np.cos(a), np.sin(a)], axis=1)
    off = np.stack([cb * rot, sb * rot, np.broadcast_to(1.0 - rot, cb.shape),
                    -cb * first, -sb * first, cb * second, sb * second])
    return jnp.asarray(base, F32), jnp.asarray(off, F32)


def _rope_fill(base_ref, off_ref, tab_s):
    ca = base_ref[0:1, :]
    sa = base_ref[1:2, :]
    tab_s[0] = ca * off_ref[0] - sa * off_ref[1] + off_ref[2]
    tab_s[1] = sa * off_ref[3] + ca * off_ref[4]
    tab_s[2] = sa * off_ref[5] + ca * off_ref[6]


ROPE_ROWS = 64


def _rope_rows(tab_ref, half, rows):
    c, sa, sb = tab_ref[0, rows, :], tab_ref[1, rows, :], tab_ref[2, rows, :]
    return lambda x: x * c + pltpu.roll(x, LANES - half, 1) * sa + pltpu.roll(x, half, 1) * sb


def _lanes(r, s):
    return slice(r * ATT_WIDTH + s * LANES, r * ATT_WIDTH + (s + 1) * LANES)


def _emit_dilated(nat_s, x4_s, o4_ref, o16_ref):
    for s in range(ATT_WIDTH // LANES):
        for r in range(4):
            x4 = nat_s[s, pl.ds(r, RUN // 4, stride=4), :]
            o4_ref[:, _lanes(r, s)] = x4.astype(BF16)
            x4_s[r] = x4
        for r in range(4):
            for c in range(4):
                o16_ref[:, _lanes(4 * c + r, s)] = x4_s[r, pl.ds(c, RUN // 16, stride=4), :].astype(BF16)


def _mixin_kernel(h_ref, g_ref, w_ref, ba_ref, oa_ref, br_ref, or_ref, *refs, sample):
    if sample:
        qa_ref, kk_ref, vk_ref, qr_ref, kr_ref, vr_ref, gr_ref, ta_ref, tr_ref = refs
    else:
        (gn_ref, q4_ref, k4_ref, v4_ref, q16_ref, k16_ref, v16_ref, kk_ref, vk_ref, y_ref, st_ref,
         ta_ref, tr_ref, nat_s, natk_s, x4_s, qr_ref, kr_ref, vr_ref, gr_ref,
         dec_s, wt_s, wh_s, st_s) = refs
        _ret_init(dec_s, wt_s, wh_s, st_s)
    _rope_fill(ba_ref, oa_ref, ta_ref)
    _rope_fill(br_ref, or_ref, tr_ref)
    u = _rms(h_ref[...], g_ref[...]).astype(BF16)
    scale = HEAD_DIM ** -0.5 * LOG2_E
    nslab = ATT_WIDTH // LANES

    q = _dot(u, w_ref[:, 0:ATT_WIDTH])
    k = _dot(u, w_ref[:, ATT_WIDTH:2 * ATT_WIDTH])
    for rb in range(h_ref.shape[0] // ROPE_ROWS):
        rows = slice(rb * ROPE_ROWS, (rb + 1) * ROPE_ROWS)
        rope = _rope_rows(ta_ref, ROT_DIM // 2, rows)
        for s in range(nslab):
            sl = slice(s * LANES, (s + 1) * LANES)
            qs = rope(q[rows, sl]) * scale
            ks = rope(k[rows, sl])
            if sample:
                qa_ref[rows, sl] = qs
            else:
                nat_s[s, rows, :] = qs
                natk_s[s, rows, :] = ks
            kk_ref[rows, sl] = ks
    if not sample:
        _emit_dilated(nat_s, x4_s, q4_ref, q16_ref)
        _emit_dilated(natk_s, x4_s, k4_ref, k16_ref)

    v = _dot(u, w_ref[:, 2 * ATT_WIDTH:3 * ATT_WIDTH])
    if not sample:
        for s in range(nslab):
            nat_s[s] = v[:, s * LANES:(s + 1) * LANES]
        _emit_dilated(nat_s, x4_s, v4_ref, v16_ref)
    vk_ref[...] = v

    o = 3 * ATT_WIDTH
    qk = _dot(u, w_ref[:, o:o + 2 * RET_QK])
    for rb in range(h_ref.shape[0] // ROPE_ROWS):
        rows = slice(rb * ROPE_ROWS, (rb + 1) * ROPE_ROWS)
        rope = _rope_rows(tr_ref, RET_DK // 2, rows)
        for s in range(RET_QK // LANES):
            sl = slice(s * LANES, (s + 1) * LANES)
            sk = slice(RET_QK + s * LANES, RET_QK + (s + 1) * LANES)
            qr_ref[rows, sl] = rope(qk[rows, sl]).astype(qr_ref.dtype)
            kr_ref[rows, sl] = (rope(qk[rows, sk]) * (RET_DK ** -0.5)).astype(kr_ref.dtype)
    o += 2 * RET_QK
    vr_ref[...] = _dot(u, w_ref[:, o:o + RET_V]).astype(vr_ref.dtype)
    o += RET_V
    gr_ref[...] = _dot(u, w_ref[:, o:o + RET_V])
    if not sample:
        _ret_body(qr_ref, kr_ref, vr_ref, gr_ref, gn_ref, y_ref, st_ref, dec_s, wt_s, wh_s, st_s,
                  qr_ref.shape[0] // RET_CHUNK)


def _mixin(h, g, w, base_pos, off_pos, keep_rows, tm, sample, gn=None):
    t = h.shape[0]
    nt = t // tm
    first_keep = nt - keep_rows // tm
    base_a, off_a = _rope_consts(base_pos, off_pos, ROT_DIM, ROPE_THETA, HEAD_DIM)
    base_r, off_r = _rope_consts(base_pos, off_pos, RET_DK, RET_THETA, RET_DK)

    def row(w_):
        return pl.BlockSpec((tm, w_), lambda i: (i, 0))

    keep_spec = pl.BlockSpec((tm, ATT_WIDTH), lambda i: (jnp.maximum(i - first_keep, 0), 0))
    keep_shape = ((keep_rows, ATT_WIDTH), F32)
    base_spec = pl.BlockSpec((None, 2, LANES), lambda i: (i, 0, 0))
    off_spec = _resident_spec((7, tm, LANES))
    tab = pltpu.VMEM((3, tm, LANES), F32)
    ins = [h, g, w, base_a, off_a, base_r, off_r]
    in_specs = [row(D_MODEL), _full_spec((1, D_MODEL)), _resident_spec((D_MODEL, IN_WIDTH)),
                base_spec, off_spec, base_spec, off_spec]
    if sample:
        specs = [row(ATT_WIDTH), keep_spec, keep_spec, row(RET_QK), row(RET_QK), row(RET_V), row(RET_V)]
        shapes = [((t, ATT_WIDTH), F32), keep_shape, keep_shape,
                  ((t, RET_QK), F32), ((t, RET_QK), F32), ((t, RET_V), F32), ((t, RET_V), F32)]
        scratch = [tab, tab]
    else:
        assert tm == RUN and t % SUPER == 0
        rps = SUPER // RUN
        ins.append(gn)
        in_specs.append(_full_spec((1, RET_V)))
        s4 = pl.BlockSpec((None, RUN // 4, 4 * ATT_WIDTH), lambda i: (i, 0, 0))
        s16 = pl.BlockSpec((None, RUN // 16, 16 * ATT_WIDTH), lambda i: (i // rps, i % rps, 0))
        st_shape = (N_RET_HEADS // 2, 2 * RET_DK, RET_DV)
        specs = [s4] * 3 + [s16] * 3 + [keep_spec, keep_spec, row(RET_V), _full_spec(st_shape)]
        shapes = ([((t // RUN, RUN // 4, 4 * ATT_WIDTH), BF16)] * 3
                  + [((t // SUPER, SUPER // 16, 16 * ATT_WIDTH), BF16)] * 3
                  + [keep_shape, keep_shape, ((t, RET_V), BF16), (st_shape, F32)])
        dec = pltpu.VMEM((N_RET_HEADS, RET_CHUNK, RET_CHUNK), F32)
        nat = pltpu.VMEM((ATT_WIDTH // LANES, RUN, LANES), F32)
        scratch = [tab, tab, nat, nat, pltpu.VMEM((4, RUN // 4, LANES), F32),
                   pltpu.VMEM((tm, RET_QK), BF16), pltpu.VMEM((tm, RET_QK), BF16),
                   pltpu.VMEM((tm, RET_V), BF16), pltpu.VMEM((tm, RET_V), F32),
                   dec, dec, dec, pltpu.VMEM(st_shape, F32)]
    return pl.pallas_call(
        functools.partial(_mixin_kernel, sample=sample), grid=(nt,),
        in_specs=in_specs, out_specs=specs,
        out_shape=[jax.ShapeDtypeStruct(s, d) for s, d in shapes],
        scratch_shapes=scratch, compiler_params=_params(1), name="mix_in")(*ins)


def _attn_bias_tables():
    a = np.arange(QB)[:, None]
    c = np.arange(2 * QB)[None, :]
    steps = BRANCHES[0][0]
    dist = QB + a - c
    band = (dist >= 0) & (dist <= steps)
    cur = c >= QB
    tok_q = 4 * (a % 32) + a // 32
    cc = c % QB
    tok_k = 4 * (cc % 32) + cc // 32 + QB * (c // QB) - QB
    dist2 = tok_q - tok_k
    band2 = (dist2 >= 0) & (dist2 <= steps)
    masks = np.stack([band, band & cur, band2, band2 & cur])
    return np.where(masks, 0.0, NEG).astype(np.float32)


def _attn_qblock(get_q, get_kc, get_kp, get_vc, get_vp, bias, consts, sink):
    head_a, head_a_win, ones_a, ones_b = consts
    for s in range(ATT_WIDTH // LANES):
        q2 = get_q(s)
        kwin = jnp.concatenate([get_kp(s), get_kc(s)], axis=0)
        vwin = jnp.concatenate([get_vp(s), get_vc(s)], axis=0)
        zq = jnp.zeros_like(q2)
        qq = jnp.concatenate([jnp.where(head_a, q2, zq), jnp.where(head_a, zq, q2)], axis=0)
        sc = _dot_nt(qq, kwin)
        s_a = sc[:QB] + bias
        s_b = sc[QB:] + bias
        m_a = jnp.max(s_a, axis=-1, keepdims=True)
        m_b = jnp.max(s_b, axis=-1, keepdims=True)
        p = jnp.concatenate([jnp.exp2(s_a - m_a).astype(BF16), jnp.exp2(s_b - m_b).astype(BF16)],
                            axis=1)
        zv = jnp.zeros_like(vwin)
        w = jnp.concatenate(
            [jnp.concatenate([jnp.where(head_a_win, vwin, zv), ones_a], axis=1),
             jnp.concatenate([jnp.where(head_a_win, zv, vwin), ones_b], axis=1)], axis=0)
        r = _dot(p, w)
        sink(s, (jnp.where(head_a, m_a, m_b), r[:, LANES:], r[:, :LANES]))


def _merge(old, new):
    mo, lo, ao = old
    m2, l2, a2 = new
    mn = jnp.maximum(mo, m2)
    eo = jnp.exp2(mo - mn)
    e2 = jnp.exp2(m2 - mn)
    return mn, eo * lo + e2 * l2, eo * ao + e2 * a2


def _attn_kernel(q16, k16c, k16p, v16c, v16p, q4, k4c, k4p, v4c, v4p, bias_ref, o_ref,
                 m_s, l_s, a_s, nat_s):
    sb = pl.program_id(0)
    ph = pl.program_id(1)
    j = pl.program_id(2)
    lane = lax.broadcasted_iota(jnp.int32, (QB, LANES), 1)
    head_a = lane < HEAD_DIM
    lane_w = lax.broadcasted_iota(jnp.int32, (2 * QB, LANES), 1)
    head_a_win = lane_w < HEAD_DIM
    ones_a = jnp.where(head_a_win, 1.0, 0.0).astype(BF16)
    ones_b = jnp.where(head_a_win, 0.0, 1.0).astype(BF16)
    consts = (head_a, head_a_win, ones_a, ones_b)
    nslab = ATT_WIDTH // LANES
    sub = QB // 4
    lanes = _lanes

    @pl.when(ph == 0)
    def _():
        bias = bias_ref[(sb == 0).astype(jnp.int32)]
        for r in range(4):
            def scatter(s, res):
                for u in range(SUPER // RUN):
                    rows = pl.ds(u * RUN + r * QB + j, sub, stride=4)
                    for ref, val in zip((m_s, l_s, a_s), res):
                        ref[s, rows, :] = val[u * sub:(u + 1) * sub]

            _attn_qblock(lambda s: q16[:, lanes(r, s)], lambda s: k16c[:, lanes(r, s)],
                         lambda s: k16p[:, lanes(r, s)], lambda s: v16c[:, lanes(r, s)],
                         lambda s: v16p[:, lanes(r, s)], bias, consts, scatter)

    @pl.when(ph == 1)
    def _():
        bias = bias_ref[(sb * (SUPER // RUN) + j == 0).astype(jnp.int32)]
        for r in range(4):
            rows = pl.ds(pl.multiple_of(j * RUN + r * QB, QB), QB)

            def merge_in(s, res):
                mn, ln, an = _merge((m_s[s, rows, :], l_s[s, rows, :], a_s[s, rows, :]), res)
                m_s[s, rows, :] = mn
                l_s[s, rows, :] = ln
                a_s[s, rows, :] = an

            _attn_qblock(lambda s: q4[:, lanes(r, s)], lambda s: k4c[:, lanes(r, s)],
                         lambda s: k4p[:, lanes(r, s)], lambda s: v4c[:, lanes(r, s)],
                         lambda s: v4p[:, lanes(r, s)], bias, consts, merge_in)

    @pl.when(ph == 2)
    def _():
        first = sb * (SUPER // RUN) + j == 0
        for b in range(4):
            def cur(ref):
                return lambda s: jnp.concatenate(
                    [ref[b * sub:(b + 1) * sub, lanes(r, s)] for r in range(4)], axis=0)

            def prev(ref_c, ref_p):
                if b > 0:
                    return lambda s: jnp.concatenate(
                        [ref_c[(b - 1) * sub:b * sub, lanes(r, s)] for r in range(4)], axis=0)
                return lambda s: jnp.concatenate(
                    [ref_p[QB - sub:QB, lanes(r, s)] for r in range(4)], axis=0)

            if b == 0:
                bias = bias_ref[2 + first.astype(jnp.int32)]
            else:
                bias = bias_ref[2]
            def finish(s, res):
                for r in range(4):
                    rows = pl.ds(pl.multiple_of(j * RUN + r * QB + b * sub, sub), sub)
                    part = tuple(x[r * sub:(r + 1) * sub] for x in res)
                    _, ln, an = _merge((m_s[s, rows, :], l_s[s, rows, :], a_s[s, rows, :]), part)
                    nat_s[s, pl.ds(b * QB + r, sub, stride=4), :] = an / ln

            _attn_qblock(cur(q4), cur(k4c), prev(k4c, k4p), cur(v4c), prev(v4c, v4p),
                         bias, consts, finish)
        for s in range(nslab):
            o_ref[:, s * LANES:(s + 1) * LANES] = nat_s[s].astype(o_ref.dtype)


def _attn_prompt(q4, k4, v4, q16, k16, v16):
    nsb = q16.shape[0]
    s_len = nsb * SUPER
    rps = SUPER // RUN
    blk = (None, QB, 4 * ATT_WIDTH)

    def j16(ph, j):
        return jnp.where(ph == 0, j, rps - 1)

    def n4(sb, ph, j):
        return sb * rps + jnp.where(ph == 0, 0, j)

    c16 = pl.BlockSpec(blk, lambda sb, ph, j: (sb, 0, j16(ph, j)))
    p16 = pl.BlockSpec(blk, lambda sb, ph, j: (jnp.maximum(sb - 1, 0), 0, j16(ph, j)))
    c4 = pl.BlockSpec(blk, lambda sb, ph, j: (n4(sb, ph, j), 0, 0))
    p4 = pl.BlockSpec(blk, lambda sb, ph, j: (jnp.maximum(n4(sb, ph, j) - 1, 0), 0, 0))
    out = pl.BlockSpec((RUN, ATT_WIDTH), lambda sb, ph, j: (sb * rps + jnp.where(ph == 2, j, 0), 0))
    bias = jnp.asarray(_attn_bias_tables())
    nslab = ATT_WIDTH // LANES
    scratch = ([pltpu.VMEM((nslab, SUPER, LANES), F32) for _ in range(3)]
               + [pltpu.VMEM((nslab, RUN, LANES), F32)])
    return pl.pallas_call(
        _attn_kernel, grid=(nsb, 3, rps),
        in_specs=[c16, c16, p16, c16, p16, c4, c4, p4, c4, p4, _full_spec(bias.shape)],
        out_specs=out, out_shape=jax.ShapeDtypeStruct((s_len, ATT_WIDTH), BF16),
        scratch_shapes=scratch, compiler_params=_params(3), name="attn_prompt")(
            q16, k16, k16, v16, v16, q4, k4, k4, v4, v4, bias)


def _ret_init(dec_s, wt_s, wh_s, st_s):
    c = RET_CHUNK

    @pl.when(pl.program_id(0) == 0)
    def _():
        i = lax.broadcasted_iota(jnp.int32, (c, c), 0)
        jj = lax.broadcasted_iota(jnp.int32, (c, c), 1)
        diff = (i - jj).astype(F32)
        for h in range(N_RET_HEADS):
            dec_s[h] = jnp.where(diff >= 0, jnp.exp(diff * LOG_G[h]), 0.0)
            wt_s[h] = jnp.exp((c - 1.0 - i.astype(F32)) * LOG_G[h])
            wh_s[h] = jnp.exp((i.astype(F32) + 1.0) * LOG_G[h])
        st_s[...] = jnp.zeros_like(st_s)


def _ret_body(qr_ref, kr_ref, vr_ref, gr_ref, gn_ref, y_ref, s_ref, dec_s, wt_s, wh_s, st_s, chunks):
    c = RET_CHUNK
    lane = lax.broadcasted_iota(jnp.int32, (c, LANES), 1)
    head_a = lane < RET_DK
    row_a = lax.broadcasted_iota(jnp.int32, (LANES, RET_DV), 0) < RET_DK
    for ci in range(chunks):
        rows = slice(ci * c, (ci + 1) * c)
        for p in range(N_RET_HEADS // 2):
            sl = slice(p * LANES, (p + 1) * LANES)
            q2 = qr_ref[rows, sl]
            k2 = kr_ref[rows, sl]
            zq = jnp.zeros_like(q2)
            qq = jnp.concatenate([jnp.where(head_a, q2, zq), jnp.where(head_a, zq, q2)], axis=0)
            inner = _dot_nt(qq, k2)
            state = st_s[p]
            cross = _dot(qq, state.astype(BF16))
            vws = []
            for hh in range(2):
                h = 2 * p + hh
                hs = slice(h * RET_DV, (h + 1) * RET_DV)
                v = vr_ref[rows, hs]
                inn = (inner[hh * c:(hh + 1) * c] * dec_s[h]).astype(BF16)
                o = _dot(inn, v) + cross[hh * c:(hh + 1) * c] * wh_s[h]
                mu = jnp.mean(o, axis=-1, keepdims=True)
                xc = o - mu
                var = jnp.mean(xc * xc, axis=-1, keepdims=True)
                y = xc * lax.rsqrt(var + NORM_EPS) * gn_ref[:, hs]
                y_ref[rows, hs] = (_silu(gr_ref[rows, hs]) * y).astype(y_ref.dtype)
                vws.append((v.astype(F32) * wt_s[h]).astype(BF16))
            upd = _dot_tn(k2, jnp.concatenate(vws, axis=1))
            upd = jnp.where(row_a, upd[:, :RET_DV], upd[:, RET_DV:])
            gch = jnp.where(row_a, math.exp(c * LOG_G[2 * p]), math.exp(c * LOG_G[2 * p + 1]))
            st_s[p] = gch * state + upd

    @pl.when(pl.program_id(0) == pl.num_programs(0) - 1)
    def _():
        s_ref[...] = st_s[...]


S_ROWS = 8
S_HGRP = 4


def _sample_tables(t_new, past_len, wb):
    rows = np.concatenate([np.arange(wb), wb + np.arange(S_ROWS)])
    tab = np.zeros((S_ROWS, wb + S_ROWS), np.float32)
    for r in range(S_ROWS):
        i = r % t_new
        delta = wb + i - rows
        ok = (delta >= 0) & (past_len + i - delta >= 0) & (rows < wb + t_new)
        for window, dil in BRANCHES:
            tab[r] += ok & (delta % dil == 0) & (delta <= window)
    tab = np.tile(tab, (S_HGRP, 1))
    return tab[:, :wb], tab[:, wb:]


def _sample_attn_body(q_ref, kn_ref, vn_ref, kt_ref, vt_ref, cc_ref, cn_ref, o_ref):
    mult_c, mult_n = cc_ref[...], cn_ref[...]
    bias_c = jnp.where(mult_c > 0, 0.0, NEG)
    bias_n = jnp.where(mult_n > 0, 0.0, NEG)
    gw = S_HGRP * HEAD_DIM
    lane_head = lax.broadcasted_iota(jnp.int32, (S_ROWS, gw), 1) // HEAD_DIM
    for g in range(N_ATT_HEADS // S_HGRP):
        gs = slice(g * gw, (g + 1) * gw)
        q8 = q_ref[:, gs]
        q = jnp.concatenate([jnp.where(lane_head == h, q8, 0.0) for h in range(S_HGRP)],
                            axis=0).astype(BF16)
        s_c = _dot(q, kt_ref[g].astype(BF16)) + bias_c
        s_n = _dot_nt(q, kn_ref[:, gs].astype(BF16)) + bias_n
        m = jnp.maximum(jnp.max(s_c, axis=-1, keepdims=True), jnp.max(s_n, axis=-1, keepdims=True))
        p_c = jnp.exp2(s_c - m) * mult_c
        p_n = jnp.exp2(s_n - m) * mult_n
        l = jnp.sum(p_c, axis=-1, keepdims=True) + jnp.sum(p_n, axis=-1, keepdims=True)
        o = (_dot_nt(p_c.astype(BF16), vt_ref[g].astype(BF16))
             + _dot(p_n.astype(BF16), vn_ref[:, gs].astype(BF16))) / l
        o8 = jnp.zeros((S_ROWS, gw), F32)
        for h in range(S_HGRP):
            o8 = jnp.where(lane_head == h, o[h * S_ROWS:(h + 1) * S_ROWS], o8)
        o_ref[:, gs] = o8


FF_SPLITS = ((0, 768), (768, 1536), (1536, 2304), (2304, D_FF))


def _ffn_attn_kernel(x_ref, gpre_ref, gpost_ref, wg_ref, wu_ref, wd_ref,
                     q_ref, kn_ref, vn_ref, kt_ref, vt_ref, cc_ref, cn_ref,
                     o_ref, oa_ref, u_s, acc_s):
    j = pl.program_id(1)
    last = len(FF_SPLITS) - 1
    for k, (c0, c1) in enumerate(FF_SPLITS):
        @pl.when(j == k)
        def _():
            if k == 0:
                u = _rms(x_ref[...], gpre_ref[...]).astype(BF16)
                u_s[...] = u
            else:
                u = u_s[...]
            h = (_silu(_dot(u, wg_ref[:, c0:c1])) * _dot(u, wu_ref[:, c0:c1])).astype(BF16)
            d = _dot(h, wd_ref[c0:c1, :])
            if k == 0:
                acc_s[...] = d
            elif k < last:
                acc_s[...] += d
            else:
                o_ref[...] = x_ref[...] + 0.5 * _rms(acc_s[...] + d, gpost_ref[...])
            _sample_attn_body(q_ref, kn_ref, vn_ref, kt_ref, vt_ref, cc_ref, cn_ref, oa_ref)


def _ffn_with_sample_attn(x, gpre, gpost, wg, wu, wd, q, kn, vn, cache_k, cache_v, t_new, past_len, tm):
    t = x.shape[0]
    nb, wb = cache_k.shape[0], cache_k.shape[1]
    nsplit = len(FF_SPLITS)
    assert t_new <= S_ROWS and nb == (t // tm) * nsplit

    def rows(a):
        return jnp.pad(a.reshape(nb, t_new, ATT_WIDTH), ((0, 0), (0, S_ROWS - t_new), (0, 0)))

    def grouped(c):
        return c.transpose(0, 2, 3, 1).reshape(nb, N_ATT_HEADS // S_HGRP, S_HGRP * HEAD_DIM, wb)

    row = pl.BlockSpec((tm, D_MODEL), lambda i, j: (i, 0))
    small = pl.BlockSpec((None, S_ROWS, ATT_WIDTH), lambda i, j: (i * nsplit + j, 0, 0))
    big = pl.BlockSpec((None, N_ATT_HEADS // S_HGRP, S_HGRP * HEAD_DIM, wb),
                       lambda i, j: (i * nsplit + j, 0, 0, 0))
    tabs = [jnp.asarray(a) for a in _sample_tables(t_new, past_len, wb)]
    h1, o = pl.pallas_call(
        _ffn_attn_kernel, grid=(t // tm, nsplit),
        in_specs=[row] + _ffn_specs() + [small, small, small, big, big]
        + [_full_spec(a.shape) for a in tabs],
        out_specs=[row, small],
        out_shape=[jax.ShapeDtypeStruct((t, D_MODEL), F32),
                   jax.ShapeDtypeStruct((nb, S_ROWS, ATT_WIDTH), F32)],
        scratch_shapes=[pltpu.VMEM((tm, D_MODEL), BF16), pltpu.VMEM((tm, D_MODEL), F32)],
        compiler_params=_params(2), name="ffn_attn_sample")(
            x, gpre, gpost, wg, wu, wd, rows(q), rows(kn), rows(vn),
            grouped(cache_k), grouped(cache_v), *tabs)
    return h1, o[:, :t_new].reshape(nb * t_new, ATT_WIDTH)


R_PAIR = 2


def _sample_ret_tables(t_new):
    r = np.arange(R_PAIR * t_new)
    seq, step = r // t_new, r % t_new
    lg = np.asarray(LOG_G)[:, None, None]
    diff = (step[:, None] - step[None, :])[None]
    ok = ((seq[:, None] == seq[None, :]) & (diff[0] >= 0))[None]
    decay = np.where(ok, np.exp(diff * lg), 0.0).reshape(N_RET_HEADS * r.size, r.size)
    w_head = np.exp((step[None, :] + 1.0) * lg[:, :, 0]).reshape(-1, 1) * np.ones((1, RET_DV))
    w_tail = np.exp((t_new - 1.0 - step)[:, None, None] * lg[None, :, :, 0]) * np.ones((1, 1, RET_DV))
    g_chunk = np.repeat(np.exp(t_new * lg[:, 0, 0]), RET_DK)[:, None] * np.ones((1, RET_DV))
    return [a.astype(np.float32) for a in (decay, w_head, w_tail.reshape(r.size, -1), g_chunk)]


def _sample_ret_kernel(q_ref, k_ref, v_ref, g_ref, gn_ref, st_ref, dec_ref, wh_ref, wt_ref, gc_ref,
                       y_ref, so_ref, *, blocks, t_new):
    nrow = R_PAIR * t_new
    lane_head = lax.broadcasted_iota(jnp.int32, (nrow, RET_QK), 1) // RET_DK
    row_seq = lax.broadcasted_iota(jnp.int32, (nrow, RET_QK), 0) // t_new
    out_seq = (lax.broadcasted_iota(jnp.int32, (N_RET_HEADS * nrow, RET_DV), 0) % nrow) // t_new
    for blk in range(blocks):
        rs = slice(blk * nrow, (blk + 1) * nrow)
        q8, k8, v8 = q_ref[rs, :], k_ref[rs, :], v_ref[rs, :]
        qm = jnp.concatenate([jnp.where(lane_head == h, q8, 0.0) for h in range(N_RET_HEADS)],
                             axis=0).astype(BF16)
        inner = (_dot_nt(qm, k8.astype(BF16)) * dec_ref[...]).astype(BF16)
        o_all = _dot(inner, v8.astype(BF16))
        vw = (v8 * wt_ref[...]).astype(BF16)
        cross = None
        for s in range(R_PAIR):
            state = st_ref[blk * R_PAIR + s]
            c = _dot(qm, state.astype(BF16))
            cross = c if cross is None else jnp.where(out_seq == s, c, cross)
            km = jnp.where(row_seq == s, k8, 0.0).astype(BF16)
            upd = _dot_tn(km, vw)
            upd = jnp.concatenate([upd[h * RET_DK:(h + 1) * RET_DK, h * RET_DV:(h + 1) * RET_DV]
                                   for h in range(N_RET_HEADS)], axis=0)
            so_ref[blk * R_PAIR + s] = gc_ref[...] * state + upd
        cross = cross * wh_ref[...]
        for h in range(N_RET_HEADS):
            hs = slice(h * RET_DV, (h + 1) * RET_DV)
            oh = o_all[h * nrow:(h + 1) * nrow, hs] + cross[h * nrow:(h + 1) * nrow]
            mu = jnp.mean(oh, axis=-1, keepdims=True)
            xc = oh - mu
            var = jnp.mean(xc * xc, axis=-1, keepdims=True)
            y = xc * lax.rsqrt(var + NORM_EPS) * gn_ref[:, hs]
            y_ref[rs, hs] = _silu(g_ref[rs, hs]) * y


def _ret_sample(qr, kr, vr, gr, gn, state, t_new, blocks=8):
    nb = state.shape[0]
    assert R_PAIR * t_new == 8 and nb % (R_PAIR * blocks) == 0
    rows = R_PAIR * t_new * blocks
    tabs = [jnp.asarray(a) for a in _sample_ret_tables(t_new)]
    st = state.reshape(nb, N_RET_HEADS * RET_DK, RET_DV)

    def blk(w_):
        return pl.BlockSpec((rows, w_), lambda i: (i, 0))

    st_spec = pl.BlockSpec((R_PAIR * blocks, N_RET_HEADS * RET_DK, RET_DV), lambda i: (i, 0, 0))
    y, s = pl.pallas_call(
        functools.partial(_sample_ret_kernel, blocks=blocks, t_new=t_new),
        grid=(nb // (R_PAIR * blocks),),
        in_specs=[blk(RET_QK), blk(RET_QK), blk(RET_V), blk(RET_V), _full_spec((1, RET_V)), st_spec]
        + [_full_spec(a.shape) for a in tabs],
        out_specs=[blk(RET_V), st_spec],
        out_shape=[jax.ShapeDtypeStruct((nb * t_new, RET_V), F32), jax.ShapeDtypeStruct(st.shape, F32)],
        compiler_params=_params(1), name="ret_sample")(qr, kr, vr, gr, gn, st, *tabs)
    return y, s.reshape(state.shape)


def kernel(x_prompt, x_sample, cache_k, cache_v, state_ret, g_ffn1_pre, g_ffn1_post, w1_gate, w1_up,
           w1_down, g_mix_pre, g_mix_post, w_in, gn_w, w_out, g_ffn2_pre, g_ffn2_post, w2_gate, w2_up,
           w2_down):
    b_p, s_p, _ = x_prompt.shape
    b_s, t_s, _ = x_sample.shape
    depth = w_in.shape[0]
    assert depth == 1 and b_p == 1
    keep = min(WIN_MAX, s_p)
    l = 0
    bf = lambda w: w[l].astype(BF16)
    vec = lambda g: g[l].reshape(1, -1)
    w1 = (bf(w1_gate), bf(w1_up), bf(w1_down))
    w2 = (bf(w2_gate), bf(w2_up), bf(w2_down))
    wi, wo = bf(w_in), bf(w_out)
    g1a, g1b, gma, gmb = vec(g_ffn1_pre), vec(g_ffn1_post), vec(g_mix_pre), vec(g_mix_post)
    g2a, g2b, gn = vec(g_ffn2_pre), vec(g_ffn2_post), vec(gn_w)

    n_s = b_s * t_s
    xs = x_sample.reshape(n_s, D_MODEL)
    h1s = _ffn(xs, g1a, g1b, *w1, n_s)
    qas, kks, vks, qrs, krs, vrs, grs = _mixin(
        h1s, gma, wi, np.full((1,), PAST_LEN), np.tile(np.arange(t_s), b_s), n_s, n_s, sample=True)

    tm = RUN
    xp = x_prompt.reshape(s_p, D_MODEL)
    h1, o_att_s = _ffn_with_sample_attn(xp, g1a, g1b, *w1, qas, kks, vks, cache_k[l], cache_v[l],
                                        t_s, PAST_LEN, tm)
    q4, k4, v4, q16, k16, v16, kk, vk, y_ret, st_p = _mixin(
        h1, gma, wi, tm * np.arange(s_p // tm), np.arange(tm), keep, tm, sample=False, gn=gn)
    o_att = _attn_prompt(q4, k4, v4, q16, k16, v16)
    y_prompt = _mixout_ffn(o_att, y_ret, h1, wo, gmb, g2a, g2b, *w2, 2 * tm)

    y_ret_s, st_s = _ret_sample(qrs, krs, vrs, grs, gn, state_ret[l], t_s)
    y_sample = _mixout_ffn(o_att_s, y_ret_s, h1s, wo, gmb, g2a, g2b, *w2, n_s)

    hd = (N_ATT_HEADS, HEAD_DIM)
    return (y_prompt.reshape(b_p, s_p, D_MODEL),
            y_sample.reshape(b_s, t_s, D_MODEL),
            kk.reshape(depth, b_p, keep, *hd),
            vk.reshape(depth, b_p, keep, *hd),
            st_p.reshape(depth, b_p, N_RET_HEADS, RET_DK, RET_DV),
            kks.reshape(depth, b_s, t_s, *hd),
            vks.reshape(depth, b_s, t_s, *hd),
            st_s.reshape(depth, b_s, N_RET_HEADS, RET_DK, RET_DV))
```

```python
import functools
import math

import numpy as np
import jax
import jax.numpy as jnp
from jax import lax
from jax.experimental import pallas as pl
from jax.experimental.pallas import tpu as pltpu

F32 = jnp.float32
BF16 = jnp.bfloat16

D_MODEL = 1024
D_FF = 2816
HEAD_DIM = 64
N_ATT_HEADS = 8
ATT_WIDTH = N_ATT_HEADS * HEAD_DIM
ROT_DIM = HEAD_DIM // 4
ROPE_THETA = 500000.0
WIN_MAX = 2048
PAST_LEN = 8192
BRANCHES = ((128, 1), (512, 4), (2048, 16))
N_RET_HEADS = 4
RET_DK = 64
RET_DV = 128
RET_QK = N_RET_HEADS * RET_DK
RET_V = N_RET_HEADS * RET_DV
RET_THETA = 10000.0
RET_CHUNK = 128
IN_WIDTH = 3 * ATT_WIDTH + 2 * RET_QK + 2 * RET_V
NORM_EPS = 1e-6
NEG = -1e30

LANES = 128
QB = 128
SUPER = 2048
RUN = 512
VMEM_LIMIT = 56 * 1024 * 1024
LOG_G = tuple(math.log1p(-2.0 ** (-5.0 - h)) for h in range(N_RET_HEADS))
LOG2_E = math.log2(math.e)


def _full_spec(shape):
    nd = len(shape)
    return pl.BlockSpec(shape, lambda *_: (0,) * nd)


def _resident_spec(shape):
    nd = len(shape)
    return pl.BlockSpec(shape, lambda *_: (0,) * nd, pipeline_mode=pl.Buffered(1))


def _params(n_axes):
    return pltpu.CompilerParams(dimension_semantics=("arbitrary",) * n_axes,
                                vmem_limit_bytes=VMEM_LIMIT)


def _rms(x, g):
    return x * lax.rsqrt(jnp.mean(x * x, axis=-1, keepdims=True) + NORM_EPS) * g


def _silu(x):
    return x / (1.0 + jnp.exp(-x))


def _dot(a, b):
    return jnp.dot(a, b, preferred_element_type=F32)


def _dot_nt(a, b):
    return lax.dot_general(a, b, (((1,), (1,)), ((), ())), preferred_element_type=F32)


def _dot_tn(a, b):
    return lax.dot_general(a, b, (((0,), (0,)), ((), ())), preferred_element_type=F32)


FF_CHUNK = 512


def _swiglu(u, wg_ref, wu_ref, wd_ref):
    acc = None
    for c0 in range(0, D_FF, FF_CHUNK):
        c1 = min(c0 + FF_CHUNK, D_FF)
        g = _dot(u, wg_ref[:, c0:c1])
        up = _dot(u, wu_ref[:, c0:c1])
        h = (_silu(g) * up).astype(BF16)
        d = _dot(h, wd_ref[c0:c1, :])
        acc = d if acc is None else acc + d
    return acc


def _ffn_kernel(x_ref, gpre_ref, gpost_ref, wg_ref, wu_ref, wd_ref, o_ref):
    x = x_ref[...]
    u = _rms(x, gpre_ref[...]).astype(BF16)
    y = _swiglu(u, wg_ref, wu_ref, wd_ref)
    o_ref[...] = x + 0.5 * _rms(y, gpost_ref[...])


def _mixout_ffn_kernel(oa_ref, yr_ref, h_ref, wo_ref, gmb_ref, gpre_ref, gpost_ref,
                       wg_ref, wu_ref, wd_ref, o_ref):
    mixed = jnp.concatenate([oa_ref[...].astype(BF16), yr_ref[...].astype(BF16)], axis=1)
    x = h_ref[...] + _rms(_dot(mixed, wo_ref[...]), gmb_ref[...])
    u = _rms(x, gpre_ref[...]).astype(BF16)
    y = _swiglu(u, wg_ref, wu_ref, wd_ref)
    o_ref[...] = x + 0.5 * _rms(y, gpost_ref[...])


def _ffn_specs():
    return [_full_spec((1, D_MODEL)), _full_spec((1, D_MODEL)),
            _resident_spec((D_MODEL, D_FF)), _resident_spec((D_MODEL, D_FF)),
            _resident_spec((D_FF, D_MODEL))]


def _ffn(x, gpre, gpost, wg, wu, wd, tm):
    t = x.shape[0]
    row = pl.BlockSpec((tm, D_MODEL), lambda i: (i, 0))
    return pl.pallas_call(
        _ffn_kernel, grid=(t // tm,),
        in_specs=[row] + _ffn_specs(), out_specs=row,
        out_shape=jax.ShapeDtypeStruct((t, D_MODEL), F32),
        compiler_params=_params(1), name="ffn")(x, gpre, gpost, wg, wu, wd)


def _mixout_ffn(oa, yr, h, wo, gmb, gpre, gpost, wg, wu, wd, tm):
    t = h.shape[0]
    row = pl.BlockSpec((tm, D_MODEL), lambda i: (i, 0))
    half = pl.BlockSpec((tm, ATT_WIDTH), lambda i: (i, 0))
    return pl.pallas_call(
        _mixout_ffn_kernel, grid=(t // tm,),
        in_specs=[half, half, row, _resident_spec((D_MODEL, D_MODEL)), _full_spec((1, D_MODEL))]
        + _ffn_specs(),
        out_specs=row, out_shape=jax.ShapeDtypeStruct((t, D_MODEL), F32),
        compiler_params=_params(1), name="mixout_ffn")(oa, yr, h, wo, gmb, gpre, gpost, wg, wu, wd)


def _rope_consts(base_pos, off_pos, rot_dim, theta, head_dim):
    half = rot_dim // 2
    inv = theta ** (-np.arange(half, dtype=np.float64) * (2.0 / rot_dim))
    lane = np.arange(LANES) % head_dim
    inv_l = inv[lane % half][None, :]
    first = (lane < half).astype(np.float64)[None, :]
    second = ((lane >= half) & (lane < rot_dim)).astype(np.float64)[None, :]
    rot = first + second
    a = np.asarray(base_pos, np.float64)[:, None] * inv_l
    b = np.asarray(off_pos, np.float64)[:, None] * inv_l
    cb, sb = np.cos(b), np.sin(b)
    base = np.stack([np.cos(a), np.sin(a)], axis=1)
    off = np.stack([cb * rot, sb * rot, np.broadcast_to(1.0 - rot, cb.shape),
                    -cb * first, -sb * first, cb * second, sb * second])
    return jnp.asarray(base, F32), jnp.asarray(off, F32)


def _rope_fill(base_ref, off_ref, tab_s):
    ca = base_ref[0:1, :]
    sa = base_ref[1:2, :]
    tab_s[0] = ca * off_ref[0] - sa * off_ref[1] + off_ref[2]
    tab_s[1] = sa * off_ref[3] + ca * off_ref[4]
    tab_s[2] = sa * off_ref[5] + ca * off_ref[6]


ROPE_ROWS = 64


def _rope_rows(tab_ref, half, rows):
    c, sa, sb = tab_ref[0, rows, :], tab_ref[1, rows, :], tab_ref[2, rows, :]
    return lambda x: x * c + pltpu.roll(x, LANES - half, 1) * sa + pltpu.roll(x, half, 1) * sb


def _lanes(r, s):
    return slice(r * ATT_WIDTH + s * LANES, r * ATT_WIDTH + (s + 1) * LANES)


def _emit_dilated(nat_s, x4_s, o4_ref, o16_ref):
    for s in range(ATT_WIDTH // LANES):
        for r in range(4):
            x4 = nat_s[s, pl.ds(r, RUN // 4, stride=4), :]
            o4_ref[:, _lanes(r, s)] = x4.astype(BF16)
            x4_s[r] = x4
        for r in range(4):
            for c in range(4):
                o16_ref[:, _lanes(4 * c + r, s)] = x4_s[r, pl.ds(c, RUN // 16, stride=4), :].astype(BF16)


def _mixin_kernel(h_ref, g_ref, w_ref, ba_ref, oa_ref, br_ref, or_ref, *refs, sample):
    if sample:
        qa_ref, kk_ref, vk_ref, qr_ref, kr_ref, vr_ref, gr_ref, ta_ref, tr_ref = refs
    else:
        (gn_ref, q4_ref, k4_ref, v4_ref, q16_ref, k16_ref, v16_ref, kk_ref, vk_ref, y_ref, st_ref,
         ta_ref, tr_ref, nat_s, natk_s, x4_s, qr_ref, kr_ref, vr_ref, gr_ref,
         dec_s, wt_s, wh_s, st_s) = refs
        _ret_init(dec_s, wt_s, wh_s, st_s)
    _rope_fill(ba_ref, oa_ref, ta_ref)
    _rope_fill(br_ref, or_ref, tr_ref)
    u = _rms(h_ref[...], g_ref[...]).astype(BF16)
    scale = HEAD_DIM ** -0.5 * LOG2_E
    nslab = ATT_WIDTH // LANES

    q = _dot(u, w_ref[:, 0:ATT_WIDTH])
    k = _dot(u, w_ref[:, ATT_WIDTH:2 * ATT_WIDTH])
    for rb in range(h_ref.shape[0] // ROPE_ROWS):
        rows = slice(rb * ROPE_ROWS, (rb + 1) * ROPE_ROWS)
        rope = _rope_rows(ta_ref, ROT_DIM // 2, rows)
        for s in range(nslab):
            sl = slice(s * LANES, (s + 1) * LANES)
            qs = rope(q[rows, sl]) * scale
            ks = rope(k[rows, sl])
            if sample:
                qa_ref[rows, sl] = qs
            else:
                nat_s[s, rows, :] = qs
                natk_s[s, rows, :] = ks
            kk_ref[rows, sl] = ks
    if not sample:
        _emit_dilated(nat_s, x4_s, q4_ref, q16_ref)
        _emit_dilated(natk_s, x4_s, k4_ref, k16_ref)

    v = _dot(u, w_ref[:, 2 * ATT_WIDTH:3 * ATT_WIDTH])
    if not sample:
        for s in range(nslab):
            nat_s[s] = v[:, s * LANES:(s + 1) * LANES]
        _emit_dilated(nat_s, x4_s, v4_ref, v16_ref)
    vk_ref[...] = v

    o = 3 * ATT_WIDTH
    qk = _dot(u, w_ref[:, o:o + 2 * RET_QK])
    for rb in range(h_ref.shape[0] // ROPE_ROWS):
        rows = slice(rb * ROPE_ROWS, (rb + 1) * ROPE_ROWS)
        rope = _rope_rows(tr_ref, RET_DK // 2, rows)
        for s in range(RET_QK // LANES):
            sl = slice(s * LANES, (s + 1) * LANES)
            sk = slice(RET_QK + s * LANES, RET_QK + (s + 1) * LANES)
            qr_ref[rows, sl] = rope(qk[rows, sl]).astype(qr_ref.dtype)
            kr_ref[rows, sl] = (rope(qk[rows, sk]) * (RET_DK ** -0.5)).astype(kr_ref.dtype)
    o += 2 * RET_QK
    vr_ref[...] = _dot(u, w_ref[:, o:o + RET_V]).astype(vr_ref.dtype)
    o += RET_V
    gr_ref[...] = _dot(u, w_ref[:, o:o + RET_V])
    if not sample:
        _ret_body(qr_ref, kr_ref, vr_ref, gr_ref, gn_ref, y_ref, st_ref, dec_s, wt_s, wh_s, st_s,
                  qr_ref.shape[0] // RET_CHUNK)


def _mixin(h, g, w, base_pos, off_pos, keep_rows, tm, sample, gn=None):
    t = h.shape[0]
    nt = t // tm
    first_keep = nt - keep_rows // tm
    base_a, off_a = _rope_consts(base_pos, off_pos, ROT_DIM, ROPE_THETA, HEAD_DIM)
    base_r, off_r = _rope_consts(base_pos, off_pos, RET_DK, RET_THETA, RET_DK)

    def row(w_):
        return pl.BlockSpec((tm, w_), lambda i: (i, 0))

    keep_spec = pl.BlockSpec((tm, ATT_WIDTH), lambda i: (jnp.maximum(i - first_keep, 0), 0))
    keep_shape = ((keep_rows, ATT_WIDTH), F32)
    base_spec = pl.BlockSpec((None, 2, LANES), lambda i: (i, 0, 0))
    off_spec = _resident_spec((7, tm, LANES))
    tab = pltpu.VMEM((3, tm, LANES), F32)
    ins = [h, g, w, base_a, off_a, base_r, off_r]
    in_specs = [row(D_MODEL), _full_spec((1, D_MODEL)), _resident_spec((D_MODEL, IN_WIDTH)),
                base_spec, off_spec, base_spec, off_spec]
    if sample:
        specs = [row(ATT_WIDTH), keep_spec, keep_spec, row(RET_QK), row(RET_QK), row(RET_V), row(RET_V)]
        shapes = [((t, ATT_WIDTH), F32), keep_shape, keep_shape,
                  ((t, RET_QK), F32), ((t, RET_QK), F32), ((t, RET_V), F32), ((t, RET_V), F32)]
        scratch = [tab, tab]
    else:
        assert tm == RUN and t % SUPER == 0
        rps = SUPER // RUN
        ins.append(gn)
        in_specs.append(_full_spec((1, RET_V)))
        s4 = pl.BlockSpec((None, RUN // 4, 4 * ATT_WIDTH), lambda i: (i, 0, 0))
        s16 = pl.BlockSpec((None, RUN // 16, 16 * ATT_WIDTH), lambda i: (i // rps, i % rps, 0))
        st_shape = (N_RET_HEADS // 2, 2 * RET_DK, RET_DV)
        specs = [s4] * 3 + [s16] * 3 + [keep_spec, keep_spec, row(RET_V), _full_spec(st_shape)]
        shapes = ([((t // RUN, RUN // 4, 4 * ATT_WIDTH), BF16)] * 3
                  + [((t // SUPER, SUPER // 16, 16 * ATT_WIDTH), BF16)] * 3
                  + [keep_shape, keep_shape, ((t, RET_V), BF16), (st_shape, F32)])
        dec = pltpu.VMEM((N_RET_HEADS, RET_CHUNK, RET_CHUNK), F32)
        nat = pltpu.VMEM((ATT_WIDTH // LANES, RUN, LANES), F32)
        scratch = [tab, tab, nat, nat, pltpu.VMEM((4, RUN // 4, LANES), F32),
                   pltpu.VMEM((tm, RET_QK), BF16), pltpu.VMEM((tm, RET_QK), BF16),
                   pltpu.VMEM((tm, RET_V), BF16), pltpu.VMEM((tm, RET_V), F32),
                   dec, dec, dec, pltpu.VMEM(st_shape, F32)]
    return pl.pallas_call(
        functools.partial(_mixin_kernel, sample=sample), grid=(nt,),
        in_specs=in_specs, out_specs=specs,
        out_shape=[jax.ShapeDtypeStruct(s, d) for s, d in shapes],
        scratch_shapes=scratch, compiler_params=_params(1), name="mix_in")(*ins)


def _attn_bias_tables():
    a = np.arange(QB)[:, None]
    c = np.arange(2 * QB)[None, :]
    steps = BRANCHES[0][0]
    dist = QB + a - c
    band = (dist >= 0) & (dist <= steps)
    cur = c >= QB
    tok_q = 4 * (a % 32) + a // 32
    cc = c % QB
    tok_k = 4 * (cc % 32) + cc // 32 + QB * (c // QB) - QB
    dist2 = tok_q - tok_k
    band2 = (dist2 >= 0) & (dist2 <= steps)
    masks = np.stack([band, band & cur, band2, band2 & cur])
    return np.where(masks, 0.0, NEG).astype(np.float32)


def _attn_qblock(get_q, get_kc, get_kp, get_vc, get_vp, bias, consts, sink):
    head_a, head_a_win, ones_a, ones_b = consts
    for s in range(ATT_WIDTH // LANES):
        q2 = get_q(s)
        kwin = jnp.concatenate([get_kp(s), get_kc(s)], axis=0)
        vwin = jnp.concatenate([get_vp(s), get_vc(s)], axis=0)
        zq = jnp.zeros_like(q2)
        qq = jnp.concatenate([jnp.where(head_a, q2, zq), jnp.where(head_a, zq, q2)], axis=0)
        sc = _dot_nt(qq, kwin)
        s_a = sc[:QB] + bias
        s_b = sc[QB:] + bias
        m_a = jnp.max(s_a, axis=-1, keepdims=True)
        m_b = jnp.max(s_b, axis=-1, keepdims=True)
        p = jnp.concatenate([jnp.exp2(s_a - m_a).astype(BF16), jnp.exp2(s_b - m_b).astype(BF16)],
                            axis=1)
        zv = jnp.zeros_like(vwin)
        w = jnp.concatenate(
            [jnp.concatenate([jnp.where(head_a_win, vwin, zv), ones_a], axis=1),
             jnp.concatenate([jnp.where(head_a_win, zv, vwin), ones_b], axis=1)], axis=0)
        r = _dot(p, w)
        sink(s, (jnp.where(head_a, m_a, m_b), r[:, LANES:], r[:, :LANES]))


def _merge(old, new):
    mo, lo, ao = old
    m2, l2, a2 = new
    mn = jnp.maximum(mo, m2)
    eo = jnp.exp2(mo - mn)
    e2 = jnp.exp2(m2 - mn)
    return mn, eo * lo + e2 * l2, eo * ao + e2 * a2


def _attn_kernel(q16, k16c, k16p, v16c, v16p, q4, k4c, k4p, v4c, v4p, bias_ref, o_ref,
                 m_s, l_s, a_s, nat_s):
    sb = pl.program_id(0)
    ph = pl.program_id(1)
    j = pl.program_id(2)
    lane = lax.broadcasted_iota(jnp.int32, (QB, LANES), 1)
    head_a = lane < HEAD_DIM
    lane_w = lax.broadcasted_iota(jnp.int32, (2 * QB, LANES), 1)
    head_a_win = lane_w < HEAD_DIM
    ones_a = jnp.where(head_a_win, 1.0, 0.0).astype(BF16)
    ones_b = jnp.where(head_a_win, 0.0, 1.0).astype(BF16)
    consts = (head_a, head_a_win, ones_a, ones_b)
    nslab = ATT_WIDTH // LANES
    sub = QB // 4
    lanes = _lanes

    @pl.when(ph == 0)
    def _():
        bias = bias_ref[(sb == 0).astype(jnp.int32)]
        for r in range(4):
            def scatter(s, res):
                for u in range(SUPER // RUN):
                    rows = pl.ds(u * RUN + r * QB + j, sub, stride=4)
                    for ref, val in zip((m_s, l_s, a_s), res):
                        ref[s, rows, :] = val[u * sub:(u + 1) * sub]

            _attn_qblock(lambda s: q16[:, lanes(r, s)], lambda s: k16c[:, lanes(r, s)],
                         lambda s: k16p[:, lanes(r, s)], lambda s: v16c[:, lanes(r, s)],
                         lambda s: v16p[:, lanes(r, s)], bias, consts, scatter)

    @pl.when(ph == 1)
    def _():
        bias = bias_ref[(sb * (SUPER // RUN) + j == 0).astype(jnp.int32)]
        for r in range(4):
            rows = pl.ds(pl.multiple_of(j * RUN + r * QB, QB), QB)

            def merge_in(s, res):
                mn, ln, an = _merge((m_s[s, rows, :], l_s[s, rows, :], a_s[s, rows, :]), res)
                m_s[s, rows, :] = mn
                l_s[s, rows, :] = ln
                a_s[s, rows, :] = an

            _attn_qblock(lambda s: q4[:, lanes(r, s)], lambda s: k4c[:, lanes(r, s)],
                         lambda s: k4p[:, lanes(r, s)], lambda s: v4c[:, lanes(r, s)],
                         lambda s: v4p[:, lanes(r, s)], bias, consts, merge_in)

    @pl.when(ph == 2)
    def _():
        first = sb * (SUPER // RUN) + j == 0
        for b in range(4):
            def cur(ref):
                return lambda s: jnp.concatenate(
                    [ref[b * sub:(b + 1) * sub, lanes(r, s)] for r in range(4)], axis=0)

            def prev(ref_c, ref_p):
                if b > 0:
                    return lambda s: jnp.concatenate(
                        [ref_c[(b - 1) * sub:b * sub, lanes(r, s)] for r in range(4)], axis=0)
                return lambda s: jnp.concatenate(
                    [ref_p[QB - sub:QB, lanes(r, s)] for r in range(4)], axis=0)

            if b == 0:
                bias = bias_ref[2 + first.astype(jnp.int32)]
            else:
                bias = bias_ref[2]
            def finish(s, res):
                for r in range(4):
                    rows = pl.ds(pl.multiple_of(j * RUN + r * QB + b * sub, sub), sub)
                    part = tuple(x[r * sub:(r + 1) * sub] for x in res)
                    _, ln, an = _merge((m_s[s, rows, :], l_s[s, rows, :], a_s[s, rows, :]), part)
                    nat_s[s, pl.ds(b * QB + r, sub, stride=4), :] = an / ln

            _attn_qblock(cur(q4), cur(k4c), prev(k4c, k4p), cur(v4c), prev(v4c, v4p),
                         bias, consts, finish)
        for s in range(nslab):
            o_ref[:, s * LANES:(s + 1) * LANES] = nat_s[s].astype(o_ref.dtype)


def _attn_prompt(q4, k4, v4, q16, k16, v16):
    nsb = q16.shape[0]
    s_len = nsb * SUPER
    rps = SUPER // RUN
    blk = (None, QB, 4 * ATT_WIDTH)

    def j16(ph, j):
        return jnp.where(ph == 0, j, rps - 1)

    def n4(sb, ph, j):
        return sb * rps + jnp.where(ph == 0, 0, j)

    c16 = pl.BlockSpec(blk, lambda sb, ph, j: (sb, 0, j16(ph, j)))
    p16 = pl.BlockSpec(blk, lambda sb, ph, j: (jnp.maximum(sb - 1, 0), 0, j16(ph, j)))
    c4 = pl.BlockSpec(blk, lambda sb, ph, j: (n4(sb, ph, j), 0, 0))
    p4 = pl.BlockSpec(blk, lambda sb, ph, j: (jnp.maximum(n4(sb, ph, j) - 1, 0), 0, 0))
    out = pl.BlockSpec((RUN, ATT_WIDTH), lambda sb, ph, j: (sb * rps + jnp.where(ph == 2, j, 0), 0))
    bias = jnp.asarray(_attn_bias_tables())
    nslab = ATT_WIDTH // LANES
    scratch = ([pltpu.VMEM((nslab, SUPER, LANES), F32) for _ in range(3)]
               + [pltpu.VMEM((nslab, RUN, LANES), F32)])
    return pl.pallas_call(
        _attn_kernel, grid=(nsb, 3, rps),
        in_specs=[c16, c16, p16, c16, p16, c4, c4, p4, c4, p4, _full_spec(bias.shape)],
        out_specs=out, out_shape=jax.ShapeDtypeStruct((s_len, ATT_WIDTH), BF16),
        scratch_shapes=scratch, compiler_params=_params(3), name="attn_prompt")(
            q16, k16, k16, v16, v16, q4, k4, k4, v4, v4, bias)


def _ret_init(dec_s, wt_s, wh_s, st_s):
    c = RET_CHUNK

    @pl.when(pl.program_id(0) == 0)
    def _():
        i = lax.broadcasted_iota(jnp.int32, (c, c), 0)
        jj = lax.broadcasted_iota(jnp.int32, (c, c), 1)
        diff = (i - jj).astype(F32)
        for h in range(N_RET_HEADS):
            dec_s[h] = jnp.where(diff >= 0, jnp.exp(diff * LOG_G[h]), 0.0)
            wt_s[h] = jnp.exp((c - 1.0 - i.astype(F32)) * LOG_G[h])
            wh_s[h] = jnp.exp((i.astype(F32) + 1.0) * LOG_G[h])
        st_s[...] = jnp.zeros_like(st_s)


def _ret_body(qr_ref, kr_ref, vr_ref, gr_ref, gn_ref, y_ref, s_ref, dec_s, wt_s, wh_s, st_s, chunks):
    c = RET_CHUNK
    lane = lax.broadcasted_iota(jnp.int32, (c, LANES), 1)
    head_a = lane < RET_DK
    row_a = lax.broadcasted_iota(jnp.int32, (LANES, RET_DV), 0) < RET_DK
    for ci in range(chunks):
        rows = slice(ci * c, (ci + 1) * c)
        for p in range(N_RET_HEADS // 2):
            sl = slice(p * LANES, (p + 1) * LANES)
            q2 = qr_ref[rows, sl]
            k2 = kr_ref[rows, sl]
            zq = jnp.zeros_like(q2)
            qq = jnp.concatenate([jnp.where(head_a, q2, zq), jnp.where(head_a, zq, q2)], axis=0)
            inner = _dot_nt(qq, k2)
            state = st_s[p]
            cross = _dot(qq, state.astype(BF16))
            vws = []
            for hh in range(2):
                h = 2 * p + hh
                hs = slice(h * RET_DV, (h + 1) * RET_DV)
                v = vr_ref[rows, hs]
                inn = (inner[hh * c:(hh + 1) * c] * dec_s[h]).astype(BF16)
                o = _dot(inn, v) + cross[hh * c:(hh + 1) * c] * wh_s[h]
                mu = jnp.mean(o, axis=-1, keepdims=True)
                xc = o - mu
                var = jnp.mean(xc * xc, axis=-1, keepdims=True)
                y = xc * lax.rsqrt(var + NORM_EPS) * gn_ref[:, hs]
                y_ref[rows, hs] = (_silu(gr_ref[rows, hs]) * y).astype(y_ref.dtype)
                vws.append((v.astype(F32) * wt_s[h]).astype(BF16))
            upd = _dot_tn(k2, jnp.concatenate(vws, axis=1))
            upd = jnp.where(row_a, upd[:, :RET_DV], upd[:, RET_DV:])
            gch = jnp.where(row_a, math.exp(c * LOG_G[2 * p]), math.exp(c * LOG_G[2 * p + 1]))
            st_s[p] = gch * state + upd

    @pl.when(pl.program_id(0) == pl.num_programs(0) - 1)
    def _():
        s_ref[...] = st_s[...]


S_ROWS = 8
S_HGRP = 4


def _sample_tables(t_new, past_len, wb):
    rows = np.concatenate([np.arange(wb), wb + np.arange(S_ROWS)])
    tab = np.zeros((S_ROWS, wb + S_ROWS), np.float32)
    for r in range(S_ROWS):
        i = r % t_new
        delta = wb + i - rows
        ok = (delta >= 0) & (past_len + i - delta >= 0) & (rows < wb + t_new)
        for window, dil in BRANCHES:
            tab[r] += ok & (delta % dil == 0) & (delta <= window)
    tab = np.tile(tab, (S_HGRP, 1))
    return tab[:, :wb], tab[:, wb:]


def _sample_attn_body(q_ref, kn_ref, vn_ref, kt_ref, vt_ref, cc_ref, cn_ref, o_ref):
    mult_c, mult_n = cc_ref[...], cn_ref[...]
    bias_c = jnp.where(mult_c > 0, 0.0, NEG)
    bias_n = jnp.where(mult_n > 0, 0.0, NEG)
    gw = S_HGRP * HEAD_DIM
    lane_head = lax.broadcasted_iota(jnp.int32, (S_ROWS, gw), 1) // HEAD_DIM
    for g in range(N_ATT_HEADS // S_HGRP):
        gs = slice(g * gw, (g + 1) * gw)
        q8 = q_ref[:, gs]
        q = jnp.concatenate([jnp.where(lane_head == h, q8, 0.0) for h in range(S_HGRP)],
                            axis=0).astype(BF16)
        s_c = _dot(q, kt_ref[g].astype(BF16)) + bias_c
        s_n = _dot_nt(q, kn_ref[:, gs].astype(BF16)) + bias_n
        m = jnp.maximum(jnp.max(s_c, axis=-1, keepdims=True), jnp.max(s_n, axis=-1, keepdims=True))
        p_c = jnp.exp2(s_c - m) * mult_c
        p_n = jnp.exp2(s_n - m) * mult_n
        l = jnp.sum(p_c, axis=-1, keepdims=True) + jnp.sum(p_n, axis=-1, keepdims=True)
        o = (_dot_nt(p_c.astype(BF16), vt_ref[g].astype(BF16))
             + _dot(p_n.astype(BF16), vn_ref[:, gs].astype(BF16))) / l
        o8 = jnp.zeros((S_ROWS, gw), F32)
        for h in range(S_HGRP):
            o8 = jnp.where(lane_head == h, o[h * S_ROWS:(h + 1) * S_ROWS], o8)
        o_ref[:, gs] = o8


FF_SPLITS = ((0, 768), (768, 1536), (1536, 2304), (2304, D_FF))


def _ffn_attn_kernel(x_ref, gpre_ref, gpost_ref, wg_ref, wu_ref, wd_ref,
                     q_ref, kn_ref, vn_ref, kt_ref, vt_ref, cc_ref, cn_ref,
                     o_ref, oa_ref, u_s, acc_s):
    j = pl.program_id(1)
    last = len(FF_SPLITS) - 1
    for k, (c0, c1) in enumerate(FF_SPLITS):
        @pl.when(j == k)
        def _():
            if k == 0:
                u = _rms(x_ref[...], gpre_ref[...]).astype(BF16)
                u_s[...] = u
            else:
                u = u_s[...]
            h = (_silu(_dot(u, wg_ref[:, c0:c1])) * _dot(u, wu_ref[:, c0:c1])).astype(BF16)
            d = _dot(h, wd_ref[c0:c1, :])
            if k == 0:
                acc_s[...] = d
            elif k < last:
                acc_s[...] += d
            else:
                o_ref[...] = x_ref[...] + 0.5 * _rms(acc_s[...] + d, gpost_ref[...])
            _sample_attn_body(q_ref, kn_ref, vn_ref, kt_ref, vt_ref, cc_ref, cn_ref, oa_ref)


def _ffn_with_sample_attn(x, gpre, gpost, wg, wu, wd, q, kn, vn, cache_k, cache_v, t_new, past_len, tm):
    t = x.shape[0]
    nb, wb = cache_k.shape[0], cache_k.shape[1]
    nsplit = len(FF_SPLITS)
    assert t_new <= S_ROWS and nb == (t // tm) * nsplit

    def rows(a):
        return jnp.pad(a.reshape(nb, t_new, ATT_WIDTH), ((0, 0), (0, S_ROWS - t_new), (0, 0)))

    def grouped(c):
        return c.transpose(0, 2, 3, 1).reshape(nb, N_ATT_HEADS // S_HGRP, S_HGRP * HEAD_DIM, wb)

    row = pl.BlockSpec((tm, D_MODEL), lambda i, j: (i, 0))
    small = pl.BlockSpec((None, S_ROWS, ATT_WIDTH), lambda i, j: (i * nsplit + j, 0, 0))
    big = pl.BlockSpec((None, N_ATT_HEADS // S_HGRP, S_HGRP * HEAD_DIM, wb),
                       lambda i, j: (i * nsplit + j, 0, 0, 0))
    tabs = [jnp.asarray(a) for a in _sample_tables(t_new, past_len, wb)]
    h1, o = pl.pallas_call(
        _ffn_attn_kernel, grid=(t // tm, nsplit),
        in_specs=[row] + _ffn_specs() + [small, small, small, big, big]
        + [_full_spec(a.shape) for a in tabs],
        out_specs=[row, small],
        out_shape=[jax.ShapeDtypeStruct((t, D_MODEL), F32),
                   jax.ShapeDtypeStruct((nb, S_ROWS, ATT_WIDTH), F32)],
        scratch_shapes=[pltpu.VMEM((tm, D_MODEL), BF16), pltpu.VMEM((tm, D_MODEL), F32)],
        compiler_params=_params(2), name="ffn_attn_sample")(
            x, gpre, gpost, wg, wu, wd, rows(q), rows(kn), rows(vn),
            grouped(cache_k), grouped(cache_v), *tabs)
    return h1, o[:, :t_new].reshape(nb * t_new, ATT_WIDTH)


R_PAIR = 2


def _sample_ret_tables(t_new):
    r = np.arange(R_PAIR * t_new)
    seq, step = r // t_new, r % t_new
    lg = np.asarray(LOG_G)[:, None, None]
    diff = (step[:, None] - step[None, :])[None]
    ok = ((seq[:, None] == seq[None, :]) & (diff[0] >= 0))[None]
    decay = np.where(ok, np.exp(diff * lg), 0.0).reshape(N_RET_HEADS * r.size, r.size)
    w_head = np.exp((step[None, :] + 1.0) * lg[:, :, 0]).reshape(-1, 1) * np.ones((1, RET_DV))
    w_tail = np.exp((t_new - 1.0 - step)[:, None, None] * lg[None, :, :, 0]) * np.ones((1, 1, RET_DV))
    g_chunk = np.repeat(np.exp(t_new * lg[:, 0, 0]), RET_DK)[:, None] * np.ones((1, RET_DV))
    return [a.astype(np.float32) for a in (decay, w_head, w_tail.reshape(r.size, -1), g_chunk)]


def _sample_ret_kernel(q_ref, k_ref, v_ref, g_ref, gn_ref, st_ref, dec_ref, wh_ref, wt_ref, gc_ref,
                       y_ref, so_ref, *, blocks, t_new):
    nrow = R_PAIR * t_new
    lane_head = lax.broadcasted_iota(jnp.int32, (nrow, RET_QK), 1) // RET_DK
    row_seq = lax.broadcasted_iota(jnp.int32, (nrow, RET_QK), 0) // t_new
    out_seq = (lax.broadcasted_iota(jnp.int32, (N_RET_HEADS * nrow, RET_DV), 0) % nrow) // t_new
    for blk in range(blocks):
        rs = slice(blk * nrow, (blk + 1) * nrow)
        q8, k8, v8 = q_ref[rs, :], k_ref[rs, :], v_ref[rs, :]
        qm = jnp.concatenate([jnp.where(lane_head == h, q8, 0.0) for h in range(N_RET_HEADS)],
                             axis=0).astype(BF16)
        inner = (_dot_nt(qm, k8.astype(BF16)) * dec_ref[...]).astype(BF16)
        o_all = _dot(inner, v8.astype(BF16))
        vw = (v8 * wt_ref[...]).astype(BF16)
        cross = None
        for s in range(R_PAIR):
            state = st_ref[blk * R_PAIR + s]
            c = _dot(qm, state.astype(BF16))
            cross = c if cross is None else jnp.where(out_seq == s, c, cross)
            km = jnp.where(row_seq == s, k8, 0.0).astype(BF16)
            upd = _dot_tn(km, vw)
            upd = jnp.concatenate([upd[h * RET_DK:(h + 1) * RET_DK, h * RET_DV:(h + 1) * RET_DV]
                                   for h in range(N_RET_HEADS)], axis=0)
            so_ref[blk * R_PAIR + s] = gc_ref[...] * state + upd
        cross = cross * wh_ref[...]
        for h in range(N_RET_HEADS):
            hs = slice(h * RET_DV, (h + 1) * RET_DV)
            oh = o_all[h * nrow:(h + 1) * nrow, hs] + cross[h * nrow:(h + 1) * nrow]
            mu = jnp.mean(oh, axis=-1, keepdims=True)
            xc = oh - mu
            var = jnp.mean(xc * xc, axis=-1, keepdims=True)
            y = xc * lax.rsqrt(var + NORM_EPS) * gn_ref[:, hs]
            y_ref[rs, hs] = _silu(g_ref[rs, hs]) * y


def _ret_sample(qr, kr, vr, gr, gn, state, t_new, blocks=8):
    nb = state.shape[0]
    assert R_PAIR * t_new == 8 and nb % (R_PAIR * blocks) == 0
    rows = R_PAIR * t_new * blocks
    tabs = [jnp.asarray(a) for a in _sample_ret_tables(t_new)]
    st = state.reshape(nb, N_RET_HEADS * RET_DK, RET_DV)

    def blk(w_):
        return pl.BlockSpec((rows, w_), lambda i: (i, 0))

    st_spec = pl.BlockSpec((R_PAIR * blocks, N_RET_HEADS * RET_DK, RET_DV), lambda i: (i, 0, 0))
    y, s = pl.pallas_call(
        functools.partial(_sample_ret_kernel, blocks=blocks, t_new=t_new),
        grid=(nb // (R_PAIR * blocks),),
        in_specs=[blk(RET_QK), blk(RET_QK), blk(RET_V), blk(RET_V), _full_spec((1, RET_V)), st_spec]
        + [_full_spec(a.shape) for a in tabs],
        out_specs=[blk(RET_V), st_spec],
        out_shape=[jax.ShapeDtypeStruct((nb * t_new, RET_V), F32), jax.ShapeDtypeStruct(st.shape, F32)],
        compiler_params=_params(1), name="ret_sample")(qr, kr, vr, gr, gn, st, *tabs)
    return y, s.reshape(state.shape)


CONV_COLS = 256


def _ffn_conv_kernel(*refs, with_mix):
    if with_mix:
        (oa_ref, yr_ref, h_ref, wo_ref, gmb_ref, gpre_ref, gpost_ref, wg_ref, wu_ref, wd_ref,
         o_ref, wob_ref, wgb_ref, wub_ref, wdb_ref, x_s, u_s, acc_s) = refs
    else:
        (h_ref, gpre_ref, gpost_ref, wg_ref, wu_ref, wd_ref,
         o_ref, wgb_ref, wub_ref, wdb_ref, x_s, u_s, acc_s) = refs
    c = pl.program_id(0)

    @pl.when(c == 0)
    def _():
        x = h_ref[...]
        if with_mix:
            wo = wo_ref[...].astype(BF16)
            wob_ref[...] = wo
            mixed = jnp.concatenate([oa_ref[...].astype(BF16), yr_ref[...].astype(BF16)], axis=1)
            x = x + _rms(_dot(mixed, wo), gmb_ref[...])
        x_s[...] = x
        u_s[...] = _rms(x, gpre_ref[...]).astype(BF16)
        acc_s[...] = jnp.zeros_like(acc_s)

    wg, wu, wd = wg_ref[...].astype(BF16), wu_ref[...].astype(BF16), wd_ref[...].astype(BF16)
    wgb_ref[...] = wg
    wub_ref[...] = wu
    wdb_ref[...] = wd
    u = u_s[...]
    acc_s[...] += _dot((_silu(_dot(u, wg)) * _dot(u, wu)).astype(BF16), wd)

    @pl.when(c == pl.num_programs(0) - 1)
    def _():
        o_ref[...] = x_s[...] + 0.5 * _rms(acc_s[...], gpost_ref[...])


def _ffn_convert(h, gpre, gpost, wg, wu, wd, mix=None):
    t = h.shape[0]
    col = pl.BlockSpec((D_MODEL, CONV_COLS), lambda c: (0, c))
    rowb = pl.BlockSpec((CONV_COLS, D_MODEL), lambda c: (c, 0))
    vec = _full_spec((1, D_MODEL))
    ins, in_specs, outs, out_specs = [], [], [], []
    if mix is not None:
        att, ret, wo, gmb = mix
        ins += [att, ret]
        in_specs += [_full_spec((t, ATT_WIDTH)), _full_spec((t, RET_V))]
    ins.append(h)
    in_specs.append(_full_spec((t, D_MODEL)))
    if mix is not None:
        ins += [wo, gmb]
        in_specs += [_full_spec((D_MODEL, D_MODEL)), vec]
    ins += [gpre, gpost, wg, wu, wd]
    in_specs += [vec, vec, col, col, rowb]
    out_shape = [jax.ShapeDtypeStruct((t, D_MODEL), F32)]
    out_specs = [_full_spec((t, D_MODEL))]
    if mix is not None:
        out_shape.append(jax.ShapeDtypeStruct((D_MODEL, D_MODEL), BF16))
        out_specs.append(_full_spec((D_MODEL, D_MODEL)))
    out_shape += [jax.ShapeDtypeStruct((D_MODEL, D_FF), BF16)] * 2 + [jax.ShapeDtypeStruct((D_FF, D_MODEL), BF16)]
    out_specs += [col, col, rowb]
    return pl.pallas_call(
        functools.partial(_ffn_conv_kernel, with_mix=mix is not None), grid=(D_FF // CONV_COLS,),
        in_specs=in_specs, out_specs=out_specs, out_shape=out_shape,
        scratch_shapes=[pltpu.VMEM((t, D_MODEL), F32), pltpu.VMEM((t, D_MODEL), BF16),
                        pltpu.VMEM((t, D_MODEL), F32)],
        compiler_params=_params(1), name="ffn_convert")(*ins)


def _mixin_sample_kernel(h_ref, g_ref, w_ref, ba_ref, oa_ref, br_ref, or_ref,
                         qa_ref, kk_ref, vk_ref, qr_ref, kr_ref, vr_ref, gr_ref, wb_ref,
                         ta_s, tr_s, u_s):
    c = pl.program_id(0)

    @pl.when(c == 0)
    def _():
        _rope_fill(ba_ref, oa_ref, ta_s)
        _rope_fill(br_ref, or_ref, tr_s)
        u_s[...] = _rms(h_ref[...], g_ref[...]).astype(BF16)

    w = w_ref[...].astype(BF16)
    wb_ref[...] = w
    y = _dot(u_s[...], w)
    nblk = h_ref.shape[0] // ROPE_ROWS

    def rotary(tab_s, half, cols, dst_ref, scale):
        for rb in range(nblk):
            rows = slice(rb * ROPE_ROWS, (rb + 1) * ROPE_ROWS)
            rope = _rope_rows(tab_s, half, rows)
            for s in range(dst_ref.shape[1] // LANES):
                dst_ref[rows, s * LANES:(s + 1) * LANES] = rope(
                    y[rows, cols + s * LANES:cols + (s + 1) * LANES]) * scale

    @pl.when(c == 0)
    def _():
        rotary(ta_s, ROT_DIM // 2, 0, qa_ref, HEAD_DIM ** -0.5 * LOG2_E)

    @pl.when(c == 1)
    def _():
        rotary(ta_s, ROT_DIM // 2, 0, kk_ref, 1.0)

    @pl.when(c == 2)
    def _():
        vk_ref[...] = y

    @pl.when(c == 3)
    def _():
        rotary(tr_s, RET_DK // 2, 0, qr_ref, 1.0)
        rotary(tr_s, RET_DK // 2, RET_QK, kr_ref, RET_DK ** -0.5)

    @pl.when(c == 4)
    def _():
        vr_ref[...] = y

    @pl.when(c == 5)
    def _():
        gr_ref[...] = y


def _mixin_sample(h, g, w, base_pos, off_pos):
    t = h.shape[0]
    grp = ATT_WIDTH
    assert IN_WIDTH == 6 * grp and 2 * RET_QK == grp and RET_V == grp
    base_a, off_a = _rope_consts(base_pos, off_pos, ROT_DIM, ROPE_THETA, HEAD_DIM)
    base_r, off_r = _rope_consts(base_pos, off_pos, RET_DK, RET_THETA, RET_DK)
    wcol = pl.BlockSpec((D_MODEL, grp), lambda c: (0, c))
    base_spec = pl.BlockSpec((None, 2, LANES), lambda c: (0, 0, 0))
    widths = [ATT_WIDTH, ATT_WIDTH, ATT_WIDTH, RET_QK, RET_QK, RET_V, RET_V]
    tab = pltpu.VMEM((3, t, LANES), F32)
    return pl.pallas_call(
        _mixin_sample_kernel, grid=(IN_WIDTH // grp,),
        in_specs=[_full_spec((t, D_MODEL)), _full_spec((1, D_MODEL)), wcol,
                  base_spec, _full_spec(off_a.shape), base_spec, _full_spec(off_r.shape)],
        out_specs=[_full_spec((t, w_)) for w_ in widths] + [wcol],
        out_shape=[jax.ShapeDtypeStruct((t, w_), F32) for w_ in widths]
        + [jax.ShapeDtypeStruct((D_MODEL, IN_WIDTH), BF16)],
        scratch_shapes=[tab, tab, pltpu.VMEM((t, D_MODEL), BF16)],
        compiler_params=_params(1), name="mix_in_sample")(h, g, w, base_a, off_a, base_r, off_r)


def kernel(x_prompt, x_sample, cache_k, cache_v, state_ret, g_ffn1_pre, g_ffn1_post, w1_gate, w1_up,
           w1_down, g_mix_pre, g_mix_post, w_in, gn_w, w_out, g_ffn2_pre, g_ffn2_post, w2_gate, w2_up,
           w2_down):
    b_p, s_p, _ = x_prompt.shape
    b_s, t_s, _ = x_sample.shape
    depth = w_in.shape[0]
    assert depth == 1 and b_p == 1
    keep = min(WIN_MAX, s_p)
    l = 0
    vec = lambda g: g[l].reshape(1, -1)
    g1a, g1b, gma, gmb = vec(g_ffn1_pre), vec(g_ffn1_post), vec(g_mix_pre), vec(g_mix_post)
    g2a, g2b, gn = vec(g_ffn2_pre), vec(g_ffn2_post), vec(gn_w)

    n_s = b_s * t_s
    xs = x_sample.reshape(n_s, D_MODEL)
    h1s, *w1 = _ffn_convert(xs, g1a, g1b, w1_gate[l], w1_up[l], w1_down[l])
    qas, kks, vks, qrs, krs, vrs, grs, wi = _mixin_sample(
        h1s, gma, w_in[l], np.full((1,), PAST_LEN), np.tile(np.arange(t_s), b_s))

    tm = RUN
    xp = x_prompt.reshape(s_p, D_MODEL)
    h1, o_att_s = _ffn_with_sample_attn(xp, g1a, g1b, *w1, qas, kks, vks, cache_k[l], cache_v[l],
                                        t_s, PAST_LEN, tm)
    q4, k4, v4, q16, k16, v16, kk, vk, y_ret, st_p = _mixin(
        h1, gma, wi, tm * np.arange(s_p // tm), np.arange(tm), keep, tm, sample=False, gn=gn)
    o_att = _attn_prompt(q4, k4, v4, q16, k16, v16)

    y_ret_s, st_s = _ret_sample(qrs, krs, vrs, grs, gn, state_ret[l], t_s)
    y_sample, wo, *w2 = _ffn_convert(h1s, g2a, g2b, w2_gate[l], w2_up[l], w2_down[l],
                                     mix=(o_att_s, y_ret_s, w_out[l], gmb))
    y_prompt = _mixout_ffn(o_att, y_ret, h1, wo, gmb, g2a, g2b, *w2, 2 * tm)

    hd = (N_ATT_HEADS, HEAD_DIM)
    return (y_prompt.reshape(b_p, s_p, D_MODEL),
            y_sample.reshape(b_s, t_s, D_MODEL),
            kk.reshape(depth, b_p, keep, *hd),
            vk.reshape(depth, b_p, keep, *hd),
            st_p.reshape(depth, b_p, N_RET_HEADS, RET_DK, RET_DV),
            kks.reshape(depth, b_s, t_s, *hd),
            vks.reshape(depth, b_s, t_s, *hd),
            st_s.reshape(depth, b_s, N_RET_HEADS, RET_DK, RET_DV))
```

```python
import functools
import math

import numpy as np
import jax
import jax.numpy as jnp
from jax import lax
from jax.experimental import pallas as pl
from jax.experimental.pallas import tpu as pltpu

F32 = jnp.float32
BF16 = jnp.bfloat16

D_MODEL = 1024
D_FF = 2816
HEAD_DIM = 64
N_ATT_HEADS = 8
ATT_WIDTH = N_ATT_HEADS * HEAD_DIM
ROT_DIM = HEAD_DIM // 4
ROPE_THETA = 500000.0
WIN_MAX = 2048
PAST_LEN = 8192
BRANCHES = ((128, 1), (512, 4), (2048, 16))
N_RET_HEADS = 4
RET_DK = 64
RET_DV = 128
RET_QK = N_RET_HEADS * RET_DK
RET_V = N_RET_HEADS * RET_DV
RET_THETA = 10000.0
RET_CHUNK = 128
IN_WIDTH = 3 * ATT_WIDTH + 2 * RET_QK + 2 * RET_V
NORM_EPS = 1e-6
NEG = -1e30

LANES = 128
QB = 128
SUPER = 2048
RUN = 512
ATT_RUNS = 4
VMEM_LIMIT = 56 * 1024 * 1024
LOG_G = tuple(math.log1p(-2.0 ** (-5.0 - h)) for h in range(N_RET_HEADS))
LOG2_E = math.log2(math.e)


def _full_spec(shape):
    nd = len(shape)
    return pl.BlockSpec(shape, lambda *_: (0,) * nd)


def _resident_spec(shape):
    nd = len(shape)
    return pl.BlockSpec(shape, lambda *_: (0,) * nd, pipeline_mode=pl.Buffered(1))


def _params(n_axes):
    return pltpu.CompilerParams(dimension_semantics=("arbitrary",) * n_axes,
                                vmem_limit_bytes=VMEM_LIMIT)


def _rms(x, g):
    return x * lax.rsqrt(jnp.mean(x * x, axis=-1, keepdims=True) + NORM_EPS) * g


def _silu(x):
    return x / (1.0 + jnp.exp(-x))


def _dot(a, b):
    return jnp.dot(a, b, preferred_element_type=F32)


def _dot_nt(a, b):
    return lax.dot_general(a, b, (((1,), (1,)), ((), ())), preferred_element_type=F32)


def _dot_tn(a, b):
    return lax.dot_general(a, b, (((0,), (0,)), ((), ())), preferred_element_type=F32)


FF_CHUNK = 512


def _swiglu(u, wg_ref, wu_ref, wd_ref):
    acc = None
    for c0 in range(0, D_FF, FF_CHUNK):
        c1 = min(c0 + FF_CHUNK, D_FF)
        g = _dot(u, wg_ref[:, c0:c1])
        up = _dot(u, wu_ref[:, c0:c1])
        h = (_silu(g) * up).astype(BF16)
        d = _dot(h, wd_ref[c0:c1, :])
        acc = d if acc is None else acc + d
    return acc


def _mixout_ffn_kernel(oa_ref, yr_ref, h_ref, wo_ref, gmb_ref, gpre_ref, gpost_ref,
                       wg_ref, wu_ref, wd_ref, o_ref):
    mixed = jnp.concatenate([oa_ref[...].astype(BF16), yr_ref[...].astype(BF16)], axis=1)
    x = h_ref[...] + _rms(_dot(mixed, wo_ref[...]), gmb_ref[...])
    u = _rms(x, gpre_ref[...]).astype(BF16)
    y = _swiglu(u, wg_ref, wu_ref, wd_ref)
    o_ref[...] = x + 0.5 * _rms(y, gpost_ref[...])


def _ffn_specs():
    return [_full_spec((1, D_MODEL)), _full_spec((1, D_MODEL)),
            _resident_spec((D_MODEL, D_FF)), _resident_spec((D_MODEL, D_FF)),
            _resident_spec((D_FF, D_MODEL))]


def _mixout_ffn(oa, yr, h, wo, gmb, gpre, gpost, wg, wu, wd, tm):
    t = h.shape[0]
    row = pl.BlockSpec((tm, D_MODEL), lambda i: (i, 0))
    half = pl.BlockSpec((tm, ATT_WIDTH), lambda i: (i, 0))
    return pl.pallas_call(
        _mixout_ffn_kernel, grid=(t // tm,),
        in_specs=[half, half, row, _resident_spec((D_MODEL, D_MODEL)), _full_spec((1, D_MODEL))]
        + _ffn_specs(),
        out_specs=row, out_shape=jax.ShapeDtypeStruct((t, D_MODEL), F32),
        compiler_params=_params(1), name="mixout_ffn")(oa, yr, h, wo, gmb, gpre, gpost, wg, wu, wd)


def _rope_consts(base_pos, off_pos, rot_dim, theta, head_dim):
    half = rot_dim // 2
    inv = theta ** (-np.arange(half, dtype=np.float64) * (2.0 / rot_dim))
    lane = np.arange(LANES) % head_dim
    inv_l = inv[lane % half][None, :]
    first = (lane < half).astype(np.float64)[None, :]
    second = ((lane >= half) & (lane < rot_dim)).astype(np.float64)[None, :]
    rot = first + second
    a = np.asarray(base_pos, np.float64)[:, None] * inv_l
    b = np.asarray(off_pos, np.float64)[:, None] * inv_l
    cb, sb = np.cos(b), np.sin(b)
    base = np.stack([np.cos(a), np.sin(a)], axis=1)
    off = np.stack([cb * rot, sb * rot, np.broadcast_to(1.0 - rot, cb.shape),
                    -cb * first, -sb * first, cb * second, sb * second])
    return jnp.asarray(base, F32), jnp.asarray(off, F32)


def _rope_fill(base_ref, off_ref, tab_s):
    ca = base_ref[0:1, :]
    sa = base_ref[1:2, :]
    tab_s[0] = ca * off_ref[0] - sa * off_ref[1] + off_ref[2]
    tab_s[1] = sa * off_ref[3] + ca * off_ref[4]
    tab_s[2] = sa * off_ref[5] + ca * off_ref[6]


ROPE_ROWS = 64


def _rope_rows(tab_ref, half, rows):
    c, sa, sb = tab_ref[0, rows, :], tab_ref[1, rows, :], tab_ref[2, rows, :]
    return lambda x: x * c + pltpu.roll(x, LANES - half, 1) * sa + pltpu.roll(x, half, 1) * sb


def _lanes(r, s):
    return slice(r * ATT_WIDTH + s * LANES, r * ATT_WIDTH + (s + 1) * LANES)


def _emit_dilated(nat_s, x4_s, o4_ref, o16_ref):
    for s in range(ATT_WIDTH // LANES):
        for r in range(4):
            x4 = nat_s[s, pl.ds(r, RUN // 4, stride=4), :]
            o4_ref[:, _lanes(r, s)] = x4.astype(BF16)
            x4_s[r] = x4
        for r in range(4):
            for c in range(4):
                o16_ref[:, _lanes(4 * c + r, s)] = x4_s[r, pl.ds(c, RUN // 16, stride=4), :].astype(BF16)


def _mixin_kernel(h_ref, g_ref, w_ref, ba_ref, oa_ref, br_ref, or_ref, gn_ref,
                  q4_ref, k4_ref, v4_ref, q16_ref, k16_ref, v16_ref, kk_ref, vk_ref, y_ref, st_ref,
                  ta_ref, tr_ref, nat_s, natk_s, x4_s, qr_ref, kr_ref, vr_ref, gr_ref,
                  dec_s, wt_s, wh_s, st_s):
    _ret_init(dec_s, wt_s, wh_s, st_s)
    _rope_fill(ba_ref, oa_ref, ta_ref)
    _rope_fill(br_ref, or_ref, tr_ref)
    u = _rms(h_ref[...], g_ref[...]).astype(BF16)
    scale = HEAD_DIM ** -0.5 * LOG2_E
    nslab = ATT_WIDTH // LANES

    q = _dot(u, w_ref[:, 0:ATT_WIDTH])
    k = _dot(u, w_ref[:, ATT_WIDTH:2 * ATT_WIDTH])
    for rb in range(h_ref.shape[0] // ROPE_ROWS):
        rows = slice(rb * ROPE_ROWS, (rb + 1) * ROPE_ROWS)
        rope = _rope_rows(ta_ref, ROT_DIM // 2, rows)
        for s in range(nslab):
            sl = slice(s * LANES, (s + 1) * LANES)
            qs = rope(q[rows, sl]) * scale
            ks = rope(k[rows, sl])
            nat_s[s, rows, :] = qs
            natk_s[s, rows, :] = ks
            kk_ref[rows, sl] = ks
    _emit_dilated(nat_s, x4_s, q4_ref, q16_ref)
    _emit_dilated(natk_s, x4_s, k4_ref, k16_ref)

    v = _dot(u, w_ref[:, 2 * ATT_WIDTH:3 * ATT_WIDTH])
    for s in range(nslab):
        nat_s[s] = v[:, s * LANES:(s + 1) * LANES]
    _emit_dilated(nat_s, x4_s, v4_ref, v16_ref)
    vk_ref[...] = v

    o = 3 * ATT_WIDTH
    qk = _dot(u, w_ref[:, o:o + 2 * RET_QK])
    for rb in range(h_ref.shape[0] // ROPE_ROWS):
        rows = slice(rb * ROPE_ROWS, (rb + 1) * ROPE_ROWS)
        rope = _rope_rows(tr_ref, RET_DK // 2, rows)
        for s in range(RET_QK // LANES):
            sl = slice(s * LANES, (s + 1) * LANES)
            sk = slice(RET_QK + s * LANES, RET_QK + (s + 1) * LANES)
            qr_ref[rows, sl] = rope(qk[rows, sl]).astype(qr_ref.dtype)
            kr_ref[rows, sl] = (rope(qk[rows, sk]) * (RET_DK ** -0.5)).astype(kr_ref.dtype)
    o += 2 * RET_QK
    vr_ref[...] = _dot(u, w_ref[:, o:o + RET_V]).astype(vr_ref.dtype)
    o += RET_V
    gr_ref[...] = _dot(u, w_ref[:, o:o + RET_V])
    _ret_body(qr_ref, kr_ref, vr_ref, gr_ref, gn_ref, y_ref, st_ref, dec_s, wt_s, wh_s, st_s,
              qr_ref.shape[0] // RET_CHUNK)


def _mixin(h, g, w, base_pos, off_pos, keep_rows, tm, gn):
    t = h.shape[0]
    nt = t // tm
    assert tm == RUN and t % SUPER == 0
    first_keep = nt - keep_rows // tm
    base_a, off_a = _rope_consts(base_pos, off_pos, ROT_DIM, ROPE_THETA, HEAD_DIM)
    base_r, off_r = _rope_consts(base_pos, off_pos, RET_DK, RET_THETA, RET_DK)

    def row(w_):
        return pl.BlockSpec((tm, w_), lambda i: (i, 0))

    keep_spec = pl.BlockSpec((tm, ATT_WIDTH), lambda i: (jnp.maximum(i - first_keep, 0), 0))
    keep_shape = ((keep_rows, ATT_WIDTH), F32)
    base_spec = pl.BlockSpec((None, 2, LANES), lambda i: (i, 0, 0))
    off_spec = _resident_spec((7, tm, LANES))
    tab = pltpu.VMEM((3, tm, LANES), F32)
    in_specs = [row(D_MODEL), _full_spec((1, D_MODEL)), _resident_spec((D_MODEL, IN_WIDTH)),
                base_spec, off_spec, base_spec, off_spec, _full_spec((1, RET_V))]
    rps = SUPER // RUN
    s4 = pl.BlockSpec((None, RUN // 4, 4 * ATT_WIDTH), lambda i: (i, 0, 0))
    s16 = pl.BlockSpec((None, RUN // 16, 16 * ATT_WIDTH), lambda i: (i // rps, i % rps, 0))
    st_shape = (N_RET_HEADS // 2, 2 * RET_DK, RET_DV)
    specs = [s4] * 3 + [s16] * 3 + [keep_spec, keep_spec, row(RET_V), _full_spec(st_shape)]
    shapes = ([((t // RUN, RUN // 4, 4 * ATT_WIDTH), BF16)] * 3
              + [((t // SUPER, SUPER // 16, 16 * ATT_WIDTH), BF16)] * 3
              + [keep_shape, keep_shape, ((t, RET_V), BF16), (st_shape, F32)])
    dec = pltpu.VMEM((N_RET_HEADS, RET_CHUNK, RET_CHUNK), F32)
    nat = pltpu.VMEM((ATT_WIDTH // LANES, RUN, LANES), F32)
    scratch = [tab, tab, nat, nat, pltpu.VMEM((4, RUN // 4, LANES), F32),
               pltpu.VMEM((tm, RET_QK), BF16), pltpu.VMEM((tm, RET_QK), BF16),
               pltpu.VMEM((tm, RET_V), BF16), pltpu.VMEM((tm, RET_V), F32),
               dec, dec, dec, pltpu.VMEM(st_shape, F32)]
    return pl.pallas_call(
        _mixin_kernel, grid=(nt,), in_specs=in_specs, out_specs=specs,
        out_shape=[jax.ShapeDtypeStruct(s, d) for s, d in shapes],
        scratch_shapes=scratch, compiler_params=_params(1), name="mix_in")(
            h, g, w, base_a, off_a, base_r, off_r, gn)


def _attn_bias_tables():
    a = np.arange(QB)[:, None]
    c = np.arange(2 * QB)[None, :]
    steps = BRANCHES[0][0]
    dist = QB + a - c
    band = (dist >= 0) & (dist <= steps)
    cur = c >= QB
    tok_q = 4 * (a % 32) + a // 32
    cc = c % QB
    tok_k = 4 * (cc % 32) + cc // 32 + QB * (c // QB) - QB
    dist2 = tok_q - tok_k
    band2 = (dist2 >= 0) & (dist2 <= steps)
    masks = np.stack([band, band & cur, band2, band2 & cur])
    return np.where(masks, 0.0, NEG).astype(np.float32)


def _attn_qblock(get_q, get_kc, get_kp, get_vc, get_vp, bias, consts, sink):
    head_a, head_a_win, ones_a, ones_b = consts
    for s in range(ATT_WIDTH // LANES):
        q2 = get_q(s)
        kwin = jnp.concatenate([get_kp(s), get_kc(s)], axis=0)
        vwin = jnp.concatenate([get_vp(s), get_vc(s)], axis=0)
        zq = jnp.zeros_like(q2)
        qq = jnp.concatenate([jnp.where(head_a, q2, zq), jnp.where(head_a, zq, q2)], axis=0)
        sc = _dot_nt(qq, kwin)
        s_a = sc[:QB] + bias
        s_b = sc[QB:] + bias
        m_a = jnp.max(s_a, axis=-1, keepdims=True)
        m_b = jnp.max(s_b, axis=-1, keepdims=True)
        p = jnp.concatenate([jnp.exp2(s_a - m_a).astype(BF16), jnp.exp2(s_b - m_b).astype(BF16)],
                            axis=1)
        zv = jnp.zeros_like(vwin)
        w = jnp.concatenate(
            [jnp.concatenate([jnp.where(head_a_win, vwin, zv), ones_a], axis=1),
             jnp.concatenate([jnp.where(head_a_win, zv, vwin), ones_b], axis=1)], axis=0)
        r = _dot(p, w)
        sink(s, (jnp.where(head_a, m_a, m_b), r[:, LANES:], r[:, :LANES]))


def _merge(old, new):
    mo, lo, ao = old
    m2, l2, a2 = new
    mn = jnp.maximum(mo, m2)
    eo = jnp.exp2(mo - mn)
    e2 = jnp.exp2(m2 - mn)
    return mn, eo * lo + e2 * l2, eo * ao + e2 * a2


def _attn_kernel(q16, k16c, k16p, v16c, v16p, q4, k4c, k4p, v4c, v4p, bias_ref, o_ref,
                 m_s, l_s, a_s, nat_s):
    sb = pl.program_id(0)
    ph = pl.program_id(1)
    j = pl.program_id(2)
    lane = lax.broadcasted_iota(jnp.int32, (QB, LANES), 1)
    head_a = lane < HEAD_DIM
    lane_w = lax.broadcasted_iota(jnp.int32, (2 * QB, LANES), 1)
    head_a_win = lane_w < HEAD_DIM
    ones_a = jnp.where(head_a_win, 1.0, 0.0).astype(BF16)
    ones_b = jnp.where(head_a_win, 0.0, 1.0).astype(BF16)
    consts = (head_a, head_a_win, ones_a, ones_b)
    nslab = ATT_WIDTH // LANES
    sub = QB // 4
    lanes = _lanes

    run0 = j * ATT_RUNS
    first_run = sb * (SUPER // RUN) + run0 == 0

    @pl.when(ph == 0)
    def _():
        bias = bias_ref[(sb == 0).astype(jnp.int32)]
        for r in range(4 * ATT_RUNS):
            def scatter(s, res):
                for u in range(SUPER // RUN):
                    rows = pl.ds(u * RUN + (r % 4) * QB + run0 + r // 4, sub, stride=4)
                    for ref, val in zip((m_s, l_s, a_s), res):
                        ref[s, rows, :] = val[u * sub:(u + 1) * sub]

            _attn_qblock(lambda s: q16[:, lanes(r, s)], lambda s: k16c[:, lanes(r, s)],
                         lambda s: k16p[:, lanes(r, s)], lambda s: v16c[:, lanes(r, s)],
                         lambda s: v16p[:, lanes(r, s)], bias, consts, scatter)

    def prev_rows(ref_c, ref_p, w, rows):
        return (lambda cols: ref_p[rows, cols]) if w == 0 else (lambda cols: ref_c[w - 1, rows, cols])

    @pl.when(ph == 1)
    def _():
        for w in range(ATT_RUNS):
            bias = bias_ref[first_run.astype(jnp.int32)] if w == 0 else bias_ref[0]
            kp = prev_rows(k4c, k4p, w, slice(None))
            vp = prev_rows(v4c, v4p, w, slice(None))
            for r in range(4):
                rows = pl.ds(pl.multiple_of((run0 + w) * RUN + r * QB, QB), QB)

                def merge_in(s, res):
                    mn, ln, an = _merge((m_s[s, rows, :], l_s[s, rows, :], a_s[s, rows, :]), res)
                    m_s[s, rows, :] = mn
                    l_s[s, rows, :] = ln
                    a_s[s, rows, :] = an

                _attn_qblock(lambda s: q4[w, :, lanes(r, s)], lambda s: k4c[w, :, lanes(r, s)],
                             lambda s: kp(lanes(r, s)), lambda s: v4c[w, :, lanes(r, s)],
                             lambda s: vp(lanes(r, s)), bias, consts, merge_in)

    @pl.when(ph == 2)
    def _():
        for w in range(ATT_RUNS):
            for b in range(4):
                def cur(ref):
                    return lambda s: jnp.concatenate(
                        [ref[w, b * sub:(b + 1) * sub, lanes(r, s)] for r in range(4)], axis=0)

                def prev(ref_c, ref_p):
                    if b > 0:
                        return lambda s: jnp.concatenate(
                            [ref_c[w, (b - 1) * sub:b * sub, lanes(r, s)] for r in range(4)], axis=0)
                    tail = prev_rows(ref_c, ref_p, w, slice(QB - sub, QB))
                    return lambda s: jnp.concatenate([tail(lanes(r, s)) for r in range(4)], axis=0)

                if w == 0 and b == 0:
                    bias = bias_ref[2 + first_run.astype(jnp.int32)]
                else:
                    bias = bias_ref[2]

                def finish(s, res):
                    for r in range(4):
                        rows = pl.ds(pl.multiple_of((run0 + w) * RUN + r * QB + b * sub, sub), sub)
                        part = tuple(x[r * sub:(r + 1) * sub] for x in res)
                        _, ln, an = _merge((m_s[s, rows, :], l_s[s, rows, :], a_s[s, rows, :]), part)
                        nat_s[s, pl.ds(w * RUN + b * QB + r, sub, stride=4), :] = an / ln

                _attn_qblock(cur(q4), cur(k4c), prev(k4c, k4p), cur(v4c), prev(v4c, v4p),
                             bias, consts, finish)
        for s in range(nslab):
            o_ref[:, s * LANES:(s + 1) * LANES] = nat_s[s].astype(o_ref.dtype)


def _attn_prompt(q4, k4, v4, q16, k16, v16):
    nsb = q16.shape[0]
    s_len = nsb * SUPER
    steps = SUPER // RUN // ATT_RUNS
    wide = 4 * ATT_WIDTH
    blk16 = (None, QB, ATT_RUNS * wide)
    blk4 = (ATT_RUNS, QB, wide)
    blk4p = (None, QB, wide)

    def j16(ph, j):
        return jnp.where(ph == 0, j, steps - 1)

    def n4(sb, ph, j):
        return sb * steps + jnp.where(ph == 0, 0, j)

    c16 = pl.BlockSpec(blk16, lambda sb, ph, j: (sb, 0, j16(ph, j)))
    p16 = pl.BlockSpec(blk16, lambda sb, ph, j: (jnp.maximum(sb - 1, 0), 0, j16(ph, j)))
    c4 = pl.BlockSpec(blk4, lambda sb, ph, j: (n4(sb, ph, j), 0, 0))
    p4 = pl.BlockSpec(blk4p, lambda sb, ph, j: (jnp.maximum(ATT_RUNS * n4(sb, ph, j) - 1, 0), 0, 0))
    out = pl.BlockSpec((ATT_RUNS * RUN, ATT_WIDTH),
                       lambda sb, ph, j: (sb * steps + jnp.where(ph == 2, j, 0), 0))
    bias = jnp.asarray(_attn_bias_tables())
    nslab = ATT_WIDTH // LANES
    scratch = ([pltpu.VMEM((nslab, SUPER, LANES), F32) for _ in range(3)]
               + [pltpu.VMEM((nslab, ATT_RUNS * RUN, LANES), F32)])
    return pl.pallas_call(
        _attn_kernel, grid=(nsb, 3, steps),
        in_specs=[c16, c16, p16, c16, p16, c4, c4, p4, c4, p4, _full_spec(bias.shape)],
        out_specs=out, out_shape=jax.ShapeDtypeStruct((s_len, ATT_WIDTH), BF16),
        scratch_shapes=scratch, compiler_params=_params(3), name="attn_prompt")(
            q16, k16, k16, v16, v16, q4, k4, k4, v4, v4, bias)


def _ret_init(dec_s, wt_s, wh_s, st_s):
    c = RET_CHUNK

    @pl.when(pl.program_id(0) == 0)
    def _():
        i = lax.broadcasted_iota(jnp.int32, (c, c), 0)
        jj = lax.broadcasted_iota(jnp.int32, (c, c), 1)
        diff = (i - jj).astype(F32)
        for h in range(N_RET_HEADS):
            dec_s[h] = jnp.where(diff >= 0, jnp.exp(diff * LOG_G[h]), 0.0)
            wt_s[h] = jnp.exp((c - 1.0 - i.astype(F32)) * LOG_G[h])
            wh_s[h] = jnp.exp((i.astype(F32) + 1.0) * LOG_G[h])
        st_s[...] = jnp.zeros_like(st_s)


def _ret_body(qr_ref, kr_ref, vr_ref, gr_ref, gn_ref, y_ref, s_ref, dec_s, wt_s, wh_s, st_s, chunks):
    c = RET_CHUNK
    lane = lax.broadcasted_iota(jnp.int32, (c, LANES), 1)
    head_a = lane < RET_DK
    row_a = lax.broadcasted_iota(jnp.int32, (LANES, RET_DV), 0) < RET_DK
    for ci in range(chunks):
        rows = slice(ci * c, (ci + 1) * c)
        for p in range(N_RET_HEADS // 2):
            sl = slice(p * LANES, (p + 1) * LANES)
            q2 = qr_ref[rows, sl]
            k2 = kr_ref[rows, sl]
            zq = jnp.zeros_like(q2)
            qq = jnp.concatenate([jnp.where(head_a, q2, zq), jnp.where(head_a, zq, q2)], axis=0)
            inner = _dot_nt(qq, k2)
            state = st_s[p]
            cross = _dot(qq, state.astype(BF16))
            vws = []
            for hh in range(2):
                h = 2 * p + hh
                hs = slice(h * RET_DV, (h + 1) * RET_DV)
                v = vr_ref[rows, hs]
                inn = (inner[hh * c:(hh + 1) * c] * dec_s[h]).astype(BF16)
                o = _dot(inn, v) + cross[hh * c:(hh + 1) * c] * wh_s[h]
                mu = jnp.mean(o, axis=-1, keepdims=True)
                xc = o - mu
                var = jnp.mean(xc * xc, axis=-1, keepdims=True)
                y = xc * lax.rsqrt(var + NORM_EPS) * gn_ref[:, hs]
                y_ref[rows, hs] = (_silu(gr_ref[rows, hs]) * y).astype(y_ref.dtype)
                vws.append((v.astype(F32) * wt_s[h]).astype(BF16))
            upd = _dot_tn(k2, jnp.concatenate(vws, axis=1))
            upd = jnp.where(row_a, upd[:, :RET_DV], upd[:, RET_DV:])
            gch = jnp.where(row_a, math.exp(c * LOG_G[2 * p]), math.exp(c * LOG_G[2 * p + 1]))
            st_s[p] = gch * state + upd

    @pl.when(pl.program_id(0) == pl.num_programs(0) - 1)
    def _():
        s_ref[...] = st_s[...]


S_ROWS = 8
S_HGRP = 4


def _sample_tables(t_new, past_len, wb):
    rows = np.concatenate([np.arange(wb), wb + np.arange(S_ROWS)])
    tab = np.zeros((S_ROWS, wb + S_ROWS), np.float32)
    for r in range(S_ROWS):
        i = r % t_new
        delta = wb + i - rows
        ok = (delta >= 0) & (past_len + i - delta >= 0) & (rows < wb + t_new)
        for window, dil in BRANCHES:
            tab[r] += ok & (delta % dil == 0) & (delta <= window)
    tab = np.tile(tab, (S_HGRP, 1))
    return tab[:, :wb], tab[:, wb:]


def _sample_attn_body(q_ref, kn_ref, vn_ref, kt_ref, vt_ref, cc_ref, cn_ref, o_ref):
    mult_c, mult_n = cc_ref[...], cn_ref[...]
    bias_c = jnp.where(mult_c > 0, 0.0, NEG)
    bias_n = jnp.where(mult_n > 0, 0.0, NEG)
    gw = S_HGRP * HEAD_DIM
    lane_head = lax.broadcasted_iota(jnp.int32, (S_ROWS, gw), 1) // HEAD_DIM
    for g in range(N_ATT_HEADS // S_HGRP):
        gs = slice(g * gw, (g + 1) * gw)
        q8 = q_ref[:, gs]
        q = jnp.concatenate([jnp.where(lane_head == h, q8, 0.0) for h in range(S_HGRP)],
                            axis=0).astype(BF16)
        s_c = _dot(q, kt_ref[g].astype(BF16)) + bias_c
        s_n = _dot_nt(q, kn_ref[:, gs].astype(BF16)) + bias_n
        m = jnp.maximum(jnp.max(s_c, axis=-1, keepdims=True), jnp.max(s_n, axis=-1, keepdims=True))
        p_c = jnp.exp2(s_c - m) * mult_c
        p_n = jnp.exp2(s_n - m) * mult_n
        l = jnp.sum(p_c, axis=-1, keepdims=True) + jnp.sum(p_n, axis=-1, keepdims=True)
        o = (_dot_nt(p_c.astype(BF16), vt_ref[g].astype(BF16))
             + _dot(p_n.astype(BF16), vn_ref[:, gs].astype(BF16))) / l
        o8 = jnp.zeros((S_ROWS, gw), F32)
        for h in range(S_HGRP):
            o8 = jnp.where(lane_head == h, o[h * S_ROWS:(h + 1) * S_ROWS], o8)
        o_ref[:, gs] = o8


FF_SPLITS = ((0, 768), (768, 1536), (1536, 2304), (2304, D_FF))


def _ffn_attn_kernel(x_ref, gpre_ref, gpost_ref, wg_ref, wu_ref, wd_ref,
                     q_ref, kn_ref, vn_ref, kt_ref, vt_ref, cc_ref, cn_ref,
                     o_ref, oa_ref, u_s, acc_s):
    j = pl.program_id(1)
    last = len(FF_SPLITS) - 1
    for k, (c0, c1) in enumerate(FF_SPLITS):
        @pl.when(j == k)
        def _():
            if k == 0:
                u = _rms(x_ref[...], gpre_ref[...]).astype(BF16)
                u_s[...] = u
            else:
                u = u_s[...]
            h = (_silu(_dot(u, wg_ref[:, c0:c1])) * _dot(u, wu_ref[:, c0:c1])).astype(BF16)
            d = _dot(h, wd_ref[c0:c1, :])
            if k == 0:
                acc_s[...] = d
            elif k < last:
                acc_s[...] += d
            else:
                o_ref[...] = x_ref[...] + 0.5 * _rms(acc_s[...] + d, gpost_ref[...])
            _sample_attn_body(q_ref, kn_ref, vn_ref, kt_ref, vt_ref, cc_ref, cn_ref, oa_ref)


def _ffn_with_sample_attn(x, gpre, gpost, wg, wu, wd, q, kn, vn, cache_k, cache_v, t_new, past_len, tm):
    t = x.shape[0]
    nb, wb = cache_k.shape[0], cache_k.shape[1]
    nsplit = len(FF_SPLITS)
    assert t_new <= S_ROWS and nb == (t // tm) * nsplit

    def rows(a):
        return jnp.pad(a.reshape(nb, t_new, ATT_WIDTH), ((0, 0), (0, S_ROWS - t_new), (0, 0)))

    def grouped(c):
        return c.transpose(0, 2, 3, 1).reshape(nb, N_ATT_HEADS // S_HGRP, S_HGRP * HEAD_DIM, wb)

    row = pl.BlockSpec((tm, D_MODEL), lambda i, j: (i, 0))
    small = pl.BlockSpec((None, S_ROWS, ATT_WIDTH), lambda i, j: (i * nsplit + j, 0, 0))
    big = pl.BlockSpec((None, N_ATT_HEADS // S_HGRP, S_HGRP * HEAD_DIM, wb),
                       lambda i, j: (i * nsplit + j, 0, 0, 0))
    tabs = [jnp.asarray(a) for a in _sample_tables(t_new, past_len, wb)]
    h1, o = pl.pallas_call(
        _ffn_attn_kernel, grid=(t // tm, nsplit),
        in_specs=[row] + _ffn_specs() + [small, small, small, big, big]
        + [_full_spec(a.shape) for a in tabs],
        out_specs=[row, small],
        out_shape=[jax.ShapeDtypeStruct((t, D_MODEL), F32),
                   jax.ShapeDtypeStruct((nb, S_ROWS, ATT_WIDTH), F32)],
        scratch_shapes=[pltpu.VMEM((tm, D_MODEL), BF16), pltpu.VMEM((tm, D_MODEL), F32)],
        compiler_params=_params(2), name="ffn_attn_sample")(
            x, gpre, gpost, wg, wu, wd, rows(q), rows(kn), rows(vn),
            grouped(cache_k), grouped(cache_v), *tabs)
    return h1, o[:, :t_new].reshape(nb * t_new, ATT_WIDTH)


R_PAIR = 2


def _sample_ret_tables(t_new):
    r = np.arange(R_PAIR * t_new)
    seq, step = r // t_new, r % t_new
    lg = np.asarray(LOG_G)[:, None, None]
    diff = (step[:, None] - step[None, :])[None]
    ok = ((seq[:, None] == seq[None, :]) & (diff[0] >= 0))[None]
    decay = np.where(ok, np.exp(diff * lg), 0.0).reshape(N_RET_HEADS * r.size, r.size)
    w_head = np.exp((step[None, :] + 1.0) * lg[:, :, 0]).reshape(-1, 1) * np.ones((1, RET_DV))
    w_tail = np.exp((t_new - 1.0 - step)[:, None, None] * lg[None, :, :, 0]) * np.ones((1, 1, RET_DV))
    g_chunk = np.repeat(np.exp(t_new * lg[:, 0, 0]), RET_DK)[:, None] * np.ones((1, RET_DV))
    return [a.astype(np.float32) for a in (decay, w_head, w_tail.reshape(r.size, -1), g_chunk)]


def _sample_ret_kernel(q_ref, k_ref, v_ref, g_ref, gn_ref, st_ref, dec_ref, wh_ref, wt_ref, gc_ref,
                       y_ref, so_ref, *, blocks, t_new):
    nrow = R_PAIR * t_new
    lane_head = lax.broadcasted_iota(jnp.int32, (nrow, RET_QK), 1) // RET_DK
    row_seq = lax.broadcasted_iota(jnp.int32, (nrow, RET_QK), 0) // t_new
    out_seq = (lax.broadcasted_iota(jnp.int32, (N_RET_HEADS * nrow, RET_DV), 0) % nrow) // t_new
    for blk in range(blocks):
        rs = slice(blk * nrow, (blk + 1) * nrow)
        q8, k8, v8 = q_ref[rs, :], k_ref[rs, :], v_ref[rs, :]
        qm = jnp.concatenate([jnp.where(lane_head == h, q8, 0.0) for h in range(N_RET_HEADS)],
                             axis=0).astype(BF16)
        inner = (_dot_nt(qm, k8.astype(BF16)) * dec_ref[...]).astype(BF16)
        o_all = _dot(inner, v8.astype(BF16))
        vw = (v8 * wt_ref[...]).astype(BF16)
        cross = None
        for s in range(R_PAIR):
            state = st_ref[blk * R_PAIR + s]
            c = _dot(qm, state.astype(BF16))
            cross = c if cross is None else jnp.where(out_seq == s, c, cross)
            km = jnp.where(row_seq == s, k8, 0.0).astype(BF16)
            upd = _dot_tn(km, vw)
            upd = jnp.concatenate([upd[h * RET_DK:(h + 1) * RET_DK, h * RET_DV:(h + 1) * RET_DV]
                                   for h in range(N_RET_HEADS)], axis=0)
            so_ref[blk * R_PAIR + s] = gc_ref[...] * state + upd
        cross = cross * wh_ref[...]
        for h in range(N_RET_HEADS):
            hs = slice(h * RET_DV, (h + 1) * RET_DV)
            oh = o_all[h * nrow:(h + 1) * nrow, hs] + cross[h * nrow:(h + 1) * nrow]
            mu = jnp.mean(oh, axis=-1, keepdims=True)
            xc = oh - mu
            var = jnp.mean(xc * xc, axis=-1, keepdims=True)
            y = xc * lax.rsqrt(var + NORM_EPS) * gn_ref[:, hs]
            y_ref[rs, hs] = _silu(g_ref[rs, hs]) * y


def _ret_sample(qr, kr, vr, gr, gn, state, t_new, blocks=8):
    nb = state.shape[0]
    assert R_PAIR * t_new == 8 and nb % (R_PAIR * blocks) == 0
    rows = R_PAIR * t_new * blocks
    tabs = [jnp.asarray(a) for a in _sample_ret_tables(t_new)]
    st = state.reshape(nb, N_RET_HEADS * RET_DK, RET_DV)

    def blk(w_):
        return pl.BlockSpec((rows, w_), lambda i: (i, 0))

    st_spec = pl.BlockSpec((R_PAIR * blocks, N_RET_HEADS * RET_DK, RET_DV), lambda i: (i, 0, 0))
    y, s = pl.pallas_call(
        functools.partial(_sample_ret_kernel, blocks=blocks, t_new=t_new),
        grid=(nb // (R_PAIR * blocks),),
        in_specs=[blk(RET_QK), blk(RET_QK), blk(RET_V), blk(RET_V), _full_spec((1, RET_V)), st_spec]
        + [_full_spec(a.shape) for a in tabs],
        out_specs=[blk(RET_V), st_spec],
        out_shape=[jax.ShapeDtypeStruct((nb * t_new, RET_V), F32), jax.ShapeDtypeStruct(st.shape, F32)],
        compiler_params=_params(1), name="ret_sample")(qr, kr, vr, gr, gn, st, *tabs)
    return y, s.reshape(state.shape)


CONV_COLS = 256


def _ffn_conv_kernel(*refs, with_mix):
    if with_mix:
        (oa_ref, yr_ref, h_ref, wo_ref, gmb_ref, gpre_ref, gpost_ref, wg_ref, wu_ref, wd_ref,
         o_ref, wob_ref, wgb_ref, wub_ref, wdb_ref, x_s, u_s, acc_s) = refs
    else:
        (h_ref, gpre_ref, gpost_ref, wg_ref, wu_ref, wd_ref,
         o_ref, wgb_ref, wub_ref, wdb_ref, x_s, u_s, acc_s) = refs
    c = pl.program_id(0)

    @pl.when(c == 0)
    def _():
        x = h_ref[...]
        if with_mix:
            wo = wo_ref[...].astype(BF16)
            wob_ref[...] = wo
            mixed = jnp.concatenate([oa_ref[...].astype(BF16), yr_ref[...].astype(BF16)], axis=1)
            x = x + _rms(_dot(mixed, wo), gmb_ref[...])
        x_s[...] = x
        u_s[...] = _rms(x, gpre_ref[...]).astype(BF16)
        acc_s[...] = jnp.zeros_like(acc_s)

    wg, wu, wd = wg_ref[...].astype(BF16), wu_ref[...].astype(BF16), wd_ref[...].astype(BF16)
    wgb_ref[...] = wg
    wub_ref[...] = wu
    wdb_ref[...] = wd
    u = u_s[...]
    acc_s[...] += _dot((_silu(_dot(u, wg)) * _dot(u, wu)).astype(BF16), wd)

    @pl.when(c == pl.num_programs(0) - 1)
    def _():
        o_ref[...] = x_s[...] + 0.5 * _rms(acc_s[...], gpost_ref[...])


def _ffn_convert(h, gpre, gpost, wg, wu, wd, mix=None):
    t = h.shape[0]
    col = pl.BlockSpec((D_MODEL, CONV_COLS), lambda c: (0, c))
    rowb = pl.BlockSpec((CONV_COLS, D_MODEL), lambda c: (c, 0))
    vec = _full_spec((1, D_MODEL))
    ins, in_specs, outs, out_specs = [], [], [], []
    if mix is not None:
        att, ret, wo, gmb = mix
        ins += [att, ret]
        in_specs += [_full_spec((t, ATT_WIDTH)), _full_spec((t, RET_V))]
    ins.append(h)
    in_specs.append(_full_spec((t, D_MODEL)))
    if mix is not None:
        ins += [wo, gmb]
        in_specs += [_full_spec((D_MODEL, D_MODEL)), vec]
    ins += [gpre, gpost, wg, wu, wd]
    in_specs += [vec, vec, col, col, rowb]
    out_shape = [jax.ShapeDtypeStruct((t, D_MODEL), F32)]
    out_specs = [_full_spec((t, D_MODEL))]
    if mix is not None:
        out_shape.append(jax.ShapeDtypeStruct((D_MODEL, D_MODEL), BF16))
        out_specs.append(_full_spec((D_MODEL, D_MODEL)))
    out_shape += [jax.ShapeDtypeStruct((D_MODEL, D_FF), BF16)] * 2 + [jax.ShapeDtypeStruct((D_FF, D_MODEL), BF16)]
    out_specs += [col, col, rowb]
    return pl.pallas_call(
        functools.partial(_ffn_conv_kernel, with_mix=mix is not None), grid=(D_FF // CONV_COLS,),
        in_specs=in_specs, out_specs=out_specs, out_shape=out_shape,
        scratch_shapes=[pltpu.VMEM((t, D_MODEL), F32), pltpu.VMEM((t, D_MODEL), BF16),
                        pltpu.VMEM((t, D_MODEL), F32)],
        compiler_params=_params(1), name="ffn_convert")(*ins)


def _mixin_sample_kernel(h_ref, g_ref, w_ref, ba_ref, oa_ref, br_ref, or_ref,
                         qa_ref, kk_ref, vk_ref, qr_ref, kr_ref, vr_ref, gr_ref, wb_ref,
                         ta_s, tr_s, u_s):
    c = pl.program_id(0)

    @pl.when(c == 0)
    def _():
        _rope_fill(ba_ref, oa_ref, ta_s)
        _rope_fill(br_ref, or_ref, tr_s)
        u_s[...] = _rms(h_ref[...], g_ref[...]).astype(BF16)

    w = w_ref[...].astype(BF16)
    wb_ref[...] = w
    y = _dot(u_s[...], w)
    nblk = h_ref.shape[0] // ROPE_ROWS

    def rotary(tab_s, half, cols, dst_ref, scale):
        for rb in range(nblk):
            rows = slice(rb * ROPE_ROWS, (rb + 1) * ROPE_ROWS)
            rope = _rope_rows(tab_s, half, rows)
            for s in range(dst_ref.shape[1] // LANES):
                dst_ref[rows, s * LANES:(s + 1) * LANES] = rope(
                    y[rows, cols + s * LANES:cols + (s + 1) * LANES]) * scale

    @pl.when(c == 0)
    def _():
        rotary(ta_s, ROT_DIM // 2, 0, qa_ref, HEAD_DIM ** -0.5 * LOG2_E)

    @pl.when(c == 1)
    def _():
        rotary(ta_s, ROT_DIM // 2, 0, kk_ref, 1.0)

    @pl.when(c == 2)
    def _():
        vk_ref[...] = y

    @pl.when(c == 3)
    def _():
        rotary(tr_s, RET_DK // 2, 0, qr_ref, 1.0)
        rotary(tr_s, RET_DK // 2, RET_QK, kr_ref, RET_DK ** -0.5)

    @pl.when(c == 4)
    def _():
        vr_ref[...] = y

    @pl.when(c == 5)
    def _():
        gr_ref[...] = y


def _mixin_sample(h, g, w, base_pos, off_pos):
    t = h.shape[0]
    grp = ATT_WIDTH
    assert IN_WIDTH == 6 * grp and 2 * RET_QK == grp and RET_V == grp
    base_a, off_a = _rope_consts(base_pos, off_pos, ROT_DIM, ROPE_THETA, HEAD_DIM)
    base_r, off_r = _rope_consts(base_pos, off_pos, RET_DK, RET_THETA, RET_DK)
    wcol = pl.BlockSpec((D_MODEL, grp), lambda c: (0, c))
    base_spec = pl.BlockSpec((None, 2, LANES), lambda c: (0, 0, 0))
    widths = [ATT_WIDTH, ATT_WIDTH, ATT_WIDTH, RET_QK, RET_QK, RET_V, RET_V]
    tab = pltpu.VMEM((3, t, LANES), F32)
    return pl.pallas_call(
        _mixin_sample_kernel, grid=(IN_WIDTH // grp,),
        in_specs=[_full_spec((t, D_MODEL)), _full_spec((1, D_MODEL)), wcol,
                  base_spec, _full_spec(off_a.shape), base_spec, _full_spec(off_r.shape)],
        out_specs=[_full_spec((t, w_)) for w_ in widths] + [wcol],
        out_shape=[jax.ShapeDtypeStruct((t, w_), F32) for w_ in widths]
        + [jax.ShapeDtypeStruct((D_MODEL, IN_WIDTH), BF16)],
        scratch_shapes=[tab, tab, pltpu.VMEM((t, D_MODEL), BF16)],
        compiler_params=_params(1), name="mix_in_sample")(h, g, w, base_a, off_a, base_r, off_r)


def kernel(x_prompt, x_sample, cache_k, cache_v, state_ret, g_ffn1_pre, g_ffn1_post, w1_gate, w1_up,
           w1_down, g_mix_pre, g_mix_post, w_in, gn_w, w_out, g_ffn2_pre, g_ffn2_post, w2_gate, w2_up,
           w2_down):
    b_p, s_p, _ = x_prompt.shape
    b_s, t_s, _ = x_sample.shape
    depth = w_in.shape[0]
    assert depth == 1 and b_p == 1
    keep = min(WIN_MAX, s_p)
    l = 0
    vec = lambda g: g[l].reshape(1, -1)
    g1a, g1b, gma, gmb = vec(g_ffn1_pre), vec(g_ffn1_post), vec(g_mix_pre), vec(g_mix_post)
    g2a, g2b, gn = vec(g_ffn2_pre), vec(g_ffn2_post), vec(gn_w)

    n_s = b_s * t_s
    xs = x_sample.reshape(n_s, D_MODEL)
    h1s, *w1 = _ffn_convert(xs, g1a, g1b, w1_gate[l], w1_up[l], w1_down[l])
    qas, kks, vks, qrs, krs, vrs, grs, wi = _mixin_sample(
        h1s, gma, w_in[l], np.full((1,), PAST_LEN), np.tile(np.arange(t_s), b_s))

    tm = RUN
    xp = x_prompt.reshape(s_p, D_MODEL)
    h1, o_att_s = _ffn_with_sample_attn(xp, g1a, g1b, *w1, qas, kks, vks, cache_k[l], cache_v[l],
                                        t_s, PAST_LEN, tm)
    q4, k4, v4, q16, k16, v16, kk, vk, y_ret, st_p = _mixin(
        h1, gma, wi, tm * np.arange(s_p // tm), np.arange(tm), keep, tm, gn)
    o_att = _attn_prompt(q4, k4, v4, q16, k16, v16)

    y_ret_s, st_s = _ret_sample(qrs, krs, vrs, grs, gn, state_ret[l], t_s)
    y_sample, wo, *w2 = _ffn_convert(h1s, g2a, g2b, w2_gate[l], w2_up[l], w2_down[l],
                                     mix=(o_att_s, y_ret_s, w_out[l], gmb))
    y_prompt = _mixout_ffn(o_att, y_ret, h1, wo, gmb, g2a, g2b, *w2, 2 * tm)

    hd = (N_ATT_HEADS, HEAD_DIM)
    return (y_prompt.reshape(b_p, s_p, D_MODEL),
            y_sample.reshape(b_s, t_s, D_MODEL),
            kk.reshape(depth, b_p, keep, *hd),
            vk.reshape(depth, b_p, keep, *hd),
            st_p.reshape(depth, b_p, N_RET_HEADS, RET_DK, RET_DV),
            kks.reshape(depth, b_s, t_s, *hd),
            vks.reshape(depth, b_s, t_s, *hd),
            st_s.reshape(depth, b_s, N_RET_HEADS, RET_DK, RET_DV))
```

```python
import functools
import math

import numpy as np
import jax
import jax.numpy as jnp
from jax import lax
from jax.experimental import pallas as pl
from jax.experimental.pallas import tpu as pltpu

F32 = jnp.float32
BF16 = jnp.bfloat16

D_MODEL = 1024
D_FF = 2816
HEAD_DIM = 64
N_ATT_HEADS = 8
ATT_WIDTH = N_ATT_HEADS * HEAD_DIM
ROT_DIM = HEAD_DIM // 4
ROPE_THETA = 500000.0
WIN_MAX = 2048
PAST_LEN = 8192
BRANCHES = ((128, 1), (512, 4), (2048, 16))
N_RET_HEADS = 4
RET_DK = 64
RET_DV = 128
RET_QK = N_RET_HEADS * RET_DK
RET_V = N_RET_HEADS * RET_DV
RET_THETA = 10000.0
RET_CHUNK = 128
IN_WIDTH = 3 * ATT_WIDTH + 2 * RET_QK + 2 * RET_V
NORM_EPS = 1e-6
NEG = -1e30

LANES = 128
QB = 128
SUPER = 2048
RUN = 512
ATT_RUNS = 4
VMEM_LIMIT = 56 * 1024 * 1024
MIXIN_VMEM_LIMIT = 60 * 1024 * 1024
LOG_G = tuple(math.log1p(-2.0 ** (-5.0 - h)) for h in range(N_RET_HEADS))
LOG2_E = math.log2(math.e)


def _full_spec(shape):
    nd = len(shape)
    return pl.BlockSpec(shape, lambda *_: (0,) * nd)


def _resident_spec(shape):
    nd = len(shape)
    return pl.BlockSpec(shape, lambda *_: (0,) * nd, pipeline_mode=pl.Buffered(1))


def _params(n_axes, vmem=VMEM_LIMIT):
    return pltpu.CompilerParams(dimension_semantics=("arbitrary",) * n_axes, vmem_limit_bytes=vmem)


def _rms(x, g):
    return x * lax.rsqrt(jnp.mean(x * x, axis=-1, keepdims=True) + NORM_EPS) * g


def _silu(x):
    return x / (1.0 + jnp.exp(-x))


def _dot(a, b):
    return jnp.dot(a, b, preferred_element_type=F32)


def _dot_nt(a, b):
    return lax.dot_general(a, b, (((1,), (1,)), ((), ())), preferred_element_type=F32)


def _dot_tn(a, b):
    return lax.dot_general(a, b, (((0,), (0,)), ((), ())), preferred_element_type=F32)


FF_CHUNK = 512


def _swiglu(u, wg_ref, wu_ref, wd_ref):
    acc = None
    for c0 in range(0, D_FF, FF_CHUNK):
        c1 = min(c0 + FF_CHUNK, D_FF)
        g = _dot(u, wg_ref[:, c0:c1])
        up = _dot(u, wu_ref[:, c0:c1])
        h = (_silu(g) * up).astype(BF16)
        d = _dot(h, wd_ref[c0:c1, :])
        acc = d if acc is None else acc + d
    return acc


def _mixout_ffn_kernel(oa_ref, yr_ref, h_ref, wo_ref, gmb_ref, gpre_ref, gpost_ref,
                       wg_ref, wu_ref, wd_ref, o_ref):
    mixed = jnp.concatenate([oa_ref[...].astype(BF16), yr_ref[...].astype(BF16)], axis=1)
    x = h_ref[...] + _rms(_dot(mixed, wo_ref[...]), gmb_ref[...])
    u = _rms(x, gpre_ref[...]).astype(BF16)
    y = _swiglu(u, wg_ref, wu_ref, wd_ref)
    o_ref[...] = x + 0.5 * _rms(y, gpost_ref[...])


def _ffn_specs():
    return [_full_spec((1, D_MODEL)), _full_spec((1, D_MODEL)),
            _resident_spec((D_MODEL, D_FF)), _resident_spec((D_MODEL, D_FF)),
            _resident_spec((D_FF, D_MODEL))]


def _mixout_ffn(oa, yr, h, wo, gmb, gpre, gpost, wg, wu, wd, tm):
    t = h.shape[0]
    row = pl.BlockSpec((tm, D_MODEL), lambda i: (i, 0))
    half = pl.BlockSpec((tm, ATT_WIDTH), lambda i: (i, 0))
    return pl.pallas_call(
        _mixout_ffn_kernel, grid=(t // tm,),
        in_specs=[half, half, row, _resident_spec((D_MODEL, D_MODEL)), _full_spec((1, D_MODEL))]
        + _ffn_specs(),
        out_specs=row, out_shape=jax.ShapeDtypeStruct((t, D_MODEL), F32),
        compiler_params=_params(1), name="mixout_ffn")(oa, yr, h, wo, gmb, gpre, gpost, wg, wu, wd)


def _rope_consts(base_pos, off_pos, rot_dim, theta, head_dim):
    half = rot_dim // 2
    inv = theta ** (-np.arange(half, dtype=np.float64) * (2.0 / rot_dim))
    lane = np.arange(LANES) % head_dim
    inv_l = inv[lane % half][None, :]
    first = (lane < half).astype(np.float64)[None, :]
    second = ((lane >= half) & (lane < rot_dim)).astype(np.float64)[None, :]
    rot = first + second
    a = np.asarray(base_pos, np.float64)[:, None] * inv_l
    b = np.asarray(off_pos, np.float64)[:, None] * inv_l
    cb, sb = np.cos(b), np.sin(b)
    base = np.stack([np.cos(a), np.sin(a)], axis=1)
    off = np.stack([cb * rot, sb * rot, np.broadcast_to(1.0 - rot, cb.shape),
                    -cb * first, -sb * first, cb * second, sb * second])
    return jnp.asarray(base, F32), jnp.asarray(off, F32)


def _rope_fill(base_ref, off_ref, tab_s):
    ca = base_ref[0:1, :]
    sa = base_ref[1:2, :]
    tab_s[0] = ca * off_ref[0] - sa * off_ref[1] + off_ref[2]
    tab_s[1] = sa * off_ref[3] + ca * off_ref[4]
    tab_s[2] = sa * off_ref[5] + ca * off_ref[6]


ROPE_ROWS = 64


def _rope_rows(tab_ref, half, rows):
    c, sa, sb = tab_ref[0, rows, :], tab_ref[1, rows, :], tab_ref[2, rows, :]
    return lambda x: x * c + pltpu.roll(x, LANES - half, 1) * sa + pltpu.roll(x, half, 1) * sb


def _lanes(r, s):
    return slice(r * ATT_WIDTH + s * LANES, r * ATT_WIDTH + (s + 1) * LANES)


def _emit_dilated(nat_s, x4_s, o4_ref, o16_ref):
    l16 = RUN // 16
    for w in range(nat_s.shape[1] // RUN):
        for s in range(ATT_WIDTH // LANES):
            for r in range(4):
                x4 = nat_s[s, pl.ds(w * RUN + r, RUN // 4, stride=4), :]
                o4_ref[w, :, _lanes(r, s)] = x4.astype(BF16)
                x4_s[r] = x4
            for r in range(4):
                for c in range(4):
                    o16_ref[w * l16:(w + 1) * l16, _lanes(4 * c + r, s)] = (
                        x4_s[r, pl.ds(c, l16, stride=4), :].astype(BF16))


def _mixin_kernel(h_ref, g_ref, w_ref, ba_ref, oa_ref, br_ref, or_ref, gn_ref,
                  q4_ref, k4_ref, v4_ref, q16_ref, k16_ref, v16_ref, kk_ref, vk_ref, y_ref, st_ref,
                  ta_ref, tr_ref, nat_s, natk_s, x4_s, qr_ref, kr_ref, vr_ref, gr_ref,
                  dec_s, wt_s, wh_s, st_s):
    _ret_init(dec_s, wt_s, wh_s, st_s)
    _rope_fill(ba_ref, oa_ref, ta_ref)
    _rope_fill(br_ref, or_ref, tr_ref)
    u = _rms(h_ref[...], g_ref[...]).astype(BF16)
    scale = HEAD_DIM ** -0.5 * LOG2_E
    nslab = ATT_WIDTH // LANES

    q = _dot(u, w_ref[:, 0:ATT_WIDTH])
    k = _dot(u, w_ref[:, ATT_WIDTH:2 * ATT_WIDTH])
    for rb in range(h_ref.shape[0] // ROPE_ROWS):
        rows = slice(rb * ROPE_ROWS, (rb + 1) * ROPE_ROWS)
        rope = _rope_rows(ta_ref, ROT_DIM // 2, rows)
        for s in range(nslab):
            sl = slice(s * LANES, (s + 1) * LANES)
            qs = rope(q[rows, sl]) * scale
            ks = rope(k[rows, sl])
            nat_s[s, rows, :] = qs
            natk_s[s, rows, :] = ks
            kk_ref[rows, sl] = ks
    _emit_dilated(nat_s, x4_s, q4_ref, q16_ref)
    _emit_dilated(natk_s, x4_s, k4_ref, k16_ref)

    v = _dot(u, w_ref[:, 2 * ATT_WIDTH:3 * ATT_WIDTH])
    for s in range(nslab):
        nat_s[s] = v[:, s * LANES:(s + 1) * LANES]
    _emit_dilated(nat_s, x4_s, v4_ref, v16_ref)
    vk_ref[...] = v

    o = 3 * ATT_WIDTH
    qk = _dot(u, w_ref[:, o:o + 2 * RET_QK])
    for rb in range(h_ref.shape[0] // ROPE_ROWS):
        rows = slice(rb * ROPE_ROWS, (rb + 1) * ROPE_ROWS)
        rope = _rope_rows(tr_ref, RET_DK // 2, rows)
        for s in range(RET_QK // LANES):
            sl = slice(s * LANES, (s + 1) * LANES)
            sk = slice(RET_QK + s * LANES, RET_QK + (s + 1) * LANES)
            qr_ref[rows, sl] = rope(qk[rows, sl]).astype(qr_ref.dtype)
            kr_ref[rows, sl] = (rope(qk[rows, sk]) * (RET_DK ** -0.5)).astype(kr_ref.dtype)
    o += 2 * RET_QK
    vr_ref[...] = _dot(u, w_ref[:, o:o + RET_V]).astype(vr_ref.dtype)
    o += RET_V
    gr_ref[...] = _dot(u, w_ref[:, o:o + RET_V])
    _ret_body(qr_ref, kr_ref, vr_ref, gr_ref, gn_ref, y_ref, st_ref, dec_s, wt_s, wh_s, st_s,
              qr_ref.shape[0] // RET_CHUNK)


def _mixin(h, g, w, base_pos, off_pos, keep_rows, tm, gn):
    t = h.shape[0]
    nt = t // tm
    assert tm % RUN == 0 and SUPER % tm == 0 and t % SUPER == 0 and keep_rows % tm == 0
    first_keep = nt - keep_rows // tm
    base_a, off_a = _rope_consts(base_pos, off_pos, ROT_DIM, ROPE_THETA, HEAD_DIM)
    base_r, off_r = _rope_consts(base_pos, off_pos, RET_DK, RET_THETA, RET_DK)

    def row(w_):
        return pl.BlockSpec((tm, w_), lambda i: (i, 0))

    keep_spec = pl.BlockSpec((tm, ATT_WIDTH), lambda i: (jnp.maximum(i - first_keep, 0), 0))
    keep_shape = ((keep_rows, ATT_WIDTH), F32)
    base_spec = pl.BlockSpec((None, 2, LANES), lambda i: (i, 0, 0))
    off_spec = _resident_spec((7, tm, LANES))
    tab = pltpu.VMEM((3, tm, LANES), F32)
    in_specs = [row(D_MODEL), _full_spec((1, D_MODEL)), _resident_spec((D_MODEL, IN_WIDTH)),
                base_spec, off_spec, base_spec, off_spec, _full_spec((1, RET_V))]
    tps = SUPER // tm
    s4 = pl.BlockSpec((tm // RUN, RUN // 4, 4 * ATT_WIDTH), lambda i: (i, 0, 0))
    s16 = pl.BlockSpec((None, tm // 16, 16 * ATT_WIDTH), lambda i: (i // tps, i % tps, 0))
    st_shape = (N_RET_HEADS // 2, 2 * RET_DK, RET_DV)
    specs = [s4] * 3 + [s16] * 3 + [keep_spec, keep_spec, row(RET_V), _full_spec(st_shape)]
    shapes = ([((t // RUN, RUN // 4, 4 * ATT_WIDTH), BF16)] * 3
              + [((t // SUPER, SUPER // 16, 16 * ATT_WIDTH), BF16)] * 3
              + [keep_shape, keep_shape, ((t, RET_V), BF16), (st_shape, F32)])
    dec = pltpu.VMEM((N_RET_HEADS, RET_CHUNK, RET_CHUNK), F32)
    nat = pltpu.VMEM((ATT_WIDTH // LANES, tm, LANES), F32)
    scratch = [tab, tab, nat, nat, pltpu.VMEM((4, RUN // 4, LANES), F32),
               pltpu.VMEM((tm, RET_QK), BF16), pltpu.VMEM((tm, RET_QK), BF16),
               pltpu.VMEM((tm, RET_V), BF16), pltpu.VMEM((tm, RET_V), F32),
               dec, dec, dec, pltpu.VMEM(st_shape, F32)]
    return pl.pallas_call(
        _mixin_kernel, grid=(nt,), in_specs=in_specs, out_specs=specs,
        out_shape=[jax.ShapeDtypeStruct(s, d) for s, d in shapes],
        scratch_shapes=scratch, compiler_params=_params(1, MIXIN_VMEM_LIMIT), name="mix_in")(
            h, g, w, base_a, off_a, base_r, off_r, gn)


def _attn_bias_tables():
    a = np.arange(QB)[:, None]
    c = np.arange(2 * QB)[None, :]
    steps = BRANCHES[0][0]
    dist = QB + a - c
    band = (dist >= 0) & (dist <= steps)
    cur = c >= QB
    tok_q = 4 * (a % 32) + a // 32
    cc = c % QB
    tok_k = 4 * (cc % 32) + cc // 32 + QB * (c // QB) - QB
    dist2 = tok_q - tok_k
    band2 = (dist2 >= 0) & (dist2 <= steps)
    masks = np.stack([band, band & cur, band2, band2 & cur])
    return np.where(masks, 0.0, NEG).astype(np.float32)


def _attn_qblock(get_q, get_kc, get_kp, get_vc, get_vp, bias, consts, sink):
    head_a, head_a_win, ones_a, ones_b = consts
    for s in range(ATT_WIDTH // LANES):
        q2 = get_q(s)
        kwin = jnp.concatenate([get_kp(s), get_kc(s)], axis=0)
        vwin = jnp.concatenate([get_vp(s), get_vc(s)], axis=0)
        zq = jnp.zeros_like(q2)
        qq = jnp.concatenate([jnp.where(head_a, q2, zq), jnp.where(head_a, zq, q2)], axis=0)
        sc = _dot_nt(qq, kwin)
        s_a = sc[:QB] + bias
        s_b = sc[QB:] + bias
        m_a = jnp.max(s_a, axis=-1, keepdims=True)
        m_b = jnp.max(s_b, axis=-1, keepdims=True)
        p = jnp.concatenate([jnp.exp2(s_a - m_a).astype(BF16), jnp.exp2(s_b - m_b).astype(BF16)],
                            axis=1)
        zv = jnp.zeros_like(vwin)
        w = jnp.concatenate(
            [jnp.concatenate([jnp.where(head_a_win, vwin, zv), ones_a], axis=1),
             jnp.concatenate([jnp.where(head_a_win, zv, vwin), ones_b], axis=1)], axis=0)
        r = _dot(p, w)
        sink(s, (jnp.where(head_a, m_a, m_b), r[:, LANES:], r[:, :LANES]))


def _merge(old, new):
    mo, lo, ao = old
    m2, l2, a2 = new
    mn = jnp.maximum(mo, m2)
    eo = jnp.exp2(mo - mn)
    e2 = jnp.exp2(m2 - mn)
    return mn, eo * lo + e2 * l2, eo * ao + e2 * a2


def _attn_kernel(q16, k16c, k16p, v16c, v16p, q4, k4c, k4p, v4c, v4p, bias_ref, o_ref,
                 m_s, l_s, a_s, nat_s):
    sb = pl.program_id(0)
    ph = pl.program_id(1)
    j = pl.program_id(2)
    lane = lax.broadcasted_iota(jnp.int32, (QB, LANES), 1)
    head_a = lane < HEAD_DIM
    lane_w = lax.broadcasted_iota(jnp.int32, (2 * QB, LANES), 1)
    head_a_win = lane_w < HEAD_DIM
    ones_a = jnp.where(head_a_win, 1.0, 0.0).astype(BF16)
    ones_b = jnp.where(head_a_win, 0.0, 1.0).astype(BF16)
    consts = (head_a, head_a_win, ones_a, ones_b)
    nslab = ATT_WIDTH // LANES
    sub = QB // 4
    lanes = _lanes

    run0 = j * ATT_RUNS
    first_run = sb * (SUPER // RUN) + run0 == 0

    @pl.when(ph == 0)
    def _():
        bias = bias_ref[(sb == 0).astype(jnp.int32)]
        for r in range(4 * ATT_RUNS):
            def scatter(s, res):
                for u in range(SUPER // RUN):
                    rows = pl.ds(u * RUN + (r % 4) * QB + run0 + r // 4, sub, stride=4)
                    for ref, val in zip((m_s, l_s, a_s), res):
                        ref[s, rows, :] = val[u * sub:(u + 1) * sub]

            _attn_qblock(lambda s: q16[:, lanes(r, s)], lambda s: k16c[:, lanes(r, s)],
                         lambda s: k16p[:, lanes(r, s)], lambda s: v16c[:, lanes(r, s)],
                         lambda s: v16p[:, lanes(r, s)], bias, consts, scatter)

    def prev_rows(ref_c, ref_p, w, rows):
        return (lambda cols: ref_p[rows, cols]) if w == 0 else (lambda cols: ref_c[w - 1, rows, cols])

    @pl.when(ph == 1)
    def _():
        for w in range(ATT_RUNS):
            bias = bias_ref[first_run.astype(jnp.int32)] if w == 0 else bias_ref[0]
            kp = prev_rows(k4c, k4p, w, slice(None))
            vp = prev_rows(v4c, v4p, w, slice(None))
            for r in range(4):
                rows = pl.ds(pl.multiple_of((run0 + w) * RUN + r * QB, QB), QB)

                def merge_in(s, res):
                    mn, ln, an = _merge((m_s[s, rows, :], l_s[s, rows, :], a_s[s, rows, :]), res)
                    m_s[s, rows, :] = mn
                    l_s[s, rows, :] = ln
                    a_s[s, rows, :] = an

                _attn_qblock(lambda s: q4[w, :, lanes(r, s)], lambda s: k4c[w, :, lanes(r, s)],
                             lambda s: kp(lanes(r, s)), lambda s: v4c[w, :, lanes(r, s)],
                             lambda s: vp(lanes(r, s)), bias, consts, merge_in)

    @pl.when(ph == 2)
    def _():
        for w in range(ATT_RUNS):
            for b in range(4):
                def cur(ref):
                    return lambda s: jnp.concatenate(
                        [ref[w, b * sub:(b + 1) * sub, lanes(r, s)] for r in range(4)], axis=0)

                def prev(ref_c, ref_p):
                    if b > 0:
                        return lambda s: jnp.concatenate(
                            [ref_c[w, (b - 1) * sub:b * sub, lanes(r, s)] for r in range(4)], axis=0)
                    tail = prev_rows(ref_c, ref_p, w, slice(QB - sub, QB))
                    return lambda s: jnp.concatenate([tail(lanes(r, s)) for r in range(4)], axis=0)

                if w == 0 and b == 0:
                    bias = bias_ref[2 + first_run.astype(jnp.int32)]
                else:
                    bias = bias_ref[2]

                def finish(s, res):
                    for r in range(4):
                        rows = pl.ds(pl.multiple_of((run0 + w) * RUN + r * QB + b * sub, sub), sub)
                        part = tuple(x[r * sub:(r + 1) * sub] for x in res)
                        _, ln, an = _merge((m_s[s, rows, :], l_s[s, rows, :], a_s[s, rows, :]), part)
                        nat_s[s, pl.ds(w * RUN + b * QB + r, sub, stride=4), :] = an / ln

                _attn_qblock(cur(q4), cur(k4c), prev(k4c, k4p), cur(v4c), prev(v4c, v4p),
                             bias, consts, finish)
        for s in range(nslab):
            o_ref[:, s * LANES:(s + 1) * LANES] = nat_s[s].astype(o_ref.dtype)


def _attn_prompt(q4, k4, v4, q16, k16, v16):
    nsb = q16.shape[0]
    s_len = nsb * SUPER
    steps = SUPER // RUN // ATT_RUNS
    wide = 4 * ATT_WIDTH
    blk16 = (None, QB, ATT_RUNS * wide)
    blk4 = (ATT_RUNS, QB, wide)
    blk4p = (None, QB, wide)

    def j16(ph, j):
        return jnp.where(ph == 0, j, steps - 1)

    def n4(sb, ph, j):
        return sb * steps + jnp.where(ph == 0, 0, j)

    c16 = pl.BlockSpec(blk16, lambda sb, ph, j: (sb, 0, j16(ph, j)))
    p16 = pl.BlockSpec(blk16, lambda sb, ph, j: (jnp.maximum(sb - 1, 0), 0, j16(ph, j)))
    c4 = pl.BlockSpec(blk4, lambda sb, ph, j: (n4(sb, ph, j), 0, 0))
    p4 = pl.BlockSpec(blk4p, lambda sb, ph, j: (jnp.maximum(ATT_RUNS * n4(sb, ph, j) - 1, 0), 0, 0))
    out = pl.BlockSpec((ATT_RUNS * RUN, ATT_WIDTH),
                       lambda sb, ph, j: (sb * steps + jnp.where(ph == 2, j, 0), 0))
    bias = jnp.asarray(_attn_bias_tables())
    nslab = ATT_WIDTH // LANES
    scratch = ([pltpu.VMEM((nslab, SUPER, LANES), F32) for _ in range(3)]
               + [pltpu.VMEM((nslab, ATT_RUNS * RUN, LANES), F32)])
    return pl.pallas_call(
        _attn_kernel, grid=(nsb, 3, steps),
        in_specs=[c16, c16, p16, c16, p16, c4, c4, p4, c4, p4, _full_spec(bias.shape)],
        out_specs=out, out_shape=jax.ShapeDtypeStruct((s_len, ATT_WIDTH), BF16),
        scratch_shapes=scratch, compiler_params=_params(3), name="attn_prompt")(
            q16, k16, k16, v16, v16, q4, k4, k4, v4, v4, bias)


def _ret_init(dec_s, wt_s, wh_s, st_s):
    c = RET_CHUNK

    @pl.when(pl.program_id(0) == 0)
    def _():
        i = lax.broadcasted_iota(jnp.int32, (c, c), 0)
        jj = lax.broadcasted_iota(jnp.int32, (c, c), 1)
        diff = (i - jj).astype(F32)
        for h in range(N_RET_HEADS):
            dec_s[h] = jnp.where(diff >= 0, jnp.exp(diff * LOG_G[h]), 0.0)
            wt_s[h] = jnp.exp((c - 1.0 - i.astype(F32)) * LOG_G[h])
            wh_s[h] = jnp.exp((i.astype(F32) + 1.0) * LOG_G[h])
        st_s[...] = jnp.zeros_like(st_s)


def _ret_body(qr_ref, kr_ref, vr_ref, gr_ref, gn_ref, y_ref, s_ref, dec_s, wt_s, wh_s, st_s, chunks):
    c = RET_CHUNK
    lane = lax.broadcasted_iota(jnp.int32, (c, LANES), 1)
    head_a = lane < RET_DK
    row_a = lax.broadcasted_iota(jnp.int32, (LANES, RET_DV), 0) < RET_DK
    for ci in range(chunks):
        rows = slice(ci * c, (ci + 1) * c)
        for p in range(N_RET_HEADS // 2):
            sl = slice(p * LANES, (p + 1) * LANES)
            q2 = qr_ref[rows, sl]
            k2 = kr_ref[rows, sl]
            zq = jnp.zeros_like(q2)
            qq = jnp.concatenate([jnp.where(head_a, q2, zq), jnp.where(head_a, zq, q2)], axis=0)
            inner = _dot_nt(qq, k2)
            state = st_s[p]
            cross = _dot(qq, state.astype(BF16))
            vws = []
            for hh in range(2):
                h = 2 * p + hh
                hs = slice(h * RET_DV, (h + 1) * RET_DV)
                v = vr_ref[rows, hs]
                inn = (inner[hh * c:(hh + 1) * c] * dec_s[h]).astype(BF16)
                o = _dot(inn, v) + cross[hh * c:(hh + 1) * c] * wh_s[h]
                mu = jnp.mean(o, axis=-1, keepdims=True)
                xc = o - mu
                var = jnp.mean(xc * xc, axis=-1, keepdims=True)
                y = xc * lax.rsqrt(var + NORM_EPS) * gn_ref[:, hs]
                y_ref[rows, hs] = (_silu(gr_ref[rows, hs]) * y).astype(y_ref.dtype)
                vws.append((v.astype(F32) * wt_s[h]).astype(BF16))
            upd = _dot_tn(k2, jnp.concatenate(vws, axis=1))
            upd = jnp.where(row_a, upd[:, :RET_DV], upd[:, RET_DV:])
            gch = jnp.where(row_a, math.exp(c * LOG_G[2 * p]), math.exp(c * LOG_G[2 * p + 1]))
            st_s[p] = gch * state + upd

    @pl.when(pl.program_id(0) == pl.num_programs(0) - 1)
    def _():
        s_ref[...] = st_s[...]


S_ROWS = 8
S_HGRP = 4


def _sample_tables(t_new, past_len, wb):
    rows = np.concatenate([np.arange(wb), wb + np.arange(S_ROWS)])
    tab = np.zeros((S_ROWS, wb + S_ROWS), np.float32)
    for r in range(S_ROWS):
        i = r % t_new
        delta = wb + i - rows
        ok = (delta >= 0) & (past_len + i - delta >= 0) & (rows < wb + t_new)
        for window, dil in BRANCHES:
            tab[r] += ok & (delta % dil == 0) & (delta <= window)
    tab = np.tile(tab, (S_HGRP, 1))
    return tab[:, :wb], tab[:, wb:]


def _sample_attn_body(q_ref, kn_ref, vn_ref, kt_ref, vt_ref, cc_ref, cn_ref, o_ref):
    mult_c, mult_n = cc_ref[...], cn_ref[...]
    bias_c = jnp.where(mult_c > 0, 0.0, NEG)
    bias_n = jnp.where(mult_n > 0, 0.0, NEG)
    gw = S_HGRP * HEAD_DIM
    lane_head = lax.broadcasted_iota(jnp.int32, (S_ROWS, gw), 1) // HEAD_DIM
    for g in range(N_ATT_HEADS // S_HGRP):
        gs = slice(g * gw, (g + 1) * gw)
        q8 = q_ref[:, gs]
        q = jnp.concatenate([jnp.where(lane_head == h, q8, 0.0) for h in range(S_HGRP)],
                            axis=0).astype(BF16)
        s_c = _dot(q, kt_ref[g].astype(BF16)) + bias_c
        s_n = _dot_nt(q, kn_ref[:, gs].astype(BF16)) + bias_n
        m = jnp.maximum(jnp.max(s_c, axis=-1, keepdims=True), jnp.max(s_n, axis=-1, keepdims=True))
        p_c = jnp.exp2(s_c - m) * mult_c
        p_n = jnp.exp2(s_n - m) * mult_n
        l = jnp.sum(p_c, axis=-1, keepdims=True) + jnp.sum(p_n, axis=-1, keepdims=True)
        o = (_dot_nt(p_c.astype(BF16), vt_ref[g].astype(BF16))
             + _dot(p_n.astype(BF16), vn_ref[:, gs].astype(BF16))) / l
        o8 = jnp.zeros((S_ROWS, gw), F32)
        for h in range(S_HGRP):
            o8 = jnp.where(lane_head == h, o[h * S_ROWS:(h + 1) * S_ROWS], o8)
        o_ref[:, gs] = o8


FF_SPLITS = ((0, 768), (768, 1536), (1536, 2304), (2304, D_FF))


def _ffn_attn_kernel(x_ref, gpre_ref, gpost_ref, wg_ref, wu_ref, wd_ref,
                     q_ref, kn_ref, vn_ref, kt_ref, vt_ref, cc_ref, cn_ref,
                     o_ref, oa_ref, u_s, acc_s):
    j = pl.program_id(1)
    last = len(FF_SPLITS) - 1
    for k, (c0, c1) in enumerate(FF_SPLITS):
        @pl.when(j == k)
        def _():
            if k == 0:
                u = _rms(x_ref[...], gpre_ref[...]).astype(BF16)
                u_s[...] = u
            else:
                u = u_s[...]
            h = (_silu(_dot(u, wg_ref[:, c0:c1])) * _dot(u, wu_ref[:, c0:c1])).astype(BF16)
            d = _dot(h, wd_ref[c0:c1, :])
            if k == 0:
                acc_s[...] = d
            elif k < last:
                acc_s[...] += d
            else:
                o_ref[...] = x_ref[...] + 0.5 * _rms(acc_s[...] + d, gpost_ref[...])
            _sample_attn_body(q_ref, kn_ref, vn_ref, kt_ref, vt_ref, cc_ref, cn_ref, oa_ref)


def _ffn_with_sample_attn(x, gpre, gpost, wg, wu, wd, q, kn, vn, cache_k, cache_v, t_new, past_len, tm):
    t = x.shape[0]
    nb, wb = cache_k.shape[0], cache_k.shape[1]
    nsplit = len(FF_SPLITS)
    assert t_new <= S_ROWS and nb == (t // tm) * nsplit

    def rows(a):
        return jnp.pad(a.reshape(nb, t_new, ATT_WIDTH), ((0, 0), (0, S_ROWS - t_new), (0, 0)))

    def grouped(c):
        return c.transpose(0, 2, 3, 1).reshape(nb, N_ATT_HEADS // S_HGRP, S_HGRP * HEAD_DIM, wb)

    row = pl.BlockSpec((tm, D_MODEL), lambda i, j: (i, 0))
    small = pl.BlockSpec((None, S_ROWS, ATT_WIDTH), lambda i, j: (i * nsplit + j, 0, 0))
    big = pl.BlockSpec((None, N_ATT_HEADS // S_HGRP, S_HGRP * HEAD_DIM, wb),
                       lambda i, j: (i * nsplit + j, 0, 0, 0))
    tabs = [jnp.asarray(a) for a in _sample_tables(t_new, past_len, wb)]
    h1, o = pl.pallas_call(
        _ffn_attn_kernel, grid=(t // tm, nsplit),
        in_specs=[row] + _ffn_specs() + [small, small, small, big, big]
        + [_full_spec(a.shape) for a in tabs],
        out_specs=[row, small],
        out_shape=[jax.ShapeDtypeStruct((t, D_MODEL), F32),
                   jax.ShapeDtypeStruct((nb, S_ROWS, ATT_WIDTH), F32)],
        scratch_shapes=[pltpu.VMEM((tm, D_MODEL), BF16), pltpu.VMEM((tm, D_MODEL), F32)],
        compiler_params=_params(2), name="ffn_attn_sample")(
            x, gpre, gpost, wg, wu, wd, rows(q), rows(kn), rows(vn),
            grouped(cache_k), grouped(cache_v), *tabs)
    return h1, o[:, :t_new].reshape(nb * t_new, ATT_WIDTH)


R_PAIR = 2


def _sample_ret_tables(t_new):
    r = np.arange(R_PAIR * t_new)
    seq, step = r // t_new, r % t_new
    lg = np.asarray(LOG_G)[:, None, None]
    diff = (step[:, None] - step[None, :])[None]
    ok = ((seq[:, None] == seq[None, :]) & (diff[0] >= 0))[None]
    decay = np.where(ok, np.exp(diff * lg), 0.0).reshape(N_RET_HEADS * r.size, r.size)
    w_head = np.exp((step[None, :] + 1.0) * lg[:, :, 0]).reshape(-1, 1) * np.ones((1, RET_DV))
    w_tail = np.exp((t_new - 1.0 - step)[:, None, None] * lg[None, :, :, 0]) * np.ones((1, 1, RET_DV))
    g_chunk = np.repeat(np.exp(t_new * lg[:, 0, 0]), RET_DK)[:, None] * np.ones((1, RET_DV))
    return [a.astype(np.float32) for a in (decay, w_head, w_tail.reshape(r.size, -1), g_chunk)]


def _sample_ret_kernel(q_ref, k_ref, v_ref, g_ref, gn_ref, st_ref, dec_ref, wh_ref, wt_ref, gc_ref,
                       y_ref, so_ref, *, blocks, t_new):
    nrow = R_PAIR * t_new
    lane_head = lax.broadcasted_iota(jnp.int32, (nrow, RET_QK), 1) // RET_DK
    row_seq = lax.broadcasted_iota(jnp.int32, (nrow, RET_QK), 0) // t_new
    out_seq = (lax.broadcasted_iota(jnp.int32, (N_RET_HEADS * nrow, RET_DV), 0) % nrow) // t_new
    for blk in range(blocks):
        rs = slice(blk * nrow, (blk + 1) * nrow)
        q8, k8, v8 = q_ref[rs, :], k_ref[rs, :], v_ref[rs, :]
        qm = jnp.concatenate([jnp.where(lane_head == h, q8, 0.0) for h in range(N_RET_HEADS)],
                             axis=0).astype(BF16)
        inner = (_dot_nt(qm, k8.astype(BF16)) * dec_ref[...]).astype(BF16)
        o_all = _dot(inner, v8.astype(BF16))
        vw = (v8 * wt_ref[...]).astype(BF16)
        cross = None
        for s in range(R_PAIR):
            state = st_ref[blk * R_PAIR + s]
            c = _dot(qm, state.astype(BF16))
            cross = c if cross is None else jnp.where(out_seq == s, c, cross)
            km = jnp.where(row_seq == s, k8, 0.0).astype(BF16)
            upd = _dot_tn(km, vw)
            upd = jnp.concatenate([upd[h * RET_DK:(h + 1) * RET_DK, h * RET_DV:(h + 1) * RET_DV]
                                   for h in range(N_RET_HEADS)], axis=0)
            so_ref[blk * R_PAIR + s] = gc_ref[...] * state + upd
        cross = cross * wh_ref[...]
        for h in range(N_RET_HEADS):
            hs = slice(h * RET_DV, (h + 1) * RET_DV)
            oh = o_all[h * nrow:(h + 1) * nrow, hs] + cross[h * nrow:(h + 1) * nrow]
            mu = jnp.mean(oh, axis=-1, keepdims=True)
            xc = oh - mu
            var = jnp.mean(xc * xc, axis=-1, keepdims=True)
            y = xc * lax.rsqrt(var + NORM_EPS) * gn_ref[:, hs]
            y_ref[rs, hs] = _silu(g_ref[rs, hs]) * y


def _ret_sample(qr, kr, vr, gr, gn, state, t_new, blocks=8):
    nb = state.shape[0]
    assert R_PAIR * t_new == 8 and nb % (R_PAIR * blocks) == 0
    rows = R_PAIR * t_new * blocks
    tabs = [jnp.asarray(a) for a in _sample_ret_tables(t_new)]
    st = state.reshape(nb, N_RET_HEADS * RET_DK, RET_DV)

    def blk(w_):
        return pl.BlockSpec((rows, w_), lambda i: (i, 0))

    st_spec = pl.BlockSpec((R_PAIR * blocks, N_RET_HEADS * RET_DK, RET_DV), lambda i: (i, 0, 0))
    y, s = pl.pallas_call(
        functools.partial(_sample_ret_kernel, blocks=blocks, t_new=t_new),
        grid=(nb // (R_PAIR * blocks),),
        in_specs=[blk(RET_QK), blk(RET_QK), blk(RET_V), blk(RET_V), _full_spec((1, RET_V)), st_spec]
        + [_full_spec(a.shape) for a in tabs],
        out_specs=[blk(RET_V), st_spec],
        out_shape=[jax.ShapeDtypeStruct((nb * t_new, RET_V), F32), jax.ShapeDtypeStruct(st.shape, F32)],
        compiler_params=_params(1), name="ret_sample")(qr, kr, vr, gr, gn, st, *tabs)
    return y, s.reshape(state.shape)


CONV_COLS = 256


def _ffn_conv_kernel(*refs, with_mix):
    if with_mix:
        (oa_ref, yr_ref, h_ref, wo_ref, gmb_ref, gpre_ref, gpost_ref, wg_ref, wu_ref, wd_ref,
         o_ref, wob_ref, wgb_ref, wub_ref, wdb_ref, x_s, u_s, acc_s) = refs
    else:
        (h_ref, gpre_ref, gpost_ref, wg_ref, wu_ref, wd_ref,
         o_ref, wgb_ref, wub_ref, wdb_ref, x_s, u_s, acc_s) = refs
    c = pl.program_id(0)

    @pl.when(c == 0)
    def _():
        x = h_ref[...]
        if with_mix:
            wo = wo_ref[...].astype(BF16)
            wob_ref[...] = wo
            mixed = jnp.concatenate([oa_ref[...].astype(BF16), yr_ref[...].astype(BF16)], axis=1)
            x = x + _rms(_dot(mixed, wo), gmb_ref[...])
        x_s[...] = x
        u_s[...] = _rms(x, gpre_ref[...]).astype(BF16)
        acc_s[...] = jnp.zeros_like(acc_s)

    wg, wu, wd = wg_ref[...].astype(BF16), wu_ref[...].astype(BF16), wd_ref[...].astype(BF16)
    wgb_ref[...] = wg
    wub_ref[...] = wu
    wdb_ref[...] = wd
    u = u_s[...]
    acc_s[...] += _dot((_silu(_dot(u, wg)) * _dot(u, wu)).astype(BF16), wd)

    @pl.when(c == pl.num_programs(0) - 1)
    def _():
        o_ref[...] = x_s[...] + 0.5 * _rms(acc_s[...], gpost_ref[...])


def _ffn_convert(h, gpre, gpost, wg, wu, wd, mix=None):
    t = h.shape[0]
    col = pl.BlockSpec((D_MODEL, CONV_COLS), lambda c: (0, c))
    rowb = pl.BlockSpec((CONV_COLS, D_MODEL), lambda c: (c, 0))
    vec = _full_spec((1, D_MODEL))
    ins, in_specs, outs, out_specs = [], [], [], []
    if mix is not None:
        att, ret, wo, gmb = mix
        ins += [att, ret]
        in_specs += [_full_spec((t, ATT_WIDTH)), _full_spec((t, RET_V))]
    ins.append(h)
    in_specs.append(_full_spec((t, D_MODEL)))
    if mix is not None:
        ins += [wo, gmb]
        in_specs += [_full_spec((D_MODEL, D_MODEL)), vec]
    ins += [gpre, gpost, wg, wu, wd]
    in_specs += [vec, vec, col, col, rowb]
    out_shape = [jax.ShapeDtypeStruct((t, D_MODEL), F32)]
    out_specs = [_full_spec((t, D_MODEL))]
    if mix is not None:
        out_shape.append(jax.ShapeDtypeStruct((D_MODEL, D_MODEL), BF16))
        out_specs.append(_full_spec((D_MODEL, D_MODEL)))
    out_shape += [jax.ShapeDtypeStruct((D_MODEL, D_FF), BF16)] * 2 + [jax.ShapeDtypeStruct((D_FF, D_MODEL), BF16)]
    out_specs += [col, col, rowb]
    return pl.pallas_call(
        functools.partial(_ffn_conv_kernel, with_mix=mix is not None), grid=(D_FF // CONV_COLS,),
        in_specs=in_specs, out_specs=out_specs, out_shape=out_shape,
        scratch_shapes=[pltpu.VMEM((t, D_MODEL), F32), pltpu.VMEM((t, D_MODEL), BF16),
                        pltpu.VMEM((t, D_MODEL), F32)],
        compiler_params=_params(1), name="ffn_convert")(*ins)


def _mixin_sample_kernel(h_ref, g_ref, w_ref, ba_ref, oa_ref, br_ref, or_ref,
                         qa_ref, kk_ref, vk_ref, qr_ref, kr_ref, vr_ref, gr_ref, wb_ref,
                         ta_s, tr_s, u_s):
    c = pl.program_id(0)

    @pl.when(c == 0)
    def _():
        _rope_fill(ba_ref, oa_ref, ta_s)
        _rope_fill(br_ref, or_ref, tr_s)
        u_s[...] = _rms(h_ref[...], g_ref[...]).astype(BF16)

    w = w_ref[...].astype(BF16)
    wb_ref[...] = w
    y = _dot(u_s[...], w)
    nblk = h_ref.shape[0] // ROPE_ROWS

    def rotary(tab_s, half, cols, dst_ref, scale):
        for rb in range(nblk):
            rows = slice(rb * ROPE_ROWS, (rb + 1) * ROPE_ROWS)
            rope = _rope_rows(tab_s, half, rows)
            for s in range(dst_ref.shape[1] // LANES):
                dst_ref[rows, s * LANES:(s + 1) * LANES] = rope(
                    y[rows, cols + s * LANES:cols + (s + 1) * LANES]) * scale

    @pl.when(c == 0)
    def _():
        rotary(ta_s, ROT_DIM // 2, 0, qa_ref, HEAD_DIM ** -0.5 * LOG2_E)

    @pl.when(c == 1)
    def _():
        rotary(ta_s, ROT_DIM // 2, 0, kk_ref, 1.0)

    @pl.when(c == 2)
    def _():
        vk_ref[...] = y

    @pl.when(c == 3)
    def _():
        rotary(tr_s, RET_DK // 2, 0, qr_ref, 1.0)
        rotary(tr_s, RET_DK // 2, RET_QK, kr_ref, RET_DK ** -0.5)

    @pl.when(c == 4)
    def _():
        vr_ref[...] = y

    @pl.when(c == 5)
    def _():
        gr_ref[...] = y


def _mixin_sample(h, g, w, base_pos, off_pos):
    t = h.shape[0]
    grp = ATT_WIDTH
    assert IN_WIDTH == 6 * grp and 2 * RET_QK == grp and RET_V == grp
    base_a, off_a = _rope_consts(base_pos, off_pos, ROT_DIM, ROPE_THETA, HEAD_DIM)
    base_r, off_r = _rope_consts(base_pos, off_pos, RET_DK, RET_THETA, RET_DK)
    wcol = pl.BlockSpec((D_MODEL, grp), lambda c: (0, c))
    base_spec = pl.BlockSpec((None, 2, LANES), lambda c: (0, 0, 0))
    widths = [ATT_WIDTH, ATT_WIDTH, ATT_WIDTH, RET_QK, RET_QK, RET_V, RET_V]
    tab = pltpu.VMEM((3, t, LANES), F32)
    return pl.pallas_call(
        _mixin_sample_kernel, grid=(IN_WIDTH // grp,),
        in_specs=[_full_spec((t, D_MODEL)), _full_spec((1, D_MODEL)), wcol,
                  base_spec, _full_spec(off_a.shape), base_spec, _full_spec(off_r.shape)],
        out_specs=[_full_spec((t, w_)) for w_ in widths] + [wcol],
        out_shape=[jax.ShapeDtypeStruct((t, w_), F32) for w_ in widths]
        + [jax.ShapeDtypeStruct((D_MODEL, IN_WIDTH), BF16)],
        scratch_shapes=[tab, tab, pltpu.VMEM((t, D_MODEL), BF16)],
        compiler_params=_params(1), name="mix_in_sample")(h, g, w, base_a, off_a, base_r, off_r)


def kernel(x_prompt, x_sample, cache_k, cache_v, state_ret, g_ffn1_pre, g_ffn1_post, w1_gate, w1_up,
           w1_down, g_mix_pre, g_mix_post, w_in, gn_w, w_out, g_ffn2_pre, g_ffn2_post, w2_gate, w2_up,
           w2_down):
    b_p, s_p, _ = x_prompt.shape
    b_s, t_s, _ = x_sample.shape
    depth = w_in.shape[0]
    assert depth == 1 and b_p == 1
    keep = min(WIN_MAX, s_p)
    l = 0
    vec = lambda g: g[l].reshape(1, -1)
    g1a, g1b, gma, gmb = vec(g_ffn1_pre), vec(g_ffn1_post), vec(g_mix_pre), vec(g_mix_post)
    g2a, g2b, gn = vec(g_ffn2_pre), vec(g_ffn2_post), vec(gn_w)

    n_s = b_s * t_s
    xs = x_sample.reshape(n_s, D_MODEL)
    h1s, *w1 = _ffn_convert(xs, g1a, g1b, w1_gate[l], w1_up[l], w1_down[l])
    qas, kks, vks, qrs, krs, vrs, grs, wi = _mixin_sample(
        h1s, gma, w_in[l], np.full((1,), PAST_LEN), np.tile(np.arange(t_s), b_s))

    tm = RUN
    xp = x_prompt.reshape(s_p, D_MODEL)
    h1, o_att_s = _ffn_with_sample_attn(xp, g1a, g1b, *w1, qas, kks, vks, cache_k[l], cache_v[l],
                                        t_s, PAST_LEN, tm)
    q4, k4, v4, q16, k16, v16, kk, vk, y_ret, st_p = _mixin(
        h1, gma, wi, 2 * tm * np.arange(s_p // (2 * tm)), np.arange(2 * tm), keep, 2 * tm, gn)
    o_att = _attn_prompt(q4, k4, v4, q16, k16, v16)

    y_ret_s, st_s = _ret_sample(qrs, krs, vrs, grs, gn, state_ret[l], t_s)
    y_sample, wo, *w2 = _ffn_convert(h1s, g2a, g2b, w2_gate[l], w2_up[l], w2_down[l],
                                     mix=(o_att_s, y_ret_s, w_out[l], gmb))
    y_prompt = _mixout_ffn(o_att, y_ret, h1, wo, gmb, g2a, g2b, *w2, 2 * tm)

    hd = (N_ATT_HEADS, HEAD_DIM)
    return (y_prompt.reshape(b_p, s_p, D_MODEL),
            y_sample.reshape(b_s, t_s, D_MODEL),
            kk.reshape(depth, b_p, keep, *hd),
            vk.reshape(depth, b_p, keep, *hd),
            st_p.reshape(depth, b_p, N_RET_HEADS, RET_DK, RET_DV),
            kks.reshape(depth, b_s, t_s, *hd),
            vks.reshape(depth, b_s, t_s, *hd),
            st_s.reshape(depth, b_s, N_RET_HEADS, RET_DK, RET_DV))
```

```python
import functools
import math

import numpy as np
import jax
import jax.numpy as jnp
from jax import lax
from jax.experimental import pallas as pl
from jax.experimental.pallas import tpu as pltpu

F32 = jnp.float32
BF16 = jnp.bfloat16

D_MODEL = 1024
D_FF = 2816
HEAD_DIM = 64
N_ATT_HEADS = 8
ATT_WIDTH = N_ATT_HEADS * HEAD_DIM
ROT_DIM = HEAD_DIM // 4
ROPE_THETA = 500000.0
WIN_MAX = 2048
PAST_LEN = 8192
BRANCHES = ((128, 1), (512, 4), (2048, 16))
N_RET_HEADS = 4
RET_DK = 64
RET_DV = 128
RET_QK = N_RET_HEADS * RET_DK
RET_V = N_RET_HEADS * RET_DV
RET_THETA = 10000.0
RET_CHUNK = 128
IN_WIDTH = 3 * ATT_WIDTH + 2 * RET_QK + 2 * RET_V
NORM_EPS = 1e-6
NEG = -1e30

LANES = 128
QB = 128
SUPER = 2048
RUN = 512
ATT_RUNS = 4
VMEM_LIMIT = 56 * 1024 * 1024
MIXIN_VMEM_LIMIT = 60 * 1024 * 1024
LOG_G = tuple(math.log1p(-2.0 ** (-5.0 - h)) for h in range(N_RET_HEADS))
LOG2_E = math.log2(math.e)


def _full_spec(shape):
    nd = len(shape)
    return pl.BlockSpec(shape, lambda *_: (0,) * nd)


def _resident_spec(shape):
    nd = len(shape)
    return pl.BlockSpec(shape, lambda *_: (0,) * nd, pipeline_mode=pl.Buffered(1))


def _params(n_axes, vmem=VMEM_LIMIT):
    return pltpu.CompilerParams(dimension_semantics=("arbitrary",) * n_axes, vmem_limit_bytes=vmem)


def _rms(x, g):
    return x * lax.rsqrt(jnp.mean(x * x, axis=-1, keepdims=True) + NORM_EPS) * g


def _silu(x):
    return x / (1.0 + jnp.exp(-x))


def _dot(a, b):
    return jnp.dot(a, b, preferred_element_type=F32)


def _dot_nt(a, b):
    return lax.dot_general(a, b, (((1,), (1,)), ((), ())), preferred_element_type=F32)


def _dot_tn(a, b):
    return lax.dot_general(a, b, (((0,), (0,)), ((), ())), preferred_element_type=F32)


FF_CHUNK = 512


def _swiglu(u, wg_ref, wu_ref, wd_ref):
    acc = None
    for c0 in range(0, D_FF, FF_CHUNK):
        c1 = min(c0 + FF_CHUNK, D_FF)
        g = _dot(u, wg_ref[:, c0:c1])
        up = _dot(u, wu_ref[:, c0:c1])
        h = (_silu(g) * up).astype(BF16)
        d = _dot(h, wd_ref[c0:c1, :])
        acc = d if acc is None else acc + d
    return acc


def _mixout_ffn_kernel(oa_ref, yr_ref, h_ref, wo_ref, gmb_ref, gpre_ref, gpost_ref,
                       wg_ref, wu_ref, wd_ref, o_ref):
    mixed = jnp.concatenate([oa_ref[...].astype(BF16), yr_ref[...].astype(BF16)], axis=1)
    x = h_ref[...] + _rms(_dot(mixed, wo_ref[...]), gmb_ref[...])
    u = _rms(x, gpre_ref[...]).astype(BF16)
    y = _swiglu(u, wg_ref, wu_ref, wd_ref)
    o_ref[...] = x + 0.5 * _rms(y, gpost_ref[...])


def _ffn_specs():
    return [_full_spec((1, D_MODEL)), _full_spec((1, D_MODEL)),
            _resident_spec((D_MODEL, D_FF)), _resident_spec((D_MODEL, D_FF)),
            _resident_spec((D_FF, D_MODEL))]


def _mixout_ffn(oa, yr, h, wo, gmb, gpre, gpost, wg, wu, wd, tm):
    t = h.shape[0]
    row = pl.BlockSpec((tm, D_MODEL), lambda i: (i, 0))
    half = pl.BlockSpec((tm, ATT_WIDTH), lambda i: (i, 0))
    return pl.pallas_call(
        _mixout_ffn_kernel, grid=(t // tm,),
        in_specs=[half, half, row, _resident_spec((D_MODEL, D_MODEL)), _full_spec((1, D_MODEL))]
        + _ffn_specs(),
        out_specs=row, out_shape=jax.ShapeDtypeStruct((t, D_MODEL), F32),
        compiler_params=_params(1), name="mixout_ffn")(oa, yr, h, wo, gmb, gpre, gpost, wg, wu, wd)


def _rope_consts(base_pos, off_pos, rot_dim, theta, head_dim):
    half = rot_dim // 2
    inv = theta ** (-np.arange(half, dtype=np.float64) * (2.0 / rot_dim))
    lane = np.arange(LANES) % head_dim
    inv_l = inv[lane % half][None, :]
    first = (lane < half).astype(np.float64)[None, :]
    second = ((lane >= half) & (lane < rot_dim)).astype(np.float64)[None, :]
    rot = first + second
    a = np.asarray(base_pos, np.float64)[:, None] * inv_l
    b = np.asarray(off_pos, np.float64)[:, None] * inv_l
    cb, sb = np.cos(b), np.sin(b)
    base = np.stack([np.cos(a), np.sin(a)], axis=1)
    off = np.stack([cb * rot, sb * rot, np.broadcast_to(1.0 - rot, cb.shape),
                    -cb * first, -sb * first, cb * second, sb * second])
    return jnp.asarray(base, F32), jnp.asarray(off, F32)


def _rope_fill(base_ref, off_ref, tab_s):
    ca = base_ref[0:1, :]
    sa = base_ref[1:2, :]
    tab_s[0] = ca * off_ref[0] - sa * off_ref[1] + off_ref[2]
    tab_s[1] = sa * off_ref[3] + ca * off_ref[4]
    tab_s[2] = sa * off_ref[5] + ca * off_ref[6]


ROPE_ROWS = 64


def _rope_rows(tab_ref, half, rows):
    c, sa, sb = tab_ref[0, rows, :], tab_ref[1, rows, :], tab_ref[2, rows, :]
    return lambda x: x * c + pltpu.roll(x, LANES - half, 1) * sa + pltpu.roll(x, half, 1) * sb


def _lanes(r, s):
    return slice(r * ATT_WIDTH + s * LANES, r * ATT_WIDTH + (s + 1) * LANES)


def _emit_dilated(nat_s, x4_s, o4_ref, o16_ref):
    l16 = RUN // 16
    for w in range(nat_s.shape[1] // RUN):
        for s in range(ATT_WIDTH // LANES):
            for r in range(4):
                x4 = nat_s[s, pl.ds(w * RUN + r, RUN // 4, stride=4), :]
                o4_ref[w, :, _lanes(r, s)] = x4.astype(BF16)
                x4_s[r] = x4
            for r in range(4):
                for c in range(4):
                    o16_ref[w * l16:(w + 1) * l16, _lanes(4 * c + r, s)] = (
                        x4_s[r, pl.ds(c, l16, stride=4), :].astype(BF16))


def _mixin_kernel(h_ref, g_ref, w_ref, ba_ref, oa_ref, br_ref, or_ref, gn_ref,
                  q4_ref, k4_ref, v4_ref, q16_ref, k16_ref, v16_ref, kk_ref, vk_ref, y_ref, st_ref,
                  ta_ref, tr_ref, nat_s, natk_s, x4_s, qr_ref, kr_ref, vr_ref, gr_ref,
                  dec_s, wt_s, wh_s, st_s):
    _ret_init(dec_s, wt_s, wh_s, st_s)
    _rope_fill(ba_ref, oa_ref, ta_ref)
    _rope_fill(br_ref, or_ref, tr_ref)
    u = _rms(h_ref[...], g_ref[...]).astype(BF16)
    scale = HEAD_DIM ** -0.5 * LOG2_E
    nslab = ATT_WIDTH // LANES

    q = _dot(u, w_ref[:, 0:ATT_WIDTH])
    k = _dot(u, w_ref[:, ATT_WIDTH:2 * ATT_WIDTH])
    for rb in range(h_ref.shape[0] // ROPE_ROWS):
        rows = slice(rb * ROPE_ROWS, (rb + 1) * ROPE_ROWS)
        rope = _rope_rows(ta_ref, ROT_DIM // 2, rows)
        for s in range(nslab):
            sl = slice(s * LANES, (s + 1) * LANES)
            qs = rope(q[rows, sl]) * scale
            ks = rope(k[rows, sl])
            nat_s[s, rows, :] = qs
            natk_s[s, rows, :] = ks
            kk_ref[rows, sl] = ks
    _emit_dilated(nat_s, x4_s, q4_ref, q16_ref)
    _emit_dilated(natk_s, x4_s, k4_ref, k16_ref)

    v = _dot(u, w_ref[:, 2 * ATT_WIDTH:3 * ATT_WIDTH])
    for s in range(nslab):
        nat_s[s] = v[:, s * LANES:(s + 1) * LANES]
    _emit_dilated(nat_s, x4_s, v4_ref, v16_ref)
    vk_ref[...] = v

    o = 3 * ATT_WIDTH
    qk = _dot(u, w_ref[:, o:o + 2 * RET_QK])
    for rb in range(h_ref.shape[0] // ROPE_ROWS):
        rows = slice(rb * ROPE_ROWS, (rb + 1) * ROPE_ROWS)
        rope = _rope_rows(tr_ref, RET_DK // 2, rows)
        for s in range(RET_QK // LANES):
            sl = slice(s * LANES, (s + 1) * LANES)
            sk = slice(RET_QK + s * LANES, RET_QK + (s + 1) * LANES)
            qr_ref[rows, sl] = rope(qk[rows, sl]).astype(qr_ref.dtype)
            kr_ref[rows, sl] = (rope(qk[rows, sk]) * (RET_DK ** -0.5)).astype(kr_ref.dtype)
    o += 2 * RET_QK
    vr_ref[...] = _dot(u, w_ref[:, o:o + RET_V]).astype(vr_ref.dtype)
    o += RET_V
    gr_ref[...] = _dot(u, w_ref[:, o:o + RET_V])
    _ret_body(qr_ref, kr_ref, vr_ref, gr_ref, gn_ref, y_ref, st_ref, dec_s, wt_s, wh_s, st_s,
              qr_ref.shape[0] // RET_CHUNK)


def _mixin(h, g, w, base_pos, off_pos, keep_rows, tm, gn):
    t = h.shape[0]
    nt = t // tm
    assert tm % RUN == 0 and SUPER % tm == 0 and t % SUPER == 0 and keep_rows % tm == 0
    first_keep = nt - keep_rows // tm
    base_a, off_a = _rope_consts(base_pos, off_pos, ROT_DIM, ROPE_THETA, HEAD_DIM)
    base_r, off_r = _rope_consts(base_pos, off_pos, RET_DK, RET_THETA, RET_DK)

    def row(w_):
        return pl.BlockSpec((tm, w_), lambda i: (i, 0))

    keep_spec = pl.BlockSpec((tm, ATT_WIDTH), lambda i: (jnp.maximum(i - first_keep, 0), 0))
    keep_shape = ((keep_rows, ATT_WIDTH), F32)
    base_spec = pl.BlockSpec((None, 2, LANES), lambda i: (i, 0, 0))
    off_spec = _resident_spec((7, tm, LANES))
    tab = pltpu.VMEM((3, tm, LANES), F32)
    in_specs = [row(D_MODEL), _full_spec((1, D_MODEL)), _resident_spec((D_MODEL, IN_WIDTH)),
                base_spec, off_spec, base_spec, off_spec, _full_spec((1, RET_V))]
    tps = SUPER // tm
    s4 = pl.BlockSpec((tm // RUN, RUN // 4, 4 * ATT_WIDTH), lambda i: (i, 0, 0))
    s16 = pl.BlockSpec((None, tm // 16, 16 * ATT_WIDTH), lambda i: (i // tps, i % tps, 0))
    st_shape = (N_RET_HEADS // 2, 2 * RET_DK, RET_DV)
    specs = [s4] * 3 + [s16] * 3 + [keep_spec, keep_spec, row(RET_V), _full_spec(st_shape)]
    shapes = ([((t // RUN, RUN // 4, 4 * ATT_WIDTH), BF16)] * 3
              + [((t // SUPER, SUPER // 16, 16 * ATT_WIDTH), BF16)] * 3
              + [keep_shape, keep_shape, ((t, RET_V), BF16), (st_shape, F32)])
    dec = pltpu.VMEM((N_RET_HEADS, RET_CHUNK, RET_CHUNK), F32)
    nat = pltpu.VMEM((ATT_WIDTH // LANES, tm, LANES), F32)
    scratch = [tab, tab, nat, nat, pltpu.VMEM((4, RUN // 4, LANES), F32),
               pltpu.VMEM((tm, RET_QK), BF16), pltpu.VMEM((tm, RET_QK), BF16),
               pltpu.VMEM((tm, RET_V), BF16), pltpu.VMEM((tm, RET_V), F32),
               dec, dec, dec, pltpu.VMEM(st_shape, F32)]
    return pl.pallas_call(
        _mixin_kernel, grid=(nt,), in_specs=in_specs, out_specs=specs,
        out_shape=[jax.ShapeDtypeStruct(s, d) for s, d in shapes],
        scratch_shapes=scratch, compiler_params=_params(1, MIXIN_VMEM_LIMIT), name="mix_in")(
            h, g, w, base_a, off_a, base_r, off_r, gn)


def _attn_bias_tables():
    a = np.arange(QB)[:, None]
    c = np.arange(2 * QB)[None, :]
    steps = BRANCHES[0][0]
    dist = QB + a - c
    band = (dist >= 0) & (dist <= steps)
    cur = c >= QB
    tok_q = 4 * (a % 32) + a // 32
    cc = c % QB
    tok_k = 4 * (cc % 32) + cc // 32 + QB * (c // QB) - QB
    dist2 = tok_q - tok_k
    band2 = (dist2 >= 0) & (dist2 <= steps)
    masks = np.stack([band, band & cur, band2, band2 & cur])
    return np.where(masks, 0.0, NEG).astype(np.float32)


def _attn_qblock(get_q, get_kc, get_kp, get_vc, get_vp, bias, consts, sink):
    head_a, head_a_win, ones_a, ones_b = consts
    for s in range(ATT_WIDTH // LANES):
        q2 = get_q(s)
        kwin = jnp.concatenate([get_kp(s), get_kc(s)], axis=0)
        vwin = jnp.concatenate([get_vp(s), get_vc(s)], axis=0)
        zq = jnp.zeros_like(q2)
        qq = jnp.concatenate([jnp.where(head_a, q2, zq), jnp.where(head_a, zq, q2)], axis=0)
        sc = _dot_nt(qq, kwin)
        s_a = sc[:QB] + bias
        s_b = sc[QB:] + bias
        m_a = jnp.max(s_a, axis=-1, keepdims=True)
        m_b = jnp.max(s_b, axis=-1, keepdims=True)
        p = jnp.concatenate([jnp.exp2(s_a - m_a).astype(BF16), jnp.exp2(s_b - m_b).astype(BF16)],
                            axis=1)
        zv = jnp.zeros_like(vwin)
        w = jnp.concatenate(
            [jnp.concatenate([jnp.where(head_a_win, vwin, zv), ones_a], axis=1),
             jnp.concatenate([jnp.where(head_a_win, zv, vwin), ones_b], axis=1)], axis=0)
        r = _dot(p, w)
        sink(s, (jnp.where(head_a, m_a, m_b), r[:, LANES:], r[:, :LANES]))


def _merge(old, new):
    mo, lo, ao = old
    m2, l2, a2 = new
    mn = jnp.maximum(mo, m2)
    eo = jnp.exp2(mo - mn)
    e2 = jnp.exp2(m2 - mn)
    return mn, eo * lo + e2 * l2, eo * ao + e2 * a2


def _attn_kernel(q16, k16c, k16p, v16c, v16p, q4, k4c, k4p, v4c, v4p, bias_ref, o_ref,
                 m_s, l_s, a_s, nat_s):
    sb = pl.program_id(0)
    ph = pl.program_id(1)
    j = pl.program_id(2)
    lane = lax.broadcasted_iota(jnp.int32, (QB, LANES), 1)
    head_a = lane < HEAD_DIM
    lane_w = lax.broadcasted_iota(jnp.int32, (2 * QB, LANES), 1)
    head_a_win = lane_w < HEAD_DIM
    ones_a = jnp.where(head_a_win, 1.0, 0.0).astype(BF16)
    ones_b = jnp.where(head_a_win, 0.0, 1.0).astype(BF16)
    consts = (head_a, head_a_win, ones_a, ones_b)
    nslab = ATT_WIDTH // LANES
    sub = QB // 4
    lanes = _lanes

    run0 = j * ATT_RUNS
    first_run = sb * (SUPER // RUN) + run0 == 0

    @pl.when(ph == 0)
    def _():
        bias = bias_ref[(sb == 0).astype(jnp.int32)]
        for r in range(4 * ATT_RUNS):
            def scatter(s, res):
                for u in range(SUPER // RUN):
                    rows = pl.ds(u * RUN + (r % 4) * QB + run0 + r // 4, sub, stride=4)
                    for ref, val in zip((m_s, l_s, a_s), res):
                        ref[s, rows, :] = val[u * sub:(u + 1) * sub]

            _attn_qblock(lambda s: q16[:, lanes(r, s)], lambda s: k16c[:, lanes(r, s)],
                         lambda s: k16p[:, lanes(r, s)], lambda s: v16c[:, lanes(r, s)],
                         lambda s: v16p[:, lanes(r, s)], bias, consts, scatter)

    def prev_rows(ref_c, ref_p, w, rows):
        return (lambda cols: ref_p[rows, cols]) if w == 0 else (lambda cols: ref_c[w - 1, rows, cols])

    @pl.when(ph == 1)
    def _():
        for w in range(ATT_RUNS):
            bias = bias_ref[first_run.astype(jnp.int32)] if w == 0 else bias_ref[0]
            kp = prev_rows(k4c, k4p, w, slice(None))
            vp = prev_rows(v4c, v4p, w, slice(None))
            for r in range(4):
                rows = pl.ds(pl.multiple_of((run0 + w) * RUN + r * QB, QB), QB)

                def merge_in(s, res):
                    mn, ln, an = _merge((m_s[s, rows, :], l_s[s, rows, :], a_s[s, rows, :]), res)
                    m_s[s, rows, :] = mn
                    l_s[s, rows, :] = ln
                    a_s[s, rows, :] = an

                _attn_qblock(lambda s: q4[w, :, lanes(r, s)], lambda s: k4c[w, :, lanes(r, s)],
                             lambda s: kp(lanes(r, s)), lambda s: v4c[w, :, lanes(r, s)],
                             lambda s: vp(lanes(r, s)), bias, consts, merge_in)

    @pl.when(ph == 2)
    def _():
        for w in range(ATT_RUNS):
            for b in range(4):
                def cur(ref):
                    return lambda s: jnp.concatenate(
                        [ref[w, b * sub:(b + 1) * sub, lanes(r, s)] for r in range(4)], axis=0)

                def prev(ref_c, ref_p):
                    if b > 0:
                        return lambda s: jnp.concatenate(
                            [ref_c[w, (b - 1) * sub:b * sub, lanes(r, s)] for r in range(4)], axis=0)
                    tail = prev_rows(ref_c, ref_p, w, slice(QB - sub, QB))
                    return lambda s: jnp.concatenate([tail(lanes(r, s)) for r in range(4)], axis=0)

                if w == 0 and b == 0:
                    bias = bias_ref[2 + first_run.astype(jnp.int32)]
                else:
                    bias = bias_ref[2]

                def finish(s, res):
                    for r in range(4):
                        rows = pl.ds(pl.multiple_of((run0 + w) * RUN + r * QB + b * sub, sub), sub)
                        part = tuple(x[r * sub:(r + 1) * sub] for x in res)
                        _, ln, an = _merge((m_s[s, rows, :], l_s[s, rows, :], a_s[s, rows, :]), part)
                        nat_s[s, pl.ds(w * RUN + b * QB + r, sub, stride=4), :] = an / ln

                _attn_qblock(cur(q4), cur(k4c), prev(k4c, k4p), cur(v4c), prev(v4c, v4p),
                             bias, consts, finish)
        for s in range(nslab):
            o_ref[:, s * LANES:(s + 1) * LANES] = nat_s[s].astype(o_ref.dtype)


def _attn_prompt(q4, k4, v4, q16, k16, v16):
    nsb = q16.shape[0]
    s_len = nsb * SUPER
    steps = SUPER // RUN // ATT_RUNS
    wide = 4 * ATT_WIDTH
    blk16 = (None, QB, ATT_RUNS * wide)
    blk4 = (ATT_RUNS, QB, wide)
    blk4p = (None, QB, wide)

    def j16(ph, j):
        return jnp.where(ph == 0, j, steps - 1)

    def n4(sb, ph, j):
        return sb * steps + jnp.where(ph == 0, 0, j)

    c16 = pl.BlockSpec(blk16, lambda sb, ph, j: (sb, 0, j16(ph, j)))
    p16 = pl.BlockSpec(blk16, lambda sb, ph, j: (jnp.maximum(sb - 1, 0), 0, j16(ph, j)))
    c4 = pl.BlockSpec(blk4, lambda sb, ph, j: (n4(sb, ph, j), 0, 0))
    p4 = pl.BlockSpec(blk4p, lambda sb, ph, j: (jnp.maximum(ATT_RUNS * n4(sb, ph, j) - 1, 0), 0, 0))
    out = pl.BlockSpec((ATT_RUNS * RUN, ATT_WIDTH),
                       lambda sb, ph, j: (sb * steps + jnp.where(ph == 2, j, 0), 0))
    bias = jnp.asarray(_attn_bias_tables())
    nslab = ATT_WIDTH // LANES
    scratch = ([pltpu.VMEM((nslab, SUPER, LANES), F32) for _ in range(3)]
               + [pltpu.VMEM((nslab, ATT_RUNS * RUN, LANES), F32)])
    return pl.pallas_call(
        _attn_kernel, grid=(nsb, 3, steps),
        in_specs=[c16, c16, p16, c16, p16, c4, c4, p4, c4, p4, _full_spec(bias.shape)],
        out_specs=out, out_shape=jax.ShapeDtypeStruct((s_len, ATT_WIDTH), BF16),
        scratch_shapes=scratch, compiler_params=_params(3), name="attn_prompt")(
            q16, k16, k16, v16, v16, q4, k4, k4, v4, v4, bias)


def _ret_init(dec_s, wt_s, wh_s, st_s):
    c = RET_CHUNK

    @pl.when(pl.program_id(0) == 0)
    def _():
        i = lax.broadcasted_iota(jnp.int32, (c, c), 0)
        jj = lax.broadcasted_iota(jnp.int32, (c, c), 1)
        diff = (i - jj).astype(F32)
        for h in range(N_RET_HEADS):
            dec_s[h] = jnp.where(diff >= 0, jnp.exp(diff * LOG_G[h]), 0.0)
            wt_s[h] = jnp.exp((c - 1.0 - i.astype(F32)) * LOG_G[h])
            wh_s[h] = jnp.exp((i.astype(F32) + 1.0) * LOG_G[h])
        st_s[...] = jnp.zeros_like(st_s)


def _ret_body(qr_ref, kr_ref, vr_ref, gr_ref, gn_ref, y_ref, s_ref, dec_s, wt_s, wh_s, st_s, chunks):
    c = RET_CHUNK
    lane = lax.broadcasted_iota(jnp.int32, (c, LANES), 1)
    head_a = lane < RET_DK
    row_a = lax.broadcasted_iota(jnp.int32, (LANES, RET_DV), 0) < RET_DK
    for ci in range(chunks):
        rows = slice(ci * c, (ci + 1) * c)
        for p in range(N_RET_HEADS // 2):
            sl = slice(p * LANES, (p + 1) * LANES)
            q2 = qr_ref[rows, sl]
            k2 = kr_ref[rows, sl]
            zq = jnp.zeros_like(q2)
            qq = jnp.concatenate([jnp.where(head_a, q2, zq), jnp.where(head_a, zq, q2)], axis=0)
            inner = _dot_nt(qq, k2)
            state = st_s[p]
            cross = _dot(qq, state.astype(BF16))
            vws = []
            for hh in range(2):
                h = 2 * p + hh
                hs = slice(h * RET_DV, (h + 1) * RET_DV)
                v = vr_ref[rows, hs]
                inn = (inner[hh * c:(hh + 1) * c] * dec_s[h]).astype(BF16)
                o = _dot(inn, v) + cross[hh * c:(hh + 1) * c] * wh_s[h]
                mu = jnp.mean(o, axis=-1, keepdims=True)
                xc = o - mu
                var = jnp.mean(xc * xc, axis=-1, keepdims=True)
                y = xc * lax.rsqrt(var + NORM_EPS) * gn_ref[:, hs]
                y_ref[rows, hs] = (_silu(gr_ref[rows, hs]) * y).astype(y_ref.dtype)
                vws.append((v.astype(F32) * wt_s[h]).astype(BF16))
            upd = _dot_tn(k2, jnp.concatenate(vws, axis=1))
            upd = jnp.where(row_a, upd[:, :RET_DV], upd[:, RET_DV:])
            gch = jnp.where(row_a, math.exp(c * LOG_G[2 * p]), math.exp(c * LOG_G[2 * p + 1]))
            st_s[p] = gch * state + upd

    @pl.when(pl.program_id(0) == pl.num_programs(0) - 1)
    def _():
        s_ref[...] = st_s[...]


S_ROWS = 8
S_HGRP = 4


def _sample_tables(t_new, past_len, wb):
    rows = np.concatenate([np.arange(wb), wb + np.arange(S_ROWS)])
    tab = np.zeros((S_ROWS, wb + S_ROWS), np.float32)
    for r in range(S_ROWS):
        i = r % t_new
        delta = wb + i - rows
        ok = (delta >= 0) & (past_len + i - delta >= 0) & (rows < wb + t_new)
        for window, dil in BRANCHES:
            tab[r] += ok & (delta % dil == 0) & (delta <= window)
    tab = np.tile(tab, (S_HGRP, 1))
    return tab[:, :wb], tab[:, wb:]


def _sample_attn_body(q_ref, kn_ref, vn_ref, kt_ref, vt_ref, cc_ref, cn_ref, o_ref):
    mult_c, mult_n = cc_ref[...], cn_ref[...]
    bias_c = jnp.where(mult_c > 0, 0.0, NEG)
    bias_n = jnp.where(mult_n > 0, 0.0, NEG)
    gw = S_HGRP * HEAD_DIM
    lane_head = lax.broadcasted_iota(jnp.int32, (S_ROWS, gw), 1) // HEAD_DIM
    for g in range(N_ATT_HEADS // S_HGRP):
        gs = slice(g * gw, (g + 1) * gw)
        q8 = q_ref[:, gs]
        q = jnp.concatenate([jnp.where(lane_head == h, q8, 0.0) for h in range(S_HGRP)],
                            axis=0).astype(BF16)
        s_c = _dot(q, kt_ref[g].astype(BF16)) + bias_c
        s_n = _dot_nt(q, kn_ref[:, gs].astype(BF16)) + bias_n
        m = jnp.maximum(jnp.max(s_c, axis=-1, keepdims=True), jnp.max(s_n, axis=-1, keepdims=True))
        p_c = jnp.exp2(s_c - m) * mult_c
        p_n = jnp.exp2(s_n - m) * mult_n
        l = jnp.sum(p_c, axis=-1, keepdims=True) + jnp.sum(p_n, axis=-1, keepdims=True)
        o = (_dot_nt(p_c.astype(BF16), vt_ref[g].astype(BF16))
             + _dot(p_n.astype(BF16), vn_ref[:, gs].astype(BF16))) / l
        o8 = jnp.zeros((S_ROWS, gw), F32)
        for h in range(S_HGRP):
            o8 = jnp.where(lane_head == h, o[h * S_ROWS:(h + 1) * S_ROWS], o8)
        o_ref[:, gs] = o8


FF_SPLITS = ((0, 768), (768, 1536), (1536, 2304), (2304, D_FF))


def _ffn_attn_kernel(x_ref, gpre_ref, gpost_ref, wg_ref, wu_ref, wd_ref,
                     q_ref, kn_ref, vn_ref, kt_ref, vt_ref, cc_ref, cn_ref,
                     o_ref, oa_ref, u_s, acc_s):
    j = pl.program_id(1)
    last = len(FF_SPLITS) - 1
    for k, (c0, c1) in enumerate(FF_SPLITS):
        @pl.when(j == k)
        def _():
            if k == 0:
                u = _rms(x_ref[...], gpre_ref[...]).astype(BF16)
                u_s[...] = u
            else:
                u = u_s[...]
            h = (_silu(_dot(u, wg_ref[:, c0:c1])) * _dot(u, wu_ref[:, c0:c1])).astype(BF16)
            d = _dot(h, wd_ref[c0:c1, :])
            if k == 0:
                acc_s[...] = d
            elif k < last:
                acc_s[...] += d
            else:
                o_ref[...] = x_ref[...] + 0.5 * _rms(acc_s[...] + d, gpost_ref[...])
            _sample_attn_body(q_ref, kn_ref, vn_ref, kt_ref, vt_ref, cc_ref, cn_ref, oa_ref)


def _ffn_with_sample_attn(x, gpre, gpost, wg, wu, wd, q, kn, vn, cache_k, cache_v, t_new, past_len, tm):
    t = x.shape[0]
    nb, wb = cache_k.shape[0], cache_k.shape[1]
    nsplit = len(FF_SPLITS)
    assert t_new <= S_ROWS and nb == (t // tm) * nsplit

    def rows(a):
        return jnp.pad(a.reshape(nb, t_new, ATT_WIDTH), ((0, 0), (0, S_ROWS - t_new), (0, 0)))

    def grouped(c):
        return c.transpose(0, 2, 3, 1).reshape(nb, N_ATT_HEADS // S_HGRP, S_HGRP * HEAD_DIM, wb)

    row = pl.BlockSpec((tm, D_MODEL), lambda i, j: (i, 0))
    small = pl.BlockSpec((None, S_ROWS, ATT_WIDTH), lambda i, j: (i * nsplit + j, 0, 0))
    big = pl.BlockSpec((None, N_ATT_HEADS // S_HGRP, S_HGRP * HEAD_DIM, wb),
                       lambda i, j: (i * nsplit + j, 0, 0, 0))
    tabs = [jnp.asarray(a) for a in _sample_tables(t_new, past_len, wb)]
    h1, o = pl.pallas_call(
        _ffn_attn_kernel, grid=(t // tm, nsplit),
        in_specs=[row] + _ffn_specs() + [small, small, small, big, big]
        + [_full_spec(a.shape) for a in tabs],
        out_specs=[row, small],
        out_shape=[jax.ShapeDtypeStruct((t, D_MODEL), F32),
                   jax.ShapeDtypeStruct((nb, S_ROWS, ATT_WIDTH), F32)],
        scratch_shapes=[pltpu.VMEM((tm, D_MODEL), BF16), pltpu.VMEM((tm, D_MODEL), F32)],
        compiler_params=_params(2), name="ffn_attn_sample")(
            x, gpre, gpost, wg, wu, wd, rows(q), rows(kn), rows(vn),
            grouped(cache_k), grouped(cache_v), *tabs)
    return h1, o[:, :t_new].reshape(nb * t_new, ATT_WIDTH)


R_PAIR = 2


def _sample_ret_tables(t_new):
    r = np.arange(R_PAIR * t_new)
    seq, step = r // t_new, r % t_new
    lg = np.asarray(LOG_G)[:, None, None]
    diff = (step[:, None] - step[None, :])[None]
    ok = ((seq[:, None] == seq[None, :]) & (diff[0] >= 0))[None]
    decay = np.where(ok, np.exp(diff * lg), 0.0).reshape(N_RET_HEADS * r.size, r.size)
    w_head = np.exp((step[None, :] + 1.0) * lg[:, :, 0]).reshape(-1, 1) * np.ones((1, RET_DV))
    w_tail = np.exp((t_new - 1.0 - step)[:, None, None] * lg[None, :, :, 0]) * np.ones((1, 1, RET_DV))
    g_chunk = np.repeat(np.exp(t_new * lg[:, 0, 0]), RET_DK)[:, None] * np.ones((1, RET_DV))
    return [a.astype(np.float32) for a in (decay, w_head, w_tail.reshape(r.size, -1), g_chunk)]


def _sample_ret_kernel(q_ref, k_ref, v_ref, g_ref, gn_ref, st_ref, dec_ref, wh_ref, wt_ref, gc_ref,
                       y_ref, so_ref, *, blocks, t_new):
    nrow = R_PAIR * t_new
    lane_head = lax.broadcasted_iota(jnp.int32, (nrow, RET_QK), 1) // RET_DK
    row_seq = lax.broadcasted_iota(jnp.int32, (nrow, RET_QK), 0) // t_new
    out_seq = (lax.broadcasted_iota(jnp.int32, (N_RET_HEADS * nrow, RET_DV), 0) % nrow) // t_new
    for blk in range(blocks):
        rs = slice(blk * nrow, (blk + 1) * nrow)
        q8, k8, v8 = q_ref[rs, :], k_ref[rs, :], v_ref[rs, :]
        qm = jnp.concatenate([jnp.where(lane_head == h, q8, 0.0) for h in range(N_RET_HEADS)],
                             axis=0).astype(BF16)
        inner = (_dot_nt(qm, k8.astype(BF16)) * dec_ref[...]).astype(BF16)
        o_all = _dot(inner, v8.astype(BF16))
        vw = (v8 * wt_ref[...]).astype(BF16)
        cross = None
        for s in range(R_PAIR):
            state = st_ref[blk * R_PAIR + s]
            c = _dot(qm, state.astype(BF16))
            cross = c if cross is None else jnp.where(out_seq == s, c, cross)
            km = jnp.where(row_seq == s, k8, 0.0).astype(BF16)
            upd = _dot_tn(km, vw)
            upd = jnp.concatenate([upd[h * RET_DK:(h + 1) * RET_DK, h * RET_DV:(h + 1) * RET_DV]
                                   for h in range(N_RET_HEADS)], axis=0)
            so_ref[blk * R_PAIR + s] = gc_ref[...] * state + upd
        cross = cross * wh_ref[...]
        for h in range(N_RET_HEADS):
            hs = slice(h * RET_DV, (h + 1) * RET_DV)
            oh = o_all[h * nrow:(h + 1) * nrow, hs] + cross[h * nrow:(h + 1) * nrow]
            mu = jnp.mean(oh, axis=-1, keepdims=True)
            xc = oh - mu
            var = jnp.mean(xc * xc, axis=-1, keepdims=True)
            y = xc * lax.rsqrt(var + NORM_EPS) * gn_ref[:, hs]
            y_ref[rs, hs] = _silu(g_ref[rs, hs]) * y


def _ret_sample(qr, kr, vr, gr, gn, state, t_new, blocks=16):
    nb = state.shape[0]
    assert R_PAIR * t_new == 8 and nb % (R_PAIR * blocks) == 0
    rows = R_PAIR * t_new * blocks
    tabs = [jnp.asarray(a) for a in _sample_ret_tables(t_new)]
    st = state.reshape(nb, N_RET_HEADS * RET_DK, RET_DV)

    def blk(w_):
        return pl.BlockSpec((rows, w_), lambda i: (i, 0))

    st_spec = pl.BlockSpec((R_PAIR * blocks, N_RET_HEADS * RET_DK, RET_DV), lambda i: (i, 0, 0))
    y, s = pl.pallas_call(
        functools.partial(_sample_ret_kernel, blocks=blocks, t_new=t_new),
        grid=(nb // (R_PAIR * blocks),),
        in_specs=[blk(RET_QK), blk(RET_QK), blk(RET_V), blk(RET_V), _full_spec((1, RET_V)), st_spec]
        + [_full_spec(a.shape) for a in tabs],
        out_specs=[blk(RET_V), st_spec],
        out_shape=[jax.ShapeDtypeStruct((nb * t_new, RET_V), F32), jax.ShapeDtypeStruct(st.shape, F32)],
        compiler_params=_params(1), name="ret_sample")(qr, kr, vr, gr, gn, st, *tabs)
    return y, s.reshape(state.shape)


CONV_COLS = 256


def _ffn_conv_kernel(*refs, with_mix):
    if with_mix:
        (oa_ref, yr_ref, h_ref, wo_ref, gmb_ref, gpre_ref, gpost_ref, wg_ref, wu_ref, wd_ref,
         o_ref, wob_ref, wgb_ref, wub_ref, wdb_ref, x_s, u_s, acc_s) = refs
    else:
        (h_ref, gpre_ref, gpost_ref, wg_ref, wu_ref, wd_ref,
         o_ref, wgb_ref, wub_ref, wdb_ref, x_s, u_s, acc_s) = refs
    c = pl.program_id(0)

    @pl.when(c == 0)
    def _():
        x = h_ref[...]
        if with_mix:
            wo = wo_ref[...].astype(BF16)
            wob_ref[...] = wo
            mixed = jnp.concatenate([oa_ref[...].astype(BF16), yr_ref[...].astype(BF16)], axis=1)
            x = x + _rms(_dot(mixed, wo), gmb_ref[...])
        x_s[...] = x
        u_s[...] = _rms(x, gpre_ref[...]).astype(BF16)
        acc_s[...] = jnp.zeros_like(acc_s)

    wg, wu, wd = wg_ref[...].astype(BF16), wu_ref[...].astype(BF16), wd_ref[...].astype(BF16)
    wgb_ref[...] = wg
    wub_ref[...] = wu
    wdb_ref[...] = wd
    u = u_s[...]
    acc_s[...] += _dot((_silu(_dot(u, wg)) * _dot(u, wu)).astype(BF16), wd)

    @pl.when(c == pl.num_programs(0) - 1)
    def _():
        o_ref[...] = x_s[...] + 0.5 * _rms(acc_s[...], gpost_ref[...])


def _ffn_convert(h, gpre, gpost, wg, wu, wd, mix=None):
    t = h.shape[0]
    col = pl.BlockSpec((D_MODEL, CONV_COLS), lambda c: (0, c))
    rowb = pl.BlockSpec((CONV_COLS, D_MODEL), lambda c: (c, 0))
    vec = _full_spec((1, D_MODEL))
    ins, in_specs, outs, out_specs = [], [], [], []
    if mix is not None:
        att, ret, wo, gmb = mix
        ins += [att, ret]
        in_specs += [_full_spec((t, ATT_WIDTH)), _full_spec((t, RET_V))]
    ins.append(h)
    in_specs.append(_full_spec((t, D_MODEL)))
    if mix is not None:
        ins += [wo, gmb]
        in_specs += [_full_spec((D_MODEL, D_MODEL)), vec]
    ins += [gpre, gpost, wg, wu, wd]
    in_specs += [vec, vec, col, col, rowb]
    out_shape = [jax.ShapeDtypeStruct((t, D_MODEL), F32)]
    out_specs = [_full_spec((t, D_MODEL))]
    if mix is not None:
        out_shape.append(jax.ShapeDtypeStruct((D_MODEL, D_MODEL), BF16))
        out_specs.append(_full_spec((D_MODEL, D_MODEL)))
    out_shape += [jax.ShapeDtypeStruct((D_MODEL, D_FF), BF16)] * 2 + [jax.ShapeDtypeStruct((D_FF, D_MODEL), BF16)]
    out_specs += [col, col, rowb]
    return pl.pallas_call(
        functools.partial(_ffn_conv_kernel, with_mix=mix is not None), grid=(D_FF // CONV_COLS,),
        in_specs=in_specs, out_specs=out_specs, out_shape=out_shape,
        scratch_shapes=[pltpu.VMEM((t, D_MODEL), F32), pltpu.VMEM((t, D_MODEL), BF16),
                        pltpu.VMEM((t, D_MODEL), F32)],
        compiler_params=_params(1), name="ffn_convert")(*ins)


def _mixin_sample_kernel(h_ref, g_ref, w_ref, ba_ref, oa_ref, br_ref, or_ref,
                         qa_ref, kk_ref, vk_ref, qr_ref, kr_ref, vr_ref, gr_ref, wb_ref,
                         ta_s, tr_s, u_s):
    c = pl.program_id(0)

    @pl.when(c == 0)
    def _():
        _rope_fill(ba_ref, oa_ref, ta_s)
        _rope_fill(br_ref, or_ref, tr_s)
        u_s[...] = _rms(h_ref[...], g_ref[...]).astype(BF16)

    w = w_ref[...].astype(BF16)
    wb_ref[...] = w
    y = _dot(u_s[...], w)
    nblk = h_ref.shape[0] // ROPE_ROWS

    def rotary(tab_s, half, cols, dst_ref, scale):
        for rb in range(nblk):
            rows = slice(rb * ROPE_ROWS, (rb + 1) * ROPE_ROWS)
            rope = _rope_rows(tab_s, half, rows)
            for s in range(dst_ref.shape[1] // LANES):
                dst_ref[rows, s * LANES:(s + 1) * LANES] = rope(
                    y[rows, cols + s * LANES:cols + (s + 1) * LANES]) * scale

    @pl.when(c == 0)
    def _():
        rotary(ta_s, ROT_DIM // 2, 0, qa_ref, HEAD_DIM ** -0.5 * LOG2_E)

    @pl.when(c == 1)
    def _():
        rotary(ta_s, ROT_DIM // 2, 0, kk_ref, 1.0)

    @pl.when(c == 2)
    def _():
        vk_ref[...] = y

    @pl.when(c == 3)
    def _():
        rotary(tr_s, RET_DK // 2, 0, qr_ref, 1.0)
        rotary(tr_s, RET_DK // 2, RET_QK, kr_ref, RET_DK ** -0.5)

    @pl.when(c == 4)
    def _():
        vr_ref[...] = y

    @pl.when(c == 5)
    def _():
        gr_ref[...] = y


def _mixin_sample(h, g, w, base_pos, off_pos):
    t = h.shape[0]
    grp = ATT_WIDTH
    assert IN_WIDTH == 6 * grp and 2 * RET_QK == grp and RET_V == grp
    base_a, off_a = _rope_consts(base_pos, off_pos, ROT_DIM, ROPE_THETA, HEAD_DIM)
    base_r, off_r = _rope_consts(base_pos, off_pos, RET_DK, RET_THETA, RET_DK)
    wcol = pl.BlockSpec((D_MODEL, grp), lambda c: (0, c))
    base_spec = pl.BlockSpec((None, 2, LANES), lambda c: (0, 0, 0))
    widths = [ATT_WIDTH, ATT_WIDTH, ATT_WIDTH, RET_QK, RET_QK, RET_V, RET_V]
    tab = pltpu.VMEM((3, t, LANES), F32)
    return pl.pallas_call(
        _mixin_sample_kernel, grid=(IN_WIDTH // grp,),
        in_specs=[_full_spec((t, D_MODEL)), _full_spec((1, D_MODEL)), wcol,
                  base_spec, _full_spec(off_a.shape), base_spec, _full_spec(off_r.shape)],
        out_specs=[_full_spec((t, w_)) for w_ in widths] + [wcol],
        out_shape=[jax.ShapeDtypeStruct((t, w_), F32) for w_ in widths]
        + [jax.ShapeDtypeStruct((D_MODEL, IN_WIDTH), BF16)],
        scratch_shapes=[tab, tab, pltpu.VMEM((t, D_MODEL), BF16)],
        compiler_params=_params(1), name="mix_in_sample")(h, g, w, base_a, off_a, base_r, off_r)


def kernel(x_prompt, x_sample, cache_k, cache_v, state_ret, g_ffn1_pre, g_ffn1_post, w1_gate, w1_up,
           w1_down, g_mix_pre, g_mix_post, w_in, gn_w, w_out, g_ffn2_pre, g_ffn2_post, w2_gate, w2_up,
           w2_down):
    b_p, s_p, _ = x_prompt.shape
    b_s, t_s, _ = x_sample.shape
    depth = w_in.shape[0]
    assert depth == 1 and b_p == 1
    keep = min(WIN_MAX, s_p)
    l = 0
    vec = lambda g: g[l].reshape(1, -1)
    g1a, g1b, gma, gmb = vec(g_ffn1_pre), vec(g_ffn1_post), vec(g_mix_pre), vec(g_mix_post)
    g2a, g2b, gn = vec(g_ffn2_pre), vec(g_ffn2_post), vec(gn_w)

    n_s = b_s * t_s
    xs = x_sample.reshape(n_s, D_MODEL)
    h1s, *w1 = _ffn_convert(xs, g1a, g1b, w1_gate[l], w1_up[l], w1_down[l])
    qas, kks, vks, qrs, krs, vrs, grs, wi = _mixin_sample(
        h1s, gma, w_in[l], np.full((1,), PAST_LEN), np.tile(np.arange(t_s), b_s))

    tm = RUN
    xp = x_prompt.reshape(s_p, D_MODEL)
    h1, o_att_s = _ffn_with_sample_attn(xp, g1a, g1b, *w1, qas, kks, vks, cache_k[l], cache_v[l],
                                        t_s, PAST_LEN, tm)
    q4, k4, v4, q16, k16, v16, kk, vk, y_ret, st_p = _mixin(
        h1, gma, wi, 2 * tm * np.arange(s_p // (2 * tm)), np.arange(2 * tm), keep, 2 * tm, gn)
    o_att = _attn_prompt(q4, k4, v4, q16, k16, v16)

    y_ret_s, st_s = _ret_sample(qrs, krs, vrs, grs, gn, state_ret[l], t_s)
    y_sample, wo, *w2 = _ffn_convert(h1s, g2a, g2b, w2_gate[l], w2_up[l], w2_down[l],
                                     mix=(o_att_s, y_ret_s, w_out[l], gmb))
    y_prompt = _mixout_ffn(o_att, y_ret, h1, wo, gmb, g2a, g2b, *w2, 2 * tm)

    hd = (N_ATT_HEADS, HEAD_DIM)
    return (y_prompt.reshape(b_p, s_p, D_MODEL),
            y_sample.reshape(b_s, t_s, D_MODEL),
            kk.reshape(depth, b_p, keep, *hd),
            vk.reshape(depth, b_p, keep, *hd),
            st_p.reshape(depth, b_p, N_RET_HEADS, RET_DK, RET_DV),
            kks.reshape(depth, b_s, t_s, *hd),
            vks.reshape(depth, b_s, t_s, *hd),
            st_s.reshape(depth, b_s, N_RET_HEADS, RET_DK, RET_DV))
```

```python
import functools
import math

import numpy as np
import jax
import jax.numpy as jnp
from jax import lax
from jax.experimental import pallas as pl
from jax.experimental.pallas import tpu as pltpu

F32 = jnp.float32
BF16 = jnp.bfloat16

D_MODEL = 1024
D_FF = 2816
HEAD_DIM = 64
N_ATT_HEADS = 8
ATT_WIDTH = N_ATT_HEADS * HEAD_DIM
ROT_DIM = HEAD_DIM // 4
ROPE_THETA = 500000.0
WIN_MAX = 2048
PAST_LEN = 8192
BRANCHES = ((128, 1), (512, 4), (2048, 16))
N_RET_HEADS = 4
RET_DK = 64
RET_DV = 128
RET_QK = N_RET_HEADS * RET_DK
RET_V = N_RET_HEADS * RET_DV
RET_THETA = 10000.0
RET_CHUNK = 128
IN_WIDTH = 3 * ATT_WIDTH + 2 * RET_QK + 2 * RET_V
NORM_EPS = 1e-6
NEG = -1e30

LANES = 128
QB = 128
SUPER = 2048
RUN = 512
ATT_RUNS = 4
VMEM_LIMIT = 56 * 1024 * 1024
MIXIN_VMEM_LIMIT = 60 * 1024 * 1024
LOG_G = tuple(math.log1p(-2.0 ** (-5.0 - h)) for h in range(N_RET_HEADS))
LOG2_E = math.log2(math.e)


def _full_spec(shape):
    nd = len(shape)
    return pl.BlockSpec(shape, lambda *_: (0,) * nd)


def _resident_spec(shape):
    nd = len(shape)
    return pl.BlockSpec(shape, lambda *_: (0,) * nd, pipeline_mode=pl.Buffered(1))


def _params(n_axes, vmem=VMEM_LIMIT):
    return pltpu.CompilerParams(dimension_semantics=("arbitrary",) * n_axes, vmem_limit_bytes=vmem)


def _rms(x, g):
    return x * lax.rsqrt(jnp.mean(x * x, axis=-1, keepdims=True) + NORM_EPS) * g


def _silu(x):
    return x / (1.0 + jnp.exp(-x))


def _dot(a, b):
    return jnp.dot(a, b, preferred_element_type=F32)


def _dot_nt(a, b):
    return lax.dot_general(a, b, (((1,), (1,)), ((), ())), preferred_element_type=F32)


def _dot_tn(a, b):
    return lax.dot_general(a, b, (((0,), (0,)), ((), ())), preferred_element_type=F32)


FF_CHUNK = 512


def _swiglu(u, wg_ref, wu_ref, wd_ref):
    acc = None
    for c0 in range(0, D_FF, FF_CHUNK):
        c1 = min(c0 + FF_CHUNK, D_FF)
        g = _dot(u, wg_ref[:, c0:c1])
        up = _dot(u, wu_ref[:, c0:c1])
        h = (_silu(g) * up).astype(BF16)
        d = _dot(h, wd_ref[c0:c1, :])
        acc = d if acc is None else acc + d
    return acc


def _mixout_ffn_kernel(oa_ref, yr_ref, h_ref, wo_ref, gmb_ref, gpre_ref, gpost_ref,
                       wg_ref, wu_ref, wd_ref, o_ref):
    mixed = jnp.concatenate([oa_ref[...].astype(BF16), yr_ref[...].astype(BF16)], axis=1)
    x = h_ref[...] + _rms(_dot(mixed, wo_ref[...]), gmb_ref[...])
    u = _rms(x, gpre_ref[...]).astype(BF16)
    y = _swiglu(u, wg_ref, wu_ref, wd_ref)
    o_ref[...] = x + 0.5 * _rms(y, gpost_ref[...])


def _ffn_specs():
    return [_full_spec((1, D_MODEL)), _full_spec((1, D_MODEL)),
            _resident_spec((D_MODEL, D_FF)), _resident_spec((D_MODEL, D_FF)),
            _resident_spec((D_FF, D_MODEL))]


def _mixout_ffn(oa, yr, h, wo, gmb, gpre, gpost, wg, wu, wd, tm):
    t = h.shape[0]
    row = pl.BlockSpec((tm, D_MODEL), lambda i: (i, 0))
    half = pl.BlockSpec((tm, ATT_WIDTH), lambda i: (i, 0))
    return pl.pallas_call(
        _mixout_ffn_kernel, grid=(t // tm,),
        in_specs=[half, half, row, _resident_spec((D_MODEL, D_MODEL)), _full_spec((1, D_MODEL))]
        + _ffn_specs(),
        out_specs=row, out_shape=jax.ShapeDtypeStruct((t, D_MODEL), F32),
        compiler_params=_params(1), name="mixout_ffn")(oa, yr, h, wo, gmb, gpre, gpost, wg, wu, wd)


def _rope_consts(base_pos, off_pos, rot_dim, theta, head_dim):
    half = rot_dim // 2
    inv = theta ** (-np.arange(half, dtype=np.float64) * (2.0 / rot_dim))
    lane = np.arange(LANES) % head_dim
    inv_l = inv[lane % half][None, :]
    first = (lane < half).astype(np.float64)[None, :]
    second = ((lane >= half) & (lane < rot_dim)).astype(np.float64)[None, :]
    rot = first + second
    a = np.asarray(base_pos, np.float64)[:, None] * inv_l
    b = np.asarray(off_pos, np.float64)[:, None] * inv_l
    cb, sb = np.cos(b), np.sin(b)
    base = np.stack([np.cos(a), np.sin(a)], axis=1)
    off = np.stack([cb * rot, sb * rot, np.broadcast_to(1.0 - rot, cb.shape),
                    -cb * first, -sb * first, cb * second, sb * second])
    return jnp.asarray(base, F32), jnp.asarray(off, F32)


def _rope_fill(base_ref, off_ref, tab_s):
    ca = base_ref[0:1, :]
    sa = base_ref[1:2, :]
    tab_s[0] = ca * off_ref[0] - sa * off_ref[1] + off_ref[2]
    tab_s[1] = sa * off_ref[3] + ca * off_ref[4]
    tab_s[2] = sa * off_ref[5] + ca * off_ref[6]


ROPE_ROWS = 64


def _rope_rows(tab_ref, half, rows):
    c, sa, sb = tab_ref[0, rows, :], tab_ref[1, rows, :], tab_ref[2, rows, :]
    return lambda x: x * c + pltpu.roll(x, LANES - half, 1) * sa + pltpu.roll(x, half, 1) * sb


def _lanes(r, s):
    return slice(r * ATT_WIDTH + s * LANES, r * ATT_WIDTH + (s + 1) * LANES)


def _emit_dilated(nat_s, x4_s, o4_ref, o16_ref):
    l16 = RUN // 16
    for w in range(nat_s.shape[1] // RUN):
        for s in range(ATT_WIDTH // LANES):
            for r in range(4):
                x4 = nat_s[s, pl.ds(w * RUN + r, RUN // 4, stride=4), :]
                o4_ref[w, :, _lanes(r, s)] = x4.astype(BF16)
                x4_s[r] = x4
            for r in range(4):
                for c in range(4):
                    o16_ref[w * l16:(w + 1) * l16, _lanes(4 * c + r, s)] = (
                        x4_s[r, pl.ds(c, l16, stride=4), :].astype(BF16))


def _mixin_kernel(h_ref, g_ref, w_ref, ba_ref, oa_ref, br_ref, or_ref, gn_ref,
                  q4_ref, k4_ref, v4_ref, q16_ref, k16_ref, v16_ref, kk_ref, vk_ref, y_ref, st_ref,
                  ta_ref, tr_ref, nat_s, natk_s, x4_s, qr_ref, kr_ref, vr_ref, gr_ref,
                  dec_s, wt_s, wh_s, st_s, *, first_keep):
    _ret_init(dec_s, wt_s, wh_s, st_s)
    _rope_fill(ba_ref, oa_ref, ta_ref)
    _rope_fill(br_ref, or_ref, tr_ref)
    u = _rms(h_ref[...], g_ref[...]).astype(BF16)
    scale = HEAD_DIM ** -0.5 * LOG2_E
    nslab = ATT_WIDTH // LANES

    q = _dot(u, w_ref[:, 0:ATT_WIDTH])
    k = _dot(u, w_ref[:, ATT_WIDTH:2 * ATT_WIDTH])
    for rb in range(h_ref.shape[0] // ROPE_ROWS):
        rows = slice(rb * ROPE_ROWS, (rb + 1) * ROPE_ROWS)
        rope = _rope_rows(ta_ref, ROT_DIM // 2, rows)
        for s in range(nslab):
            sl = slice(s * LANES, (s + 1) * LANES)
            qs = rope(q[rows, sl]) * scale
            ks = rope(k[rows, sl])
            nat_s[s, rows, :] = qs
            natk_s[s, rows, :] = ks
    _emit_dilated(nat_s, x4_s, q4_ref, q16_ref)
    _emit_dilated(natk_s, x4_s, k4_ref, k16_ref)

    v = _dot(u, w_ref[:, 2 * ATT_WIDTH:3 * ATT_WIDTH])
    for s in range(nslab):
        nat_s[s] = v[:, s * LANES:(s + 1) * LANES]
    _emit_dilated(nat_s, x4_s, v4_ref, v16_ref)

    o = 3 * ATT_WIDTH
    qk = _dot(u, w_ref[:, o:o + 2 * RET_QK])
    for rb in range(h_ref.shape[0] // ROPE_ROWS):
        rows = slice(rb * ROPE_ROWS, (rb + 1) * ROPE_ROWS)
        rope = _rope_rows(tr_ref, RET_DK // 2, rows)
        for s in range(RET_QK // LANES):
            sl = slice(s * LANES, (s + 1) * LANES)
            sk = slice(RET_QK + s * LANES, RET_QK + (s + 1) * LANES)
            qr_ref[rows, sl] = rope(qk[rows, sl]).astype(qr_ref.dtype)
            kr_ref[rows, sl] = (rope(qk[rows, sk]) * (RET_DK ** -0.5)).astype(kr_ref.dtype)
    o += 2 * RET_QK
    vr_ref[...] = _dot(u, w_ref[:, o:o + RET_V]).astype(vr_ref.dtype)
    o += RET_V
    gr_ref[...] = _dot(u, w_ref[:, o:o + RET_V])
    _ret_body(qr_ref, kr_ref, vr_ref, gr_ref, gn_ref, y_ref, st_ref, dec_s, wt_s, wh_s, st_s,
              qr_ref.shape[0] // RET_CHUNK)

    @pl.when(pl.program_id(0) >= first_keep)
    def _():
        for s in range(nslab):
            kk_ref[:, s * LANES:(s + 1) * LANES] = natk_s[s]
            vk_ref[:, s * LANES:(s + 1) * LANES] = nat_s[s]


def _mixin(h, g, w, base_pos, off_pos, keep_rows, tm, gn):
    t = h.shape[0]
    nt = t // tm
    assert tm % RUN == 0 and SUPER % tm == 0 and t % SUPER == 0 and keep_rows % tm == 0
    first_keep = nt - keep_rows // tm
    base_a, off_a = _rope_consts(base_pos, off_pos, ROT_DIM, ROPE_THETA, HEAD_DIM)
    base_r, off_r = _rope_consts(base_pos, off_pos, RET_DK, RET_THETA, RET_DK)

    def row(w_):
        return pl.BlockSpec((tm, w_), lambda i: (i, 0))

    keep_spec = pl.BlockSpec((tm, ATT_WIDTH), lambda i: (jnp.maximum(i - first_keep, 0), 0))
    keep_shape = ((keep_rows, ATT_WIDTH), F32)
    base_spec = pl.BlockSpec((None, 2, LANES), lambda i: (i, 0, 0))
    off_spec = _resident_spec((7, tm, LANES))
    tab = pltpu.VMEM((3, tm, LANES), F32)
    in_specs = [row(D_MODEL), _full_spec((1, D_MODEL)), _resident_spec((D_MODEL, IN_WIDTH)),
                base_spec, off_spec, base_spec, off_spec, _full_spec((1, RET_V))]
    tps = SUPER // tm
    s4 = pl.BlockSpec((tm // RUN, RUN // 4, 4 * ATT_WIDTH), lambda i: (i, 0, 0))
    s16 = pl.BlockSpec((None, tm // 16, 16 * ATT_WIDTH), lambda i: (i // tps, i % tps, 0))
    st_shape = (N_RET_HEADS // 2, 2 * RET_DK, RET_DV)
    specs = [s4] * 3 + [s16] * 3 + [keep_spec, keep_spec, row(RET_V), _full_spec(st_shape)]
    shapes = ([((t // RUN, RUN // 4, 4 * ATT_WIDTH), BF16)] * 3
              + [((t // SUPER, SUPER // 16, 16 * ATT_WIDTH), BF16)] * 3
              + [keep_shape, keep_shape, ((t, RET_V), BF16), (st_shape, F32)])
    dec = pltpu.VMEM((N_RET_HEADS, RET_CHUNK, RET_CHUNK), F32)
    nat = pltpu.VMEM((ATT_WIDTH // LANES, tm, LANES), F32)
    scratch = [tab, tab, nat, nat, pltpu.VMEM((4, RUN // 4, LANES), F32),
               pltpu.VMEM((tm, RET_QK), BF16), pltpu.VMEM((tm, RET_QK), BF16),
               pltpu.VMEM((tm, RET_V), BF16), pltpu.VMEM((tm, RET_V), F32),
               dec, dec, dec, pltpu.VMEM(st_shape, F32)]
    return pl.pallas_call(
        functools.partial(_mixin_kernel, first_keep=first_keep), grid=(nt,),
        in_specs=in_specs, out_specs=specs,
        out_shape=[jax.ShapeDtypeStruct(s, d) for s, d in shapes],
        scratch_shapes=scratch, compiler_params=_params(1, MIXIN_VMEM_LIMIT), name="mix_in")(
            h, g, w, base_a, off_a, base_r, off_r, gn)


def _attn_bias_tables():
    a = np.arange(QB)[:, None]
    c = np.arange(2 * QB)[None, :]
    steps = BRANCHES[0][0]
    dist = QB + a - c
    band = (dist >= 0) & (dist <= steps)
    cur = c >= QB
    tok_q = 4 * (a % 32) + a // 32
    cc = c % QB
    tok_k = 4 * (cc % 32) + cc // 32 + QB * (c // QB) - QB
    dist2 = tok_q - tok_k
    band2 = (dist2 >= 0) & (dist2 <= steps)
    masks = np.stack([band, band & cur, band2, band2 & cur])
    return np.where(masks, 0.0, NEG).astype(np.float32)


def _attn_qblock(get_q, get_kc, get_kp, get_vc, get_vp, bias, consts, sink):
    head_a, head_a_win, ones_a, ones_b = consts
    for s in range(ATT_WIDTH // LANES):
        q2 = get_q(s)
        kwin = jnp.concatenate([get_kp(s), get_kc(s)], axis=0)
        vwin = jnp.concatenate([get_vp(s), get_vc(s)], axis=0)
        zq = jnp.zeros_like(q2)
        qq = jnp.concatenate([jnp.where(head_a, q2, zq), jnp.where(head_a, zq, q2)], axis=0)
        sc = _dot_nt(qq, kwin)
        s_a = sc[:QB] + bias
        s_b = sc[QB:] + bias
        m_a = jnp.max(s_a, axis=-1, keepdims=True)
        m_b = jnp.max(s_b, axis=-1, keepdims=True)
        p = jnp.concatenate([jnp.exp2(s_a - m_a).astype(BF16), jnp.exp2(s_b - m_b).astype(BF16)],
                            axis=1)
        zv = jnp.zeros_like(vwin)
        w = jnp.concatenate(
            [jnp.concatenate([jnp.where(head_a_win, vwin, zv), ones_a], axis=1),
             jnp.concatenate([jnp.where(head_a_win, zv, vwin), ones_b], axis=1)], axis=0)
        r = _dot(p, w)
        sink(s, (jnp.where(head_a, m_a, m_b), r[:, LANES:], r[:, :LANES]))


def _merge(old, new):
    mo, lo, ao = old
    m2, l2, a2 = new
    mn = jnp.maximum(mo, m2)
    eo = jnp.exp2(mo - mn)
    e2 = jnp.exp2(m2 - mn)
    return mn, eo * lo + e2 * l2, eo * ao + e2 * a2


def _attn_kernel(q16, k16c, k16p, v16c, v16p, q4, k4c, k4p, v4c, v4p, bias_ref, o_ref,
                 m_s, l_s, a_s, nat_s):
    sb = pl.program_id(0)
    ph = pl.program_id(1)
    j = pl.program_id(2)
    lane = lax.broadcasted_iota(jnp.int32, (QB, LANES), 1)
    head_a = lane < HEAD_DIM
    lane_w = lax.broadcasted_iota(jnp.int32, (2 * QB, LANES), 1)
    head_a_win = lane_w < HEAD_DIM
    ones_a = jnp.where(head_a_win, 1.0, 0.0).astype(BF16)
    ones_b = jnp.where(head_a_win, 0.0, 1.0).astype(BF16)
    consts = (head_a, head_a_win, ones_a, ones_b)
    nslab = ATT_WIDTH // LANES
    sub = QB // 4
    lanes = _lanes

    run0 = j * ATT_RUNS
    first_run = sb * (SUPER // RUN) + run0 == 0

    @pl.when(ph == 0)
    def _():
        bias = bias_ref[(sb == 0).astype(jnp.int32)]
        for r in range(4 * ATT_RUNS):
            def scatter(s, res):
                for u in range(SUPER // RUN):
                    rows = pl.ds(u * RUN + (r % 4) * QB + run0 + r // 4, sub, stride=4)
                    for ref, val in zip((m_s, l_s, a_s), res):
                        ref[s, rows, :] = val[u * sub:(u + 1) * sub]

            _attn_qblock(lambda s: q16[:, lanes(r, s)], lambda s: k16c[:, lanes(r, s)],
                         lambda s: k16p[:, lanes(r, s)], lambda s: v16c[:, lanes(r, s)],
                         lambda s: v16p[:, lanes(r, s)], bias, consts, scatter)

    def prev_rows(ref_c, ref_p, w, rows):
        return (lambda cols: ref_p[rows, cols]) if w == 0 else (lambda cols: ref_c[w - 1, rows, cols])

    @pl.when(ph == 1)
    def _():
        for w in range(ATT_RUNS):
            bias = bias_ref[first_run.astype(jnp.int32)] if w == 0 else bias_ref[0]
            kp = prev_rows(k4c, k4p, w, slice(None))
            vp = prev_rows(v4c, v4p, w, slice(None))
            for r in range(4):
                rows = pl.ds(pl.multiple_of((run0 + w) * RUN + r * QB, QB), QB)

                def merge_in(s, res):
                    mn, ln, an = _merge((m_s[s, rows, :], l_s[s, rows, :], a_s[s, rows, :]), res)
                    m_s[s, rows, :] = mn
                    l_s[s, rows, :] = ln
                    a_s[s, rows, :] = an

                _attn_qblock(lambda s: q4[w, :, lanes(r, s)], lambda s: k4c[w, :, lanes(r, s)],
                             lambda s: kp(lanes(r, s)), lambda s: v4c[w, :, lanes(r, s)],
                             lambda s: vp(lanes(r, s)), bias, consts, merge_in)

    @pl.when(ph == 2)
    def _():
        for w in range(ATT_RUNS):
            for b in range(4):
                def cur(ref):
                    return lambda s: jnp.concatenate(
                        [ref[w, b * sub:(b + 1) * sub, lanes(r, s)] for r in range(4)], axis=0)

                def prev(ref_c, ref_p):
                    if b > 0:
                        return lambda s: jnp.concatenate(
                            [ref_c[w, (b - 1) * sub:b * sub, lanes(r, s)] for r in range(4)], axis=0)
                    tail = prev_rows(ref_c, ref_p, w, slice(QB - sub, QB))
                    return lambda s: jnp.concatenate([tail(lanes(r, s)) for r in range(4)], axis=0)

                if w == 0 and b == 0:
                    bias = bias_ref[2 + first_run.astype(jnp.int32)]
                else:
                    bias = bias_ref[2]

                def finish(s, res):
                    for r in range(4):
                        rows = pl.ds(pl.multiple_of((run0 + w) * RUN + r * QB + b * sub, sub), sub)
                        part = tuple(x[r * sub:(r + 1) * sub] for x in res)
                        _, ln, an = _merge((m_s[s, rows, :], l_s[s, rows, :], a_s[s, rows, :]), part)
                        nat_s[s, pl.ds(w * RUN + b * QB + r, sub, stride=4), :] = an / ln

                _attn_qblock(cur(q4), cur(k4c), prev(k4c, k4p), cur(v4c), prev(v4c, v4p),
                             bias, consts, finish)
        for s in range(nslab):
            o_ref[:, s * LANES:(s + 1) * LANES] = nat_s[s].astype(o_ref.dtype)


def _attn_prompt(q4, k4, v4, q16, k16, v16):
    nsb = q16.shape[0]
    s_len = nsb * SUPER
    steps = SUPER // RUN // ATT_RUNS
    wide = 4 * ATT_WIDTH
    blk16 = (None, QB, ATT_RUNS * wide)
    blk4 = (ATT_RUNS, QB, wide)
    blk4p = (None, QB, wide)

    def j16(ph, j):
        return jnp.where(ph == 0, j, steps - 1)

    def n4(sb, ph, j):
        return sb * steps + jnp.where(ph == 0, 0, j)

    c16 = pl.BlockSpec(blk16, lambda sb, ph, j: (sb, 0, j16(ph, j)))
    p16 = pl.BlockSpec(blk16, lambda sb, ph, j: (jnp.maximum(sb - 1, 0), 0, j16(ph, j)))
    c4 = pl.BlockSpec(blk4, lambda sb, ph, j: (n4(sb, ph, j), 0, 0))
    p4 = pl.BlockSpec(blk4p, lambda sb, ph, j: (jnp.maximum(ATT_RUNS * n4(sb, ph, j) - 1, 0), 0, 0))
    out = pl.BlockSpec((ATT_RUNS * RUN, ATT_WIDTH),
                       lambda sb, ph, j: (sb * steps + jnp.where(ph == 2, j, 0), 0))
    bias = jnp.asarray(_attn_bias_tables())
    nslab = ATT_WIDTH // LANES
    scratch = ([pltpu.VMEM((nslab, SUPER, LANES), F32) for _ in range(3)]
               + [pltpu.VMEM((nslab, ATT_RUNS * RUN, LANES), F32)])
    return pl.pallas_call(
        _attn_kernel, grid=(nsb, 3, steps),
        in_specs=[c16, c16, p16, c16, p16, c4, c4, p4, c4, p4, _full_spec(bias.shape)],
        out_specs=out, out_shape=jax.ShapeDtypeStruct((s_len, ATT_WIDTH), BF16),
        scratch_shapes=scratch, compiler_params=_params(3), name="attn_prompt")(
            q16, k16, k16, v16, v16, q4, k4, k4, v4, v4, bias)


def _ret_init(dec_s, wt_s, wh_s, st_s):
    c = RET_CHUNK

    @pl.when(pl.program_id(0) == 0)
    def _():
        i = lax.broadcasted_iota(jnp.int32, (c, c), 0)
        jj = lax.broadcasted_iota(jnp.int32, (c, c), 1)
        diff = (i - jj).astype(F32)
        for h in range(N_RET_HEADS):
            dec_s[h] = jnp.where(diff >= 0, jnp.exp(diff * LOG_G[h]), 0.0)
            wt_s[h] = jnp.exp((c - 1.0 - i.astype(F32)) * LOG_G[h])
            wh_s[h] = jnp.exp((i.astype(F32) + 1.0) * LOG_G[h])
        st_s[...] = jnp.zeros_like(st_s)


def _ret_body(qr_ref, kr_ref, vr_ref, gr_ref, gn_ref, y_ref, s_ref, dec_s, wt_s, wh_s, st_s, chunks):
    c = RET_CHUNK
    lane = lax.broadcasted_iota(jnp.int32, (c, LANES), 1)
    head_a = lane < RET_DK
    row_a = lax.broadcasted_iota(jnp.int32, (LANES, RET_DV), 0) < RET_DK
    for ci in range(chunks):
        rows = slice(ci * c, (ci + 1) * c)
        for p in range(N_RET_HEADS // 2):
            sl = slice(p * LANES, (p + 1) * LANES)
            q2 = qr_ref[rows, sl]
            k2 = kr_ref[rows, sl]
            zq = jnp.zeros_like(q2)
            qq = jnp.concatenate([jnp.where(head_a, q2, zq), jnp.where(head_a, zq, q2)], axis=0)
            inner = _dot_nt(qq, k2)
            state = st_s[p]
            cross = _dot(qq, state.astype(BF16))
            vws = []
            for hh in range(2):
                h = 2 * p + hh
                hs = slice(h * RET_DV, (h + 1) * RET_DV)
                v = vr_ref[rows, hs]
                inn = (inner[hh * c:(hh + 1) * c] * dec_s[h]).astype(BF16)
                o = _dot(inn, v) + cross[hh * c:(hh + 1) * c] * wh_s[h]
                mu = jnp.mean(o, axis=-1, keepdims=True)
                xc = o - mu
                var = jnp.mean(xc * xc, axis=-1, keepdims=True)
                y = xc * lax.rsqrt(var + NORM_EPS) * gn_ref[:, hs]
                y_ref[rows, hs] = (_silu(gr_ref[rows, hs]) * y).astype(y_ref.dtype)
                vws.append((v.astype(F32) * wt_s[h]).astype(BF16))
            upd = _dot_tn(k2, jnp.concatenate(vws, axis=1))
            upd = jnp.where(row_a, upd[:, :RET_DV], upd[:, RET_DV:])
            gch = jnp.where(row_a, math.exp(c * LOG_G[2 * p]), math.exp(c * LOG_G[2 * p + 1]))
            st_s[p] = gch * state + upd

    @pl.when(pl.program_id(0) == pl.num_programs(0) - 1)
    def _():
        s_ref[...] = st_s[...]


S_ROWS = 8
S_HGRP = 4


def _sample_tables(t_new, past_len, wb):
    rows = np.concatenate([np.arange(wb), wb + np.arange(S_ROWS)])
    tab = np.zeros((S_ROWS, wb + S_ROWS), np.float32)
    for r in range(S_ROWS):
        i = r % t_new
        delta = wb + i - rows
        ok = (delta >= 0) & (past_len + i - delta >= 0) & (rows < wb + t_new)
        for window, dil in BRANCHES:
            tab[r] += ok & (delta % dil == 0) & (delta <= window)
    tab = np.tile(tab, (S_HGRP, 1))
    return tab[:, :wb], tab[:, wb:]


def _sample_attn_body(q_ref, kn_ref, vn_ref, kt_ref, vt_ref, cc_ref, cn_ref, o_ref):
    mult_c, mult_n = cc_ref[...], cn_ref[...]
    bias_c = jnp.where(mult_c > 0, 0.0, NEG)
    bias_n = jnp.where(mult_n > 0, 0.0, NEG)
    gw = S_HGRP * HEAD_DIM
    lane_head = lax.broadcasted_iota(jnp.int32, (S_ROWS, gw), 1) // HEAD_DIM
    for g in range(N_ATT_HEADS // S_HGRP):
        gs = slice(g * gw, (g + 1) * gw)
        q8 = q_ref[:, gs]
        q = jnp.concatenate([jnp.where(lane_head == h, q8, 0.0) for h in range(S_HGRP)],
                            axis=0).astype(BF16)
        s_c = _dot(q, kt_ref[g].astype(BF16)) + bias_c
        s_n = _dot_nt(q, kn_ref[:, gs].astype(BF16)) + bias_n
        m = jnp.maximum(jnp.max(s_c, axis=-1, keepdims=True), jnp.max(s_n, axis=-1, keepdims=True))
        p_c = jnp.exp2(s_c - m) * mult_c
        p_n = jnp.exp2(s_n - m) * mult_n
        l = jnp.sum(p_c, axis=-1, keepdims=True) + jnp.sum(p_n, axis=-1, keepdims=True)
        o = (_dot_nt(p_c.astype(BF16), vt_ref[g].astype(BF16))
             + _dot(p_n.astype(BF16), vn_ref[:, gs].astype(BF16))) / l
        o8 = jnp.zeros((S_ROWS, gw), F32)
        for h in range(S_HGRP):
            o8 = jnp.where(lane_head == h, o[h * S_ROWS:(h + 1) * S_ROWS], o8)
        o_ref[:, gs] = o8


FF_SPLITS = ((0, 768), (768, 1536), (1536, 2304), (2304, D_FF))


def _ffn_attn_kernel(x_ref, gpre_ref, gpost_ref, wg_ref, wu_ref, wd_ref,
                     q_ref, kn_ref, vn_ref, kt_ref, vt_ref, cc_ref, cn_ref,
                     o_ref, oa_ref, u_s, acc_s):
    j = pl.program_id(1)
    last = len(FF_SPLITS) - 1
    for k, (c0, c1) in enumerate(FF_SPLITS):
        @pl.when(j == k)
        def _():
            if k == 0:
                u = _rms(x_ref[...], gpre_ref[...]).astype(BF16)
                u_s[...] = u
            else:
                u = u_s[...]
            h = (_silu(_dot(u, wg_ref[:, c0:c1])) * _dot(u, wu_ref[:, c0:c1])).astype(BF16)
            d = _dot(h, wd_ref[c0:c1, :])
            if k == 0:
                acc_s[...] = d
            elif k < last:
                acc_s[...] += d
            else:
                o_ref[...] = x_ref[...] + 0.5 * _rms(acc_s[...] + d, gpost_ref[...])
            _sample_attn_body(q_ref, kn_ref, vn_ref, kt_ref, vt_ref, cc_ref, cn_ref, oa_ref)


def _ffn_with_sample_attn(x, gpre, gpost, wg, wu, wd, q, kn, vn, cache_k, cache_v, t_new, past_len, tm):
    t = x.shape[0]
    nb, wb = cache_k.shape[0], cache_k.shape[1]
    nsplit = len(FF_SPLITS)
    assert t_new <= S_ROWS and nb == (t // tm) * nsplit

    def rows(a):
        return jnp.pad(a.reshape(nb, t_new, ATT_WIDTH), ((0, 0), (0, S_ROWS - t_new), (0, 0)))

    def grouped(c):
        return c.transpose(0, 2, 3, 1).reshape(nb, N_ATT_HEADS // S_HGRP, S_HGRP * HEAD_DIM, wb)

    row = pl.BlockSpec((tm, D_MODEL), lambda i, j: (i, 0))
    small = pl.BlockSpec((None, S_ROWS, ATT_WIDTH), lambda i, j: (i * nsplit + j, 0, 0))
    big = pl.BlockSpec((None, N_ATT_HEADS // S_HGRP, S_HGRP * HEAD_DIM, wb),
                       lambda i, j: (i * nsplit + j, 0, 0, 0))
    tabs = [jnp.asarray(a) for a in _sample_tables(t_new, past_len, wb)]
    h1, o = pl.pallas_call(
        _ffn_attn_kernel, grid=(t // tm, nsplit),
        in_specs=[row] + _ffn_specs() + [small, small, small, big, big]
        + [_full_spec(a.shape) for a in tabs],
        out_specs=[row, small],
        out_shape=[jax.ShapeDtypeStruct((t, D_MODEL), F32),
                   jax.ShapeDtypeStruct((nb, S_ROWS, ATT_WIDTH), F32)],
        scratch_shapes=[pltpu.VMEM((tm, D_MODEL), BF16), pltpu.VMEM((tm, D_MODEL), F32)],
        compiler_params=_params(2), name="ffn_attn_sample")(
            x, gpre, gpost, wg, wu, wd, rows(q), rows(kn), rows(vn),
            grouped(cache_k), grouped(cache_v), *tabs)
    return h1, o[:, :t_new].reshape(nb * t_new, ATT_WIDTH)


R_PAIR = 2


def _sample_ret_tables(t_new):
    r = np.arange(R_PAIR * t_new)
    seq, step = r // t_new, r % t_new
    lg = np.asarray(LOG_G)[:, None, None]
    diff = (step[:, None] - step[None, :])[None]
    ok = ((seq[:, None] == seq[None, :]) & (diff[0] >= 0))[None]
    decay = np.where(ok, np.exp(diff * lg), 0.0).reshape(N_RET_HEADS * r.size, r.size)
    w_head = np.exp((step[None, :] + 1.0) * lg[:, :, 0]).reshape(-1, 1) * np.ones((1, RET_DV))
    w_tail = np.exp((t_new - 1.0 - step)[:, None, None] * lg[None, :, :, 0]) * np.ones((1, 1, RET_DV))
    g_chunk = np.repeat(np.exp(t_new * lg[:, 0, 0]), RET_DK)[:, None] * np.ones((1, RET_DV))
    return [a.astype(np.float32) for a in (decay, w_head, w_tail.reshape(r.size, -1), g_chunk)]


def _sample_ret_kernel(q_ref, k_ref, v_ref, g_ref, gn_ref, st_ref, dec_ref, wh_ref, wt_ref, gc_ref,
                       y_ref, so_ref, *, blocks, t_new):
    nrow = R_PAIR * t_new
    lane_head = lax.broadcasted_iota(jnp.int32, (nrow, RET_QK), 1) // RET_DK
    row_seq = lax.broadcasted_iota(jnp.int32, (nrow, RET_QK), 0) // t_new
    out_seq = (lax.broadcasted_iota(jnp.int32, (N_RET_HEADS * nrow, RET_DV), 0) % nrow) // t_new
    for blk in range(blocks):
        rs = slice(blk * nrow, (blk + 1) * nrow)
        q8, k8, v8 = q_ref[rs, :], k_ref[rs, :], v_ref[rs, :]
        qm = jnp.concatenate([jnp.where(lane_head == h, q8, 0.0) for h in range(N_RET_HEADS)],
                             axis=0).astype(BF16)
        inner = (_dot_nt(qm, k8.astype(BF16)) * dec_ref[...]).astype(BF16)
        o_all = _dot(inner, v8.astype(BF16))
        vw = (v8 * wt_ref[...]).astype(BF16)
        cross = None
        for s in range(R_PAIR):
            state = st_ref[blk * R_PAIR + s]
            c = _dot(qm, state.astype(BF16))
            cross = c if cross is None else jnp.where(out_seq == s, c, cross)
            km = jnp.where(row_seq == s, k8, 0.0).astype(BF16)
            upd = _dot_tn(km, vw)
            upd = jnp.concatenate([upd[h * RET_DK:(h + 1) * RET_DK, h * RET_DV:(h + 1) * RET_DV]
                                   for h in range(N_RET_HEADS)], axis=0)
            so_ref[blk * R_PAIR + s] = gc_ref[...] * state + upd
        cross = cross * wh_ref[...]
        for h in range(N_RET_HEADS):
            hs = slice(h * RET_DV, (h + 1) * RET_DV)
            oh = o_all[h * nrow:(h + 1) * nrow, hs] + cross[h * nrow:(h + 1) * nrow]
            mu = jnp.mean(oh, axis=-1, keepdims=True)
            xc = oh - mu
            var = jnp.mean(xc * xc, axis=-1, keepdims=True)
            y = xc * lax.rsqrt(var + NORM_EPS) * gn_ref[:, hs]
            y_ref[rs, hs] = _silu(g_ref[rs, hs]) * y


def _ret_sample(qr, kr, vr, gr, gn, state, t_new, blocks=8):
    nb = state.shape[0]
    assert R_PAIR * t_new == 8 and nb % (R_PAIR * blocks) == 0
    rows = R_PAIR * t_new * blocks
    tabs = [jnp.asarray(a) for a in _sample_ret_tables(t_new)]
    st = state.reshape(nb, N_RET_HEADS * RET_DK, RET_DV)

    def blk(w_):
        return pl.BlockSpec((rows, w_), lambda i: (i, 0))

    st_spec = pl.BlockSpec((R_PAIR * blocks, N_RET_HEADS * RET_DK, RET_DV), lambda i: (i, 0, 0))
    y, s = pl.pallas_call(
        functools.partial(_sample_ret_kernel, blocks=blocks, t_new=t_new),
        grid=(nb // (R_PAIR * blocks),),
        in_specs=[blk(RET_QK), blk(RET_QK), blk(RET_V), blk(RET_V), _full_spec((1, RET_V)), st_spec]
        + [_full_spec(a.shape) for a in tabs],
        out_specs=[blk(RET_V), st_spec],
        out_shape=[jax.ShapeDtypeStruct((nb * t_new, RET_V), F32), jax.ShapeDtypeStruct(st.shape, F32)],
        compiler_params=_params(1), name="ret_sample")(qr, kr, vr, gr, gn, st, *tabs)
    return y, s.reshape(state.shape)


CONV_COLS = 256


def _ffn_conv_kernel(*refs, with_mix):
    if with_mix:
        (oa_ref, yr_ref, h_ref, wo_ref, gmb_ref, gpre_ref, gpost_ref, wg_ref, wu_ref, wd_ref,
         o_ref, wob_ref, wgb_ref, wub_ref, wdb_ref, x_s, u_s, acc_s) = refs
    else:
        (h_ref, gpre_ref, gpost_ref, wg_ref, wu_ref, wd_ref,
         o_ref, wgb_ref, wub_ref, wdb_ref, x_s, u_s, acc_s) = refs
    c = pl.program_id(0)

    @pl.when(c == 0)
    def _():
        x = h_ref[...]
        if with_mix:
            wo = wo_ref[...].astype(BF16)
            wob_ref[...] = wo
            mixed = jnp.concatenate([oa_ref[...].astype(BF16), yr_ref[...].astype(BF16)], axis=1)
            x = x + _rms(_dot(mixed, wo), gmb_ref[...])
        x_s[...] = x
        u_s[...] = _rms(x, gpre_ref[...]).astype(BF16)
        acc_s[...] = jnp.zeros_like(acc_s)

    wg, wu, wd = wg_ref[...].astype(BF16), wu_ref[...].astype(BF16), wd_ref[...].astype(BF16)
    wgb_ref[...] = wg
    wub_ref[...] = wu
    wdb_ref[...] = wd
    u = u_s[...]
    acc_s[...] += _dot((_silu(_dot(u, wg)) * _dot(u, wu)).astype(BF16), wd)

    @pl.when(c == pl.num_programs(0) - 1)
    def _():
        o_ref[...] = x_s[...] + 0.5 * _rms(acc_s[...], gpost_ref[...])


def _ffn_convert(h, gpre, gpost, wg, wu, wd, mix=None):
    t = h.shape[0]
    col = pl.BlockSpec((D_MODEL, CONV_COLS), lambda c: (0, c))
    rowb = pl.BlockSpec((CONV_COLS, D_MODEL), lambda c: (c, 0))
    vec = _full_spec((1, D_MODEL))
    ins, in_specs, outs, out_specs = [], [], [], []
    if mix is not None:
        att, ret, wo, gmb = mix
        ins += [att, ret]
        in_specs += [_full_spec((t, ATT_WIDTH)), _full_spec((t, RET_V))]
    ins.append(h)
    in_specs.append(_full_spec((t, D_MODEL)))
    if mix is not None:
        ins += [wo, gmb]
        in_specs += [_full_spec((D_MODEL, D_MODEL)), vec]
    ins += [gpre, gpost, wg, wu, wd]
    in_specs += [vec, vec, col, col, rowb]
    out_shape = [jax.ShapeDtypeStruct((t, D_MODEL), F32)]
    out_specs = [_full_spec((t, D_MODEL))]
    if mix is not None:
        out_shape.append(jax.ShapeDtypeStruct((D_MODEL, D_MODEL), BF16))
        out_specs.append(_full_spec((D_MODEL, D_MODEL)))
    out_shape += [jax.ShapeDtypeStruct((D_MODEL, D_FF), BF16)] * 2 + [jax.ShapeDtypeStruct((D_FF, D_MODEL), BF16)]
    out_specs += [col, col, rowb]
    return pl.pallas_call(
        functools.partial(_ffn_conv_kernel, with_mix=mix is not None), grid=(D_FF // CONV_COLS,),
        in_specs=in_specs, out_specs=out_specs, out_shape=out_shape,
        scratch_shapes=[pltpu.VMEM((t, D_MODEL), F32), pltpu.VMEM((t, D_MODEL), BF16),
                        pltpu.VMEM((t, D_MODEL), F32)],
        compiler_params=_params(1), name="ffn_convert")(*ins)


def _mixin_sample_kernel(h_ref, g_ref, w_ref, ba_ref, oa_ref, br_ref, or_ref,
                         qa_ref, kk_ref, vk_ref, qr_ref, kr_ref, vr_ref, gr_ref, wb_ref,
                         ta_s, tr_s, u_s):
    c = pl.program_id(0)

    @pl.when(c == 0)
    def _():
        _rope_fill(ba_ref, oa_ref, ta_s)
        _rope_fill(br_ref, or_ref, tr_s)
        u_s[...] = _rms(h_ref[...], g_ref[...]).astype(BF16)

    w = w_ref[...].astype(BF16)
    wb_ref[...] = w
    y = _dot(u_s[...], w)
    nblk = h_ref.shape[0] // ROPE_ROWS

    def rotary(tab_s, half, cols, dst_ref, scale):
        for rb in range(nblk):
            rows = slice(rb * ROPE_ROWS, (rb + 1) * ROPE_ROWS)
            rope = _rope_rows(tab_s, half, rows)
            for s in range(dst_ref.shape[1] // LANES):
                dst_ref[rows, s * LANES:(s + 1) * LANES] = rope(
                    y[rows, cols + s * LANES:cols + (s + 1) * LANES]) * scale

    @pl.when(c == 0)
    def _():
        rotary(ta_s, ROT_DIM // 2, 0, qa_ref, HEAD_DIM ** -0.5 * LOG2_E)

    @pl.when(c == 1)
    def _():
        rotary(ta_s, ROT_DIM // 2, 0, kk_ref, 1.0)

    @pl.when(c == 2)
    def _():
        vk_ref[...] = y

    @pl.when(c == 3)
    def _():
        rotary(tr_s, RET_DK // 2, 0, qr_ref, 1.0)
        rotary(tr_s, RET_DK // 2, RET_QK, kr_ref, RET_DK ** -0.5)

    @pl.when(c == 4)
    def _():
        vr_ref[...] = y

    @pl.when(c == 5)
    def _():
        gr_ref[...] = y


def _mixin_sample(h, g, w, base_pos, off_pos):
    t = h.shape[0]
    grp = ATT_WIDTH
    assert IN_WIDTH == 6 * grp and 2 * RET_QK == grp and RET_V == grp
    base_a, off_a = _rope_consts(base_pos, off_pos, ROT_DIM, ROPE_THETA, HEAD_DIM)
    base_r, off_r = _rope_consts(base_pos, off_pos, RET_DK, RET_THETA, RET_DK)
    wcol = pl.BlockSpec((D_MODEL, grp), lambda c: (0, c))
    base_spec = pl.BlockSpec((None, 2, LANES), lambda c: (0, 0, 0))
    widths = [ATT_WIDTH, ATT_WIDTH, ATT_WIDTH, RET_QK, RET_QK, RET_V, RET_V]
    tab = pltpu.VMEM((3, t, LANES), F32)
    return pl.pallas_call(
        _mixin_sample_kernel, grid=(IN_WIDTH // grp,),
        in_specs=[_full_spec((t, D_MODEL)), _full_spec((1, D_MODEL)), wcol,
                  base_spec, _full_spec(off_a.shape), base_spec, _full_spec(off_r.shape)],
        out_specs=[_full_spec((t, w_)) for w_ in widths] + [wcol],
        out_shape=[jax.ShapeDtypeStruct((t, w_), F32) for w_ in widths]
        + [jax.ShapeDtypeStruct((D_MODEL, IN_WIDTH), BF16)],
        scratch_shapes=[tab, tab, pltpu.VMEM((t, D_MODEL), BF16)],
        compiler_params=_params(1), name="mix_in_sample")(h, g, w, base_a, off_a, base_r, off_r)


def kernel(x_prompt, x_sample, cache_k, cache_v, state_ret, g_ffn1_pre, g_ffn1_post, w1_gate, w1_up,
           w1_down, g_mix_pre, g_mix_post, w_in, gn_w, w_out, g_ffn2_pre, g_ffn2_post, w2_gate, w2_up,
           w2_down):
    b_p, s_p, _ = x_prompt.shape
    b_s, t_s, _ = x_sample.shape
    depth = w_in.shape[0]
    assert depth == 1 and b_p == 1
    keep = min(WIN_MAX, s_p)
    l = 0
    vec = lambda g: g[l].reshape(1, -1)
    g1a, g1b, gma, gmb = vec(g_ffn1_pre), vec(g_ffn1_post), vec(g_mix_pre), vec(g_mix_post)
    g2a, g2b, gn = vec(g_ffn2_pre), vec(g_ffn2_post), vec(gn_w)

    n_s = b_s * t_s
    xs = x_sample.reshape(n_s, D_MODEL)
    h1s, *w1 = _ffn_convert(xs, g1a, g1b, w1_gate[l], w1_up[l], w1_down[l])
    qas, kks, vks, qrs, krs, vrs, grs, wi = _mixin_sample(
        h1s, gma, w_in[l], np.full((1,), PAST_LEN), np.tile(np.arange(t_s), b_s))

    tm = RUN
    xp = x_prompt.reshape(s_p, D_MODEL)
    h1, o_att_s = _ffn_with_sample_attn(xp, g1a, g1b, *w1, qas, kks, vks, cache_k[l], cache_v[l],
                                        t_s, PAST_LEN, tm)
    q4, k4, v4, q16, k16, v16, kk, vk, y_ret, st_p = _mixin(
        h1, gma, wi, 2 * tm * np.arange(s_p // (2 * tm)), np.arange(2 * tm), keep, 2 * tm, gn)
    o_att = _attn_prompt(q4, k4, v4, q16, k16, v16)

    y_ret_s, st_s = _ret_sample(qrs, krs, vrs, grs, gn, state_ret[l], t_s)
    y_sample, wo, *w2 = _ffn_convert(h1s, g2a, g2b, w2_gate[l], w2_up[l], w2_down[l],
                                     mix=(o_att_s, y_ret_s, w_out[l], gmb))
    y_prompt = _mixout_ffn(o_att, y_ret, h1, wo, gmb, g2a, g2b, *w2, 2 * tm)

    hd = (N_ATT_HEADS, HEAD_DIM)
    return (y_prompt.reshape(b_p, s_p, D_MODEL),
            y_sample.reshape(b_s, t_s, D_MODEL),
            kk.reshape(depth, b_p, keep, *hd),
            vk.reshape(depth, b_p, keep, *hd),
            st_p.reshape(depth, b_p, N_RET_HEADS, RET_DK, RET_DV),
            kks.reshape(depth, b_s, t_s, *hd),
            vks.reshape(depth, b_s, t_s, *hd),
            st_s.reshape(depth, b_s, N_RET_HEADS, RET_DK, RET_DV))
```

```python
import functools
import math

import numpy as np
import jax
import jax.numpy as jnp
from jax import lax
from jax.experimental import pallas as pl
from jax.experimental.pallas import tpu as pltpu

F32 = jnp.float32
BF16 = jnp.bfloat16

D_MODEL = 1024
D_FF = 2816
HEAD_DIM = 64
N_ATT_HEADS = 8
ATT_WIDTH = N_ATT_HEADS * HEAD_DIM
ROT_DIM = HEAD_DIM // 4
ROPE_THETA = 500000.0
WIN_MAX = 2048
PAST_LEN = 8192
BRANCHES = ((128, 1), (512, 4), (2048, 16))
N_RET_HEADS = 4
RET_DK = 64
RET_DV = 128
RET_QK = N_RET_HEADS * RET_DK
RET_V = N_RET_HEADS * RET_DV
RET_THETA = 10000.0
RET_CHUNK = 128
IN_WIDTH = 3 * ATT_WIDTH + 2 * RET_QK + 2 * RET_V
NORM_EPS = 1e-6
NEG = -1e30

LANES = 128
QB = 128
SUPER = 2048
RUN = 512
ATT_RUNS = 4
VMEM_LIMIT = 56 * 1024 * 1024
MIXIN_VMEM_LIMIT = 60 * 1024 * 1024
LOG_G = tuple(math.log1p(-2.0 ** (-5.0 - h)) for h in range(N_RET_HEADS))
LOG2_E = math.log2(math.e)


def _full_spec(shape):
    nd = len(shape)
    return pl.BlockSpec(shape, lambda *_: (0,) * nd)


def _resident_spec(shape):
    nd = len(shape)
    return pl.BlockSpec(shape, lambda *_: (0,) * nd, pipeline_mode=pl.Buffered(1))


def _params(n_axes, vmem=VMEM_LIMIT):
    return pltpu.CompilerParams(dimension_semantics=("arbitrary",) * n_axes, vmem_limit_bytes=vmem)


def _rms(x, g):
    return x * lax.rsqrt(jnp.mean(x * x, axis=-1, keepdims=True) + NORM_EPS) * g


def _silu(x):
    return x / (1.0 + jnp.exp(-x))


def _dot(a, b):
    return jnp.dot(a, b, preferred_element_type=F32)


def _dot_nt(a, b):
    return lax.dot_general(a, b, (((1,), (1,)), ((), ())), preferred_element_type=F32)


def _dot_tn(a, b):
    return lax.dot_general(a, b, (((0,), (0,)), ((), ())), preferred_element_type=F32)


FF_CHUNK = 512


def _swiglu(u, wg_ref, wu_ref, wd_ref):
    acc = None
    for c0 in range(0, D_FF, FF_CHUNK):
        c1 = min(c0 + FF_CHUNK, D_FF)
        g = _dot(u, wg_ref[:, c0:c1])
        up = _dot(u, wu_ref[:, c0:c1])
        h = (_silu(g) * up).astype(BF16)
        d = _dot(h, wd_ref[c0:c1, :])
        acc = d if acc is None else acc + d
    return acc


def _mixout_ffn_kernel(oa_ref, yr_ref, h_ref, wo_ref, gmb_ref, gpre_ref, gpost_ref,
                       wg_ref, wu_ref, wd_ref, o_ref):
    mixed = jnp.concatenate([oa_ref[...].astype(BF16), yr_ref[...].astype(BF16)], axis=1)
    x = h_ref[...] + _rms(_dot(mixed, wo_ref[...]), gmb_ref[...])
    u = _rms(x, gpre_ref[...]).astype(BF16)
    y = _swiglu(u, wg_ref, wu_ref, wd_ref)
    o_ref[...] = x + 0.5 * _rms(y, gpost_ref[...])


def _ffn_specs():
    return [_full_spec((1, D_MODEL)), _full_spec((1, D_MODEL)),
            _resident_spec((D_MODEL, D_FF)), _resident_spec((D_MODEL, D_FF)),
            _resident_spec((D_FF, D_MODEL))]


def _mixout_ffn(oa, yr, h, wo, gmb, gpre, gpost, wg, wu, wd, tm):
    t = h.shape[0]
    row = pl.BlockSpec((tm, D_MODEL), lambda i: (i, 0))
    half = pl.BlockSpec((tm, ATT_WIDTH), lambda i: (i, 0))
    return pl.pallas_call(
        _mixout_ffn_kernel, grid=(t // tm,),
        in_specs=[half, half, row, _resident_spec((D_MODEL, D_MODEL)), _full_spec((1, D_MODEL))]
        + _ffn_specs(),
        out_specs=row, out_shape=jax.ShapeDtypeStruct((t, D_MODEL), F32),
        compiler_params=_params(1), name="mixout_ffn")(oa, yr, h, wo, gmb, gpre, gpost, wg, wu, wd)


def _rope_consts(base_pos, off_pos, rot_dim, theta, head_dim):
    half = rot_dim // 2
    inv = theta ** (-np.arange(half, dtype=np.float64) * (2.0 / rot_dim))
    lane = np.arange(LANES) % head_dim
    inv_l = inv[lane % half][None, :]
    first = (lane < half).astype(np.float64)[None, :]
    second = ((lane >= half) & (lane < rot_dim)).astype(np.float64)[None, :]
    rot = first + second
    a = np.asarray(base_pos, np.float64)[:, None] * inv_l
    b = np.asarray(off_pos, np.float64)[:, None] * inv_l
    cb, sb = np.cos(b), np.sin(b)
    base = np.stack([np.cos(a), np.sin(a)], axis=1)
    off = np.stack([cb * rot, sb * rot, np.broadcast_to(1.0 - rot, cb.shape),
                    -cb * first, -sb * first, cb * second, sb * second])
    return jnp.asarray(base, F32), jnp.asarray(off, F32)


def _rope_fill(base_ref, off_ref, tab_s):
    ca = base_ref[0:1, :]
    sa = base_ref[1:2, :]
    tab_s[0] = ca * off_ref[0] - sa * off_ref[1] + off_ref[2]
    tab_s[1] = sa * off_ref[3] + ca * off_ref[4]
    tab_s[2] = sa * off_ref[5] + ca * off_ref[6]


ROPE_ROWS = 64


def _rope_rows(tab_ref, half, rows):
    c, sa, sb = tab_ref[0, rows, :], tab_ref[1, rows, :], tab_ref[2, rows, :]
    return lambda x: x * c + pltpu.roll(x, LANES - half, 1) * sa + pltpu.roll(x, half, 1) * sb


def _lanes(r, s):
    return slice(r * ATT_WIDTH + s * LANES, r * ATT_WIDTH + (s + 1) * LANES)


def _emit_dilated(nat_s, x4_s, o4_ref, o16_ref):
    l16 = RUN // 16
    for w in range(nat_s.shape[1] // RUN):
        for s in range(ATT_WIDTH // LANES):
            for r in range(4):
                x4 = nat_s[s, pl.ds(w * RUN + r, RUN // 4, stride=4), :]
                o4_ref[w, :, _lanes(r, s)] = x4.astype(BF16)
                x4_s[r] = x4
            for r in range(4):
                for c in range(4):
                    o16_ref[w * l16:(w + 1) * l16, _lanes(4 * c + r, s)] = (
                        x4_s[r, pl.ds(c, l16, stride=4), :].astype(BF16))


def _mixin_kernel(h_ref, g_ref, w_ref, ba_ref, oa_ref, br_ref, or_ref, gn_ref,
                  q4_ref, k4_ref, v4_ref, q16_ref, k16_ref, v16_ref, kk_ref, vk_ref, y_ref, st_ref,
                  ta_ref, tr_ref, nat_s, natk_s, x4_s, qr_ref, kr_ref, vr_ref, gr_ref,
                  dec_s, wt_s, wh_s, st_s, *, first_keep):
    _ret_init(dec_s, wt_s, wh_s, st_s)
    _rope_fill(ba_ref, oa_ref, ta_ref)
    _rope_fill(br_ref, or_ref, tr_ref)
    u = _rms(h_ref[...], g_ref[...]).astype(BF16)
    scale = HEAD_DIM ** -0.5 * LOG2_E
    nslab = ATT_WIDTH // LANES

    q = _dot(u, w_ref[:, 0:ATT_WIDTH])
    k = _dot(u, w_ref[:, ATT_WIDTH:2 * ATT_WIDTH])
    for rb in range(h_ref.shape[0] // ROPE_ROWS):
        rows = slice(rb * ROPE_ROWS, (rb + 1) * ROPE_ROWS)
        rope = _rope_rows(ta_ref, ROT_DIM // 2, rows)
        for s in range(nslab):
            sl = slice(s * LANES, (s + 1) * LANES)
            qs = rope(q[rows, sl]) * scale
            ks = rope(k[rows, sl])
            nat_s[s, rows, :] = qs
            natk_s[s, rows, :] = ks
    _emit_dilated(nat_s, x4_s, q4_ref, q16_ref)
    _emit_dilated(natk_s, x4_s, k4_ref, k16_ref)

    v = _dot(u, w_ref[:, 2 * ATT_WIDTH:3 * ATT_WIDTH])
    for s in range(nslab):
        nat_s[s] = v[:, s * LANES:(s + 1) * LANES]
    _emit_dilated(nat_s, x4_s, v4_ref, v16_ref)

    o = 3 * ATT_WIDTH
    qk = _dot(u, w_ref[:, o:o + 2 * RET_QK])
    for rb in range(h_ref.shape[0] // ROPE_ROWS):
        rows = slice(rb * ROPE_ROWS, (rb + 1) * ROPE_ROWS)
        rope = _rope_rows(tr_ref, RET_DK // 2, rows)
        for s in range(RET_QK // LANES):
            sl = slice(s * LANES, (s + 1) * LANES)
            sk = slice(RET_QK + s * LANES, RET_QK + (s + 1) * LANES)
            qr_ref[rows, sl] = rope(qk[rows, sl]).astype(qr_ref.dtype)
            kr_ref[rows, sl] = (rope(qk[rows, sk]) * (RET_DK ** -0.5)).astype(kr_ref.dtype)
    o += 2 * RET_QK
    vr_ref[...] = _dot(u, w_ref[:, o:o + RET_V]).astype(vr_ref.dtype)
    o += RET_V
    gr_ref[...] = _dot(u, w_ref[:, o:o + RET_V])
    _ret_body(qr_ref, kr_ref, vr_ref, gr_ref, gn_ref, y_ref, st_ref, dec_s, wt_s, wh_s, st_s,
              qr_ref.shape[0] // RET_CHUNK)

    @pl.when(pl.program_id(0) >= first_keep)
    def _():
        for s in range(nslab):
            kk_ref[:, s * LANES:(s + 1) * LANES] = natk_s[s]
            vk_ref[:, s * LANES:(s + 1) * LANES] = nat_s[s]


def _mixin(h, g, w, base_pos, off_pos, keep_rows, tm, gn):
    t = h.shape[0]
    nt = t // tm
    assert tm % RUN == 0 and SUPER % tm == 0 and t % SUPER == 0 and keep_rows % tm == 0
    first_keep = nt - keep_rows // tm
    base_a, off_a = _rope_consts(base_pos, off_pos, ROT_DIM, ROPE_THETA, HEAD_DIM)
    base_r, off_r = _rope_consts(base_pos, off_pos, RET_DK, RET_THETA, RET_DK)

    def row(w_):
        return pl.BlockSpec((tm, w_), lambda i: (i, 0))

    keep_spec = pl.BlockSpec((tm, ATT_WIDTH), lambda i: (jnp.maximum(i - first_keep, 0), 0))
    keep_shape = ((keep_rows, ATT_WIDTH), F32)
    base_spec = pl.BlockSpec((None, 2, LANES), lambda i: (i, 0, 0))
    off_spec = _resident_spec((7, tm, LANES))
    tab = pltpu.VMEM((3, tm, LANES), F32)
    in_specs = [row(D_MODEL), _full_spec((1, D_MODEL)), _resident_spec((D_MODEL, IN_WIDTH)),
                base_spec, off_spec, base_spec, off_spec, _full_spec((1, RET_V))]
    tps = SUPER // tm
    s4 = pl.BlockSpec((tm // RUN, RUN // 4, 4 * ATT_WIDTH), lambda i: (i, 0, 0))
    s16 = pl.BlockSpec((None, tm // 16, 16 * ATT_WIDTH), lambda i: (i // tps, i % tps, 0))
    st_shape = (N_RET_HEADS // 2, 2 * RET_DK, RET_DV)
    specs = [s4] * 3 + [s16] * 3 + [keep_spec, keep_spec, row(RET_V), _full_spec(st_shape)]
    shapes = ([((t // RUN, RUN // 4, 4 * ATT_WIDTH), BF16)] * 3
              + [((t // SUPER, SUPER // 16, 16 * ATT_WIDTH), BF16)] * 3
              + [keep_shape, keep_shape, ((t, RET_V), BF16), (st_shape, F32)])
    dec = pltpu.VMEM((N_RET_HEADS, RET_CHUNK, RET_CHUNK), F32)
    nat = pltpu.VMEM((ATT_WIDTH // LANES, tm, LANES), F32)
    scratch = [tab, tab, nat, nat, pltpu.VMEM((4, RUN // 4, LANES), F32),
               pltpu.VMEM((tm, RET_QK), BF16), pltpu.VMEM((tm, RET_QK), BF16),
               pltpu.VMEM((tm, RET_V), BF16), pltpu.VMEM((tm, RET_V), F32),
               dec, dec, dec, pltpu.VMEM(st_shape, F32)]
    return pl.pallas_call(
        functools.partial(_mixin_kernel, first_keep=first_keep), grid=(nt,),
        in_specs=in_specs, out_specs=specs,
        out_shape=[jax.ShapeDtypeStruct(s, d) for s, d in shapes],
        scratch_shapes=scratch, compiler_params=_params(1, MIXIN_VMEM_LIMIT), name="mix_in")(
            h, g, w, base_a, off_a, base_r, off_r, gn)


def _attn_bias_tables():
    a = np.arange(QB)[:, None]
    c = np.arange(2 * QB)[None, :]
    steps = BRANCHES[0][0]
    dist = QB + a - c
    band = (dist >= 0) & (dist <= steps)
    cur = c >= QB
    tok_q = 4 * (a % 32) + a // 32
    cc = c % QB
    tok_k = 4 * (cc % 32) + cc // 32 + QB * (c // QB) - QB
    dist2 = tok_q - tok_k
    band2 = (dist2 >= 0) & (dist2 <= steps)
    masks = np.stack([band, band & cur, band2, band2 & cur])
    return np.where(masks, 0.0, NEG).astype(np.float32)


def _attn_qblock(get_q, get_kc, get_kp, get_vc, get_vp, bias, consts, sink):
    head_a, head_a_win, ones_a, ones_b = consts
    for s in range(ATT_WIDTH // LANES):
        q2 = get_q(s)
        kwin = jnp.concatenate([get_kp(s), get_kc(s)], axis=0)
        vwin = jnp.concatenate([get_vp(s), get_vc(s)], axis=0)
        zq = jnp.zeros_like(q2)
        qq = jnp.concatenate([jnp.where(head_a, q2, zq), jnp.where(head_a, zq, q2)], axis=0)
        sc = _dot_nt(qq, kwin)
        s_a = sc[:QB] + bias
        s_b = sc[QB:] + bias
        m_a = jnp.max(s_a, axis=-1, keepdims=True)
        m_b = jnp.max(s_b, axis=-1, keepdims=True)
        p = jnp.concatenate([jnp.exp2(s_a - m_a).astype(BF16), jnp.exp2(s_b - m_b).astype(BF16)],
                            axis=1)
        zv = jnp.zeros_like(vwin)
        w = jnp.concatenate(
            [jnp.concatenate([jnp.where(head_a_win, vwin, zv), ones_a], axis=1),
             jnp.concatenate([jnp.where(head_a_win, zv, vwin), ones_b], axis=1)], axis=0)
        r = _dot(p, w)
        sink(s, (jnp.where(head_a, m_a, m_b), r[:, LANES:], r[:, :LANES]))


def _merge(old, new):
    mo, lo, ao = old
    m2, l2, a2 = new
    mn = jnp.maximum(mo, m2)
    eo = jnp.exp2(mo - mn)
    e2 = jnp.exp2(m2 - mn)
    return mn, eo * lo + e2 * l2, eo * ao + e2 * a2


def _attn_kernel(q16, k16c, k16p, v16c, v16p, q4, k4c, k4p, v4c, v4p, bias_ref, o_ref,
                 m_s, l_s, a_s, nat_s):
    sb = pl.program_id(0)
    ph = pl.program_id(1)
    j = pl.program_id(2)
    lane = lax.broadcasted_iota(jnp.int32, (QB, LANES), 1)
    head_a = lane < HEAD_DIM
    lane_w = lax.broadcasted_iota(jnp.int32, (2 * QB, LANES), 1)
    head_a_win = lane_w < HEAD_DIM
    ones_a = jnp.where(head_a_win, 1.0, 0.0).astype(BF16)
    ones_b = jnp.where(head_a_win, 0.0, 1.0).astype(BF16)
    consts = (head_a, head_a_win, ones_a, ones_b)
    nslab = ATT_WIDTH // LANES
    sub = QB // 4
    lanes = _lanes

    run0 = j * ATT_RUNS
    first_run = sb * (SUPER // RUN) + run0 == 0

    @pl.when(ph == 0)
    def _():
        bias = bias_ref[(sb == 0).astype(jnp.int32)]
        for r in range(4 * ATT_RUNS):
            def scatter(s, res):
                for u in range(SUPER // RUN):
                    rows = pl.ds(u * RUN + (r % 4) * QB + run0 + r // 4, sub, stride=4)
                    for ref, val in zip((m_s, l_s, a_s), res):
                        ref[s, rows, :] = val[u * sub:(u + 1) * sub]

            _attn_qblock(lambda s: q16[:, lanes(r, s)], lambda s: k16c[:, lanes(r, s)],
                         lambda s: k16p[:, lanes(r, s)], lambda s: v16c[:, lanes(r, s)],
                         lambda s: v16p[:, lanes(r, s)], bias, consts, scatter)

    def prev_rows(ref_c, ref_p, w, rows):
        return (lambda cols: ref_p[rows, cols]) if w == 0 else (lambda cols: ref_c[w - 1, rows, cols])

    @pl.when(ph == 1)
    def _():
        for w in range(ATT_RUNS):
            bias = bias_ref[first_run.astype(jnp.int32)] if w == 0 else bias_ref[0]
            kp = prev_rows(k4c, k4p, w, slice(None))
            vp = prev_rows(v4c, v4p, w, slice(None))
            for r in range(4):
                rows = pl.ds(pl.multiple_of((run0 + w) * RUN + r * QB, QB), QB)

                def merge_in(s, res):
                    mn, ln, an = _merge((m_s[s, rows, :], l_s[s, rows, :], a_s[s, rows, :]), res)
                    m_s[s, rows, :] = mn
                    l_s[s, rows, :] = ln
                    a_s[s, rows, :] = an

                _attn_qblock(lambda s: q4[w, :, lanes(r, s)], lambda s: k4c[w, :, lanes(r, s)],
                             lambda s: kp(lanes(r, s)), lambda s: v4c[w, :, lanes(r, s)],
                             lambda s: vp(lanes(r, s)), bias, consts, merge_in)

    @pl.when(ph == 2)
    def _():
        for w in range(ATT_RUNS):
            for b in range(4):
                def cur(ref):
                    return lambda s: jnp.concatenate(
                        [ref[w, b * sub:(b + 1) * sub, lanes(r, s)] for r in range(4)], axis=0)

                def prev(ref_c, ref_p):
                    if b > 0:
                        return lambda s: jnp.concatenate(
                            [ref_c[w, (b - 1) * sub:b * sub, lanes(r, s)] for r in range(4)], axis=0)
                    tail = prev_rows(ref_c, ref_p, w, slice(QB - sub, QB))
                    return lambda s: jnp.concatenate([tail(lanes(r, s)) for r in range(4)], axis=0)

                if w == 0 and b == 0:
                    bias = bias_ref[2 + first_run.astype(jnp.int32)]
                else:
                    bias = bias_ref[2]

                def finish(s, res):
                    for r in range(4):
                        rows = pl.ds(pl.multiple_of((run0 + w) * RUN + r * QB + b * sub, sub), sub)
                        part = tuple(x[r * sub:(r + 1) * sub] for x in res)
                        _, ln, an = _merge((m_s[s, rows, :], l_s[s, rows, :], a_s[s, rows, :]), part)
                        nat_s[s, pl.ds(w * RUN + b * QB + r, sub, stride=4), :] = an / ln

                _attn_qblock(cur(q4), cur(k4c), prev(k4c, k4p), cur(v4c), prev(v4c, v4p),
                             bias, consts, finish)
        for s in range(nslab):
            o_ref[:, s * LANES:(s + 1) * LANES] = nat_s[s].astype(o_ref.dtype)


def _attn_prompt(q4, k4, v4, q16, k16, v16):
    nsb = q16.shape[0]
    s_len = nsb * SUPER
    steps = SUPER // RUN // ATT_RUNS
    wide = 4 * ATT_WIDTH
    blk16 = (None, QB, ATT_RUNS * wide)
    blk4 = (ATT_RUNS, QB, wide)
    blk4p = (None, QB, wide)

    def j16(ph, j):
        return jnp.where(ph == 0, j, steps - 1)

    def n4(sb, ph, j):
        return sb * steps + jnp.where(ph == 0, 0, j)

    c16 = pl.BlockSpec(blk16, lambda sb, ph, j: (sb, 0, j16(ph, j)))
    p16 = pl.BlockSpec(blk16, lambda sb, ph, j: (jnp.maximum(sb - 1, 0), 0, j16(ph, j)))
    c4 = pl.BlockSpec(blk4, lambda sb, ph, j: (n4(sb, ph, j), 0, 0))
    p4 = pl.BlockSpec(blk4p, lambda sb, ph, j: (jnp.maximum(ATT_RUNS * n4(sb, ph, j) - 1, 0), 0, 0))
    out = pl.BlockSpec((ATT_RUNS * RUN, ATT_WIDTH),
                       lambda sb, ph, j: (sb * steps + jnp.where(ph == 2, j, 0), 0))
    bias = jnp.asarray(_attn_bias_tables())
    nslab = ATT_WIDTH // LANES
    scratch = ([pltpu.VMEM((nslab, SUPER, LANES), F32) for _ in range(3)]
               + [pltpu.VMEM((nslab, ATT_RUNS * RUN, LANES), F32)])
    return pl.pallas_call(
        _attn_kernel, grid=(nsb, 3, steps),
        in_specs=[c16, c16, p16, c16, p16, c4, c4, p4, c4, p4, _full_spec(bias.shape)],
        out_specs=out, out_shape=jax.ShapeDtypeStruct((s_len, ATT_WIDTH), BF16),
        scratch_shapes=scratch, compiler_params=_params(3), name="attn_prompt")(
            q16, k16, k16, v16, v16, q4, k4, k4, v4, v4, bias)


def _ret_init(dec_s, wt_s, wh_s, st_s):
    c = RET_CHUNK

    @pl.when(pl.program_id(0) == 0)
    def _():
        i = lax.broadcasted_iota(jnp.int32, (c, c), 0)
        jj = lax.broadcasted_iota(jnp.int32, (c, c), 1)
        diff = (i - jj).astype(F32)
        for h in range(N_RET_HEADS):
            dec_s[h] = jnp.where(diff >= 0, jnp.exp(diff * LOG_G[h]), 0.0)
            wt_s[h] = jnp.exp((c - 1.0 - i.astype(F32)) * LOG_G[h])
            wh_s[h] = jnp.exp((i.astype(F32) + 1.0) * LOG_G[h])
        st_s[...] = jnp.zeros_like(st_s)


def _ret_body(qr_ref, kr_ref, vr_ref, gr_ref, gn_ref, y_ref, s_ref, dec_s, wt_s, wh_s, st_s, chunks):
    c = RET_CHUNK
    lane = lax.broadcasted_iota(jnp.int32, (c, LANES), 1)
    head_a = lane < RET_DK
    row_a = lax.broadcasted_iota(jnp.int32, (LANES, RET_DV), 0) < RET_DK
    for ci in range(chunks):
        rows = slice(ci * c, (ci + 1) * c)
        for p in range(N_RET_HEADS // 2):
            sl = slice(p * LANES, (p + 1) * LANES)
            q2 = qr_ref[rows, sl]
            k2 = kr_ref[rows, sl]
            zq = jnp.zeros_like(q2)
            qq = jnp.concatenate([jnp.where(head_a, q2, zq), jnp.where(head_a, zq, q2)], axis=0)
            inner = _dot_nt(qq, k2)
            state = st_s[p]
            cross = _dot(qq, state.astype(BF16))
            vws = []
            for hh in range(2):
                h = 2 * p + hh
                hs = slice(h * RET_DV, (h + 1) * RET_DV)
                v = vr_ref[rows, hs]
                inn = (inner[hh * c:(hh + 1) * c] * dec_s[h]).astype(BF16)
                o = _dot(inn, v) + cross[hh * c:(hh + 1) * c] * wh_s[h]
                mu = jnp.mean(o, axis=-1, keepdims=True)
                xc = o - mu
                var = jnp.mean(xc * xc, axis=-1, keepdims=True)
                y = xc * lax.rsqrt(var + NORM_EPS) * gn_ref[:, hs]
                y_ref[rows, hs] = (_silu(gr_ref[rows, hs]) * y).astype(y_ref.dtype)
                vws.append((v.astype(F32) * wt_s[h]).astype(BF16))
            upd = _dot_tn(k2, jnp.concatenate(vws, axis=1))
            upd = jnp.where(row_a, upd[:, :RET_DV], upd[:, RET_DV:])
            gch = jnp.where(row_a, math.exp(c * LOG_G[2 * p]), math.exp(c * LOG_G[2 * p + 1]))
            st_s[p] = gch * state + upd

    @pl.when(pl.program_id(0) == pl.num_programs(0) - 1)
    def _():
        s_ref[...] = st_s[...]


S_ROWS = 8
S_HGRP = 4


def _sample_tables(t_new, past_len, wb):
    rows = np.concatenate([np.arange(wb), wb + np.arange(S_ROWS)])
    tab = np.zeros((S_ROWS, wb + S_ROWS), np.float32)
    for r in range(S_ROWS):
        i = r % t_new
        delta = wb + i - rows
        ok = (delta >= 0) & (past_len + i - delta >= 0) & (rows < wb + t_new)
        for window, dil in BRANCHES:
            tab[r] += ok & (delta % dil == 0) & (delta <= window)
    tab = np.tile(tab, (S_HGRP, 1))
    return tab[:, :wb], tab[:, wb:]


def _sample_attn_body(q_ref, kn_ref, vn_ref, kt_ref, vt_ref, cc_ref, cn_ref, o_ref):
    mult_c, mult_n = cc_ref[...], cn_ref[...]
    bias_c = jnp.where(mult_c > 0, 0.0, NEG)
    bias_n = jnp.where(mult_n > 0, 0.0, NEG)
    gw = S_HGRP * HEAD_DIM
    lane_head = lax.broadcasted_iota(jnp.int32, (S_ROWS, gw), 1) // HEAD_DIM
    for g in range(N_ATT_HEADS // S_HGRP):
        gs = slice(g * gw, (g + 1) * gw)
        q8 = q_ref[:, gs]
        q = jnp.concatenate([jnp.where(lane_head == h, q8, 0.0) for h in range(S_HGRP)],
                            axis=0).astype(BF16)
        s_c = _dot(q, kt_ref[g].astype(BF16)) + bias_c
        s_n = _dot_nt(q, kn_ref[:, gs].astype(BF16)) + bias_n
        m = jnp.maximum(jnp.max(s_c, axis=-1, keepdims=True), jnp.max(s_n, axis=-1, keepdims=True))
        p_c = jnp.exp2(s_c - m) * mult_c
        p_n = jnp.exp2(s_n - m) * mult_n
        l = jnp.sum(p_c, axis=-1, keepdims=True) + jnp.sum(p_n, axis=-1, keepdims=True)
        o = (_dot_nt(p_c.astype(BF16), vt_ref[g].astype(BF16))
             + _dot(p_n.astype(BF16), vn_ref[:, gs].astype(BF16))) / l
        o8 = jnp.zeros((S_ROWS, gw), F32)
        for h in range(S_HGRP):
            o8 = jnp.where(lane_head == h, o[h * S_ROWS:(h + 1) * S_ROWS], o8)
        o_ref[:, gs] = o8


FF_SPLITS = ((0, 768), (768, 1536), (1536, 2304), (2304, D_FF))


CACHE_RING = 3


def _ffn_attn_kernel(x_ref, gpre_ref, gpost_ref, wg_ref, wu_ref, wd_ref,
                     q_ref, kn_ref, vn_ref, kt_hbm, vt_hbm, cc_ref, cn_ref,
                     o_ref, oa_ref, u_s, acc_s, kbuf, vbuf, sem):
    j = pl.program_id(1)
    nsplit = len(FF_SPLITS)
    t = pl.program_id(0) * nsplit + j
    nseq = pl.num_programs(0) * nsplit
    ahead = CACHE_RING - 1

    def copies(seq, slot):
        return (pltpu.make_async_copy(kt_hbm.at[seq], kbuf.at[slot], sem.at[0, slot]),
                pltpu.make_async_copy(vt_hbm.at[seq], vbuf.at[slot], sem.at[1, slot]))

    @pl.when(t == 0)
    def _():
        for s in range(ahead):
            for c in copies(s, s):
                c.start()

    @pl.when(t + ahead < nseq)
    def _():
        for c in copies(t + ahead, lax.rem(t + ahead, CACHE_RING)):
            c.start()

    slot = lax.rem(t, CACHE_RING)
    for c in copies(t, slot):
        c.wait()
    kt_ref, vt_ref = kbuf.at[slot], vbuf.at[slot]
    last = len(FF_SPLITS) - 1
    for k, (c0, c1) in enumerate(FF_SPLITS):
        @pl.when(j == k)
        def _():
            if k == 0:
                u = _rms(x_ref[...], gpre_ref[...]).astype(BF16)
                u_s[...] = u
            else:
                u = u_s[...]
            h = (_silu(_dot(u, wg_ref[:, c0:c1])) * _dot(u, wu_ref[:, c0:c1])).astype(BF16)
            d = _dot(h, wd_ref[c0:c1, :])
            if k == 0:
                acc_s[...] = d
            elif k < last:
                acc_s[...] += d
            else:
                o_ref[...] = x_ref[...] + 0.5 * _rms(acc_s[...] + d, gpost_ref[...])
            _sample_attn_body(q_ref, kn_ref, vn_ref, kt_ref, vt_ref, cc_ref, cn_ref, oa_ref)


def _ffn_with_sample_attn(x, gpre, gpost, wg, wu, wd, q, kn, vn, cache_k, cache_v, t_new, past_len, tm):
    t = x.shape[0]
    nb, wb = cache_k.shape[0], cache_k.shape[1]
    nsplit = len(FF_SPLITS)
    assert t_new <= S_ROWS and nb == (t // tm) * nsplit

    def rows(a):
        return jnp.pad(a.reshape(nb, t_new, ATT_WIDTH), ((0, 0), (0, S_ROWS - t_new), (0, 0)))

    def grouped(c):
        return c.transpose(0, 2, 3, 1).reshape(nb, N_ATT_HEADS // S_HGRP, S_HGRP * HEAD_DIM, wb)

    row = pl.BlockSpec((tm, D_MODEL), lambda i, j: (i, 0))
    small = pl.BlockSpec((None, S_ROWS, ATT_WIDTH), lambda i, j: (i * nsplit + j, 0, 0))
    big = pl.BlockSpec(memory_space=pl.ANY)
    slots = pltpu.VMEM((CACHE_RING, N_ATT_HEADS // S_HGRP, S_HGRP * HEAD_DIM, wb), F32)
    tabs = [jnp.asarray(a) for a in _sample_tables(t_new, past_len, wb)]
    h1, o = pl.pallas_call(
        _ffn_attn_kernel, grid=(t // tm, nsplit),
        in_specs=[row] + _ffn_specs() + [small, small, small, big, big]
        + [_full_spec(a.shape) for a in tabs],
        out_specs=[row, small],
        out_shape=[jax.ShapeDtypeStruct((t, D_MODEL), F32),
                   jax.ShapeDtypeStruct((nb, S_ROWS, ATT_WIDTH), F32)],
        scratch_shapes=[pltpu.VMEM((tm, D_MODEL), BF16), pltpu.VMEM((tm, D_MODEL), F32), slots, slots,
                        pltpu.SemaphoreType.DMA((2, CACHE_RING))],
        compiler_params=_params(2, MIXIN_VMEM_LIMIT), name="ffn_attn_sample")(
            x, gpre, gpost, wg, wu, wd, rows(q), rows(kn), rows(vn),
            grouped(cache_k), grouped(cache_v), *tabs)
    return h1, o[:, :t_new].reshape(nb * t_new, ATT_WIDTH)


R_PAIR = 2


def _sample_ret_tables(t_new):
    r = np.arange(R_PAIR * t_new)
    seq, step = r // t_new, r % t_new
    lg = np.asarray(LOG_G)[:, None, None]
    diff = (step[:, None] - step[None, :])[None]
    ok = ((seq[:, None] == seq[None, :]) & (diff[0] >= 0))[None]
    decay = np.where(ok, np.exp(diff * lg), 0.0).reshape(N_RET_HEADS * r.size, r.size)
    w_head = np.exp((step[None, :] + 1.0) * lg[:, :, 0]).reshape(-1, 1) * np.ones((1, RET_DV))
    w_tail = np.exp((t_new - 1.0 - step)[:, None, None] * lg[None, :, :, 0]) * np.ones((1, 1, RET_DV))
    g_chunk = np.repeat(np.exp(t_new * lg[:, 0, 0]), RET_DK)[:, None] * np.ones((1, RET_DV))
    return [a.astype(np.float32) for a in (decay, w_head, w_tail.reshape(r.size, -1), g_chunk)]


def _sample_ret_kernel(q_ref, k_ref, v_ref, g_ref, gn_ref, st_ref, dec_ref, wh_ref, wt_ref, gc_ref,
                       y_ref, so_ref, *, blocks, t_new):
    nrow = R_PAIR * t_new
    lane_head = lax.broadcasted_iota(jnp.int32, (nrow, RET_QK), 1) // RET_DK
    row_seq = lax.broadcasted_iota(jnp.int32, (nrow, RET_QK), 0) // t_new
    out_seq = (lax.broadcasted_iota(jnp.int32, (N_RET_HEADS * nrow, RET_DV), 0) % nrow) // t_new
    for blk in range(blocks):
        rs = slice(blk * nrow, (blk + 1) * nrow)
        q8, k8, v8 = q_ref[rs, :], k_ref[rs, :], v_ref[rs, :]
        qm = jnp.concatenate([jnp.where(lane_head == h, q8, 0.0) for h in range(N_RET_HEADS)],
                             axis=0).astype(BF16)
        inner = (_dot_nt(qm, k8.astype(BF16)) * dec_ref[...]).astype(BF16)
        o_all = _dot(inner, v8.astype(BF16))
        vw = (v8 * wt_ref[...]).astype(BF16)
        cross = None
        for s in range(R_PAIR):
            state = st_ref[blk * R_PAIR + s]
            c = _dot(qm, state.astype(BF16))
            cross = c if cross is None else jnp.where(out_seq == s, c, cross)
            km = jnp.where(row_seq == s, k8, 0.0).astype(BF16)
            upd = _dot_tn(km, vw)
            upd = jnp.concatenate([upd[h * RET_DK:(h + 1) * RET_DK, h * RET_DV:(h + 1) * RET_DV]
                                   for h in range(N_RET_HEADS)], axis=0)
            so_ref[blk * R_PAIR + s] = gc_ref[...] * state + upd
        cross = cross * wh_ref[...]
        for h in range(N_RET_HEADS):
            hs = slice(h * RET_DV, (h + 1) * RET_DV)
            oh = o_all[h * nrow:(h + 1) * nrow, hs] + cross[h * nrow:(h + 1) * nrow]
            mu = jnp.mean(oh, axis=-1, keepdims=True)
            xc = oh - mu
            var = jnp.mean(xc * xc, axis=-1, keepdims=True)
            y = xc * lax.rsqrt(var + NORM_EPS) * gn_ref[:, hs]
            y_ref[rs, hs] = _silu(g_ref[rs, hs]) * y


def _ret_sample(qr, kr, vr, gr, gn, state, t_new, blocks=8):
    nb = state.shape[0]
    assert R_PAIR * t_new == 8 and nb % (R_PAIR * blocks) == 0
    rows = R_PAIR * t_new * blocks
    tabs = [jnp.asarray(a) for a in _sample_ret_tables(t_new)]
    st = state.reshape(nb, N_RET_HEADS * RET_DK, RET_DV)

    def blk(w_):
        return pl.BlockSpec((rows, w_), lambda i: (i, 0))

    st_spec = pl.BlockSpec((R_PAIR * blocks, N_RET_HEADS * RET_DK, RET_DV), lambda i: (i, 0, 0))
    y, s = pl.pallas_call(
        functools.partial(_sample_ret_kernel, blocks=blocks, t_new=t_new),
        grid=(nb // (R_PAIR * blocks),),
        in_specs=[blk(RET_QK), blk(RET_QK), blk(RET_V), blk(RET_V), _full_spec((1, RET_V)), st_spec]
        + [_full_spec(a.shape) for a in tabs],
        out_specs=[blk(RET_V), st_spec],
        out_shape=[jax.ShapeDtypeStruct((nb * t_new, RET_V), F32), jax.ShapeDtypeStruct(st.shape, F32)],
        compiler_params=_params(1), name="ret_sample")(qr, kr, vr, gr, gn, st, *tabs)
    return y, s.reshape(state.shape)


CONV_COLS = 256


def _ffn_conv_kernel(*refs, with_mix):
    if with_mix:
        (oa_ref, yr_ref, h_ref, wo_ref, gmb_ref, gpre_ref, gpost_ref, wg_ref, wu_ref, wd_ref,
         o_ref, wob_ref, wgb_ref, wub_ref, wdb_ref, x_s, u_s, acc_s) = refs
    else:
        (h_ref, gpre_ref, gpost_ref, wg_ref, wu_ref, wd_ref,
         o_ref, wgb_ref, wub_ref, wdb_ref, x_s, u_s, acc_s) = refs
    c = pl.program_id(0)

    @pl.when(c == 0)
    def _():
        x = h_ref[...]
        if with_mix:
            wo = wo_ref[...].astype(BF16)
            wob_ref[...] = wo
            mixed = jnp.concatenate([oa_ref[...].astype(BF16), yr_ref[...].astype(BF16)], axis=1)
            x = x + _rms(_dot(mixed, wo), gmb_ref[...])
        x_s[...] = x
        u_s[...] = _rms(x, gpre_ref[...]).astype(BF16)
        acc_s[...] = jnp.zeros_like(acc_s)

    wg, wu, wd = wg_ref[...].astype(BF16), wu_ref[...].astype(BF16), wd_ref[...].astype(BF16)
    wgb_ref[...] = wg
    wub_ref[...] = wu
    wdb_ref[...] = wd
    u = u_s[...]
    acc_s[...] += _dot((_silu(_dot(u, wg)) * _dot(u, wu)).astype(BF16), wd)

    @pl.when(c == pl.num_programs(0) - 1)
    def _():
        o_ref[...] = x_s[...] + 0.5 * _rms(acc_s[...], gpost_ref[...])


def _ffn_convert(h, gpre, gpost, wg, wu, wd, mix=None):
    t = h.shape[0]
    col = pl.BlockSpec((D_MODEL, CONV_COLS), lambda c: (0, c))
    rowb = pl.BlockSpec((CONV_COLS, D_MODEL), lambda c: (c, 0))
    vec = _full_spec((1, D_MODEL))
    ins, in_specs, outs, out_specs = [], [], [], []
    if mix is not None:
        att, ret, wo, gmb = mix
        ins += [att, ret]
        in_specs += [_full_spec((t, ATT_WIDTH)), _full_spec((t, RET_V))]
    ins.append(h)
    in_specs.append(_full_spec((t, D_MODEL)))
    if mix is not None:
        ins += [wo, gmb]
        in_specs += [_full_spec((D_MODEL, D_MODEL)), vec]
    ins += [gpre, gpost, wg, wu, wd]
    in_specs += [vec, vec, col, col, rowb]
    out_shape = [jax.ShapeDtypeStruct((t, D_MODEL), F32)]
    out_specs = [_full_spec((t, D_MODEL))]
    if mix is not None:
        out_shape.append(jax.ShapeDtypeStruct((D_MODEL, D_MODEL), BF16))
        out_specs.append(_full_spec((D_MODEL, D_MODEL)))
    out_shape += [jax.ShapeDtypeStruct((D_MODEL, D_FF), BF16)] * 2 + [jax.ShapeDtypeStruct((D_FF, D_MODEL), BF16)]
    out_specs += [col, col, rowb]
    return pl.pallas_call(
        functools.partial(_ffn_conv_kernel, with_mix=mix is not None), grid=(D_FF // CONV_COLS,),
        in_specs=in_specs, out_specs=out_specs, out_shape=out_shape,
        scratch_shapes=[pltpu.VMEM((t, D_MODEL), F32), pltpu.VMEM((t, D_MODEL), BF16),
                        pltpu.VMEM((t, D_MODEL), F32)],
        compiler_params=_params(1), name="ffn_convert")(*ins)


def _mixin_sample_kernel(h_ref, g_ref, w_ref, ba_ref, oa_ref, br_ref, or_ref,
                         qa_ref, kk_ref, vk_ref, qr_ref, kr_ref, vr_ref, gr_ref, wb_ref,
                         ta_s, tr_s, u_s):
    c = pl.program_id(0)

    @pl.when(c == 0)
    def _():
        _rope_fill(ba_ref, oa_ref, ta_s)
        _rope_fill(br_ref, or_ref, tr_s)
        u_s[...] = _rms(h_ref[...], g_ref[...]).astype(BF16)

    w = w_ref[...].astype(BF16)
    wb_ref[...] = w
    y = _dot(u_s[...], w)
    nblk = h_ref.shape[0] // ROPE_ROWS

    def rotary(tab_s, half, cols, dst_ref, scale):
        for rb in range(nblk):
            rows = slice(rb * ROPE_ROWS, (rb + 1) * ROPE_ROWS)
            rope = _rope_rows(tab_s, half, rows)
            for s in range(dst_ref.shape[1] // LANES):
                dst_ref[rows, s * LANES:(s + 1) * LANES] = rope(
                    y[rows, cols + s * LANES:cols + (s + 1) * LANES]) * scale

    @pl.when(c == 0)
    def _():
        rotary(ta_s, ROT_DIM // 2, 0, qa_ref, HEAD_DIM ** -0.5 * LOG2_E)

    @pl.when(c == 1)
    def _():
        rotary(ta_s, ROT_DIM // 2, 0, kk_ref, 1.0)

    @pl.when(c == 2)
    def _():
        vk_ref[...] = y

    @pl.when(c == 3)
    def _():
        rotary(tr_s, RET_DK // 2, 0, qr_ref, 1.0)
        rotary(tr_s, RET_DK // 2, RET_QK, kr_ref, RET_DK ** -0.5)

    @pl.when(c == 4)
    def _():
        vr_ref[...] = y

    @pl.when(c == 5)
    def _():
        gr_ref[...] = y


def _mixin_sample(h, g, w, base_pos, off_pos):
    t = h.shape[0]
    grp = ATT_WIDTH
    assert IN_WIDTH == 6 * grp and 2 * RET_QK == grp and RET_V == grp
    base_a, off_a = _rope_consts(base_pos, off_pos, ROT_DIM, ROPE_THETA, HEAD_DIM)
    base_r, off_r = _rope_consts(base_pos, off_pos, RET_DK, RET_THETA, RET_DK)
    wcol = pl.BlockSpec((D_MODEL, grp), lambda c: (0, c))
    base_spec = pl.BlockSpec((None, 2, LANES), lambda c: (0, 0, 0))
    widths = [ATT_WIDTH, ATT_WIDTH, ATT_WIDTH, RET_QK, RET_QK, RET_V, RET_V]
    tab = pltpu.VMEM((3, t, LANES), F32)
    return pl.pallas_call(
        _mixin_sample_kernel, grid=(IN_WIDTH // grp,),
        in_specs=[_full_spec((t, D_MODEL)), _full_spec((1, D_MODEL)), wcol,
                  base_spec, _full_spec(off_a.shape), base_spec, _full_spec(off_r.shape)],
        out_specs=[_full_spec((t, w_)) for w_ in widths] + [wcol],
        out_shape=[jax.ShapeDtypeStruct((t, w_), F32) for w_ in widths]
        + [jax.ShapeDtypeStruct((D_MODEL, IN_WIDTH), BF16)],
        scratch_shapes=[tab, tab, pltpu.VMEM((t, D_MODEL), BF16)],
        compiler_params=_params(1), name="mix_in_sample")(h, g, w, base_a, off_a, base_r, off_r)


def kernel(x_prompt, x_sample, cache_k, cache_v, state_ret, g_ffn1_pre, g_ffn1_post, w1_gate, w1_up,
           w1_down, g_mix_pre, g_mix_post, w_in, gn_w, w_out, g_ffn2_pre, g_ffn2_post, w2_gate, w2_up,
           w2_down):
    b_p, s_p, _ = x_prompt.shape
    b_s, t_s, _ = x_sample.shape
    depth = w_in.shape[0]
    assert depth == 1 and b_p == 1
    keep = min(WIN_MAX, s_p)
    l = 0
    vec = lambda g: g[l].reshape(1, -1)
    g1a, g1b, gma, gmb = vec(g_ffn1_pre), vec(g_ffn1_post), vec(g_mix_pre), vec(g_mix_post)
    g2a, g2b, gn = vec(g_ffn2_pre), vec(g_ffn2_post), vec(gn_w)

    n_s = b_s * t_s
    xs = x_sample.reshape(n_s, D_MODEL)
    h1s, *w1 = _ffn_convert(xs, g1a, g1b, w1_gate[l], w1_up[l], w1_down[l])
    qas, kks, vks, qrs, krs, vrs, grs, wi = _mixin_sample(
        h1s, gma, w_in[l], np.full((1,), PAST_LEN), np.tile(np.arange(t_s), b_s))

    tm = RUN
    xp = x_prompt.reshape(s_p, D_MODEL)
    h1, o_att_s = _ffn_with_sample_attn(xp, g1a, g1b, *w1, qas, kks, vks, cache_k[l], cache_v[l],
                                        t_s, PAST_LEN, tm)
    q4, k4, v4, q16, k16, v16, kk, vk, y_ret, st_p = _mixin(
        h1, gma, wi, 2 * tm * np.arange(s_p // (2 * tm)), np.arange(2 * tm), keep, 2 * tm, gn)
    o_att = _attn_prompt(q4, k4, v4, q16, k16, v16)

    y_ret_s, st_s = _ret_sample(qrs, krs, vrs, grs, gn, state_ret[l], t_s)
    y_sample, wo, *w2 = _ffn_convert(h1s, g2a, g2b, w2_gate[l], w2_up[l], w2_down[l],
                                     mix=(o_att_s, y_ret_s, w_out[l], gmb))
    y_prompt = _mixout_ffn(o_att, y_ret, h1, wo, gmb, g2a, g2b, *w2, 2 * tm)

    hd = (N_ATT_HEADS, HEAD_DIM)
    return (y_prompt.reshape(b_p, s_p, D_MODEL),
            y_sample.reshape(b_s, t_s, D_MODEL),
            kk.reshape(depth, b_p, keep, *hd),
            vk.reshape(depth, b_p, keep, *hd),
            st_p.reshape(depth, b_p, N_RET_HEADS, RET_DK, RET_DV),
            kks.reshape(depth, b_s, t_s, *hd),
            vks.reshape(depth, b_s, t_s, *hd),
            st_s.reshape(depth, b_s, N_RET_HEADS, RET_DK, RET_DV))
```

```python
import functools
import math

import numpy as np
import jax
import jax.numpy as jnp
from jax import lax
from jax.experimental import pallas as pl
from jax.experimental.pallas import tpu as pltpu

F32 = jnp.float32
BF16 = jnp.bfloat16

D_MODEL = 1024
D_FF = 2816
HEAD_DIM = 64
N_ATT_HEADS = 8
ATT_WIDTH = N_ATT_HEADS * HEAD_DIM
ROT_DIM = HEAD_DIM // 4
ROPE_THETA = 500000.0
WIN_MAX = 2048
PAST_LEN = 8192
BRANCHES = ((128, 1), (512, 4), (2048, 16))
N_RET_HEADS = 4
RET_DK = 64
RET_DV = 128
RET_QK = N_RET_HEADS * RET_DK
RET_V = N_RET_HEADS * RET_DV
RET_THETA = 10000.0
RET_CHUNK = 128
IN_WIDTH = 3 * ATT_WIDTH + 2 * RET_QK + 2 * RET_V
NORM_EPS = 1e-6
NEG = -1e30

LANES = 128
QB = 128
SUPER = 2048
RUN = 512
ATT_RUNS = 4
VMEM_LIMIT = 56 * 1024 * 1024
MIXIN_VMEM_LIMIT = 60 * 1024 * 1024
LOG_G = tuple(math.log1p(-2.0 ** (-5.0 - h)) for h in range(N_RET_HEADS))
LOG2_E = math.log2(math.e)


def _full_spec(shape):
    nd = len(shape)
    return pl.BlockSpec(shape, lambda *_: (0,) * nd)


def _resident_spec(shape):
    nd = len(shape)
    return pl.BlockSpec(shape, lambda *_: (0,) * nd, pipeline_mode=pl.Buffered(1))


def _params(n_axes, vmem=VMEM_LIMIT):
    return pltpu.CompilerParams(dimension_semantics=("arbitrary",) * n_axes, vmem_limit_bytes=vmem)


def _rms(x, g):
    return x * lax.rsqrt(jnp.mean(x * x, axis=-1, keepdims=True) + NORM_EPS) * g


def _silu(x):
    return x / (1.0 + jnp.exp(-x))


def _dot(a, b):
    return jnp.dot(a, b, preferred_element_type=F32)


def _dot_nt(a, b):
    return lax.dot_general(a, b, (((1,), (1,)), ((), ())), preferred_element_type=F32)


def _dot_tn(a, b):
    return lax.dot_general(a, b, (((0,), (0,)), ((), ())), preferred_element_type=F32)


FF_CHUNK = 512


def _swiglu(u, wg_ref, wu_ref, wd_ref):
    acc = None
    for c0 in range(0, D_FF, FF_CHUNK):
        c1 = min(c0 + FF_CHUNK, D_FF)
        g = _dot(u, wg_ref[:, c0:c1])
        up = _dot(u, wu_ref[:, c0:c1])
        h = (_silu(g) * up).astype(BF16)
        d = _dot(h, wd_ref[c0:c1, :])
        acc = d if acc is None else acc + d
    return acc


def _mixout_ffn_kernel(oa_ref, yr_ref, h_ref, wo_ref, gmb_ref, gpre_ref, gpost_ref,
                       wg_ref, wu_ref, wd_ref, o_ref):
    mixed = jnp.concatenate([oa_ref[...].astype(BF16), yr_ref[...].astype(BF16)], axis=1)
    x = h_ref[...] + _rms(_dot(mixed, wo_ref[...]), gmb_ref[...])
    u = _rms(x, gpre_ref[...]).astype(BF16)
    y = _swiglu(u, wg_ref, wu_ref, wd_ref)
    o_ref[...] = x + 0.5 * _rms(y, gpost_ref[...])


def _ffn_specs():
    return [_full_spec((1, D_MODEL)), _full_spec((1, D_MODEL)),
            _resident_spec((D_MODEL, D_FF)), _resident_spec((D_MODEL, D_FF)),
            _resident_spec((D_FF, D_MODEL))]


def _mixout_ffn(oa, yr, h, wo, gmb, gpre, gpost, wg, wu, wd, tm):
    t = h.shape[0]
    row = pl.BlockSpec((tm, D_MODEL), lambda i: (i, 0))
    half = pl.BlockSpec((tm, ATT_WIDTH), lambda i: (i, 0))
    return pl.pallas_call(
        _mixout_ffn_kernel, grid=(t // tm,),
        in_specs=[half, half, row, _resident_spec((D_MODEL, D_MODEL)), _full_spec((1, D_MODEL))]
        + _ffn_specs(),
        out_specs=row, out_shape=jax.ShapeDtypeStruct((t, D_MODEL), F32),
        compiler_params=_params(1), name="mixout_ffn")(oa, yr, h, wo, gmb, gpre, gpost, wg, wu, wd)


def _rope_consts(base_pos, off_pos, rot_dim, theta, head_dim):
    half = rot_dim // 2
    inv = theta ** (-np.arange(half, dtype=np.float64) * (2.0 / rot_dim))
    lane = np.arange(LANES) % head_dim
    inv_l = inv[lane % half][None, :]
    first = (lane < half).astype(np.float64)[None, :]
    second = ((lane >= half) & (lane < rot_dim)).astype(np.float64)[None, :]
    rot = first + second
    a = np.asarray(base_pos, np.float64)[:, None] * inv_l
    b = np.asarray(off_pos, np.float64)[:, None] * inv_l
    cb, sb = np.cos(b), np.sin(b)
    base = np.stack([np.cos(a), np.sin(a)], axis=1)
    off = np.stack([cb * rot, sb * rot, np.broadcast_to(1.0 - rot, cb.shape),
                    -cb * first, -sb * first, cb * second, sb * second])
    return jnp.asarray(base, F32), jnp.asarray(off, F32)


def _rope_fill(base_ref, off_ref, tab_s):
    ca = base_ref[0:1, :]
    sa = base_ref[1:2, :]
    tab_s[0] = ca * off_ref[0] - sa * off_ref[1] + off_ref[2]
    tab_s[1] = sa * off_ref[3] + ca * off_ref[4]
    tab_s[2] = sa * off_ref[5] + ca * off_ref[6]


ROPE_ROWS = 64


def _rope_rows(tab_ref, half, rows):
    c, sa, sb = tab_ref[0, rows, :], tab_ref[1, rows, :], tab_ref[2, rows, :]
    return lambda x: x * c + pltpu.roll(x, LANES - half, 1) * sa + pltpu.roll(x, half, 1) * sb


def _lanes(r, s):
    return slice(r * ATT_WIDTH + s * LANES, r * ATT_WIDTH + (s + 1) * LANES)


def _emit_dilated(nat_s, x4_s, o4_ref, o16_ref):
    l16 = RUN // 16
    for w in range(nat_s.shape[1] // RUN):
        for s in range(ATT_WIDTH // LANES):
            for r in range(4):
                x4 = nat_s[s, pl.ds(w * RUN + r, RUN // 4, stride=4), :]
                o4_ref[w, :, _lanes(r, s)] = x4.astype(BF16)
                x4_s[r] = x4
            for r in range(4):
                for c in range(4):
                    o16_ref[w * l16:(w + 1) * l16, _lanes(4 * c + r, s)] = (
                        x4_s[r, pl.ds(c, l16, stride=4), :].astype(BF16))


def _mixin_kernel(h_ref, g_ref, w_ref, ba_ref, oa_ref, br_ref, or_ref, gn_ref,
                  q4_ref, k4_ref, v4_ref, q16_ref, k16_ref, v16_ref, kk_ref, vk_ref, y_ref, st_ref,
                  ta_ref, tr_ref, nat_s, natk_s, x4_s, qr_ref, kr_ref, vr_ref, gr_ref,
                  dec_s, wt_s, wh_s, st_s, *, first_keep):
    _ret_init(dec_s, wt_s, wh_s, st_s)
    _rope_fill(ba_ref, oa_ref, ta_ref)
    _rope_fill(br_ref, or_ref, tr_ref)
    u = _rms(h_ref[...], g_ref[...]).astype(BF16)
    scale = HEAD_DIM ** -0.5 * LOG2_E
    nslab = ATT_WIDTH // LANES

    q = _dot(u, w_ref[:, 0:ATT_WIDTH])
    k = _dot(u, w_ref[:, ATT_WIDTH:2 * ATT_WIDTH])
    for rb in range(h_ref.shape[0] // ROPE_ROWS):
        rows = slice(rb * ROPE_ROWS, (rb + 1) * ROPE_ROWS)
        rope = _rope_rows(ta_ref, ROT_DIM // 2, rows)
        for s in range(nslab):
            sl = slice(s * LANES, (s + 1) * LANES)
            qs = rope(q[rows, sl]) * scale
            ks = rope(k[rows, sl])
            nat_s[s, rows, :] = qs
            natk_s[s, rows, :] = ks
    _emit_dilated(nat_s, x4_s, q4_ref, q16_ref)
    _emit_dilated(natk_s, x4_s, k4_ref, k16_ref)

    v = _dot(u, w_ref[:, 2 * ATT_WIDTH:3 * ATT_WIDTH])
    for s in range(nslab):
        nat_s[s] = v[:, s * LANES:(s + 1) * LANES]
    _emit_dilated(nat_s, x4_s, v4_ref, v16_ref)

    o = 3 * ATT_WIDTH
    qk = _dot(u, w_ref[:, o:o + 2 * RET_QK])
    for rb in range(h_ref.shape[0] // ROPE_ROWS):
        rows = slice(rb * ROPE_ROWS, (rb + 1) * ROPE_ROWS)
        rope = _rope_rows(tr_ref, RET_DK // 2, rows)
        for s in range(RET_QK // LANES):
            sl = slice(s * LANES, (s + 1) * LANES)
            sk = slice(RET_QK + s * LANES, RET_QK + (s + 1) * LANES)
            qr_ref[rows, sl] = rope(qk[rows, sl]).astype(qr_ref.dtype)
            kr_ref[rows, sl] = (rope(qk[rows, sk]) * (RET_DK ** -0.5)).astype(kr_ref.dtype)
    o += 2 * RET_QK
    vr_ref[...] = _dot(u, w_ref[:, o:o + RET_V]).astype(vr_ref.dtype)
    o += RET_V
    gr_ref[...] = _dot(u, w_ref[:, o:o + RET_V])
    _ret_body(qr_ref, kr_ref, vr_ref, gr_ref, gn_ref, y_ref, st_ref, dec_s, wt_s, wh_s, st_s,
              qr_ref.shape[0] // RET_CHUNK)

    @pl.when(pl.program_id(0) >= first_keep)
    def _():
        for s in range(nslab):
            kk_ref[:, s * LANES:(s + 1) * LANES] = natk_s[s]
            vk_ref[:, s * LANES:(s + 1) * LANES] = nat_s[s]


def _mixin(h, g, w, base_pos, off_pos, keep_rows, tm, gn):
    t = h.shape[0]
    nt = t // tm
    assert tm % RUN == 0 and SUPER % tm == 0 and t % SUPER == 0 and keep_rows % tm == 0
    first_keep = nt - keep_rows // tm
    base_a, off_a = _rope_consts(base_pos, off_pos, ROT_DIM, ROPE_THETA, HEAD_DIM)
    base_r, off_r = _rope_consts(base_pos, off_pos, RET_DK, RET_THETA, RET_DK)

    def row(w_):
        return pl.BlockSpec((tm, w_), lambda i: (i, 0))

    keep_spec = pl.BlockSpec((tm, ATT_WIDTH), lambda i: (jnp.maximum(i - first_keep, 0), 0))
    keep_shape = ((keep_rows, ATT_WIDTH), F32)
    base_spec = pl.BlockSpec((None, 2, LANES), lambda i: (i, 0, 0))
    off_spec = _resident_spec((7, tm, LANES))
    tab = pltpu.VMEM((3, tm, LANES), F32)
    in_specs = [row(D_MODEL), _full_spec((1, D_MODEL)), _resident_spec((D_MODEL, IN_WIDTH)),
                base_spec, off_spec, base_spec, off_spec, _full_spec((1, RET_V))]
    tps = SUPER // tm
    s4 = pl.BlockSpec((tm // RUN, RUN // 4, 4 * ATT_WIDTH), lambda i: (i, 0, 0))
    s16 = pl.BlockSpec((None, tm // 16, 16 * ATT_WIDTH), lambda i: (i // tps, i % tps, 0))
    st_shape = (N_RET_HEADS // 2, 2 * RET_DK, RET_DV)
    specs = [s4] * 3 + [s16] * 3 + [keep_spec, keep_spec, row(RET_V), _full_spec(st_shape)]
    shapes = ([((t // RUN, RUN // 4, 4 * ATT_WIDTH), BF16)] * 3
              + [((t // SUPER, SUPER // 16, 16 * ATT_WIDTH), BF16)] * 3
              + [keep_shape, keep_shape, ((t, RET_V), BF16), (st_shape, F32)])
    dec = pltpu.VMEM((N_RET_HEADS, RET_CHUNK, RET_CHUNK), F32)
    nat = pltpu.VMEM((ATT_WIDTH // LANES, tm, LANES), F32)
    scratch = [tab, tab, nat, nat, pltpu.VMEM((4, RUN // 4, LANES), F32),
               pltpu.VMEM((tm, RET_QK), BF16), pltpu.VMEM((tm, RET_QK), BF16),
               pltpu.VMEM((tm, RET_V), BF16), pltpu.VMEM((tm, RET_V), F32),
               dec, dec, dec, pltpu.VMEM(st_shape, F32)]
    return pl.pallas_call(
        functools.partial(_mixin_kernel, first_keep=first_keep), grid=(nt,),
        in_specs=in_specs, out_specs=specs,
        out_shape=[jax.ShapeDtypeStruct(s, d) for s, d in shapes],
        scratch_shapes=scratch, compiler_params=_params(1, MIXIN_VMEM_LIMIT), name="mix_in")(
            h, g, w, base_a, off_a, base_r, off_r, gn)


def _attn_bias_tables():
    a = np.arange(QB)[:, None]
    c = np.arange(2 * QB)[None, :]
    steps = BRANCHES[0][0]
    dist = QB + a - c
    band = (dist >= 0) & (dist <= steps)
    cur = c >= QB
    tok_q = 4 * (a % 32) + a // 32
    cc = c % QB
    tok_k = 4 * (cc % 32) + cc // 32 + QB * (c // QB) - QB
    dist2 = tok_q - tok_k
    band2 = (dist2 >= 0) & (dist2 <= steps)
    masks = np.stack([band, band & cur, band2, band2 & cur])
    return np.where(masks, 0.0, NEG).astype(np.float32)


def _attn_qblock(get_q, get_kc, get_kp, get_vc, get_vp, bias, consts, sink):
    head_a, head_a_win, ones_a, ones_b = consts
    for s in range(ATT_WIDTH // LANES):
        q2 = get_q(s)
        kwin = jnp.concatenate([get_kp(s), get_kc(s)], axis=0)
        vwin = jnp.concatenate([get_vp(s), get_vc(s)], axis=0)
        zq = jnp.zeros_like(q2)
        qq = jnp.concatenate([jnp.where(head_a, q2, zq), jnp.where(head_a, zq, q2)], axis=0)
        sc = _dot_nt(qq, kwin)
        s_a = sc[:QB] + bias
        s_b = sc[QB:] + bias
        m_a = jnp.max(s_a, axis=-1, keepdims=True)
        m_b = jnp.max(s_b, axis=-1, keepdims=True)
        p = jnp.concatenate([jnp.exp2(s_a - m_a).astype(BF16), jnp.exp2(s_b - m_b).astype(BF16)],
                            axis=1)
        zv = jnp.zeros_like(vwin)
        w = jnp.concatenate(
            [jnp.concatenate([jnp.where(head_a_win, vwin, zv), ones_a], axis=1),
             jnp.concatenate([jnp.where(head_a_win, zv, vwin), ones_b], axis=1)], axis=0)
        r = _dot(p, w)
        sink(s, (jnp.where(head_a, m_a, m_b), r[:, LANES:], r[:, :LANES]))


def _merge(old, new):
    mo, lo, ao = old
    m2, l2, a2 = new
    mn = jnp.maximum(mo, m2)
    eo = jnp.exp2(mo - mn)
    e2 = jnp.exp2(m2 - mn)
    return mn, eo * lo + e2 * l2, eo * ao + e2 * a2


def _attn_kernel(q16, k16c, k16p, v16c, v16p, q4, k4c, k4p, v4c, v4p, bias_ref, o_ref,
                 m_s, l_s, a_s, nat_s):
    sb = pl.program_id(0)
    ph = pl.program_id(1)
    j = pl.program_id(2)
    lane = lax.broadcasted_iota(jnp.int32, (QB, LANES), 1)
    head_a = lane < HEAD_DIM
    lane_w = lax.broadcasted_iota(jnp.int32, (2 * QB, LANES), 1)
    head_a_win = lane_w < HEAD_DIM
    ones_a = jnp.where(head_a_win, 1.0, 0.0).astype(BF16)
    ones_b = jnp.where(head_a_win, 0.0, 1.0).astype(BF16)
    consts = (head_a, head_a_win, ones_a, ones_b)
    nslab = ATT_WIDTH // LANES
    sub = QB // 4
    lanes = _lanes

    run0 = j * ATT_RUNS
    first_run = sb * (SUPER // RUN) + run0 == 0

    @pl.when(ph == 0)
    def _():
        bias = bias_ref[(sb == 0).astype(jnp.int32)]
        for r in range(4 * ATT_RUNS):
            def scatter(s, res):
                for u in range(SUPER // RUN):
                    rows = pl.ds(u * RUN + (r % 4) * QB + run0 + r // 4, sub, stride=4)
                    for ref, val in zip((m_s, l_s, a_s), res):
                        ref[s, rows, :] = val[u * sub:(u + 1) * sub]

            _attn_qblock(lambda s: q16[:, lanes(r, s)], lambda s: k16c[:, lanes(r, s)],
                         lambda s: k16p[:, lanes(r, s)], lambda s: v16c[:, lanes(r, s)],
                         lambda s: v16p[:, lanes(r, s)], bias, consts, scatter)

    def prev_rows(ref_c, ref_p, w, rows):
        return (lambda cols: ref_p[rows, cols]) if w == 0 else (lambda cols: ref_c[w - 1, rows, cols])

    @pl.when(ph == 1)
    def _():
        for w in range(ATT_RUNS):
            bias = bias_ref[first_run.astype(jnp.int32)] if w == 0 else bias_ref[0]
            kp = prev_rows(k4c, k4p, w, slice(None))
            vp = prev_rows(v4c, v4p, w, slice(None))
            for r in range(4):
                rows = pl.ds(pl.multiple_of((run0 + w) * RUN + r * QB, QB), QB)

                def merge_in(s, res):
                    mn, ln, an = _merge((m_s[s, rows, :], l_s[s, rows, :], a_s[s, rows, :]), res)
                    m_s[s, rows, :] = mn
                    l_s[s, rows, :] = ln
                    a_s[s, rows, :] = an

                _attn_qblock(lambda s: q4[w, :, lanes(r, s)], lambda s: k4c[w, :, lanes(r, s)],
                             lambda s: kp(lanes(r, s)), lambda s: v4c[w, :, lanes(r, s)],
                             lambda s: vp(lanes(r, s)), bias, consts, merge_in)

    @pl.when(ph == 2)
    def _():
        for w in range(ATT_RUNS):
            for b in range(4):
                def cur(ref):
                    return lambda s: jnp.concatenate(
                        [ref[w, b * sub:(b + 1) * sub, lanes(r, s)] for r in range(4)], axis=0)

                def prev(ref_c, ref_p):
                    if b > 0:
                        return lambda s: jnp.concatenate(
                            [ref_c[w, (b - 1) * sub:b * sub, lanes(r, s)] for r in range(4)], axis=0)
                    tail = prev_rows(ref_c, ref_p, w, slice(QB - sub, QB))
                    return lambda s: jnp.concatenate([tail(lanes(r, s)) for r in range(4)], axis=0)

                if w == 0 and b == 0:
                    bias = bias_ref[2 + first_run.astype(jnp.int32)]
                else:
                    bias = bias_ref[2]

                def finish(s, res):
                    for r in range(4):
                        rows = pl.ds(pl.multiple_of((run0 + w) * RUN + r * QB + b * sub, sub), sub)
                        part = tuple(x[r * sub:(r + 1) * sub] for x in res)
                        _, ln, an = _merge((m_s[s, rows, :], l_s[s, rows, :], a_s[s, rows, :]), part)
                        nat_s[s, pl.ds(w * RUN + b * QB + r, sub, stride=4), :] = an / ln

                _attn_qblock(cur(q4), cur(k4c), prev(k4c, k4p), cur(v4c), prev(v4c, v4p),
                             bias, consts, finish)
        for s in range(nslab):
            o_ref[:, s * LANES:(s + 1) * LANES] = nat_s[s].astype(o_ref.dtype)


def _attn_prompt(q4, k4, v4, q16, k16, v16):
    nsb = q16.shape[0]
    s_len = nsb * SUPER
    steps = SUPER // RUN // ATT_RUNS
    wide = 4 * ATT_WIDTH
    blk16 = (None, QB, ATT_RUNS * wide)
    blk4 = (ATT_RUNS, QB, wide)
    blk4p = (None, QB, wide)

    def j16(ph, j):
        return jnp.where(ph == 0, j, steps - 1)

    def n4(sb, ph, j):
        return sb * steps + jnp.where(ph == 0, 0, j)

    c16 = pl.BlockSpec(blk16, lambda sb, ph, j: (sb, 0, j16(ph, j)))
    p16 = pl.BlockSpec(blk16, lambda sb, ph, j: (jnp.maximum(sb - 1, 0), 0, j16(ph, j)))
    c4 = pl.BlockSpec(blk4, lambda sb, ph, j: (n4(sb, ph, j), 0, 0))
    p4 = pl.BlockSpec(blk4p, lambda sb, ph, j: (jnp.maximum(ATT_RUNS * n4(sb, ph, j) - 1, 0), 0, 0))
    out = pl.BlockSpec((ATT_RUNS * RUN, ATT_WIDTH),
                       lambda sb, ph, j: (sb * steps + jnp.where(ph == 2, j, 0), 0))
    bias = jnp.asarray(_attn_bias_tables())
    nslab = ATT_WIDTH // LANES
    scratch = ([pltpu.VMEM((nslab, SUPER, LANES), F32) for _ in range(3)]
               + [pltpu.VMEM((nslab, ATT_RUNS * RUN, LANES), F32)])
    return pl.pallas_call(
        _attn_kernel, grid=(nsb, 3, steps),
        in_specs=[c16, c16, p16, c16, p16, c4, c4, p4, c4, p4, _full_spec(bias.shape)],
        out_specs=out, out_shape=jax.ShapeDtypeStruct((s_len, ATT_WIDTH), BF16),
        scratch_shapes=scratch, compiler_params=_params(3), name="attn_prompt")(
            q16, k16, k16, v16, v16, q4, k4, k4, v4, v4, bias)


def _ret_init(dec_s, wt_s, wh_s, st_s):
    c = RET_CHUNK

    @pl.when(pl.program_id(0) == 0)
    def _():
        i = lax.broadcasted_iota(jnp.int32, (c, c), 0)
        jj = lax.broadcasted_iota(jnp.int32, (c, c), 1)
        diff = (i - jj).astype(F32)
        for h in range(N_RET_HEADS):
            dec_s[h] = jnp.where(diff >= 0, jnp.exp(diff * LOG_G[h]), 0.0)
            wt_s[h] = jnp.exp((c - 1.0 - i.astype(F32)) * LOG_G[h])
            wh_s[h] = jnp.exp((i.astype(F32) + 1.0) * LOG_G[h])
        st_s[...] = jnp.zeros_like(st_s)


def _ret_body(qr_ref, kr_ref, vr_ref, gr_ref, gn_ref, y_ref, s_ref, dec_s, wt_s, wh_s, st_s, chunks):
    c = RET_CHUNK
    lane = lax.broadcasted_iota(jnp.int32, (c, LANES), 1)
    head_a = lane < RET_DK
    row_a = lax.broadcasted_iota(jnp.int32, (LANES, RET_DV), 0) < RET_DK
    for ci in range(chunks):
        rows = slice(ci * c, (ci + 1) * c)
        for p in range(N_RET_HEADS // 2):
            sl = slice(p * LANES, (p + 1) * LANES)
            q2 = qr_ref[rows, sl]
            k2 = kr_ref[rows, sl]
            zq = jnp.zeros_like(q2)
            qq = jnp.concatenate([jnp.where(head_a, q2, zq), jnp.where(head_a, zq, q2)], axis=0)
            inner = _dot_nt(qq, k2)
            state = st_s[p]
            cross = _dot(qq, state.astype(BF16))
            vws = []
            for hh in range(2):
                h = 2 * p + hh
                hs = slice(h * RET_DV, (h + 1) * RET_DV)
                v = vr_ref[rows, hs]
                inn = (inner[hh * c:(hh + 1) * c] * dec_s[h]).astype(BF16)
                o = _dot(inn, v) + cross[hh * c:(hh + 1) * c] * wh_s[h]
                mu = jnp.mean(o, axis=-1, keepdims=True)
                xc = o - mu
                var = jnp.mean(xc * xc, axis=-1, keepdims=True)
                y = xc * lax.rsqrt(var + NORM_EPS) * gn_ref[:, hs]
                y_ref[rows, hs] = (_silu(gr_ref[rows, hs]) * y).astype(y_ref.dtype)
                vws.append((v.astype(F32) * wt_s[h]).astype(BF16))
            upd = _dot_tn(k2, jnp.concatenate(vws, axis=1))
            upd = jnp.where(row_a, upd[:, :RET_DV], upd[:, RET_DV:])
            gch = jnp.where(row_a, math.exp(c * LOG_G[2 * p]), math.exp(c * LOG_G[2 * p + 1]))
            st_s[p] = gch * state + upd

    @pl.when(pl.program_id(0) == pl.num_programs(0) - 1)
    def _():
        s_ref[...] = st_s[...]


S_ROWS = 8
S_HGRP = 4


def _sample_tables(t_new, past_len, wb):
    rows = np.concatenate([np.arange(wb), wb + np.arange(S_ROWS)])
    tab = np.zeros((S_ROWS, wb + S_ROWS), np.float32)
    for r in range(S_ROWS):
        i = r % t_new
        delta = wb + i - rows
        ok = (delta >= 0) & (past_len + i - delta >= 0) & (rows < wb + t_new)
        for window, dil in BRANCHES:
            tab[r] += ok & (delta % dil == 0) & (delta <= window)
    tab = np.tile(tab, (S_HGRP, 1))
    return tab[:, :wb], tab[:, wb:]


def _sample_attn_body(q_ref, kn_ref, vn_ref, kt_ref, vt_ref, cc_ref, cn_ref, o_ref):
    mult_c, mult_n = cc_ref[...], cn_ref[...]
    bias_c = jnp.where(mult_c > 0, 0.0, NEG)
    bias_n = jnp.where(mult_n > 0, 0.0, NEG)
    gw = S_HGRP * HEAD_DIM
    lane_head = lax.broadcasted_iota(jnp.int32, (S_ROWS, gw), 1) // HEAD_DIM
    for g in range(N_ATT_HEADS // S_HGRP):
        gs = slice(g * gw, (g + 1) * gw)
        q8 = q_ref[:, gs]
        q = jnp.concatenate([jnp.where(lane_head == h, q8, 0.0) for h in range(S_HGRP)],
                            axis=0).astype(BF16)
        s_c = _dot(q, kt_ref[g].astype(BF16)) + bias_c
        s_n = _dot_nt(q, kn_ref[:, gs].astype(BF16)) + bias_n
        m = jnp.maximum(jnp.max(s_c, axis=-1, keepdims=True), jnp.max(s_n, axis=-1, keepdims=True))
        p_c = jnp.exp2(s_c - m) * mult_c
        p_n = jnp.exp2(s_n - m) * mult_n
        l = jnp.sum(p_c, axis=-1, keepdims=True) + jnp.sum(p_n, axis=-1, keepdims=True)
        o = (_dot_nt(p_c.astype(BF16), vt_ref[g].astype(BF16))
             + _dot(p_n.astype(BF16), vn_ref[:, gs].astype(BF16))) / l
        o8 = jnp.zeros((S_ROWS, gw), F32)
        for h in range(S_HGRP):
            o8 = jnp.where(lane_head == h, o[h * S_ROWS:(h + 1) * S_ROWS], o8)
        o_ref[:, gs] = o8


FF_SPLITS = ((0, 768), (768, 1536), (1536, 2304), (2304, D_FF))


CACHE_RING = 3


def _ffn_attn_kernel(x_ref, gpre_ref, gpost_ref, wg_ref, wu_ref, wd_ref,
                     q_ref, kn_ref, vn_ref, kt_hbm, vt_hbm, cc_ref, cn_ref,
                     o_ref, oa_ref, u_s, acc_s, kbuf, vbuf, sem):
    j = pl.program_id(1)
    nsplit = len(FF_SPLITS)
    t = pl.program_id(0) * nsplit + j
    nseq = pl.num_programs(0) * nsplit
    ahead = CACHE_RING - 1

    def copies(seq, slot):
        return (pltpu.make_async_copy(kt_hbm.at[seq], kbuf.at[slot], sem.at[0, slot]),
                pltpu.make_async_copy(vt_hbm.at[seq], vbuf.at[slot], sem.at[1, slot]))

    @pl.when(t == 0)
    def _():
        for s in range(ahead):
            for prio, c in enumerate(copies(s, s)):
                c.start(priority=prio)

    @pl.when(t + ahead < nseq)
    def _():
        for prio, c in enumerate(copies(t + ahead, lax.rem(t + ahead, CACHE_RING))):
            c.start(priority=prio)

    slot = lax.rem(t, CACHE_RING)
    for c in copies(t, slot):
        c.wait()
    kt_ref, vt_ref = kbuf.at[slot], vbuf.at[slot]
    last = len(FF_SPLITS) - 1
    for k, (c0, c1) in enumerate(FF_SPLITS):
        @pl.when(j == k)
        def _():
            if k == 0:
                u = _rms(x_ref[...], gpre_ref[...]).astype(BF16)
                u_s[...] = u
            else:
                u = u_s[...]
            h = (_silu(_dot(u, wg_ref[:, c0:c1])) * _dot(u, wu_ref[:, c0:c1])).astype(BF16)
            d = _dot(h, wd_ref[c0:c1, :])
            if k == 0:
                acc_s[...] = d
            elif k < last:
                acc_s[...] += d
            else:
                o_ref[...] = x_ref[...] + 0.5 * _rms(acc_s[...] + d, gpost_ref[...])
            _sample_attn_body(q_ref, kn_ref, vn_ref, kt_ref, vt_ref, cc_ref, cn_ref, oa_ref)


def _ffn_with_sample_attn(x, gpre, gpost, wg, wu, wd, q, kn, vn, cache_k, cache_v, t_new, past_len, tm):
    t = x.shape[0]
    nb, wb = cache_k.shape[0], cache_k.shape[1]
    nsplit = len(FF_SPLITS)
    assert t_new <= S_ROWS and nb == (t // tm) * nsplit

    def rows(a):
        return jnp.pad(a.reshape(nb, t_new, ATT_WIDTH), ((0, 0), (0, S_ROWS - t_new), (0, 0)))

    def grouped(c):
        return c.transpose(0, 2, 3, 1).reshape(nb, N_ATT_HEADS // S_HGRP, S_HGRP * HEAD_DIM, wb)

    row = pl.BlockSpec((tm, D_MODEL), lambda i, j: (i, 0))
    small = pl.BlockSpec((None, S_ROWS, ATT_WIDTH), lambda i, j: (i * nsplit + j, 0, 0))
    big = pl.BlockSpec(memory_space=pl.ANY)
    slots = pltpu.VMEM((CACHE_RING, N_ATT_HEADS // S_HGRP, S_HGRP * HEAD_DIM, wb), F32)
    tabs = [jnp.asarray(a) for a in _sample_tables(t_new, past_len, wb)]
    h1, o = pl.pallas_call(
        _ffn_attn_kernel, grid=(t // tm, nsplit),
        in_specs=[row] + _ffn_specs() + [small, small, small, big, big]
        + [_full_spec(a.shape) for a in tabs],
        out_specs=[row, small],
        out_shape=[jax.ShapeDtypeStruct((t, D_MODEL), F32),
                   jax.ShapeDtypeStruct((nb, S_ROWS, ATT_WIDTH), F32)],
        scratch_shapes=[pltpu.VMEM((tm, D_MODEL), BF16), pltpu.VMEM((tm, D_MODEL), F32), slots, slots,
                        pltpu.SemaphoreType.DMA((2, CACHE_RING))],
        compiler_params=_params(2, MIXIN_VMEM_LIMIT), name="ffn_attn_sample")(
            x, gpre, gpost, wg, wu, wd, rows(q), rows(kn), rows(vn),
            grouped(cache_k), grouped(cache_v), *tabs)
    return h1, o[:, :t_new].reshape(nb * t_new, ATT_WIDTH)


R_PAIR = 2


def _sample_ret_tables(t_new):
    r = np.arange(R_PAIR * t_new)
    seq, step = r // t_new, r % t_new
    lg = np.asarray(LOG_G)[:, None, None]
    diff = (step[:, None] - step[None, :])[None]
    ok = ((seq[:, None] == seq[None, :]) & (diff[0] >= 0))[None]
    decay = np.where(ok, np.exp(diff * lg), 0.0).reshape(N_RET_HEADS * r.size, r.size)
    w_head = np.exp((step[None, :] + 1.0) * lg[:, :, 0]).reshape(-1, 1) * np.ones((1, RET_DV))
    w_tail = np.exp((t_new - 1.0 - step)[:, None, None] * lg[None, :, :, 0]) * np.ones((1, 1, RET_DV))
    g_chunk = np.repeat(np.exp(t_new * lg[:, 0, 0]), RET_DK)[:, None] * np.ones((1, RET_DV))
    return [a.astype(np.float32) for a in (decay, w_head, w_tail.reshape(r.size, -1), g_chunk)]


def _sample_ret_kernel(q_ref, k_ref, v_ref, g_ref, gn_ref, st_ref, dec_ref, wh_ref, wt_ref, gc_ref,
                       y_ref, so_ref, *, blocks, t_new):
    nrow = R_PAIR * t_new
    lane_head = lax.broadcasted_iota(jnp.int32, (nrow, RET_QK), 1) // RET_DK
    row_seq = lax.broadcasted_iota(jnp.int32, (nrow, RET_QK), 0) // t_new
    out_seq = (lax.broadcasted_iota(jnp.int32, (N_RET_HEADS * nrow, RET_DV), 0) % nrow) // t_new
    for blk in range(blocks):
        rs = slice(blk * nrow, (blk + 1) * nrow)
        q8, k8, v8 = q_ref[rs, :], k_ref[rs, :], v_ref[rs, :]
        qm = jnp.concatenate([jnp.where(lane_head == h, q8, 0.0) for h in range(N_RET_HEADS)],
                             axis=0).astype(BF16)
        inner = (_dot_nt(qm, k8.astype(BF16)) * dec_ref[...]).astype(BF16)
        o_all = _dot(inner, v8.astype(BF16))
        vw = (v8 * wt_ref[...]).astype(BF16)
        cross = None
        for s in range(R_PAIR):
            state = st_ref[blk * R_PAIR + s]
            c = _dot(qm, state.astype(BF16))
            cross = c if cross is None else jnp.where(out_seq == s, c, cross)
            km = jnp.where(row_seq == s, k8, 0.0).astype(BF16)
            upd = _dot_tn(km, vw)
            upd = jnp.concatenate([upd[h * RET_DK:(h + 1) * RET_DK, h * RET_DV:(h + 1) * RET_DV]
                                   for h in range(N_RET_HEADS)], axis=0)
            so_ref[blk * R_PAIR + s] = gc_ref[...] * state + upd
        cross = cross * wh_ref[...]
        for h in range(N_RET_HEADS):
            hs = slice(h * RET_DV, (h + 1) * RET_DV)
            oh = o_all[h * nrow:(h + 1) * nrow, hs] + cross[h * nrow:(h + 1) * nrow]
            mu = jnp.mean(oh, axis=-1, keepdims=True)
            xc = oh - mu
            var = jnp.mean(xc * xc, axis=-1, keepdims=True)
            y = xc * lax.rsqrt(var + NORM_EPS) * gn_ref[:, hs]
            y_ref[rs, hs] = _silu(g_ref[rs, hs]) * y


def _ret_sample(qr, kr, vr, gr, gn, state, t_new, blocks=8):
    nb = state.shape[0]
    assert R_PAIR * t_new == 8 and nb % (R_PAIR * blocks) == 0
    rows = R_PAIR * t_new * blocks
    tabs = [jnp.asarray(a) for a in _sample_ret_tables(t_new)]
    st = state.reshape(nb, N_RET_HEADS * RET_DK, RET_DV)

    def blk(w_):
        return pl.BlockSpec((rows, w_), lambda i: (i, 0))

    st_spec = pl.BlockSpec((R_PAIR * blocks, N_RET_HEADS * RET_DK, RET_DV), lambda i: (i, 0, 0))
    y, s = pl.pallas_call(
        functools.partial(_sample_ret_kernel, blocks=blocks, t_new=t_new),
        grid=(nb // (R_PAIR * blocks),),
        in_specs=[blk(RET_QK), blk(RET_QK), blk(RET_V), blk(RET_V), _full_spec((1, RET_V)), st_spec]
        + [_full_spec(a.shape) for a in tabs],
        out_specs=[blk(RET_V), st_spec],
        out_shape=[jax.ShapeDtypeStruct((nb * t_new, RET_V), F32), jax.ShapeDtypeStruct(st.shape, F32)],
        compiler_params=_params(1), name="ret_sample")(qr, kr, vr, gr, gn, st, *tabs)
    return y, s.reshape(state.shape)


CONV_COLS = 256


def _ffn_conv_kernel(*refs, with_mix):
    if with_mix:
        (oa_ref, yr_ref, h_ref, wo_ref, gmb_ref, gpre_ref, gpost_ref, wg_ref, wu_ref, wd_ref,
         o_ref, wob_ref, wgb_ref, wub_ref, wdb_ref, x_s, u_s, acc_s) = refs
    else:
        (h_ref, gpre_ref, gpost_ref, wg_ref, wu_ref, wd_ref,
         o_ref, wgb_ref, wub_ref, wdb_ref, x_s, u_s, acc_s) = refs
    c = pl.program_id(0)

    @pl.when(c == 0)
    def _():
        x = h_ref[...]
        if with_mix:
            wo = wo_ref[...].astype(BF16)
            wob_ref[...] = wo
            mixed = jnp.concatenate([oa_ref[...].astype(BF16), yr_ref[...].astype(BF16)], axis=1)
            x = x + _rms(_dot(mixed, wo), gmb_ref[...])
        x_s[...] = x
        u_s[...] = _rms(x, gpre_ref[...]).astype(BF16)
        acc_s[...] = jnp.zeros_like(acc_s)

    wg, wu, wd = wg_ref[...].astype(BF16), wu_ref[...].astype(BF16), wd_ref[...].astype(BF16)
    wgb_ref[...] = wg
    wub_ref[...] = wu
    wdb_ref[...] = wd
    u = u_s[...]
    acc_s[...] += _dot((_silu(_dot(u, wg)) * _dot(u, wu)).astype(BF16), wd)

    @pl.when(c == pl.num_programs(0) - 1)
    def _():
        o_ref[...] = x_s[...] + 0.5 * _rms(acc_s[...], gpost_ref[...])


def _ffn_convert(h, gpre, gpost, wg, wu, wd, mix=None):
    t = h.shape[0]
    col = pl.BlockSpec((D_MODEL, CONV_COLS), lambda c: (0, c))
    rowb = pl.BlockSpec((CONV_COLS, D_MODEL), lambda c: (c, 0))
    vec = _full_spec((1, D_MODEL))
    ins, in_specs, outs, out_specs = [], [], [], []
    if mix is not None:
        att, ret, wo, gmb = mix
        ins += [att, ret]
        in_specs += [_full_spec((t, ATT_WIDTH)), _full_spec((t, RET_V))]
    ins.append(h)
    in_specs.append(_full_spec((t, D_MODEL)))
    if mix is not None:
        ins += [wo, gmb]
        in_specs += [_full_spec((D_MODEL, D_MODEL)), vec]
    ins += [gpre, gpost, wg, wu, wd]
    in_specs += [vec, vec, col, col, rowb]
    out_shape = [jax.ShapeDtypeStruct((t, D_MODEL), F32)]
    out_specs = [_full_spec((t, D_MODEL))]
    if mix is not None:
        out_shape.append(jax.ShapeDtypeStruct((D_MODEL, D_MODEL), BF16))
        out_specs.append(_full_spec((D_MODEL, D_MODEL)))
    out_shape += [jax.ShapeDtypeStruct((D_MODEL, D_FF), BF16)] * 2 + [jax.ShapeDtypeStruct((D_FF, D_MODEL), BF16)]
    out_specs += [col, col, rowb]
    return pl.pallas_call(
        functools.partial(_ffn_conv_kernel, with_mix=mix is not None), grid=(D_FF // CONV_COLS,),
        in_specs=in_specs, out_specs=out_specs, out_shape=out_shape,
        scratch_shapes=[pltpu.VMEM((t, D_MODEL), F32), pltpu.VMEM((t, D_MODEL), BF16),
                        pltpu.VMEM((t, D_MODEL), F32)],
        compiler_params=_params(1), name="ffn_convert")(*ins)


def _mixin_sample_kernel(h_ref, g_ref, w_ref, ba_ref, oa_ref, br_ref, or_ref,
                         qa_ref, kk_ref, vk_ref, qr_ref, kr_ref, vr_ref, gr_ref, wb_ref,
                         ta_s, tr_s, u_s):
    c = pl.program_id(0)

    @pl.when(c == 0)
    def _():
        _rope_fill(ba_ref, oa_ref, ta_s)
        _rope_fill(br_ref, or_ref, tr_s)
        u_s[...] = _rms(h_ref[...], g_ref[...]).astype(BF16)

    w = w_ref[...].astype(BF16)
    wb_ref[...] = w
    y = _dot(u_s[...], w)
    nblk = h_ref.shape[0] // ROPE_ROWS

    def rotary(tab_s, half, cols, dst_ref, scale):
        for rb in range(nblk):
            rows = slice(rb * ROPE_ROWS, (rb + 1) * ROPE_ROWS)
            rope = _rope_rows(tab_s, half, rows)
            for s in range(dst_ref.shape[1] // LANES):
                dst_ref[rows, s * LANES:(s + 1) * LANES] = rope(
                    y[rows, cols + s * LANES:cols + (s + 1) * LANES]) * scale

    @pl.when(c == 0)
    def _():
        rotary(ta_s, ROT_DIM // 2, 0, qa_ref, HEAD_DIM ** -0.5 * LOG2_E)

    @pl.when(c == 1)
    def _():
        rotary(ta_s, ROT_DIM // 2, 0, kk_ref, 1.0)

    @pl.when(c == 2)
    def _():
        vk_ref[...] = y

    @pl.when(c == 3)
    def _():
        rotary(tr_s, RET_DK // 2, 0, qr_ref, 1.0)
        rotary(tr_s, RET_DK // 2, RET_QK, kr_ref, RET_DK ** -0.5)

    @pl.when(c == 4)
    def _():
        vr_ref[...] = y

    @pl.when(c == 5)
    def _():
        gr_ref[...] = y


def _mixin_sample(h, g, w, base_pos, off_pos):
    t = h.shape[0]
    grp = ATT_WIDTH
    assert IN_WIDTH == 6 * grp and 2 * RET_QK == grp and RET_V == grp
    base_a, off_a = _rope_consts(base_pos, off_pos, ROT_DIM, ROPE_THETA, HEAD_DIM)
    base_r, off_r = _rope_consts(base_pos, off_pos, RET_DK, RET_THETA, RET_DK)
    wcol = pl.BlockSpec((D_MODEL, grp), lambda c: (0, c))
    base_spec = pl.BlockSpec((None, 2, LANES), lambda c: (0, 0, 0))
    widths = [ATT_WIDTH, ATT_WIDTH, ATT_WIDTH, RET_QK, RET_QK, RET_V, RET_V]
    tab = pltpu.VMEM((3, t, LANES), F32)
    return pl.pallas_call(
        _mixin_sample_kernel, grid=(IN_WIDTH // grp,),
        in_specs=[_full_spec((t, D_MODEL)), _full_spec((1, D_MODEL)), wcol,
                  base_spec, _full_spec(off_a.shape), base_spec, _full_spec(off_r.shape)],
        out_specs=[_full_spec((t, w_)) for w_ in widths] + [wcol],
        out_shape=[jax.ShapeDtypeStruct((t, w_), F32) for w_ in widths]
        + [jax.ShapeDtypeStruct((D_MODEL, IN_WIDTH), BF16)],
        scratch_shapes=[tab, tab, pltpu.VMEM((t, D_MODEL), BF16)],
        compiler_params=_params(1), name="mix_in_sample")(h, g, w, base_a, off_a, base_r, off_r)


def kernel(x_prompt, x_sample, cache_k, cache_v, state_ret, g_ffn1_pre, g_ffn1_post, w1_gate, w1_up,
           w1_down, g_mix_pre, g_mix_post, w_in, gn_w, w_out, g_ffn2_pre, g_ffn2_post, w2_gate, w2_up,
           w2_down):
    b_p, s_p, _ = x_prompt.shape
    b_s, t_s, _ = x_sample.shape
    depth = w_in.shape[0]
    assert depth == 1 and b_p == 1
    keep = min(WIN_MAX, s_p)
    l = 0
    vec = lambda g: g[l].reshape(1, -1)
    g1a, g1b, gma, gmb = vec(g_ffn1_pre), vec(g_ffn1_post), vec(g_mix_pre), vec(g_mix_post)
    g2a, g2b, gn = vec(g_ffn2_pre), vec(g_ffn2_post), vec(gn_w)

    n_s = b_s * t_s
    xs = x_sample.reshape(n_s, D_MODEL)
    h1s, *w1 = _ffn_convert(xs, g1a, g1b, w1_gate[l], w1_up[l], w1_down[l])
    qas, kks, vks, qrs, krs, vrs, grs, wi = _mixin_sample(
        h1s, gma, w_in[l], np.full((1,), PAST_LEN), np.tile(np.arange(t_s), b_s))

    tm = RUN
    xp = x_prompt.reshape(s_p, D_MODEL)
    h1, o_att_s = _ffn_with_sample_attn(xp, g1a, g1b, *w1, qas, kks, vks, cache_k[l], cache_v[l],
                                        t_s, PAST_LEN, tm)
    q4, k4, v4, q16, k16, v16, kk, vk, y_ret, st_p = _mixin(
        h1, gma, wi, 2 * tm * np.arange(s_p // (2 * tm)), np.arange(2 * tm), keep, 2 * tm, gn)
    o_att = _attn_prompt(q4, k4, v4, q16, k16, v16)

    y_ret_s, st_s = _ret_sample(qrs, krs, vrs, grs, gn, state_ret[l], t_s)
    y_sample, wo, *w2 = _ffn_convert(h1s, g2a, g2b, w2_gate[l], w2_up[l], w2_down[l],
                                     mix=(o_att_s, y_ret_s, w_out[l], gmb))
    y_prompt = _mixout_ffn(o_att, y_ret, h1, wo, gmb, g2a, g2b, *w2, 2 * tm)

    hd = (N_ATT_HEADS, HEAD_DIM)
    return (y_prompt.reshape(b_p, s_p, D_MODEL),
            y_sample.reshape(b_s, t_s, D_MODEL),
            kk.reshape(depth, b_p, keep, *hd),
            vk.reshape(depth, b_p, keep, *hd),
            st_p.reshape(depth, b_p, N_RET_HEADS, RET_DK, RET_DV),
            kks.reshape(depth, b_s, t_s, *hd),
            vks.reshape(depth, b_s, t_s, *hd),
            st_s.reshape(depth, b_s, N_RET_HEADS, RET_DK, RET_DV))
```

```python
import functools
import math

import numpy as np
import jax
import jax.numpy as jnp
from jax import lax
from jax.experimental import pallas as pl
from jax.experimental.pallas import tpu as pltpu

F32 = jnp.float32
BF16 = jnp.bfloat16

D_MODEL = 1024
D_FF = 2816
HEAD_DIM = 64
N_ATT_HEADS = 8
ATT_WIDTH = N_ATT_HEADS * HEAD_DIM
ROT_DIM = HEAD_DIM // 4
ROPE_THETA = 500000.0
WIN_MAX = 2048
PAST_LEN = 8192
BRANCHES = ((128, 1), (512, 4), (2048, 16))
N_RET_HEADS = 4
RET_DK = 64
RET_DV = 128
RET_QK = N_RET_HEADS * RET_DK
RET_V = N_RET_HEADS * RET_DV
RET_THETA = 10000.0
RET_CHUNK = 128
IN_WIDTH = 3 * ATT_WIDTH + 2 * RET_QK + 2 * RET_V
NORM_EPS = 1e-6
NEG = -1e30

LANES = 128
QB = 128
SUPER = 2048
RUN = 512
ATT_RUNS = 4
VMEM_LIMIT = 56 * 1024 * 1024
MIXIN_VMEM_LIMIT = 60 * 1024 * 1024
LOG_G = tuple(math.log1p(-2.0 ** (-5.0 - h)) for h in range(N_RET_HEADS))
LOG2_E = math.log2(math.e)


def _full_spec(shape):
    nd = len(shape)
    return pl.BlockSpec(shape, lambda *_: (0,) * nd)


def _resident_spec(shape):
    nd = len(shape)
    return pl.BlockSpec(shape, lambda *_: (0,) * nd, pipeline_mode=pl.Buffered(1))


def _params(n_axes, vmem=VMEM_LIMIT):
    return pltpu.CompilerParams(dimension_semantics=("arbitrary",) * n_axes, vmem_limit_bytes=vmem)


def _rms(x, g):
    return x * lax.rsqrt(jnp.mean(x * x, axis=-1, keepdims=True) + NORM_EPS) * g


def _silu(x):
    return x / (1.0 + jnp.exp(-x))


def _dot(a, b):
    return jnp.dot(a, b, preferred_element_type=F32)


def _dot_nt(a, b):
    return lax.dot_general(a, b, (((1,), (1,)), ((), ())), preferred_element_type=F32)


def _dot_tn(a, b):
    return lax.dot_general(a, b, (((0,), (0,)), ((), ())), preferred_element_type=F32)


FF_CHUNK = 512


def _swiglu(u, wg_ref, wu_ref, wd_ref):
    acc = None
    for c0 in range(0, D_FF, FF_CHUNK):
        c1 = min(c0 + FF_CHUNK, D_FF)
        g = _dot(u, wg_ref[:, c0:c1])
        up = _dot(u, wu_ref[:, c0:c1])
        h = (_silu(g) * up).astype(BF16)
        d = _dot(h, wd_ref[c0:c1, :])
        acc = d if acc is None else acc + d
    return acc


def _mixout_ffn_kernel(oa_ref, yr_ref, h_ref, wo_ref, gmb_ref, gpre_ref, gpost_ref,
                       wg_ref, wu_ref, wd_ref, o_ref):
    mixed = jnp.concatenate([oa_ref[...].astype(BF16), yr_ref[...].astype(BF16)], axis=1)
    x = h_ref[...] + _rms(_dot(mixed, wo_ref[...]), gmb_ref[...])
    u = _rms(x, gpre_ref[...]).astype(BF16)
    y = _swiglu(u, wg_ref, wu_ref, wd_ref)
    o_ref[...] = x + 0.5 * _rms(y, gpost_ref[...])


def _ffn_specs():
    return [_full_spec((1, D_MODEL)), _full_spec((1, D_MODEL)),
            _resident_spec((D_MODEL, D_FF)), _resident_spec((D_MODEL, D_FF)),
            _resident_spec((D_FF, D_MODEL))]


def _mixout_ffn(oa, yr, h, wo, gmb, gpre, gpost, wg, wu, wd, tm):
    t = h.shape[0]
    row = pl.BlockSpec((tm, D_MODEL), lambda i: (i, 0))
    half = pl.BlockSpec((tm, ATT_WIDTH), lambda i: (i, 0))
    return pl.pallas_call(
        _mixout_ffn_kernel, grid=(t // tm,),
        in_specs=[half, half, row, _resident_spec((D_MODEL, D_MODEL)), _full_spec((1, D_MODEL))]
        + _ffn_specs(),
        out_specs=row, out_shape=jax.ShapeDtypeStruct((t, D_MODEL), F32),
        compiler_params=_params(1), name="mixout_ffn")(oa, yr, h, wo, gmb, gpre, gpost, wg, wu, wd)


def _rope_consts(base_pos, off_pos, rot_dim, theta, head_dim):
    half = rot_dim // 2
    inv = theta ** (-np.arange(half, dtype=np.float64) * (2.0 / rot_dim))
    lane = np.arange(LANES) % head_dim
    inv_l = inv[lane % half][None, :]
    first = (lane < half).astype(np.float64)[None, :]
    second = ((lane >= half) & (lane < rot_dim)).astype(np.float64)[None, :]
    rot = first + second
    a = np.asarray(base_pos, np.float64)[:, None] * inv_l
    b = np.asarray(off_pos, np.float64)[:, None] * inv_l
    cb, sb = np.cos(b), np.sin(b)
    base = np.stack([np.cos(a), np.sin(a)], axis=1)
    off = np.stack([cb * rot, sb * rot, np.broadcast_to(1.0 - rot, cb.shape),
                    -cb * first, -sb * first, cb * second, sb * second])
    return jnp.asarray(base, F32), jnp.asarray(off, F32)


def _rope_fill(base_ref, off_ref, tab_s):
    ca = base_ref[0:1, :]
    sa = base_ref[1:2, :]
    tab_s[0] = ca * off_ref[0] - sa * off_ref[1] + off_ref[2]
    tab_s[1] = sa * off_ref[3] + ca * off_ref[4]
    tab_s[2] = sa * off_ref[5] + ca * off_ref[6]


ROPE_ROWS = 64


def _rope_rows(tab_ref, half, rows):
    c, sa, sb = tab_ref[0, rows, :], tab_ref[1, rows, :], tab_ref[2, rows, :]
    return lambda x: x * c + pltpu.roll(x, LANES - half, 1) * sa + pltpu.roll(x, half, 1) * sb


def _lanes(r, s):
    return slice(r * ATT_WIDTH + s * LANES, r * ATT_WIDTH + (s + 1) * LANES)


def _emit_dilated(nat_s, x4_s, o4_ref, o16_ref):
    l16 = RUN // 16
    for w in range(nat_s.shape[1] // RUN):
        for s in range(ATT_WIDTH // LANES):
            for r in range(4):
                x4 = nat_s[s, pl.ds(w * RUN + r, RUN // 4, stride=4), :]
                o4_ref[w, :, _lanes(r, s)] = x4.astype(BF16)
                x4_s[r] = x4
            for r in range(4):
                for c in range(4):
                    o16_ref[w * l16:(w + 1) * l16, _lanes(4 * c + r, s)] = (
                        x4_s[r, pl.ds(c, l16, stride=4), :].astype(BF16))


def _mixin_kernel(h_ref, g_ref, w_ref, ba_ref, oa_ref, br_ref, or_ref, gn_ref,
                  q4_ref, k4_ref, v4_ref, q16_ref, k16_ref, v16_ref, kk_ref, vk_ref, y_ref, st_ref,
                  ta_ref, tr_ref, nat_s, natk_s, x4_s, qr_ref, kr_ref, vr_ref, gr_ref,
                  dec_s, wt_s, wh_s, st_s, *, first_keep):
    _ret_init(dec_s, wt_s, wh_s, st_s)
    _rope_fill(ba_ref, oa_ref, ta_ref)
    _rope_fill(br_ref, or_ref, tr_ref)
    u = _rms(h_ref[...], g_ref[...]).astype(BF16)
    scale = HEAD_DIM ** -0.5 * LOG2_E
    nslab = ATT_WIDTH // LANES

    q = _dot(u, w_ref[:, 0:ATT_WIDTH])
    k = _dot(u, w_ref[:, ATT_WIDTH:2 * ATT_WIDTH])
    for rb in range(h_ref.shape[0] // ROPE_ROWS):
        rows = slice(rb * ROPE_ROWS, (rb + 1) * ROPE_ROWS)
        rope = _rope_rows(ta_ref, ROT_DIM // 2, rows)
        for s in range(nslab):
            sl = slice(s * LANES, (s + 1) * LANES)
            qs = rope(q[rows, sl]) * scale
            ks = rope(k[rows, sl])
            nat_s[s, rows, :] = qs
            natk_s[s, rows, :] = ks
    _emit_dilated(nat_s, x4_s, q4_ref, q16_ref)
    _emit_dilated(natk_s, x4_s, k4_ref, k16_ref)

    v = _dot(u, w_ref[:, 2 * ATT_WIDTH:3 * ATT_WIDTH])
    for s in range(nslab):
        nat_s[s] = v[:, s * LANES:(s + 1) * LANES]
    _emit_dilated(nat_s, x4_s, v4_ref, v16_ref)

    o = 3 * ATT_WIDTH
    qk = _dot(u, w_ref[:, o:o + 2 * RET_QK])
    for rb in range(h_ref.shape[0] // ROPE_ROWS):
        rows = slice(rb * ROPE_ROWS, (rb + 1) * ROPE_ROWS)
        rope = _rope_rows(tr_ref, RET_DK // 2, rows)
        for s in range(RET_QK // LANES):
            sl = slice(s * LANES, (s + 1) * LANES)
            sk = slice(RET_QK + s * LANES, RET_QK + (s + 1) * LANES)
            qr_ref[rows, sl] = rope(qk[rows, sl]).astype(qr_ref.dtype)
            kr_ref[rows, sl] = (rope(qk[rows, sk]) * (RET_DK ** -0.5)).astype(kr_ref.dtype)
    o += 2 * RET_QK
    vr_ref[...] = _dot(u, w_ref[:, o:o + RET_V]).astype(vr_ref.dtype)
    o += RET_V
    gr_ref[...] = _dot(u, w_ref[:, o:o + RET_V])
    _ret_body(qr_ref, kr_ref, vr_ref, gr_ref, gn_ref, y_ref, st_ref, dec_s, wt_s, wh_s, st_s,
              qr_ref.shape[0] // RET_CHUNK)

    @pl.when(pl.program_id(0) >= first_keep)
    def _():
        for s in range(nslab):
            kk_ref[:, s * LANES:(s + 1) * LANES] = natk_s[s]
            vk_ref[:, s * LANES:(s + 1) * LANES] = nat_s[s]


def _mixin(h, g, w, base_pos, off_pos, keep_rows, tm, gn):
    t = h.shape[0]
    nt = t // tm
    assert tm % RUN == 0 and SUPER % tm == 0 and t % SUPER == 0 and keep_rows % tm == 0
    first_keep = nt - keep_rows // tm
    base_a, off_a = _rope_consts(base_pos, off_pos, ROT_DIM, ROPE_THETA, HEAD_DIM)
    base_r, off_r = _rope_consts(base_pos, off_pos, RET_DK, RET_THETA, RET_DK)

    def row(w_):
        return pl.BlockSpec((tm, w_), lambda i: (i, 0))

    keep_spec = pl.BlockSpec((tm, ATT_WIDTH), lambda i: (jnp.maximum(i - first_keep, 0), 0))
    keep_shape = ((keep_rows, ATT_WIDTH), F32)
    base_spec = pl.BlockSpec((None, 2, LANES), lambda i: (i, 0, 0))
    off_spec = _resident_spec((7, tm, LANES))
    tab = pltpu.VMEM((3, tm, LANES), F32)
    in_specs = [row(D_MODEL), _full_spec((1, D_MODEL)), _resident_spec((D_MODEL, IN_WIDTH)),
                base_spec, off_spec, base_spec, off_spec, _full_spec((1, RET_V))]
    tps = SUPER // tm
    s4 = pl.BlockSpec((tm // RUN, RUN // 4, 4 * ATT_WIDTH), lambda i: (i, 0, 0))
    s16 = pl.BlockSpec((None, tm // 16, 16 * ATT_WIDTH), lambda i: (i // tps, i % tps, 0))
    st_shape = (N_RET_HEADS // 2, 2 * RET_DK, RET_DV)
    specs = [s4] * 3 + [s16] * 3 + [keep_spec, keep_spec, row(RET_V), _full_spec(st_shape)]
    shapes = ([((t // RUN, RUN // 4, 4 * ATT_WIDTH), BF16)] * 3
              + [((t // SUPER, SUPER // 16, 16 * ATT_WIDTH), BF16)] * 3
              + [keep_shape, keep_shape, ((t, RET_V), BF16), (st_shape, F32)])
    dec = pltpu.VMEM((N_RET_HEADS, RET_CHUNK, RET_CHUNK), F32)
    nat = pltpu.VMEM((ATT_WIDTH // LANES, tm, LANES), F32)
    scratch = [tab, tab, nat, nat, pltpu.VMEM((4, RUN // 4, LANES), F32),
               pltpu.VMEM((tm, RET_QK), BF16), pltpu.VMEM((tm, RET_QK), BF16),
               pltpu.VMEM((tm, RET_V), BF16), pltpu.VMEM((tm, RET_V), F32),
               dec, dec, dec, pltpu.VMEM(st_shape, F32)]
    return pl.pallas_call(
        functools.partial(_mixin_kernel, first_keep=first_keep), grid=(nt,),
        in_specs=in_specs, out_specs=specs,
        out_shape=[jax.ShapeDtypeStruct(s, d) for s, d in shapes],
        scratch_shapes=scratch, compiler_params=_params(1, MIXIN_VMEM_LIMIT), name="mix_in")(
            h, g, w, base_a, off_a, base_r, off_r, gn)


def _attn_bias_tables():
    a = np.arange(QB)[:, None]
    c = np.arange(2 * QB)[None, :]
    steps = BRANCHES[0][0]
    dist = QB + a - c
    band = (dist >= 0) & (dist <= steps)
    cur = c >= QB
    tok_q = 4 * (a % 32) + a // 32
    cc = c % QB
    tok_k = 4 * (cc % 32) + cc // 32 + QB * (c // QB) - QB
    dist2 = tok_q - tok_k
    band2 = (dist2 >= 0) & (dist2 <= steps)
    masks = np.stack([band, band & cur, band2, band2 & cur])
    return np.where(masks, 0.0, NEG).astype(np.float32)


def _attn_qblock(get_q, get_kc, get_kp, get_vc, get_vp, bias, consts, sink):
    head_a, head_a_win, ones_a, ones_b = consts
    for s in range(ATT_WIDTH // LANES):
        q2 = get_q(s)
        kwin = jnp.concatenate([get_kp(s), get_kc(s)], axis=0)
        vwin = jnp.concatenate([get_vp(s), get_vc(s)], axis=0)
        zq = jnp.zeros_like(q2)
        qq = jnp.concatenate([jnp.where(head_a, q2, zq), jnp.where(head_a, zq, q2)], axis=0)
        sc = _dot_nt(qq, kwin)
        s_a = sc[:QB] + bias
        s_b = sc[QB:] + bias
        m_a = jnp.max(s_a, axis=-1, keepdims=True)
        m_b = jnp.max(s_b, axis=-1, keepdims=True)
        p = jnp.concatenate([jnp.exp2(s_a - m_a).astype(BF16), jnp.exp2(s_b - m_b).astype(BF16)],
                            axis=1)
        zv = jnp.zeros_like(vwin)
        w = jnp.concatenate(
            [jnp.concatenate([jnp.where(head_a_win, vwin, zv), ones_a], axis=1),
             jnp.concatenate([jnp.where(head_a_win, zv, vwin), ones_b], axis=1)], axis=0)
        r = _dot(p, w)
        sink(s, (jnp.where(head_a, m_a, m_b), r[:, LANES:], r[:, :LANES]))


def _merge(old, new):
    mo, lo, ao = old
    m2, l2, a2 = new
    mn = jnp.maximum(mo, m2)
    eo = jnp.exp2(mo - mn)
    e2 = jnp.exp2(m2 - mn)
    return mn, eo * lo + e2 * l2, eo * ao + e2 * a2


def _attn_kernel(q16, k16c, k16p, v16c, v16p, q4, k4c, k4p, v4c, v4p, bias_ref, o_ref,
                 m_s, l_s, a_s, nat_s):
    sb = pl.program_id(0)
    ph = pl.program_id(1)
    j = pl.program_id(2)
    lane = lax.broadcasted_iota(jnp.int32, (QB, LANES), 1)
    head_a = lane < HEAD_DIM
    lane_w = lax.broadcasted_iota(jnp.int32, (2 * QB, LANES), 1)
    head_a_win = lane_w < HEAD_DIM
    ones_a = jnp.where(head_a_win, 1.0, 0.0).astype(BF16)
    ones_b = jnp.where(head_a_win, 0.0, 1.0).astype(BF16)
    consts = (head_a, head_a_win, ones_a, ones_b)
    nslab = ATT_WIDTH // LANES
    sub = QB // 4
    lanes = _lanes

    run0 = j * ATT_RUNS
    first_run = sb * (SUPER // RUN) + run0 == 0

    @pl.when(ph == 0)
    def _():
        bias = bias_ref[(sb == 0).astype(jnp.int32)]
        for r in range(4 * ATT_RUNS):
            def scatter(s, res):
                for u in range(SUPER // RUN):
                    rows = pl.ds(u * RUN + (r % 4) * QB + run0 + r // 4, sub, stride=4)
                    for ref, val in zip((m_s, l_s, a_s), res):
                        ref[s, rows, :] = val[u * sub:(u + 1) * sub]

            _attn_qblock(lambda s: q16[:, lanes(r, s)], lambda s: k16c[:, lanes(r, s)],
                         lambda s: k16p[:, lanes(r, s)], lambda s: v16c[:, lanes(r, s)],
                         lambda s: v16p[:, lanes(r, s)], bias, consts, scatter)

    def prev_rows(ref_c, ref_p, w, rows):
        return (lambda cols: ref_p[rows, cols]) if w == 0 else (lambda cols: ref_c[w - 1, rows, cols])

    @pl.when(ph == 1)
    def _():
        for w in range(ATT_RUNS):
            bias = bias_ref[first_run.astype(jnp.int32)] if w == 0 else bias_ref[0]
            kp = prev_rows(k4c, k4p, w, slice(None))
            vp = prev_rows(v4c, v4p, w, slice(None))
            for r in range(4):
                rows = pl.ds(pl.multiple_of((run0 + w) * RUN + r * QB, QB), QB)

                def merge_in(s, res):
                    mn, ln, an = _merge((m_s[s, rows, :], l_s[s, rows, :], a_s[s, rows, :]), res)
                    m_s[s, rows, :] = mn
                    l_s[s, rows, :] = ln
                    a_s[s, rows, :] = an

                _attn_qblock(lambda s: q4[w, :, lanes(r, s)], lambda s: k4c[w, :, lanes(r, s)],
                             lambda s: kp(lanes(r, s)), lambda s: v4c[w, :, lanes(r, s)],
                             lambda s: vp(lanes(r, s)), bias, consts, merge_in)

    @pl.when(ph == 2)
    def _():
        for w in range(ATT_RUNS):
            for b in range(4):
                def cur(ref):
                    return lambda s: jnp.concatenate(
                        [ref[w, b * sub:(b + 1) * sub, lanes(r, s)] for r in range(4)], axis=0)

                def prev(ref_c, ref_p):
                    if b > 0:
                        return lambda s: jnp.concatenate(
                            [ref_c[w, (b - 1) * sub:b * sub, lanes(r, s)] for r in range(4)], axis=0)
                    tail = prev_rows(ref_c, ref_p, w, slice(QB - sub, QB))
                    return lambda s: jnp.concatenate([tail(lanes(r, s)) for r in range(4)], axis=0)

                if w == 0 and b == 0:
                    bias = bias_ref[2 + first_run.astype(jnp.int32)]
                else:
                    bias = bias_ref[2]

                def finish(s, res):
                    for r in range(4):
                        rows = pl.ds(pl.multiple_of((run0 + w) * RUN + r * QB + b * sub, sub), sub)
                        part = tuple(x[r * sub:(r + 1) * sub] for x in res)
                        _, ln, an = _merge((m_s[s, rows, :], l_s[s, rows, :], a_s[s, rows, :]), part)
                        nat_s[s, pl.ds(w * RUN + b * QB + r, sub, stride=4), :] = an / ln

                _attn_qblock(cur(q4), cur(k4c), prev(k4c, k4p), cur(v4c), prev(v4c, v4p),
                             bias, consts, finish)
        for s in range(nslab):
            o_ref[:, s * LANES:(s + 1) * LANES] = nat_s[s].astype(o_ref.dtype)


def _attn_prompt(q4, k4, v4, q16, k16, v16):
    nsb = q16.shape[0]
    s_len = nsb * SUPER
    steps = SUPER // RUN // ATT_RUNS
    wide = 4 * ATT_WIDTH
    blk16 = (None, QB, ATT_RUNS * wide)
    blk4 = (ATT_RUNS, QB, wide)
    blk4p = (None, QB, wide)

    def j16(ph, j):
        return jnp.where(ph == 0, j, steps - 1)

    def n4(sb, ph, j):
        return sb * steps + jnp.where(ph == 0, 0, j)

    c16 = pl.BlockSpec(blk16, lambda sb, ph, j: (sb, 0, j16(ph, j)))
    p16 = pl.BlockSpec(blk16, lambda sb, ph, j: (jnp.maximum(sb - 1, 0), 0, j16(ph, j)))
    c4 = pl.BlockSpec(blk4, lambda sb, ph, j: (n4(sb, ph, j), 0, 0))
    p4 = pl.BlockSpec(blk4p, lambda sb, ph, j: (jnp.maximum(ATT_RUNS * n4(sb, ph, j) - 1, 0), 0, 0))
    out = pl.BlockSpec((ATT_RUNS * RUN, ATT_WIDTH),
                       lambda sb, ph, j: (sb * steps + jnp.where(ph == 2, j, 0), 0))
    bias = jnp.asarray(_attn_bias_tables())
    nslab = ATT_WIDTH // LANES
    scratch = ([pltpu.VMEM((nslab, SUPER, LANES), F32) for _ in range(3)]
               + [pltpu.VMEM((nslab, ATT_RUNS * RUN, LANES), F32)])
    return pl.pallas_call(
        _attn_kernel, grid=(nsb, 3, steps),
        in_specs=[c16, c16, p16, c16, p16, c4, c4, p4, c4, p4, _full_spec(bias.shape)],
        out_specs=out, out_shape=jax.ShapeDtypeStruct((s_len, ATT_WIDTH), BF16),
        scratch_shapes=scratch, compiler_params=_params(3), name="attn_prompt")(
            q16, k16, k16, v16, v16, q4, k4, k4, v4, v4, bias)


def _ret_init(dec_s, wt_s, wh_s, st_s):
    c = RET_CHUNK

    @pl.when(pl.program_id(0) == 0)
    def _():
        i = lax.broadcasted_iota(jnp.int32, (c, c), 0)
        jj = lax.broadcasted_iota(jnp.int32, (c, c), 1)
        diff = (i - jj).astype(F32)
        for h in range(N_RET_HEADS):
            dec_s[h] = jnp.where(diff >= 0, jnp.exp(diff * LOG_G[h]), 0.0)
            wt_s[h] = jnp.exp((c - 1.0 - i.astype(F32)) * LOG_G[h])
            wh_s[h] = jnp.exp((i.astype(F32) + 1.0) * LOG_G[h])
        st_s[...] = jnp.zeros_like(st_s)


def _ret_body(qr_ref, kr_ref, vr_ref, gr_ref, gn_ref, y_ref, s_ref, dec_s, wt_s, wh_s, st_s, chunks):
    c = RET_CHUNK
    lane = lax.broadcasted_iota(jnp.int32, (c, LANES), 1)
    head_a = lane < RET_DK
    row_a = lax.broadcasted_iota(jnp.int32, (LANES, RET_DV), 0) < RET_DK
    for ci in range(chunks):
        rows = slice(ci * c, (ci + 1) * c)
        for p in range(N_RET_HEADS // 2):
            sl = slice(p * LANES, (p + 1) * LANES)
            q2 = qr_ref[rows, sl]
            k2 = kr_ref[rows, sl]
            zq = jnp.zeros_like(q2)
            qq = jnp.concatenate([jnp.where(head_a, q2, zq), jnp.where(head_a, zq, q2)], axis=0)
            inner = _dot_nt(qq, k2)
            state = st_s[p]
            cross = _dot(qq, state.astype(BF16))
            vws = []
            for hh in range(2):
                h = 2 * p + hh
                hs = slice(h * RET_DV, (h + 1) * RET_DV)
                v = vr_ref[rows, hs]
                inn = (inner[hh * c:(hh + 1) * c] * dec_s[h]).astype(BF16)
                o = _dot(inn, v) + cross[hh * c:(hh + 1) * c] * wh_s[h]
                mu = jnp.mean(o, axis=-1, keepdims=True)
                xc = o - mu
                var = jnp.mean(xc * xc, axis=-1, keepdims=True)
                y = xc * lax.rsqrt(var + NORM_EPS) * gn_ref[:, hs]
                y_ref[rows, hs] = (_silu(gr_ref[rows, hs]) * y).astype(y_ref.dtype)
                vws.append((v.astype(F32) * wt_s[h]).astype(BF16))
            upd = _dot_tn(k2, jnp.concatenate(vws, axis=1))
            upd = jnp.where(row_a, upd[:, :RET_DV], upd[:, RET_DV:])
            gch = jnp.where(row_a, math.exp(c * LOG_G[2 * p]), math.exp(c * LOG_G[2 * p + 1]))
            st_s[p] = gch * state + upd

    @pl.when(pl.program_id(0) == pl.num_programs(0) - 1)
    def _():
        s_ref[...] = st_s[...]


S_ROWS = 8
S_HGRP = 4


def _sample_tables(t_new, past_len, wb):
    rows = np.concatenate([np.arange(wb), wb + np.arange(S_ROWS)])
    tab = np.zeros((S_ROWS, wb + S_ROWS), np.float32)
    for r in range(S_ROWS):
        i = r % t_new
        delta = wb + i - rows
        ok = (delta >= 0) & (past_len + i - delta >= 0) & (rows < wb + t_new)
        for window, dil in BRANCHES:
            tab[r] += ok & (delta % dil == 0) & (delta <= window)
    tab = np.tile(tab, (S_HGRP, 1))
    return tab[:, :wb], tab[:, wb:]


def _sample_attn_body(q_ref, kn_ref, vn_ref, kt_ref, vt_ref, cc_ref, cn_ref, o_ref):
    mult_c, mult_n = cc_ref[...], cn_ref[...]
    bias_c = jnp.where(mult_c > 0, 0.0, NEG)
    bias_n = jnp.where(mult_n > 0, 0.0, NEG)
    gw = S_HGRP * HEAD_DIM
    lane_head = lax.broadcasted_iota(jnp.int32, (S_ROWS, gw), 1) // HEAD_DIM
    for g in range(N_ATT_HEADS // S_HGRP):
        gs = slice(g * gw, (g + 1) * gw)
        q8 = q_ref[:, gs]
        q = jnp.concatenate([jnp.where(lane_head == h, q8, 0.0) for h in range(S_HGRP)],
                            axis=0).astype(BF16)
        s_c = _dot(q, kt_ref[g].astype(BF16)) + bias_c
        s_n = _dot_nt(q, kn_ref[:, gs].astype(BF16)) + bias_n
        m = jnp.maximum(jnp.max(s_c, axis=-1, keepdims=True), jnp.max(s_n, axis=-1, keepdims=True))
        p_c = jnp.exp2(s_c - m) * mult_c
        p_n = jnp.exp2(s_n - m) * mult_n
        l = jnp.sum(p_c, axis=-1, keepdims=True) + jnp.sum(p_n, axis=-1, keepdims=True)
        o = (_dot_nt(p_c.astype(BF16), vt_ref[g].astype(BF16))
             + _dot(p_n.astype(BF16), vn_ref[:, gs].astype(BF16))) / l
        o8 = jnp.zeros((S_ROWS, gw), F32)
        for h in range(S_HGRP):
            o8 = jnp.where(lane_head == h, o[h * S_ROWS:(h + 1) * S_ROWS], o8)
        o_ref[:, gs] = o8


FF_SPLITS = ((0, 768), (768, 1536), (1536, 2304), (2304, D_FF))


CACHE_RING = 3


def _ffn_attn_kernel(x_ref, gpre_ref, gpost_ref, wg_ref, wu_ref, wd_ref,
                     q_ref, kn_ref, vn_ref, kt_hbm, vt_hbm, cc_ref, cn_ref,
                     o_ref, oa_ref, u_s, acc_s, kbuf, vbuf, sem):
    j = pl.program_id(1)
    nsplit = len(FF_SPLITS)
    t = pl.program_id(0) * nsplit + j
    nseq = pl.num_programs(0) * nsplit
    ahead = CACHE_RING - 1

    def copies(seq, slot):
        return (pltpu.make_async_copy(kt_hbm.at[seq], kbuf.at[slot], sem.at[0, slot]),
                pltpu.make_async_copy(vt_hbm.at[seq], vbuf.at[slot], sem.at[1, slot]))

    @pl.when(t == 0)
    def _():
        for s in range(ahead):
            for c in copies(s, s):
                c.start()

    @pl.when(t + ahead < nseq)
    def _():
        for c in copies(t + ahead, lax.rem(t + ahead, CACHE_RING)):
            c.start()

    slot = lax.rem(t, CACHE_RING)
    kt_ref, vt_ref = kbuf.at[slot], vbuf.at[slot]
    last = len(FF_SPLITS) - 1
    for k, (c0, c1) in enumerate(FF_SPLITS):
        @pl.when(j == k)
        def _():
            if k == 0:
                u = _rms(x_ref[...], gpre_ref[...]).astype(BF16)
                u_s[...] = u
            else:
                u = u_s[...]
            h = (_silu(_dot(u, wg_ref[:, c0:c1])) * _dot(u, wu_ref[:, c0:c1])).astype(BF16)
            d = _dot(h, wd_ref[c0:c1, :])
            if k == 0:
                acc_s[...] = d
            elif k < last:
                acc_s[...] += d
            else:
                o_ref[...] = x_ref[...] + 0.5 * _rms(acc_s[...] + d, gpost_ref[...])
            for c in copies(t, slot):
                c.wait()
            _sample_attn_body(q_ref, kn_ref, vn_ref, kt_ref, vt_ref, cc_ref, cn_ref, oa_ref)


def _ffn_with_sample_attn(x, gpre, gpost, wg, wu, wd, q, kn, vn, cache_k, cache_v, t_new, past_len, tm):
    t = x.shape[0]
    nb, wb = cache_k.shape[0], cache_k.shape[1]
    nsplit = len(FF_SPLITS)
    assert t_new <= S_ROWS and nb == (t // tm) * nsplit

    def rows(a):
        return jnp.pad(a.reshape(nb, t_new, ATT_WIDTH), ((0, 0), (0, S_ROWS - t_new), (0, 0)))

    def grouped(c):
        return c.transpose(0, 2, 3, 1).reshape(nb, N_ATT_HEADS // S_HGRP, S_HGRP * HEAD_DIM, wb)

    row = pl.BlockSpec((tm, D_MODEL), lambda i, j: (i, 0))
    small = pl.BlockSpec((None, S_ROWS, ATT_WIDTH), lambda i, j: (i * nsplit + j, 0, 0))
    big = pl.BlockSpec(memory_space=pl.ANY)
    slots = pltpu.VMEM((CACHE_RING, N_ATT_HEADS // S_HGRP, S_HGRP * HEAD_DIM, wb), F32)
    tabs = [jnp.asarray(a) for a in _sample_tables(t_new, past_len, wb)]
    h1, o = pl.pallas_call(
        _ffn_attn_kernel, grid=(t // tm, nsplit),
        in_specs=[row] + _ffn_specs() + [small, small, small, big, big]
        + [_full_spec(a.shape) for a in tabs],
        out_specs=[row, small],
        out_shape=[jax.ShapeDtypeStruct((t, D_MODEL), F32),
                   jax.ShapeDtypeStruct((nb, S_ROWS, ATT_WIDTH), F32)],
        scratch_shapes=[pltpu.VMEM((tm, D_MODEL), BF16), pltpu.VMEM((tm, D_MODEL), F32), slots, slots,
                        pltpu.SemaphoreType.DMA((2, CACHE_RING))],
        compiler_params=_params(2, MIXIN_VMEM_LIMIT), name="ffn_attn_sample")(
            x, gpre, gpost, wg, wu, wd, rows(q), rows(kn), rows(vn),
            grouped(cache_k), grouped(cache_v), *tabs)
    return h1, o[:, :t_new].reshape(nb * t_new, ATT_WIDTH)


R_PAIR = 2


def _sample_ret_tables(t_new):
    r = np.arange(R_PAIR * t_new)
    seq, step = r // t_new, r % t_new
    lg = np.asarray(LOG_G)[:, None, None]
    diff = (step[:, None] - step[None, :])[None]
    ok = ((seq[:, None] == seq[None, :]) & (diff[0] >= 0))[None]
    decay = np.where(ok, np.exp(diff * lg), 0.0).reshape(N_RET_HEADS * r.size, r.size)
    w_head = np.exp((step[None, :] + 1.0) * lg[:, :, 0]).reshape(-1, 1) * np.ones((1, RET_DV))
    w_tail = np.exp((t_new - 1.0 - step)[:, None, None] * lg[None, :, :, 0]) * np.ones((1, 1, RET_DV))
    g_chunk = np.repeat(np.exp(t_new * lg[:, 0, 0]), RET_DK)[:, None] * np.ones((1, RET_DV))
    return [a.astype(np.float32) for a in (decay, w_head, w_tail.reshape(r.size, -1), g_chunk)]


def _sample_ret_kernel(q_ref, k_ref, v_ref, g_ref, gn_ref, st_ref, dec_ref, wh_ref, wt_ref, gc_ref,
                       y_ref, so_ref, *, blocks, t_new):
    nrow = R_PAIR * t_new
    lane_head = lax.broadcasted_iota(jnp.int32, (nrow, RET_QK), 1) // RET_DK
    row_seq = lax.broadcasted_iota(jnp.int32, (nrow, RET_QK), 0) // t_new
    out_seq = (lax.broadcasted_iota(jnp.int32, (N_RET_HEADS * nrow, RET_DV), 0) % nrow) // t_new
    for blk in range(blocks):
        rs = slice(blk * nrow, (blk + 1) * nrow)
        q8, k8, v8 = q_ref[rs, :], k_ref[rs, :], v_ref[rs, :]
        qm = jnp.concatenate([jnp.where(lane_head == h, q8, 0.0) for h in range(N_RET_HEADS)],
                             axis=0).astype(BF16)
        inner = (_dot_nt(qm, k8.astype(BF16)) * dec_ref[...]).astype(BF16)
        o_all = _dot(inner, v8.astype(BF16))
        vw = (v8 * wt_ref[...]).astype(BF16)
        cross = None
        for s in range(R_PAIR):
            state = st_ref[blk * R_PAIR + s]
            c = _dot(qm, state.astype(BF16))
            cross = c if cross is None else jnp.where(out_seq == s, c, cross)
            km = jnp.where(row_seq == s, k8, 0.0).astype(BF16)
            upd = _dot_tn(km, vw)
            upd = jnp.concatenate([upd[h * RET_DK:(h + 1) * RET_DK, h * RET_DV:(h + 1) * RET_DV]
                                   for h in range(N_RET_HEADS)], axis=0)
            so_ref[blk * R_PAIR + s] = gc_ref[...] * state + upd
        cross = cross * wh_ref[...]
        for h in range(N_RET_HEADS):
            hs = slice(h * RET_DV, (h + 1) * RET_DV)
            oh = o_all[h * nrow:(h + 1) * nrow, hs] + cross[h * nrow:(h + 1) * nrow]
            mu = jnp.mean(oh, axis=-1, keepdims=True)
            xc = oh - mu
            var = jnp.mean(xc * xc, axis=-1, keepdims=True)
            y = xc * lax.rsqrt(var + NORM_EPS) * gn_ref[:, hs]
            y_ref[rs, hs] = _silu(g_ref[rs, hs]) * y


def _ret_sample(qr, kr, vr, gr, gn, state, t_new, blocks=8):
    nb = state.shape[0]
    assert R_PAIR * t_new == 8 and nb % (R_PAIR * blocks) == 0
    rows = R_PAIR * t_new * blocks
    tabs = [jnp.asarray(a) for a in _sample_ret_tables(t_new)]
    st = state.reshape(nb, N_RET_HEADS * RET_DK, RET_DV)

    def blk(w_):
        return pl.BlockSpec((rows, w_), lambda i: (i, 0))

    st_spec = pl.BlockSpec((R_PAIR * blocks, N_RET_HEADS * RET_DK, RET_DV), lambda i: (i, 0, 0))
    y, s = pl.pallas_call(
        functools.partial(_sample_ret_kernel, blocks=blocks, t_new=t_new),
        grid=(nb // (R_PAIR * blocks),),
        in_specs=[blk(RET_QK), blk(RET_QK), blk(RET_V), blk(RET_V), _full_spec((1, RET_V)), st_spec]
        + [_full_spec(a.shape) for a in tabs],
        out_specs=[blk(RET_V), st_spec],
        out_shape=[jax.ShapeDtypeStruct((nb * t_new, RET_V), F32), jax.ShapeDtypeStruct(st.shape, F32)],
        compiler_params=_params(1), name="ret_sample")(qr, kr, vr, gr, gn, st, *tabs)
    return y, s.reshape(state.shape)


CONV_COLS = 256


def _ffn_conv_kernel(*refs, with_mix):
    if with_mix:
        (oa_ref, yr_ref, h_ref, wo_ref, gmb_ref, gpre_ref, gpost_ref, wg_ref, wu_ref, wd_ref,
         o_ref, wob_ref, wgb_ref, wub_ref, wdb_ref, x_s, u_s, acc_s) = refs
    else:
        (h_ref, gpre_ref, gpost_ref, wg_ref, wu_ref, wd_ref,
         o_ref, wgb_ref, wub_ref, wdb_ref, x_s, u_s, acc_s) = refs
    c = pl.program_id(0)

    @pl.when(c == 0)
    def _():
        x = h_ref[...]
        if with_mix:
            wo = wo_ref[...].astype(BF16)
            wob_ref[...] = wo
            mixed = jnp.concatenate([oa_ref[...].astype(BF16), yr_ref[...].astype(BF16)], axis=1)
            x = x + _rms(_dot(mixed, wo), gmb_ref[...])
        x_s[...] = x
        u_s[...] = _rms(x, gpre_ref[...]).astype(BF16)
        acc_s[...] = jnp.zeros_like(acc_s)

    wg, wu, wd = wg_ref[...].astype(BF16), wu_ref[...].astype(BF16), wd_ref[...].astype(BF16)
    wgb_ref[...] = wg
    wub_ref[...] = wu
    wdb_ref[...] = wd
    u = u_s[...]
    acc_s[...] += _dot((_silu(_dot(u, wg)) * _dot(u, wu)).astype(BF16), wd)

    @pl.when(c == pl.num_programs(0) - 1)
    def _():
        o_ref[...] = x_s[...] + 0.5 * _rms(acc_s[...], gpost_ref[...])


def _ffn_convert(h, gpre, gpost, wg, wu, wd, mix=None):
    t = h.shape[0]
    col = pl.BlockSpec((D_MODEL, CONV_COLS), lambda c: (0, c))
    rowb = pl.BlockSpec((CONV_COLS, D_MODEL), lambda c: (c, 0))
    vec = _full_spec((1, D_MODEL))
    ins, in_specs, outs, out_specs = [], [], [], []
    if mix is not None:
        att, ret, wo, gmb = mix
        ins += [att, ret]
        in_specs += [_full_spec((t, ATT_WIDTH)), _full_spec((t, RET_V))]
    ins.append(h)
    in_specs.append(_full_spec((t, D_MODEL)))
    if mix is not None:
        ins += [wo, gmb]
        in_specs += [_full_spec((D_MODEL, D_MODEL)), vec]
    ins += [gpre, gpost, wg, wu, wd]
    in_specs += [vec, vec, col, col, rowb]
    out_shape = [jax.ShapeDtypeStruct((t, D_MODEL), F32)]
    out_specs = [_full_spec((t, D_MODEL))]
    if mix is not None:
        out_shape.append(jax.ShapeDtypeStruct((D_MODEL, D_MODEL), BF16))
        out_specs.append(_full_spec((D_MODEL, D_MODEL)))
    out_shape += [jax.ShapeDtypeStruct((D_MODEL, D_FF), BF16)] * 2 + [jax.ShapeDtypeStruct((D_FF, D_MODEL), BF16)]
    out_specs += [col, col, rowb]
    return pl.pallas_call(
        functools.partial(_ffn_conv_kernel, with_mix=mix is not None), grid=(D_FF // CONV_COLS,),
        in_specs=in_specs, out_specs=out_specs, out_shape=out_shape,
        scratch_shapes=[pltpu.VMEM((t, D_MODEL), F32), pltpu.VMEM((t, D_MODEL), BF16),
                        pltpu.VMEM((t, D_MODEL), F32)],
        compiler_params=_params(1), name="ffn_convert")(*ins)


def _mixin_sample_kernel(h_ref, g_ref, w_ref, ba_ref, oa_ref, br_ref, or_ref,
                         qa_ref, kk_ref, vk_ref, qr_ref, kr_ref, vr_ref, gr_ref, wb_ref,
                         ta_s, tr_s, u_s):
    c = pl.program_id(0)

    @pl.when(c == 0)
    def _():
        _rope_fill(ba_ref, oa_ref, ta_s)
        _rope_fill(br_ref, or_ref, tr_s)
        u_s[...] = _rms(h_ref[...], g_ref[...]).astype(BF16)

    w = w_ref[...].astype(BF16)
    wb_ref[...] = w
    y = _dot(u_s[...], w)
    nblk = h_ref.shape[0] // ROPE_ROWS

    def rotary(tab_s, half, cols, dst_ref, scale):
        for rb in range(nblk):
            rows = slice(rb * ROPE_ROWS, (rb + 1) * ROPE_ROWS)
            rope = _rope_rows(tab_s, half, rows)
            for s in range(dst_ref.shape[1] // LANES):
                dst_ref[rows, s * LANES:(s + 1) * LANES] = rope(
                    y[rows, cols + s * LANES:cols + (s + 1) * LANES]) * scale

    @pl.when(c == 0)
    def _():
        rotary(ta_s, ROT_DIM // 2, 0, qa_ref, HEAD_DIM ** -0.5 * LOG2_E)

    @pl.when(c == 1)
    def _():
        rotary(ta_s, ROT_DIM // 2, 0, kk_ref, 1.0)

    @pl.when(c == 2)
    def _():
        vk_ref[...] = y

    @pl.when(c == 3)
    def _():
        rotary(tr_s, RET_DK // 2, 0, qr_ref, 1.0)
        rotary(tr_s, RET_DK // 2, RET_QK, kr_ref, RET_DK ** -0.5)

    @pl.when(c == 4)
    def _():
        vr_ref[...] = y

    @pl.when(c == 5)
    def _():
        gr_ref[...] = y


def _mixin_sample(h, g, w, base_pos, off_pos):
    t = h.shape[0]
    grp = ATT_WIDTH
    assert IN_WIDTH == 6 * grp and 2 * RET_QK == grp and RET_V == grp
    base_a, off_a = _rope_consts(base_pos, off_pos, ROT_DIM, ROPE_THETA, HEAD_DIM)
    base_r, off_r = _rope_consts(base_pos, off_pos, RET_DK, RET_THETA, RET_DK)
    wcol = pl.BlockSpec((D_MODEL, grp), lambda c: (0, c))
    base_spec = pl.BlockSpec((None, 2, LANES), lambda c: (0, 0, 0))
    widths = [ATT_WIDTH, ATT_WIDTH, ATT_WIDTH, RET_QK, RET_QK, RET_V, RET_V]
    tab = pltpu.VMEM((3, t, LANES), F32)
    return pl.pallas_call(
        _mixin_sample_kernel, grid=(IN_WIDTH // grp,),
        in_specs=[_full_spec((t, D_MODEL)), _full_spec((1, D_MODEL)), wcol,
                  base_spec, _full_spec(off_a.shape), base_spec, _full_spec(off_r.shape)],
        out_specs=[_full_spec((t, w_)) for w_ in widths] + [wcol],
        out_shape=[jax.ShapeDtypeStruct((t, w_), F32) for w_ in widths]
        + [jax.ShapeDtypeStruct((D_MODEL, IN_WIDTH), BF16)],
        scratch_shapes=[tab, tab, pltpu.VMEM((t, D_MODEL), BF16)],
        compiler_params=_params(1), name="mix_in_sample")(h, g, w, base_a, off_a, base_r, off_r)


def kernel(x_prompt, x_sample, cache_k, cache_v, state_ret, g_ffn1_pre, g_ffn1_post, w1_gate, w1_up,
           w1_down, g_mix_pre, g_mix_post, w_in, gn_w, w_out, g_ffn2_pre, g_ffn2_post, w2_gate, w2_up,
           w2_down):
    b_p, s_p, _ = x_prompt.shape
    b_s, t_s, _ = x_sample.shape
    depth = w_in.shape[0]
    assert depth == 1 and b_p == 1
    keep = min(WIN_MAX, s_p)
    l = 0
    vec = lambda g: g[l].reshape(1, -1)
    g1a, g1b, gma, gmb = vec(g_ffn1_pre), vec(g_ffn1_post), vec(g_mix_pre), vec(g_mix_post)
    g2a, g2b, gn = vec(g_ffn2_pre), vec(g_ffn2_post), vec(gn_w)

    n_s = b_s * t_s
    xs = x_sample.reshape(n_s, D_MODEL)
    h1s, *w1 = _ffn_convert(xs, g1a, g1b, w1_gate[l], w1_up[l], w1_down[l])
    qas, kks, vks, qrs, krs, vrs, grs, wi = _mixin_sample(
        h1s, gma, w_in[l], np.full((1,), PAST_LEN), np.tile(np.arange(t_s), b_s))

    tm = RUN
    xp = x_prompt.reshape(s_p, D_MODEL)
    h1, o_att_s = _ffn_with_sample_attn(xp, g1a, g1b, *w1, qas, kks, vks, cache_k[l], cache_v[l],
                                        t_s, PAST_LEN, tm)
    q4, k4, v4, q16, k16, v16, kk, vk, y_ret, st_p = _mixin(
        h1, gma, wi, 2 * tm * np.arange(s_p // (2 * tm)), np.arange(2 * tm), keep, 2 * tm, gn)
    o_att = _attn_prompt(q4, k4, v4, q16, k16, v16)

    y_ret_s, st_s = _ret_sample(qrs, krs, vrs, grs, gn, state_ret[l], t_s)
    y_sample, wo, *w2 = _ffn_convert(h1s, g2a, g2b, w2_gate[l], w2_up[l], w2_down[l],
                                     mix=(o_att_s, y_ret_s, w_out[l], gmb))
    y_prompt = _mixout_ffn(o_att, y_ret, h1, wo, gmb, g2a, g2b, *w2, 2 * tm)

    hd = (N_ATT_HEADS, HEAD_DIM)
    return (y_prompt.reshape(b_p, s_p, D_MODEL),
            y_sample.reshape(b_s, t_s, D_MODEL),
            kk.reshape(depth, b_p, keep, *hd),
            vk.reshape(depth, b_p, keep, *hd),
            st_p.reshape(depth, b_p, N_RET_HEADS, RET_DK, RET_DV),
            kks.reshape(depth, b_s, t_s, *hd),
            vks.reshape(depth, b_s, t_s, *hd),
            st_s.reshape(depth, b_s, N_RET_HEADS, RET_DK, RET_DV))
```
